```python
import math
import jax, jax.numpy as jnp
from jax import lax
import numpy as np

D_MODEL = 1024
BATCH = 4
SEQ = 4096
DEPTH = 4

N_EVEN = (DEPTH + 1) // 2
N_ODD = DEPTH // 2
RMS_EPS = 1e-6

GDN_HEADS = 4
GDN_DK = 128
GDN_DV = 128
GDN_CONV = 4
GDN_CHUNK = 64
GDN_QK = GDN_HEADS * GDN_DK
GDN_VD = GDN_HEADS * GDN_DV
GDN_SPLITS = (GDN_QK, GDN_QK, GDN_VD, GDN_VD, GDN_HEADS, GDN_HEADS)
GDN_COLS = sum(GDN_SPLITS)

RWKV_HEADS = 8
RWKV_HEAD = 64
RWKV_DIM = RWKV_HEADS * RWKV_HEAD
DECAY_LORA = 32
AAA_LORA = 32
MV_LORA = 32
GATE_LORA = 96
RWKV_LN_EPS = 64e-5
RWKV_SPLITS = (RWKV_DIM, RWKV_DIM, RWKV_DIM, DECAY_LORA, AAA_LORA, GATE_LORA)
RWKV_COLS = sum(RWKV_SPLITS)

EV_COLS = GDN_COLS + RWKV_COLS
MIX_WIDTH_EVEN = GDN_VD + RWKV_DIM

DIL_PATTERNS = ((128, 1), (512, 4), (2048, 16))
N_DIL_GROUPS = len(DIL_PATTERNS)
DIL_HEADS = 8
DIL_HEAD_DIM = 64
DIL_WIDTH = DIL_HEADS * DIL_HEAD_DIM
DIL_QKV = N_DIL_GROUPS * DIL_WIDTH
OD_COLS = 3 * DIL_QKV
ATT_BLOCK = 128
ROPE_THETA = 10000.0

N_GROUPS = 4
EXPERTS_PER_GROUP = 4
N_EXPERTS = N_GROUPS * EXPERTS_PER_GROUP
EXPERT_FF = 256
TOP_K_INNER = 2

kernel_name = "hybrid_gdn_rwkv7_dilated_hmoe"


def rmsnorm(x, gain):
    xf = x.astype(jnp.float32)
    y = xf * lax.rsqrt(jnp.mean(xf * xf, axis=-1, keepdims=True) + RMS_EPS)
    return (y * gain.astype(jnp.float32)).astype(x.dtype)


def l2norm(t):
    return t * lax.rsqrt(jnp.sum(t * t, axis=-1, keepdims=True) + 1e-6)


def split_cols(h, sizes):
    return jnp.split(h, np.cumsum(sizes)[:-1].tolist(), axis=-1)


def token_shift(h):
    return jnp.pad(h, ((0, 0), (1, 0), (0, 0)))[:, :-1]


def causal_depthwise_conv(x, w):
    K = w.shape[0]
    S = x.shape[1]
    xp = jnp.pad(x, ((0, 0), (K - 1, 0), (0, 0)))
    return sum(xp[:, i:i + S] * w[i] for i in range(K))


def gated_delta_rule_chunked(q, k, v, g, beta):
    B, S, H, Dk = q.shape
    Dv = v.shape[-1]
    C = GDN_CHUNK
    N = S // C

    def to_chunks(t):
        return jnp.moveaxis(t.reshape(B, N, C, H, *t.shape[3:]), 3, 1)

    q, k, v, g, beta = (to_chunks(t) for t in (q, k, v, g, beta))
    gc = jnp.cumsum(g, axis=-1)
    causal = jnp.tril(jnp.ones((C, C), bool))
    strict = jnp.tril(jnp.ones((C, C), bool), -1)
    diff = gc[..., :, None] - gc[..., None, :]
    decay = jnp.where(causal, jnp.exp(jnp.where(causal, diff, 0.0)), 0.0)
    kb = k * beta[..., None]
    vb = v * beta[..., None]
    L = jnp.where(strict, jnp.einsum('bhnik,bhnjk->bhnij', kb, k) * decay, 0.0)
    eye = jnp.eye(C, dtype=q.dtype)
    rhs = jnp.concatenate([vb, kb * jnp.exp(gc)[..., None]], axis=-1)
    sol = lax.linalg.triangular_solve(L + eye, rhs, left_side=True, lower=True, unit_diagonal=True)
    u, w = sol[..., :Dv], sol[..., Dv:]
    aqk = jnp.einsum('bhnik,bhnjk->bhnij', q, k) * decay
    q_dec = q * jnp.exp(gc)[..., None]
    g_last = gc[..., -1]
    k_dec = k * jnp.exp(g_last[..., None] - gc)[..., None]

    def step(state, inp):
        u_n, w_n, aqk_n, qd_n, kd_n, gl_n = inp
        v_new = u_n - jnp.einsum('bhck,bhkv->bhcv', w_n, state)
        o = jnp.einsum('bhck,bhkv->bhcv', qd_n, state) + jnp.einsum('bhij,bhjv->bhiv', aqk_n, v_new)
        state = state * jnp.exp(gl_n)[..., None, None] + jnp.einsum('bhck,bhcv->bhkv', kd_n, v_new)
        return state, o

    xs = tuple(jnp.moveaxis(t, 2, 0) for t in (u, w, aqk, q_dec, k_dec, g_last))
    s0 = jnp.zeros((B, H, Dk, Dv), q.dtype)
    _, o = lax.scan(step, s0, xs)
    return jnp.transpose(o, (1, 0, 3, 2, 4)).reshape(B, S, H, Dv)


def gdn_mixer(hA, conv_w, A_log, dt_bias, norm_w):
    B, S, _ = hA.shape
    f32 = jnp.float32
    q, k, v, z, a, b = split_cols(hA, GDN_SPLITS)
    qkv = jax.nn.silu(causal_depthwise_conv(jnp.concatenate([q, k, v], axis=-1), conv_w)).astype(f32)
    q, k, v = split_cols(qkv, (GDN_QK, GDN_QK, GDN_VD))
    q = l2norm(q.reshape(B, S, GDN_HEADS, GDN_DK)) * (GDN_DK ** -0.5)
    k = l2norm(k.reshape(B, S, GDN_HEADS, GDN_DK))
    v = v.reshape(B, S, GDN_HEADS, GDN_DV)
    beta = jax.nn.sigmoid(b.astype(f32))
    g = -jnp.exp(A_log.astype(f32)) * jax.nn.softplus(a.astype(f32) + dt_bias.astype(f32))
    o = gated_delta_rule_chunked(q, k, v, g, beta)
    o = rmsnorm(o, norm_w) * jax.nn.silu(z.astype(f32).reshape(B, S, GDN_HEADS, GDN_DV))
    return o.reshape(B, S, GDN_VD)


def rwkv7_scan(r, w, k, v, a, b):
    B, S, H, N = r.shape

    def step(state, inp):
        r_t, w_t, k_t, v_t, a_t, b_t = inp
        sa = jnp.einsum('bhvk,bhk->bhv', state, a_t)
        state = state * w_t[:, :, None, :] + sa[..., None] * b_t[:, :, None, :] + v_t[..., None] * k_t[:, :, None, :]
        return state, jnp.einsum('bhvk,bhk->bhv', state, r_t)

    xs = tuple(jnp.moveaxis(t, 1, 0) for t in (r, w, k, v, a, b))
    s0 = jnp.zeros((B, H, N, N), r.dtype)
    _, y = lax.scan(step, s0, xs)
    return jnp.moveaxis(y, 0, 1)


def rwkv7_mixer(hB, mu, w0, w2, a0, a2, g2, k_k, k_a, r_k, ln_w, ln_b, v_first, vres):
    B, S, _ = hB.shape
    f32 = jnp.float32
    hB = hB.astype(f32)
    hB = hB + mu * (token_shift(hB) - hB)
    r, k, v, hw, ha, hg = split_cols(hB, RWKV_SPLITS)
    w_log = -jax.nn.softplus(-(w0 + jnp.tanh(hw) @ w2)) - 0.5
    decay = jnp.exp(-jnp.exp(w_log))
    a = jax.nn.sigmoid(a0 + ha @ a2)
    g = jax.nn.sigmoid(hg) @ g2
    if vres is None:
        v_first = v
    else:
        hv, vres_mu, v0, v2 = vres
        hv = hv.astype(f32)
        hv = hv + vres_mu * (token_shift(hv) - hv)
        v = v + (v_first - v) * jax.nn.sigmoid(v0 + hv @ v2)

    def heads(t):
        return t.reshape(B, S, RWKV_HEADS, RWKV_HEAD)

    kk = l2norm(heads(k * k_k))
    k = k * (1.0 + (a - 1.0) * k_a)
    r_h, k_h, v_h, a_h = heads(r), heads(k), heads(v), heads(a)
    y = rwkv7_scan(r_h, heads(decay), k_h, v_h, -kk, kk * a_h)
    mean = jnp.mean(y, axis=-1, keepdims=True)
    var = jnp.mean(jnp.square(y - mean), axis=-1, keepdims=True)
    y = ((y - mean) * lax.rsqrt(var + RWKV_LN_EPS)).reshape(B, S, RWKV_DIM) * ln_w + ln_b
    bonus = jnp.sum(r_h * k_h * r_k, axis=-1, keepdims=True) * v_h
    y = y + bonus.reshape(B, S, RWKV_DIM)
    return y * g, v_first


def rope(x, positions):
    half = x.shape[-1] // 2
    inv_freq = ROPE_THETA ** (-jnp.arange(half, dtype=jnp.float32) * 2.0 / x.shape[-1])
    ang = positions.astype(jnp.float32)[..., None] * inv_freq
    cos, sin = jnp.cos(ang)[:, :, None, :], jnp.sin(ang)[:, :, None, :]
    x1, x2 = x[..., :half], x[..., half:]
    return jnp.concatenate([x1 * cos - x2 * sin, x2 * cos + x1 * sin], axis=-1)


def dilated_window_attention(q, k, v, window, dilation):
    B, S, H, Dh = q.shape
    span = window // dilation
    L = S // dilation
    nb = -(-L // ATT_BLOCK)
    Lp = nb * ATT_BLOCK

    def to_blocks(t):
        t = t.reshape(B, L, dilation, H, Dh)
        t = jnp.pad(t, ((0, 0), (0, Lp - L), (0, 0), (0, 0), (0, 0)))
        return jnp.transpose(t.reshape(B, nb, ATT_BLOCK, dilation, H, Dh), (0, 3, 4, 1, 2, 5))

    def with_prev(t):
        prev = jnp.pad(t, ((0, 0), (0, 0), (0, 0), (1, 0), (0, 0), (0, 0)))[:, :, :, :-1]
        return jnp.concatenate([prev, t], axis=-2)

    qb = to_blocks(q)
    kw = with_prev(to_blocks(k))
    vw = with_prev(to_blocks(v))
    s = jnp.einsum('brhnqd,brhnkd->brhnqk', qb, kw) * (Dh ** -0.5)
    qi = jnp.arange(ATT_BLOCK)[:, None]
    kj = jnp.arange(2 * ATT_BLOCK)[None, :]
    rel = ATT_BLOCK + qi - kj
    blk = jnp.arange(nb)[:, None, None]
    valid = (rel >= 0) & (rel <= span) & ((blk > 0) | (kj >= ATT_BLOCK))
    s = jnp.where(valid, s, -jnp.inf)
    m = jnp.max(s, axis=-1, keepdims=True)
    p = jnp.exp(s - m)
    l = jnp.sum(p, axis=-1, keepdims=True)
    o = jnp.einsum('brhnqk,brhnkd->brhnqd', p, vw) / l
    lse = (m + jnp.log(l))[..., 0]

    def from_blocks(t):
        t = jnp.moveaxis(t, (3, 4), (1, 2))
        t = t.reshape(B, Lp, dilation, H, *t.shape[5:])[:, :L]
        return t.reshape(B, S, H, *t.shape[4:])

    return from_blocks(o), from_blocks(lse)


def dilated_mixer(h, positions):
    B, S, _ = h.shape
    q, k, v = split_cols(h.astype(jnp.float32), (DIL_QKV, DIL_QKV, DIL_QKV))
    shp = (B, S, N_DIL_GROUPS * DIL_HEADS, DIL_HEAD_DIM)
    q = rope(q.reshape(shp), positions)
    k = rope(k.reshape(shp), positions)
    v = v.reshape(shp)
    outs, lses = [], []
    for gi, (window, dilation) in enumerate(DIL_PATTERNS):
        sl = slice(gi * DIL_HEADS, (gi + 1) * DIL_HEADS)
        o, lse = dilated_window_attention(q[:, :, sl], k[:, :, sl], v[:, :, sl], window, dilation)
        outs.append(o)
        lses.append(lse)
    wts = jax.nn.softmax(jnp.stack(lses), axis=0)
    o = jnp.sum(wts[..., None] * jnp.stack(outs), axis=0)
    return o.reshape(B, S, DIL_WIDTH)


def hierarchical_moe(x, w_group, w_expert, w1, w3, w2):
    B, S, D = x.shape
    t = x.reshape(B * S, D)
    f32 = jnp.float32
    group_logits = (t @ w_group).astype(f32)
    group_prob = jax.nn.softmax(group_logits, axis=-1)
    g_sel = jnp.argmax(group_logits, axis=-1)
    p_group = jnp.take_along_axis(group_prob, g_sel[:, None], axis=-1)
    exp_logits = jnp.einsum('td,gde->tge', t, w_expert).astype(f32)
    sel_logits = jnp.take_along_axis(exp_logits, g_sel[:, None, None], axis=1)[:, 0]
    top_val, top_idx = lax.top_k(sel_logits, TOP_K_INNER)
    inner = jax.nn.softmax(top_val, axis=-1) * p_group
    expert_id = g_sel[:, None] * EXPERTS_PER_GROUP + top_idx
    gates = jnp.sum(jax.nn.one_hot(expert_id, N_EXPERTS, dtype=f32) * inner[..., None], axis=1)
    h = jax.nn.silu(jnp.einsum('td,edf->tef', t, w1)) * jnp.einsum('td,edf->tef', t, w3)
    y = jnp.einsum('tef,efd->td', h * gates[..., None].astype(h.dtype), w2)
    return y.reshape(B, S, D)


def setup_inputs(seed: int = 0) -> dict:
    key = jax.random.key(seed)
    ks = iter(jax.random.split(key, 48))
    f32 = jnp.float32

    def nrm(shape, scale):
        return jax.random.normal(next(ks), shape, f32) * scale

    def uni(shape, lo, hi):
        return jax.random.uniform(next(ks), shape, f32, lo, hi)

    NE, NO = N_EVEN, N_ODD
    res_scale = (2 * DEPTH) ** -0.5
    dt = jnp.exp(uni((NE, GDN_HEADS), math.log(1e-3), math.log(1e-1)))
    return {
        'x': nrm((BATCH, SEQ, D_MODEL), 1.0),
        'positions': jnp.broadcast_to(jnp.arange(SEQ, dtype=jnp.int32), (BATCH, SEQ)),
        'ev_norm': 1.0 + nrm((NE, D_MODEL), 0.02),
        'ev_w_in': nrm((NE, D_MODEL, EV_COLS), D_MODEL ** -0.5),
        'rwkv_vres_down': nrm((NE - 1, D_MODEL, MV_LORA), D_MODEL ** -0.5),
        'ev_w_out': nrm((NE, MIX_WIDTH_EVEN, D_MODEL), MIX_WIDTH_EVEN ** -0.5 * res_scale),
        'gdn_conv_w': nrm((NE, GDN_CONV, 2 * GDN_QK + GDN_VD), GDN_CONV ** -0.5),
        'gdn_A_log': jnp.log(uni((NE, GDN_HEADS), 1.0, 16.0)),
        'gdn_dt_bias': dt + jnp.log(-jnp.expm1(-dt)),
        'gdn_norm': 1.0 + nrm((NE, GDN_DV), 0.02),
        'rwkv_mu': uni((NE, RWKV_COLS), 0.0, 1.0),
        'rwkv_w0': uni((NE, RWKV_DIM), -6.5, -1.5),
        'rwkv_w2': nrm((NE, DECAY_LORA, RWKV_DIM), 0.1),
        'rwkv_a0': nrm((NE, RWKV_DIM), 0.1),
        'rwkv_a2': nrm((NE, AAA_LORA, RWKV_DIM), AAA_LORA ** -0.5),
        'rwkv_g2': nrm((NE, GATE_LORA, RWKV_DIM), GATE_LORA ** -0.5),
        'rwkv_k_k': 0.85 + nrm((NE, RWKV_DIM), 0.02),
        'rwkv_k_a': 1.0 + nrm((NE, RWKV_DIM), 0.02),
        'rwkv_r_k': nrm((NE, RWKV_HEADS, RWKV_HEAD), 0.1),
        'rwkv_ln_w': 1.0 + nrm((NE, RWKV_DIM), 0.02),
        'rwkv_ln_b': nrm((NE, RWKV_DIM), 0.02),
        'rwkv_vres_mu': uni((NE - 1, MV_LORA), 0.0, 1.0),
        'rwkv_v0': 1.0 + nrm((NE - 1, RWKV_DIM), 0.1),
        'rwkv_v2': nrm((NE - 1, MV_LORA, RWKV_DIM), MV_LORA ** -0.5),
        'od_norm': 1.0 + nrm((NO, D_MODEL), 0.02),
        'od_w_in': nrm((NO, D_MODEL, OD_COLS), D_MODEL ** -0.5),
        'od_w_out': nrm((NO, DIL_WIDTH, D_MODEL), DIL_WIDTH ** -0.5 * res_scale),
        'ffn_norm': 1.0 + nrm((DEPTH, D_MODEL), 0.02),
        'moe_w_group': nrm((DEPTH, D_MODEL, N_GROUPS), D_MODEL ** -0.5),
        'moe_w_expert': nrm((DEPTH, N_GROUPS, D_MODEL, EXPERTS_PER_GROUP), D_MODEL ** -0.5),
        'moe_w1': nrm((DEPTH, N_EXPERTS, D_MODEL, EXPERT_FF), D_MODEL ** -0.5),
        'moe_w3': nrm((DEPTH, N_EXPERTS, D_MODEL, EXPERT_FF), D_MODEL ** -0.5),
        'moe_w2': nrm((DEPTH, N_EXPERTS, EXPERT_FF, D_MODEL), EXPERT_FF ** -0.5 * res_scale),
        'final_norm': 1.0 + nrm((D_MODEL,), 0.02),
    }


def reference(x, positions, ev_norm, ev_w_in, rwkv_vres_down, ev_w_out,
              gdn_conv_w, gdn_A_log, gdn_dt_bias, gdn_norm,
              rwkv_mu, rwkv_w0, rwkv_w2, rwkv_a0, rwkv_a2, rwkv_g2, rwkv_k_k, rwkv_k_a, rwkv_r_k,
              rwkv_ln_w, rwkv_ln_b, rwkv_vres_mu, rwkv_v0, rwkv_v2,
              od_norm, od_w_in, od_w_out,
              ffn_norm, moe_w_group, moe_w_expert, moe_w1, moe_w3, moe_w2,
              final_norm):
    v_first = None
    for layer in range(DEPTH):
        if layer % 2 == 0:
            i = layer // 2
            h = rmsnorm(x, ev_norm[i])
            if i == 0:
                w_in = ev_w_in[0]
            else:
                w_in = jnp.concatenate([ev_w_in[i], rwkv_vres_down[i - 1]], axis=-1)
            proj = h @ w_in
            hA = proj[..., :GDN_COLS]
            hB = proj[..., GDN_COLS:EV_COLS]
            yA = gdn_mixer(hA, gdn_conv_w[i], gdn_A_log[i], gdn_dt_bias[i], gdn_norm[i])
            vres = None if i == 0 else (proj[..., EV_COLS:], rwkv_vres_mu[i - 1], rwkv_v0[i - 1], rwkv_v2[i - 1])
            yB, v_first = rwkv7_mixer(hB, rwkv_mu[i], rwkv_w0[i], rwkv_w2[i], rwkv_a0[i], rwkv_a2[i], rwkv_g2[i],
                                      rwkv_k_k[i], rwkv_k_a[i], rwkv_r_k[i], rwkv_ln_w[i], rwkv_ln_b[i],
                                      v_first, vres)
            mixed = jnp.concatenate([yA, yB], axis=-1).astype(x.dtype) @ ev_w_out[i]
        else:
            i = layer // 2
            h = rmsnorm(x, od_norm[i])
            mixed = dilated_mixer(h @ od_w_in[i], positions).astype(x.dtype) @ od_w_out[i]
        x = x + mixed
        x = x + hierarchical_moe(rmsnorm(x, ffn_norm[layer]), moe_w_group[layer], moe_w_expert[layer],
                                 moe_w1[layer], moe_w3[layer], moe_w2[layer])
    return rmsnorm(x, final_norm)
```

```python
import functools
import math

import jax
import jax.numpy as jnp
import numpy as np
from jax import lax
from jax.experimental import pallas as pl
from jax.experimental.pallas import tpu as pltpu

F32 = jnp.float32
BF16 = jnp.bfloat16
HIGHEST = lax.Precision.HIGHEST

RMS_EPS = 1e-6
L2_EPS = 1e-6
RWKV_LN_EPS = 64e-5
ROPE_THETA = 10000.0

LANES = 128
SUBLANES = 8
VMEM_LIMIT = 56 * 1024 * 1024

GDN_HEADS = 4
GDN_DK = 128
GDN_DV = 128
GDN_CONV = 4
RWKV_HEADS = 8
RWKV_HEAD = 64
RWKV_DIM = RWKV_HEADS * RWKV_HEAD
LORA_SEG = LANES
DIL_PATTERNS = ((128, 1), (512, 4), (2048, 16))
DIL_HEADS = 8
DIL_HEAD_DIM = 64
DIL_WIDTH = DIL_HEADS * DIL_HEAD_DIM
ATT_BLOCK = 128
N_GROUPS = 4
EXPERTS_PER_GROUP = 4
N_EXPERTS = N_GROUPS * EXPERTS_PER_GROUP
CHUNK = 64

EV_Q, EV_K, EV_V, EV_Z = 0, 512, 1024, 1536
EV_R, EV_RK, EV_RV, EV_LORA = 2048, 2560, 3072, 3584
EV_AB = EV_LORA + 4 * LORA_SEG
EV_NPAD = EV_AB + LANES


def _cparams(sem):
    return pltpu.CompilerParams(dimension_semantics=sem, vmem_limit_bytes=VMEM_LIMIT)


def _sigmoid(x):
    return 1.0 / (1.0 + jnp.exp(-x))


def _softplus(x):
    return jnp.maximum(x, 0.0) + jnp.log(1.0 + jnp.exp(-jnp.abs(x)))


def _rmsnorm(x, gain):
    return x * lax.rsqrt(jnp.mean(x * x, axis=-1, keepdims=True) + RMS_EPS) * gain


def _resident(shape):
    nd = len(shape)
    return pl.BlockSpec(shape, lambda *_: (0,) * nd, pipeline_mode=pl.Buffered(1))


def _rope_table_kernel(pos_ref, freq_ref, sign_ref, cos_ref, sin_ref):
    ang = pos_ref[...].astype(F32) * freq_ref[...]
    cos_ref[...] = jnp.cos(ang)
    sin_ref[...] = jnp.sin(ang) * sign_ref[...]


def rope_tables(positions_flat, tm=1024):
    T = positions_flat.shape[0]
    half = DIL_HEAD_DIM // 2
    inv_freq = ROPE_THETA ** (-jnp.arange(half, dtype=F32) * 2.0 / DIL_HEAD_DIM)
    lane = np.arange(LANES)
    freq = inv_freq[lane % half][None, :]
    sign = jnp.asarray(np.where(lane % DIL_HEAD_DIM < half, -1.0, 1.0), F32)[None, :]
    return pl.pallas_call(
        _rope_table_kernel,
        grid=(T // tm,),
        in_specs=[pl.BlockSpec((tm, 1), lambda i: (i, 0)), _resident((1, LANES)), _resident((1, LANES))],
        out_specs=[pl.BlockSpec((tm, LANES), lambda i: (i, 0))] * 2,
        out_shape=[jax.ShapeDtypeStruct((T, LANES), F32)] * 2,
        compiler_params=_cparams(("parallel",)),
        name="rope_tables",
    )(positions_flat.reshape(T, 1), freq, sign)


def _norm_proj_kernel(x_ref, g_ref, w_ref, o_ref):
    h = _rmsnorm(x_ref[...], g_ref[...]).astype(BF16)
    o_ref[...] = jnp.dot(h, w_ref[...], preferred_element_type=F32)


def norm_proj(x, gain, w, tm=256):
    T, D = x.shape
    N = w.shape[1]
    return pl.pallas_call(
        _norm_proj_kernel,
        grid=(T // tm,),
        in_specs=[pl.BlockSpec((tm, D), lambda i: (i, 0)), _resident((1, D)), _resident((D, N))],
        out_specs=pl.BlockSpec((tm, N), lambda i: (i, 0)),
        out_shape=jax.ShapeDtypeStruct((T, N), F32),
        compiler_params=_cparams(("parallel",)),
        name="norm_proj",
    )(x, gain.reshape(1, D), w)


def _norm_proj_rope_kernel(x_ref, g_ref, w_ref, cos_ref, sin_ref, o_ref, *, n_rope, n_q, q_scale):
    h = _rmsnorm(x_ref[...], g_ref[...]).astype(BF16)
    cos = cos_ref[...]
    sin = sin_ref[...]
    lane = lax.broadcasted_iota(jnp.int32, cos.shape, 1)
    first_half = (lane % DIL_HEAD_DIM) < (DIL_HEAD_DIM // 2)
    N = w_ref.shape[1]
    step = 4 * LANES
    for c0 in range(0, N, step):
        acc = jnp.dot(h, w_ref[:, c0:c0 + step], preferred_element_type=F32)
        if c0 >= n_rope:
            o_ref[:, c0:c0 + step] = acc
            continue
        for j in range(step // LANES):
            blk = acc[:, j * LANES:(j + 1) * LANES]
            rot = jnp.where(first_half, pltpu.roll(blk, LANES - 32, axis=1), pltpu.roll(blk, 32, axis=1))
            out = blk * cos + rot * sin
            if c0 < n_q:
                out = out * q_scale
            o_ref[:, c0 + j * LANES:c0 + (j + 1) * LANES] = out


def norm_proj_rope(x, gain, w, cos, sin, n_rope, n_q, q_scale, tm=256):
    T, D = x.shape
    N = w.shape[1]
    return pl.pallas_call(
        functools.partial(_norm_proj_rope_kernel, n_rope=n_rope, n_q=n_q, q_scale=q_scale),
        grid=(T // tm,),
        in_specs=[pl.BlockSpec((tm, D), lambda i: (i, 0)), _resident((1, D)), _resident((D, N)),
                  pl.BlockSpec((tm, LANES), lambda i: (i, 0)), pl.BlockSpec((tm, LANES), lambda i: (i, 0))],
        out_specs=pl.BlockSpec((tm, N), lambda i: (i, 0)),
        out_shape=jax.ShapeDtypeStruct((T, N), F32),
        compiler_params=_cparams(("parallel",)),
        name="norm_proj_rope",
    )(x, gain.reshape(1, D), w, cos, sin)


def _bmm(a, b, precision=None):
    return jnp.einsum('nik,nkj->nij', a, b, preferred_element_type=F32, precision=precision)


def _bmm_nt(a, b, precision=None):
    return jnp.einsum('nik,njk->nij', a, b, preferred_element_type=F32, precision=precision)


def _bmm_tn(a, b, precision=None):
    return jnp.einsum('nci,ncj->nij', a, b, preferred_element_type=F32, precision=precision)


def _unit_lower_inverse(L):
    C = L.shape[-1]
    ri = lax.broadcasted_iota(jnp.int32, (C, C), 0)
    ci = lax.broadcasted_iota(jnp.int32, (C, C), 1)
    same16 = (ri // 16) == (ci // 16)
    same32 = (ri // 32) == (ci // 32)
    eye = (ri == ci).astype(F32)
    Ld = jnp.where(same16, L, 0.0)
    X = eye - Ld
    P = Ld
    for _ in range(3):
        P = _bmm(P, P, HIGHEST)
        X = X + _bmm(X, P, HIGHEST)
    for off in (jnp.where(same32 & ~same16, L, 0.0), jnp.where(~same32, L, 0.0)):
        X = X - _bmm(_bmm(X, off, HIGHEST), X, HIGHEST)
    return X


def _shifted_rows(buf_ref, cur, prev, first, shifts):
    TB = cur.shape[0]
    buf_ref[0:SUBLANES, :] = jnp.where(first, 0.0, prev)
    buf_ref[SUBLANES:, :] = cur
    return [buf_ref[SUBLANES - s:SUBLANES - s + TB, :] for s in shifts]


def _split_heads(x, n_heads, width):
    return jnp.stack([x[:, h * width:(h + 1) * width] for h in range(n_heads)], axis=0)


def _merge_heads(x):
    return jnp.concatenate([x[h] for h in range(x.shape[0])], axis=-1)


def _gdn_kernel(q_ref, k_ref, v_ref, z_ref, ab_ref, qp_ref, kp_ref, vp_ref, cw_ref, alog_ref, dtb_ref, nw_ref,
                o_ref, buf_ref, s_ref, qd_ref, pg_ref, qg_ref, ov_ref, egl_ref, oc_ref):
    H, DK, DV, C = GDN_HEADS, GDN_DK, GDN_DV, CHUNK
    TB = q_ref.shape[1]
    NC = TB // C
    first = pl.program_id(1) == 0

    @pl.when(first)
    def _():
        s_ref[...] = jnp.zeros_like(s_ref)

    def conv_silu(cur_ref, prev_ref, j):
        cur = cur_ref[0]
        x3, x2, x1 = _shifted_rows(buf_ref, cur, prev_ref[0], first, (3, 2, 1))
        w = cw_ref[j]
        y = x3 * w[0:1] + x2 * w[1:2] + x1 * w[2:3] + cur * w[3:4]
        return y * _sigmoid(y)

    q = _split_heads(conv_silu(q_ref, qp_ref, 0), H, DK)
    k = _split_heads(conv_silu(k_ref, kp_ref, 1), H, DK)
    v = _split_heads(conv_silu(v_ref, vp_ref, 2), H, DV)
    q = q * lax.rsqrt(jnp.sum(q * q, axis=-1, keepdims=True) + L2_EPS) * (DK ** -0.5)
    k = k * lax.rsqrt(jnp.sum(k * k, axis=-1, keepdims=True) + L2_EPS)
    ab = ab_ref[0]
    a = jnp.stack([ab[:, h:h + 1] for h in range(H)], axis=0)
    b = jnp.stack([ab[:, H + h:H + h + 1] for h in range(H)], axis=0)
    beta = _sigmoid(b)
    g = -jnp.exp(alog_ref[...]) * _softplus(a + dtb_ref[...])

    N = H * NC
    q = q.reshape(N, C, DK)
    k = k.reshape(N, C, DK)
    v = v.reshape(N, C, DV)
    beta = beta.reshape(N, C, 1)
    g = g.reshape(N, C, 1)

    ri = lax.broadcasted_iota(jnp.int32, (C, C), 0)
    ci = lax.broadcasted_iota(jnp.int32, (C, C), 1)
    causal = ri >= ci
    strict = ri > ci
    gb = jnp.broadcast_to(g, (N, C, C))
    g_row = jnp.sum(jnp.where(ri == ci, gb, 0.0), axis=1, keepdims=True)
    gc_col = jnp.sum(jnp.where(causal, jnp.broadcast_to(g_row, (N, C, C)), 0.0), axis=2, keepdims=True)
    gc_row = jnp.sum(jnp.where(ri <= ci, gb, 0.0), axis=1, keepdims=True)
    decay = jnp.where(causal, jnp.exp(jnp.where(causal, gc_col - gc_row, 0.0)), 0.0)

    kb = k * beta
    vb = v * beta
    L = jnp.where(strict, _bmm_nt(kb, k) * decay, 0.0)
    tinv = _unit_lower_inverse(L)
    egc = jnp.exp(gc_col)
    u = _bmm(tinv, vb)
    w = _bmm(tinv, kb * egc)
    aqk = _bmm_nt(q, k) * decay
    g_last = gc_col[:, C - 1:C, :]
    k_dec = k * jnp.exp(g_last - gc_col)
    qd_ref[...] = (q * egc - _bmm(aqk, w)).reshape(H, NC, C, DK)
    ov_ref[...] = _bmm(aqk, u).reshape(H, NC, C, DV)
    pg_ref[...] = _bmm_tn(k_dec, w).reshape(H, NC, DK, DK)
    qg_ref[...] = _bmm_tn(k_dec, u).reshape(H, NC, DK, DV)
    egl_ref[...] = jnp.broadcast_to(jnp.exp(g_last), (N, 1, LANES)).reshape(H, NC, 1, LANES)

    for c in range(NC):
        S = s_ref[...]
        oc_ref[:, c] = _bmm(qd_ref[:, c], S) + ov_ref[:, c]
        s_ref[...] = S * egl_ref[:, c] - _bmm(pg_ref[:, c], S) + qg_ref[:, c]

    o = oc_ref[...].reshape(H, TB, DV)
    o = _rmsnorm(o, nw_ref[...])
    z = _split_heads(z_ref[0], H, DV)
    o_ref[0] = _merge_heads(o * (z * _sigmoid(z)))


def gdn_mixer(proj3, conv_w, a_log, dt_bias, norm_w, tb=512):
    B, S, _ = proj3.shape
    H, DK, DV, C = GDN_HEADS, GDN_DK, GDN_DV, CHUNK
    W = H * DK
    NC = tb // C
    nq, nk, nv, nz, nab = EV_Q // W, EV_K // W, EV_V // W, EV_Z // W, EV_AB // LANES
    rows8 = tb // SUBLANES

    def cur(cb):
        return pl.BlockSpec((1, tb, W), lambda b, i, cb=cb: (b, i, cb))

    def prev(cb):
        return pl.BlockSpec((1, SUBLANES, W), lambda b, i, cb=cb: (b, jnp.maximum(i * rows8 - 1, 0), cb))

    cw = conv_w.reshape(GDN_CONV, 3, W).transpose(1, 0, 2)
    return pl.pallas_call(
        _gdn_kernel,
        grid=(B, S // tb),
        in_specs=[cur(nq), cur(nk), cur(nv), cur(nz),
                  pl.BlockSpec((1, tb, LANES), lambda b, i: (b, i, nab)),
                  prev(nq), prev(nk), prev(nv),
                  _resident((3, GDN_CONV, W)), _resident((H, 1, 1)), _resident((H, 1, 1)), _resident((1, 1, DV))],
        out_specs=pl.BlockSpec((1, tb, H * DV), lambda b, i: (b, i, 0)),
        out_shape=jax.ShapeDtypeStruct((B, S, H * DV), F32),
        scratch_shapes=[pltpu.VMEM((tb + SUBLANES, W), F32),
                        pltpu.VMEM((H, DK, DV), F32),
                        pltpu.VMEM((H, NC, C, DK), F32),
                        pltpu.VMEM((H, NC, DK, DK), F32),
                        pltpu.VMEM((H, NC, DK, DV), F32),
                        pltpu.VMEM((H, NC, C, DV), F32),
                        pltpu.VMEM((H, NC, 1, LANES), F32),
                        pltpu.VMEM((H, NC, C, DV), F32)],
        compiler_params=_cparams(("parallel", "arbitrary")),
        name="gdn_mixer",
    )(proj3, proj3, proj3, proj3, proj3, proj3, proj3, proj3, cw,
      a_log.reshape(H, 1, 1), dt_bias.reshape(H, 1, 1), norm_w.reshape(1, 1, DV))


def _rwkv_kernel(*refs, has_vres):
    if has_vres:
        (r_ref, k_ref, v_ref, lo_ref, rp_ref, kp_ref, vp_ref, lop_ref, vf_ref,
         mu_ref, mul_ref, vec_ref, w2_ref, a2_ref, g2_ref, v2_ref,
         y_ref, buf_ref, bufl_ref, s_ref, rr_ref, pp_ref, qq_ref, yv_ref, egc_ref, yc_ref) = refs
    else:
        (r_ref, k_ref, v_ref, lo_ref, rp_ref, kp_ref, vp_ref, lop_ref,
         mu_ref, mul_ref, vec_ref, w2_ref, a2_ref, g2_ref,
         y_ref, vf_out_ref, buf_ref, bufl_ref, s_ref, rr_ref, pp_ref, qq_ref, yv_ref, egc_ref, yc_ref) = refs
    H, D, C = RWKV_HEADS, RWKV_HEAD, CHUNK
    TB = r_ref.shape[1]
    NC = TB // C
    first = pl.program_id(1) == 0

    @pl.when(first)
    def _():
        s_ref[...] = jnp.zeros_like(s_ref)

    def mix(cur_ref, prev_ref, mu, buf):
        cur = cur_ref[0]
        (sh,) = _shifted_rows(buf, cur, prev_ref[0], first, (1,))
        return cur + mu * (sh - cur)

    r = mix(r_ref, rp_ref, mu_ref[0:1], buf_ref)
    k = mix(k_ref, kp_ref, mu_ref[1:2], buf_ref)
    v = mix(v_ref, vp_ref, mu_ref[2:3], buf_ref)
    lo = mix(lo_ref, lop_ref, mul_ref[...], bufl_ref)
    hw, ha, hg = (lo[:, j * LORA_SEG:(j + 1) * LORA_SEG] for j in range(3))
    w0, a0, k_k, k_a, r_k, ln_w, ln_b = (vec_ref[j:j + 1] for j in range(7))

    w_log = -_softplus(-(w0 + jnp.dot(jnp.tanh(hw), w2_ref[...], preferred_element_type=F32))) - 0.5
    lw = -jnp.exp(w_log)
    a = _sigmoid(a0 + jnp.dot(ha, a2_ref[...], preferred_element_type=F32))
    gate = jnp.dot(_sigmoid(hg), g2_ref[...], preferred_element_type=F32)
    if has_vres:
        hv = lo[:, 3 * LORA_SEG:4 * LORA_SEG]
        v0 = vec_ref[7:8]
        v = v + (vf_ref[0] - v) * _sigmoid(v0 + jnp.dot(hv, v2_ref[...], preferred_element_type=F32))
    else:
        vf_out_ref[0] = v

    ri = lax.broadcasted_iota(jnp.int32, (C, C), 0)
    ci = lax.broadcasted_iota(jnp.int32, (C, C), 1)
    tril = (ri >= ci).astype(F32)
    gi = jnp.concatenate(
        [jnp.dot(tril, lw[c * C:(c + 1) * C], preferred_element_type=F32, precision=HIGHEST) for c in range(NC)], axis=0)

    kk = _split_heads(k * k_k, H, D)
    kk = kk * lax.rsqrt(jnp.sum(kk * kk, axis=-1, keepdims=True) + L2_EPS)
    k = k * (1.0 + (a - 1.0) * k_a)
    r_h = _split_heads(r, H, D)
    k_h = _split_heads(k, H, D)
    v_h = _split_heads(v, H, D)
    a_h = _split_heads(a, H, D)
    gi_h = _split_heads(gi, H, D)
    lw_h = _split_heads(lw, H, D)
    bonus = jnp.sum(r_h * k_h * _split_heads(r_k, H, D), axis=-1, keepdims=True) * v_h

    N = H * NC
    rc, kc, vc, kkc, ac, gic, lwc = (t.reshape(N, C, D) for t in (r_h, k_h, v_h, kk, a_h, gi_h, lw_h))
    g_end = gic[:, C - 1:C, :]
    e_in = jnp.exp(gic)
    e_neg = jnp.exp(-gic)
    e_tail = jnp.exp(g_end - gic)
    at = -kkc * jnp.exp(gic - lwc)
    bvec = kkc * ac
    rt = rc * e_in
    bt = bvec * e_neg
    kt = kc * e_neg
    strict = ri > ci
    causal = ri >= ci
    a_ab = jnp.where(strict, _bmm_nt(at, bt), 0.0)
    a_ak = jnp.where(strict, _bmm_nt(at, kt), 0.0)
    a_rb = jnp.where(causal, _bmm_nt(rt, bt), 0.0)
    a_rk = jnp.where(causal, _bmm_nt(rt, kt), 0.0)
    tinv = _unit_lower_inverse(-a_ab)
    at2 = _bmm(tinv, at)
    uv = _bmm(tinv, _bmm(a_ak, vc))
    b_dec = bvec * e_tail
    k_dec = kc * e_tail
    rr_ref[...] = (rt + _bmm(a_rb, at2)).reshape(H, NC, C, D)
    yv_ref[...] = (_bmm(a_rb, uv) + _bmm(a_rk, vc)).reshape(H, NC, C, D)
    pp_ref[...] = _bmm_tn(at2, b_dec).reshape(H, NC, D, D)
    qq_ref[...] = (_bmm_tn(uv, b_dec) + _bmm_tn(vc, k_dec)).reshape(H, NC, D, D)
    egc_ref[...] = jnp.exp(g_end).reshape(H, NC, 1, D)

    for c in range(NC):
        S = s_ref[...]
        yc_ref[:, c] = _bmm_nt(rr_ref[:, c], S) + yv_ref[:, c]
        s_ref[...] = S * egc_ref[:, c] + _bmm(S, pp_ref[:, c]) + qq_ref[:, c]

    y = yc_ref[...].reshape(H, TB, D)
    mean = jnp.mean(y, axis=-1, keepdims=True)
    yc = y - mean
    var = jnp.mean(yc * yc, axis=-1, keepdims=True)
    y = _merge_heads(yc * lax.rsqrt(var + RWKV_LN_EPS)) * ln_w + ln_b
    y_ref[0] = (y + _merge_heads(bonus)) * gate


def rwkv_mixer(proj3, mu, mu_lora, vecs, w2, a2, g2, v_first=None, v2=None, tb=256):
    B, S, _ = proj3.shape
    H, D, C = RWKV_HEADS, RWKV_HEAD, CHUNK
    W = RWKV_DIM
    WL = 4 * LORA_SEG
    NC = tb // C
    rows8 = tb // SUBLANES
    has_vres = v_first is not None

    def cur(col, width):
        return pl.BlockSpec((1, tb, width), lambda b, i, cb=col // width: (b, i, cb))

    def prev(col, width):
        return pl.BlockSpec((1, SUBLANES, width),
                            lambda b, i, cb=col // width: (b, jnp.maximum(i * rows8 - 1, 0), cb))

    in_specs = [cur(EV_R, W), cur(EV_RK, W), cur(EV_RV, W), cur(EV_LORA, WL),
                prev(EV_R, W), prev(EV_RK, W), prev(EV_RV, W), prev(EV_LORA, WL)]
    args = [proj3] * 8
    if has_vres:
        in_specs.append(pl.BlockSpec((1, tb, W), lambda b, i: (b, i, 0)))
        args.append(v_first)
    in_specs += [_resident(mu.shape), _resident(mu_lora.shape), _resident(vecs.shape),
                 _resident(w2.shape), _resident(a2.shape), _resident(g2.shape)]
    args += [mu, mu_lora, vecs, w2, a2, g2]
    if has_vres:
        in_specs.append(_resident(v2.shape))
        args.append(v2)
    out_block = pl.BlockSpec((1, tb, W), lambda b, i: (b, i, 0))
    out_sds = jax.ShapeDtypeStruct((B, S, W), F32)
    res = pl.pallas_call(
        functools.partial(_rwkv_kernel, has_vres=has_vres),
        grid=(B, S // tb),
        in_specs=in_specs,
        out_specs=out_block if has_vres else [out_block, out_block],
        out_shape=out_sds if has_vres else [out_sds, out_sds],
        scratch_shapes=[pltpu.VMEM((tb + SUBLANES, W), F32),
                        pltpu.VMEM((tb + SUBLANES, WL), F32),
                        pltpu.VMEM((H, D, D), F32),
                        pltpu.VMEM((H, NC, C, D), F32),
                        pltpu.VMEM((H, NC, D, D), F32),
                        pltpu.VMEM((H, NC, D, D), F32),
                        pltpu.VMEM((H, NC, C, D), F32),
                        pltpu.VMEM((H, NC, 1, D), F32),
                        pltpu.VMEM((H, NC, C, D), F32)],
        compiler_params=_cparams(("parallel", "arbitrary")),
        name="rwkv_mixer",
    )(*args)
    if has_vres:
        return res, v_first
    return res[0], res[1]


def _dil_attn_kernel(q_ref, kc_ref, kp_ref, vc_ref, vp_ref, o_ref, lse_ref):
    Hh, Dh, Bk = DIL_HEADS, DIL_HEAD_DIM, ATT_BLOCK
    has_prev = pl.program_id(2) > 0
    qi = lax.broadcasted_iota(jnp.int32, (Bk, Bk), 0)
    kj = lax.broadcasted_iota(jnp.int32, (Bk, Bk), 1)
    ok_prev = (kj >= qi) & has_prev
    ok_cur = kj <= qi
    q = q_ref[0]
    kc, kp, vc, vp = kc_ref[0], kp_ref[0], vc_ref[0], vp_ref[0]
    lane = lax.broadcasted_iota(jnp.int32, (Bk, LANES), 1)
    lse_all = jnp.zeros((Bk, LANES), F32)
    outs = []
    for h in range(Hh):
        sl = slice(h * Dh, (h + 1) * Dh)
        qh = q[:, sl]
        s_p = jnp.where(ok_prev, lax.dot_general(qh, kp[:, sl], (((1,), (1,)), ((), ())), preferred_element_type=F32), -jnp.inf)
        s_c = jnp.where(ok_cur, lax.dot_general(qh, kc[:, sl], (((1,), (1,)), ((), ())), preferred_element_type=F32), -jnp.inf)
        m = jnp.maximum(jnp.max(s_p, axis=-1, keepdims=True), jnp.max(s_c, axis=-1, keepdims=True))
        p_p = jnp.exp(s_p - m)
        p_c = jnp.exp(s_c - m)
        l = jnp.sum(p_p, axis=-1, keepdims=True) + jnp.sum(p_c, axis=-1, keepdims=True)
        o = jnp.dot(p_p, vp[:, sl], preferred_element_type=F32) + jnp.dot(p_c, vc[:, sl], preferred_element_type=F32)
        outs.append(o / l)
        lse_all = jnp.where(lane == h, m + jnp.log(l), lse_all)
    o_ref[0] = jnp.concatenate(outs, axis=-1)
    lse_ref[0] = lse_all


def dil_attention_group(qkv3, group, dilation, n_colblocks):
    B, S, NCOL = qkv3.shape
    L = S // dilation
    nb = L // ATT_BLOCK
    W = DIL_WIDTH
    n_groups = len(DIL_PATTERNS)
    x = qkv3.reshape(B, L, dilation * NCOL)

    def spec(cb, prev):
        if prev:
            return pl.BlockSpec((1, ATT_BLOCK, W), lambda b, r, n: (b, jnp.maximum(n - 1, 0), r * n_colblocks + cb))
        return pl.BlockSpec((1, ATT_BLOCK, W), lambda b, r, n: (b, n, r * n_colblocks + cb))

    qb, kb, vb = group, n_groups + group, 2 * n_groups + group
    o, lse = pl.pallas_call(
        _dil_attn_kernel,
        grid=(B, dilation, nb),
        in_specs=[spec(qb, False), spec(kb, False), spec(kb, True), spec(vb, False), spec(vb, True)],
        out_specs=[pl.BlockSpec((1, ATT_BLOCK, W), lambda b, r, n: (b, n, r)),
                   pl.BlockSpec((1, ATT_BLOCK, LANES), lambda b, r, n: (b, n, r))],
        out_shape=[jax.ShapeDtypeStruct((B, L, dilation * W), F32),
                   jax.ShapeDtypeStruct((B, L, dilation * LANES), F32)],
        compiler_params=_cparams(("parallel", "parallel", "arbitrary")),
        name=f"dil_attn_d{dilation}",
    )(x, x, x, x, x)
    return o.reshape(B, S, W), lse.reshape(B, S, LANES)


def _out_proj_even_kernel(x_ref, ya_ref, yb_ref, wa_ref, wb_ref, o_ref):
    o_ref[...] = (x_ref[...]
                  + jnp.dot(ya_ref[...].astype(BF16), wa_ref[...], preferred_element_type=F32)
                  + jnp.dot(yb_ref[...].astype(BF16), wb_ref[...], preferred_element_type=F32))


def out_proj_even(x, ya, yb, wa, wb, tm=512):
    T, D = x.shape
    return pl.pallas_call(
        _out_proj_even_kernel,
        grid=(T // tm,),
        in_specs=[pl.BlockSpec((tm, D), lambda i: (i, 0)),
                  pl.BlockSpec((tm, ya.shape[1]), lambda i: (i, 0)),
                  pl.BlockSpec((tm, yb.shape[1]), lambda i: (i, 0)),
                  _resident(wa.shape), _resident(wb.shape)],
        out_specs=pl.BlockSpec((tm, D), lambda i: (i, 0)),
        out_shape=jax.ShapeDtypeStruct((T, D), F32),
        compiler_params=_cparams(("parallel",)),
        name="out_proj_even",
    )(x, ya, yb, wa, wb)


def _out_proj_odd_kernel(x_ref, o0_ref, o1_ref, o2_ref, l0_ref, l1_ref, l2_ref, e_ref, w_ref, out_ref):
    l0, l1, l2 = l0_ref[...], l1_ref[...], l2_ref[...]
    m = jnp.maximum(jnp.maximum(l0, l1), l2)
    e0, e1, e2 = jnp.exp(l0 - m), jnp.exp(l1 - m), jnp.exp(l2 - m)
    inv = 1.0 / (e0 + e1 + e2)
    expand = e_ref[...]

    def wide(wt):
        return jnp.dot(wt, expand, preferred_element_type=F32, precision=HIGHEST)

    mixed = wide(e0 * inv) * o0_ref[...] + wide(e1 * inv) * o1_ref[...] + wide(e2 * inv) * o2_ref[...]
    out_ref[...] = x_ref[...] + jnp.dot(mixed.astype(BF16), w_ref[...], preferred_element_type=F32)


def out_proj_odd(x, outs, lses, w, tm=512):
    T, D = x.shape
    W = DIL_WIDTH
    expand = np.zeros((LANES, W), np.float32)
    for h in range(DIL_HEADS):
        expand[h, h * DIL_HEAD_DIM:(h + 1) * DIL_HEAD_DIM] = 1.0
    row = lambda width: pl.BlockSpec((tm, width), lambda i: (i, 0))
    return pl.pallas_call(
        _out_proj_odd_kernel,
        grid=(T // tm,),
        in_specs=[row(D), row(W), row(W), row(W), row(LANES), row(LANES), row(LANES),
                  _resident((LANES, W)), _resident(w.shape)],
        out_specs=row(D),
        out_shape=jax.ShapeDtypeStruct((T, D), F32),
        compiler_params=_cparams(("parallel",)),
        name="out_proj_odd",
    )(x, *outs, *lses, jnp.asarray(expand), w)


def _moe_kernel(x_ref, g_ref, wr_ref, w1_ref, w3_ref, w2_ref, fg_ref, o_ref, h_ref, gate_ref, acc_ref, *, final_norm):
    e = pl.program_id(1)
    lane = lax.broadcasted_iota(jnp.int32, gate_ref.shape, 1)
    NG, EPG = N_GROUPS, EXPERTS_PER_GROUP

    @pl.when(e == 0)
    def _():
        h = _rmsnorm(x_ref[...], g_ref[...])
        h_ref[...] = h.astype(BF16)
        logits = jnp.dot(h, wr_ref[...], preferred_element_type=F32, precision=HIGHEST)
        big = jnp.int32(1 << 30)
        gl = jnp.where(lane < NG, logits, -jnp.inf)
        gmax = jnp.max(gl, axis=-1, keepdims=True)
        gsel = jnp.min(jnp.where(gl == gmax, lane, big), axis=-1, keepdims=True)
        p_group = 1.0 / jnp.sum(jnp.where(lane < NG, jnp.exp(logits - gmax), 0.0), axis=-1, keepdims=True)
        lo = NG + gsel * EPG
        el = jnp.where((lane >= lo) & (lane < lo + EPG), logits, -jnp.inf)
        v1 = jnp.max(el, axis=-1, keepdims=True)
        i1 = jnp.min(jnp.where(el == v1, lane, big), axis=-1, keepdims=True)
        el2 = jnp.where(lane == i1, -jnp.inf, el)
        v2 = jnp.max(el2, axis=-1, keepdims=True)
        i2 = jnp.min(jnp.where(el2 == v2, lane, big), axis=-1, keepdims=True)
        t = jnp.exp(v2 - v1)
        w_top1 = p_group / (1.0 + t)
        w_top2 = p_group * t / (1.0 + t)
        gate_ref[...] = jnp.where(lane == i1, w_top1, 0.0) + jnp.where(lane == i2, w_top2, 0.0)
        acc_ref[...] = jnp.zeros_like(acc_ref)

    hb = h_ref[...]
    a = jnp.dot(hb, w1_ref[0], preferred_element_type=F32)
    b = jnp.dot(hb, w3_ref[0], preferred_element_type=F32)
    gate = jnp.sum(jnp.where(lane == NG + e, gate_ref[...], 0.0), axis=-1, keepdims=True)
    hh = (a * _sigmoid(a)) * b * gate
    acc_ref[...] += jnp.dot(hh.astype(BF16), w2_ref[0], preferred_element_type=F32)

    @pl.when(e == pl.num_programs(1) - 1)
    def _():
        y = x_ref[...] + acc_ref[...]
        if final_norm:
            y = _rmsnorm(y, fg_ref[...])
        o_ref[...] = y


def moe_block(x, gain, w_router, w1, w3, w2, final_gain=None, tm=1024):
    T, D = x.shape
    NE, _, FF = w1.shape
    final_norm = final_gain is not None
    fg = (final_gain if final_norm else jnp.ones((D,), F32)).reshape(1, D)
    return pl.pallas_call(
        functools.partial(_moe_kernel, final_norm=final_norm),
        grid=(T // tm, NE),
        in_specs=[pl.BlockSpec((tm, D), lambda i, e: (i, 0)),
                  _resident((1, D)), _resident((D, LANES)),
                  pl.BlockSpec((1, D, FF), lambda i, e: (e, 0, 0)),
                  pl.BlockSpec((1, D, FF), lambda i, e: (e, 0, 0)),
                  pl.BlockSpec((1, FF, D), lambda i, e: (e, 0, 0)),
                  _resident((1, D))],
        out_specs=pl.BlockSpec((tm, D), lambda i, e: (i, 0)),
        out_shape=jax.ShapeDtypeStruct((T, D), F32),
        scratch_shapes=[pltpu.VMEM((tm, D), BF16), pltpu.VMEM((tm, LANES), F32), pltpu.VMEM((tm, D), F32)],
        compiler_params=_cparams(("parallel", "arbitrary")),
        name="moe_block",
    )(x, gain.reshape(1, D), w_router, w1, w3, w2, fg)


def _place(cols, total):
    lead = cols[0][1].shape[:-1]
    out = jnp.zeros(lead + (total,), F32)
    for off, arr in cols:
        out = out.at[..., off:off + arr.shape[-1]].set(arr)
    return out


def _even_layout(t, vres=None):
    gq, gk, gv, gz, ga, gb, rr, rk, rv, lw, la, lg = jnp.split(
        t, np.cumsum([512, 512, 512, 512, 4, 4, 512, 512, 512, 32, 32, 96])[:-1].tolist(), axis=-1)
    cols = [(EV_Q, gq), (EV_K, gk), (EV_V, gv), (EV_Z, gz), (EV_AB, ga), (EV_AB + GDN_HEADS, gb),
            (EV_R, rr), (EV_RK, rk), (EV_RV, rv),
            (EV_LORA, lw), (EV_LORA + LORA_SEG, la), (EV_LORA + 2 * LORA_SEG, lg)]
    if vres is not None:
        cols.append((EV_LORA + 3 * LORA_SEG, vres))
    return _place(cols, EV_NPAD)


def _pad_rows(w, rows):
    return jnp.zeros((rows, w.shape[1]), F32).at[:w.shape[0]].set(w)


def kernel(x, positions, ev_norm, ev_w_in, rwkv_vres_down, ev_w_out, gdn_conv_w, gdn_A_log, gdn_dt_bias, gdn_norm,
           rwkv_mu, rwkv_w0, rwkv_w2, rwkv_a0, rwkv_a2, rwkv_g2, rwkv_k_k, rwkv_k_a, rwkv_r_k, rwkv_ln_w, rwkv_ln_b,
           rwkv_vres_mu, rwkv_v0, rwkv_v2, od_norm, od_w_in, od_w_out, ffn_norm, moe_w_group, moe_w_expert,
           moe_w1, moe_w3, moe_w2, final_norm):
    B, S, D = x.shape
    T = B * S
    depth = ffn_norm.shape[0]
    xf = x.reshape(T, D)
    cos, sin = rope_tables(positions.reshape(T))
    n_qk = 2 * len(DIL_PATTERNS) * DIL_WIDTH
    v_first = None
    for layer in range(depth):
        i = layer // 2
        if layer % 2 == 0:
            vres_w = None if i == 0 else rwkv_vres_down[i - 1]
            w_in = _even_layout(ev_w_in[i], vres_w).astype(BF16)
            proj3 = norm_proj(xf, ev_norm[i], w_in).reshape(B, S, EV_NPAD)
            ya = gdn_mixer(proj3, gdn_conv_w[i], gdn_A_log[i], gdn_dt_bias[i], gdn_norm[i])
            mu_r, mu_k, mu_v, mu_w, mu_a, mu_g = jnp.split(
                rwkv_mu[i], np.cumsum([RWKV_DIM, RWKV_DIM, RWKV_DIM, 32, 32])[:].tolist())
            mu = jnp.stack([mu_r, mu_k, mu_v])
            lora_cols = [(0, mu_w), (LORA_SEG, mu_a), (2 * LORA_SEG, mu_g)]
            if i > 0:
                lora_cols.append((3 * LORA_SEG, rwkv_vres_mu[i - 1]))
            mu_lora = _place(lora_cols, 4 * LORA_SEG).reshape(1, 4 * LORA_SEG)
            vec_rows = [rwkv_w0[i], rwkv_a0[i], rwkv_k_k[i], rwkv_k_a[i], rwkv_r_k[i].reshape(-1), rwkv_ln_w[i], rwkv_ln_b[i]]
            if i > 0:
                vec_rows.append(rwkv_v0[i - 1])
            vecs = jnp.stack(vec_rows)
            w2 = _pad_rows(rwkv_w2[i], LORA_SEG)
            a2 = _pad_rows(rwkv_a2[i], LORA_SEG)
            g2 = _pad_rows(rwkv_g2[i], LORA_SEG)
            if i == 0:
                yb, v_first = rwkv_mixer(proj3, mu, mu_lora, vecs, w2, a2, g2)
            else:
                yb, _ = rwkv_mixer(proj3, mu, mu_lora, vecs, w2, a2, g2, v_first, _pad_rows(rwkv_v2[i - 1], LORA_SEG))
            w_out = ev_w_out[i].astype(BF16)
            na = GDN_HEADS * GDN_DV
            xf = out_proj_even(xf, ya.reshape(T, na), yb.reshape(T, RWKV_DIM), w_out[:na], w_out[na:])
        else:
            w_in = od_w_in[i].astype(BF16)
            qkv3 = norm_proj_rope(xf, od_norm[i], w_in, cos, sin, n_qk, n_qk // 2, DIL_HEAD_DIM ** -0.5).reshape(B, S, -1)
            outs, lses = [], []
            for gi, (window, dilation) in enumerate(DIL_PATTERNS):
                assert window // dilation == ATT_BLOCK and (S // dilation) % ATT_BLOCK == 0
                o, lse = dil_attention_group(qkv3, gi, dilation, qkv3.shape[-1] // DIL_WIDTH)
                outs.append(o.reshape(T, DIL_WIDTH))
                lses.append(lse.reshape(T, LANES))
            xf = out_proj_odd(xf, outs, lses, od_w_out[i].astype(BF16))
        w_router = _place([(0, moe_w_group[layer])]
                          + [(N_GROUPS + g * EXPERTS_PER_GROUP, moe_w_expert[layer, g]) for g in range(N_GROUPS)], LANES)
        xf = moe_block(xf, ffn_norm[layer], w_router, moe_w1[layer].astype(BF16), moe_w3[layer].astype(BF16),
                       moe_w2[layer].astype(BF16), final_norm if layer == depth - 1 else None)
    return xf.reshape(B, S, D)
```

```python
import functools
import math

import jax
import jax.numpy as jnp
import numpy as np
from jax import lax
from jax.experimental import pallas as pl
from jax.experimental.pallas import tpu as pltpu

F32 = jnp.float32
BF16 = jnp.bfloat16
HIGHEST = lax.Precision.HIGHEST

RMS_EPS = 1e-6
L2_EPS = 1e-6
RWKV_LN_EPS = 64e-5
ROPE_THETA = 10000.0

LANES = 128
SUBLANES = 8
VMEM_LIMIT = 56 * 1024 * 1024

GDN_HEADS = 4
GDN_DK = 128
GDN_DV = 128
GDN_CONV = 4
RWKV_HEADS = 8
RWKV_HEAD = 64
RWKV_DIM = RWKV_HEADS * RWKV_HEAD
LORA_SEG = LANES
DIL_PATTERNS = ((128, 1), (512, 4), (2048, 16))
DIL_HEADS = 8
DIL_HEAD_DIM = 64
DIL_WIDTH = DIL_HEADS * DIL_HEAD_DIM
ATT_BLOCK = 128
N_GROUPS = 4
EXPERTS_PER_GROUP = 4
N_EXPERTS = N_GROUPS * EXPERTS_PER_GROUP
CHUNK = 64

EV_Q, EV_K, EV_V, EV_Z = 0, 512, 1024, 1536
EV_R, EV_RK, EV_RV, EV_LORA = 2048, 2560, 3072, 3584
EV_AB = EV_LORA + 4 * LORA_SEG
EV_NPAD = EV_AB + LANES


def _cparams(sem):
    return pltpu.CompilerParams(dimension_semantics=sem, vmem_limit_bytes=VMEM_LIMIT)


def _sigmoid(x):
    return 1.0 / (1.0 + jnp.exp(-x))


def _softplus(x):
    return jnp.maximum(x, 0.0) + jnp.log(1.0 + jnp.exp(-jnp.abs(x)))


def _rmsnorm(x, gain):
    return x * lax.rsqrt(jnp.mean(x * x, axis=-1, keepdims=True) + RMS_EPS) * gain


def _resident(shape):
    nd = len(shape)
    return pl.BlockSpec(shape, lambda *_: (0,) * nd, pipeline_mode=pl.Buffered(1))


def _rope_table_kernel(pos_ref, freq_ref, sign_ref, cos_ref, sin_ref):
    ang = pos_ref[...].astype(F32) * freq_ref[...]
    cos_ref[...] = jnp.cos(ang)
    sin_ref[...] = jnp.sin(ang) * sign_ref[...]


def rope_tables(positions_flat, tm=1024):
    T = positions_flat.shape[0]
    half = DIL_HEAD_DIM // 2
    inv_freq = ROPE_THETA ** (-jnp.arange(half, dtype=F32) * 2.0 / DIL_HEAD_DIM)
    lane = np.arange(LANES)
    freq = inv_freq[lane % half][None, :]
    sign = jnp.asarray(np.where(lane % DIL_HEAD_DIM < half, -1.0, 1.0), F32)[None, :]
    return pl.pallas_call(
        _rope_table_kernel,
        grid=(T // tm,),
        in_specs=[pl.BlockSpec((tm, 1), lambda i: (i, 0)), _resident((1, LANES)), _resident((1, LANES))],
        out_specs=[pl.BlockSpec((tm, LANES), lambda i: (i, 0))] * 2,
        out_shape=[jax.ShapeDtypeStruct((T, LANES), F32)] * 2,
        compiler_params=_cparams(("parallel",)),
        name="rope_tables",
    )(positions_flat.reshape(T, 1), freq, sign)


def _norm_proj_kernel(x_ref, g_ref, w_ref, o_ref):
    h = _rmsnorm(x_ref[...], g_ref[...]).astype(BF16)
    o_ref[...] = jnp.dot(h, w_ref[...], preferred_element_type=F32)


def norm_proj(x, gain, w, tm=256):
    T, D = x.shape
    N = w.shape[1]
    return pl.pallas_call(
        _norm_proj_kernel,
        grid=(T // tm,),
        in_specs=[pl.BlockSpec((tm, D), lambda i: (i, 0)), _resident((1, D)), _resident((D, N))],
        out_specs=pl.BlockSpec((tm, N), lambda i: (i, 0)),
        out_shape=jax.ShapeDtypeStruct((T, N), F32),
        compiler_params=_cparams(("parallel",)),
        name="norm_proj",
    )(x, gain.reshape(1, D), w)


def _deinterleave_store(out_ref, val, d, a_ref, b_ref):
    tm = val.shape[0]
    if d == 1:
        out_ref[0, 0] = val.astype(out_ref.dtype)
        return
    quarter = tm // 4
    for j in range(val.shape[1] // LANES):
        cs = slice(j * LANES, (j + 1) * LANES)
        a_ref[j] = val[:, cs]
        if d == 4:
            for r in range(4):
                out_ref[0, r, :, cs] = a_ref[j, pl.ds(r, quarter, stride=4), :].astype(out_ref.dtype)
            continue
        assert d == 16
        for r1 in range(4):
            b_ref[j, r1 * quarter:(r1 + 1) * quarter, :] = a_ref[j, pl.ds(r1, quarter, stride=4), :]
        for r1 in range(4):
            for r2 in range(4):
                out_ref[0, r1 + 4 * r2, :, cs] = (
                    b_ref[j, pl.ds(r1 * quarter + r2, quarter // 4, stride=4), :].astype(out_ref.dtype))


def _interleave_load(in_ref, d, a_ref, b_ref):
    if d == 1:
        return in_ref[0, 0].astype(F32)
    n_tiles, tm, _ = a_ref.shape
    quarter = tm // 4
    for j in range(n_tiles):
        cs = slice(j * LANES, (j + 1) * LANES)
        if d == 4:
            for r in range(4):
                a_ref[j, pl.ds(r, quarter, stride=4), :] = in_ref[0, r, :, cs].astype(F32)
            continue
        assert d == 16
        for r1 in range(4):
            for r2 in range(4):
                b_ref[j, pl.ds(r1 * quarter + r2, quarter // 4, stride=4), :] = in_ref[0, r1 + 4 * r2, :, cs].astype(F32)
        for r1 in range(4):
            a_ref[j, pl.ds(r1, quarter, stride=4), :] = b_ref[j, r1 * quarter:(r1 + 1) * quarter, :]
    return jnp.concatenate([a_ref[j] for j in range(n_tiles)], axis=-1)


def _norm_proj_rope_kernel(x_ref, g_ref, w_ref, cos_ref, sin_ref, *rest, q_scale):
    out_refs, (a_ref, b_ref) = rest[:-2], rest[-2:]
    n_groups = len(DIL_PATTERNS)
    h = _rmsnorm(x_ref[0], g_ref[...]).astype(BF16)
    cos = cos_ref[0]
    sin = sin_ref[0]
    lane = lax.broadcasted_iota(jnp.int32, cos.shape, 1)
    first_half = (lane % DIL_HEAD_DIM) < (DIL_HEAD_DIM // 2)
    W = DIL_WIDTH
    for c, out_ref in enumerate(out_refs):
        which, g = divmod(c, n_groups)
        acc = jnp.dot(h, w_ref[:, c * W:(c + 1) * W], preferred_element_type=F32)
        if which < 2:
            parts = []
            for j in range(W // LANES):
                blk = acc[:, j * LANES:(j + 1) * LANES]
                rot = jnp.where(first_half, pltpu.roll(blk, LANES - 32, axis=1), pltpu.roll(blk, 32, axis=1))
                out = blk * cos + rot * sin
                parts.append(out * q_scale if which == 0 else out)
            acc = jnp.concatenate(parts, axis=-1)
        _deinterleave_store(out_ref, acc, DIL_PATTERNS[g][1], a_ref, b_ref)


def norm_proj_rope(x3, gain, w, cos3, sin3, q_scale, tm=512):
    B, S, D = x3.shape
    W = DIL_WIDTH
    out_specs, out_shape = [], []
    for _ in range(3):
        for _, d in DIL_PATTERNS:
            out_specs.append(pl.BlockSpec((1, d, tm // d, W), lambda b, i: (b, 0, i, 0)))
            out_shape.append(jax.ShapeDtypeStruct((B, d, S // d, W), BF16))
    return pl.pallas_call(
        functools.partial(_norm_proj_rope_kernel, q_scale=q_scale),
        grid=(B, S // tm),
        in_specs=[pl.BlockSpec((1, tm, D), lambda b, i: (b, i, 0)), _resident((1, D)), _resident(w.shape),
                  pl.BlockSpec((1, tm, LANES), lambda b, i: (b, i, 0)),
                  pl.BlockSpec((1, tm, LANES), lambda b, i: (b, i, 0))],
        out_specs=out_specs,
        out_shape=out_shape,
        scratch_shapes=[pltpu.VMEM((W // LANES, tm, LANES), F32), pltpu.VMEM((W // LANES, tm, LANES), F32)],
        compiler_params=_cparams(("parallel", "parallel")),
        name="norm_proj_rope",
    )(x3, gain.reshape(1, D), w, cos3, sin3)


def _bmm(a, b):
    return jnp.einsum('nik,nkj->nij', a.astype(BF16), b.astype(BF16), preferred_element_type=F32)


def _bmm_nt(a, b):
    return jnp.einsum('nik,njk->nij', a.astype(BF16), b.astype(BF16), preferred_element_type=F32)


def _bmm_tn(a, b):
    return jnp.einsum('nci,ncj->nij', a.astype(BF16), b.astype(BF16), preferred_element_type=F32)


def _unit_lower_inverse(L):
    C = L.shape[-1]
    ri = lax.broadcasted_iota(jnp.int32, (C, C), 0)
    ci = lax.broadcasted_iota(jnp.int32, (C, C), 1)
    same16 = (ri // 16) == (ci // 16)
    same32 = (ri // 32) == (ci // 32)
    eye = (ri == ci).astype(F32)
    Ld = jnp.where(same16, L, 0.0)
    X = eye - Ld
    P = Ld
    for _ in range(3):
        P = _bmm(P, P)
        X = X + _bmm(X, P)
    for off in (jnp.where(same32 & ~same16, L, 0.0), jnp.where(~same32, L, 0.0)):
        X = X - _bmm(_bmm(X, off), X)
    return X


def _shifted_rows(buf_ref, cur, prev, first, shifts):
    TB = cur.shape[0]
    buf_ref[0:SUBLANES, :] = jnp.where(first, 0.0, prev)
    buf_ref[SUBLANES:, :] = cur
    return [buf_ref[SUBLANES - s:SUBLANES - s + TB, :] for s in shifts]


def _split_heads(x, n_heads, width):
    return jnp.stack([x[:, h * width:(h + 1) * width] for h in range(n_heads)], axis=0)


def _merge_heads(x):
    return jnp.concatenate([x[h] for h in range(x.shape[0])], axis=-1)


def _gdn_kernel(q_ref, k_ref, v_ref, z_ref, ab_ref, qp_ref, kp_ref, vp_ref, cw_ref, alog_ref, dtb_ref, nw_ref,
                o_ref, buf_ref, s_ref, qd_ref, pg_ref, qg_ref, ov_ref, egl_ref, oc_ref):
    H, DK, DV, C = GDN_HEADS, GDN_DK, GDN_DV, CHUNK
    TB = q_ref.shape[1]
    NC = TB // C
    first = pl.program_id(1) == 0

    @pl.when(first)
    def _():
        s_ref[...] = jnp.zeros_like(s_ref)

    def conv_silu(cur_ref, prev_ref, j):
        cur = cur_ref[0]
        x3, x2, x1 = _shifted_rows(buf_ref, cur, prev_ref[0], first, (3, 2, 1))
        w = cw_ref[j]
        y = x3 * w[0:1] + x2 * w[1:2] + x1 * w[2:3] + cur * w[3:4]
        return y * _sigmoid(y)

    q = _split_heads(conv_silu(q_ref, qp_ref, 0), H, DK)
    k = _split_heads(conv_silu(k_ref, kp_ref, 1), H, DK)
    v = _split_heads(conv_silu(v_ref, vp_ref, 2), H, DV)
    q = q * lax.rsqrt(jnp.sum(q * q, axis=-1, keepdims=True) + L2_EPS) * (DK ** -0.5)
    k = k * lax.rsqrt(jnp.sum(k * k, axis=-1, keepdims=True) + L2_EPS)
    ab = ab_ref[0]
    a = jnp.stack([ab[:, h:h + 1] for h in range(H)], axis=0)
    b = jnp.stack([ab[:, H + h:H + h + 1] for h in range(H)], axis=0)
    beta = _sigmoid(b)
    g = -jnp.exp(alog_ref[...]) * _softplus(a + dtb_ref[...])

    N = H * NC
    q = q.reshape(N, C, DK)
    k = k.reshape(N, C, DK)
    v = v.reshape(N, C, DV)
    beta = beta.reshape(N, C, 1)
    g = g.reshape(N, C, 1)

    ri = lax.broadcasted_iota(jnp.int32, (C, C), 0)
    ci = lax.broadcasted_iota(jnp.int32, (C, C), 1)
    causal = ri >= ci
    strict = ri > ci
    gb = jnp.broadcast_to(g, (N, C, C))
    g_row = jnp.sum(jnp.where(ri == ci, gb, 0.0), axis=1, keepdims=True)
    gc_col = jnp.sum(jnp.where(causal, jnp.broadcast_to(g_row, (N, C, C)), 0.0), axis=2, keepdims=True)
    gc_row = jnp.sum(jnp.where(ri <= ci, gb, 0.0), axis=1, keepdims=True)
    decay = jnp.where(causal, jnp.exp(jnp.where(causal, gc_col - gc_row, 0.0)), 0.0)

    kb = k * beta
    vb = v * beta
    kq = _bmm_nt(jnp.concatenate([kb, q], axis=1), k)
    L = jnp.where(strict, kq[:, :C] * decay, 0.0)
    aqk = kq[:, C:] * decay
    tinv = _unit_lower_inverse(L)
    egc = jnp.exp(gc_col)
    uw = _bmm(tinv, jnp.concatenate([vb, kb * egc], axis=2))
    auw = _bmm(aqk, uw)
    g_last = gc_col[:, C - 1:C, :]
    k_dec = k * jnp.exp(g_last - gc_col)
    kuw = _bmm_tn(k_dec, uw)
    qd_ref[...] = (q * egc - auw[:, :, DV:]).reshape(H, NC, C, DK)
    ov_ref[...] = auw[:, :, :DV].reshape(H, NC, C, DV)
    pg_ref[...] = kuw[:, :, DV:].reshape(H, NC, DK, DK)
    qg_ref[...] = kuw[:, :, :DV].reshape(H, NC, DK, DV)
    egl_ref[...] = jnp.broadcast_to(jnp.exp(g_last), (N, 1, LANES)).reshape(H, NC, 1, LANES)

    for c in range(NC):
        S = s_ref[...]
        oc_ref[:, c] = _bmm(qd_ref[:, c], S) + ov_ref[:, c]
        s_ref[...] = S * egl_ref[:, c] - _bmm(pg_ref[:, c], S) + qg_ref[:, c]

    o = oc_ref[...].reshape(H, TB, DV)
    o = _rmsnorm(o, nw_ref[...])
    z = _split_heads(z_ref[0], H, DV)
    o_ref[0] = _merge_heads(o * (z * _sigmoid(z)))


def gdn_mixer(proj3, conv_w, a_log, dt_bias, norm_w, tb=512):
    B, S, _ = proj3.shape
    H, DK, DV, C = GDN_HEADS, GDN_DK, GDN_DV, CHUNK
    W = H * DK
    NC = tb // C
    nq, nk, nv, nz, nab = EV_Q // W, EV_K // W, EV_V // W, EV_Z // W, EV_AB // LANES
    rows8 = tb // SUBLANES

    def cur(cb):
        return pl.BlockSpec((1, tb, W), lambda b, i, cb=cb: (b, i, cb))

    def prev(cb):
        return pl.BlockSpec((1, SUBLANES, W), lambda b, i, cb=cb: (b, jnp.maximum(i * rows8 - 1, 0), cb))

    cw = conv_w.reshape(GDN_CONV, 3, W).transpose(1, 0, 2)
    return pl.pallas_call(
        _gdn_kernel,
        grid=(B, S // tb),
        in_specs=[cur(nq), cur(nk), cur(nv), cur(nz),
                  pl.BlockSpec((1, tb, LANES), lambda b, i: (b, i, nab)),
                  prev(nq), prev(nk), prev(nv),
                  _resident((3, GDN_CONV, W)), _resident((H, 1, 1)), _resident((H, 1, 1)), _resident((1, 1, DV))],
        out_specs=pl.BlockSpec((1, tb, H * DV), lambda b, i: (b, i, 0)),
        out_shape=jax.ShapeDtypeStruct((B, S, H * DV), F32),
        scratch_shapes=[pltpu.VMEM((tb + SUBLANES, W), F32),
                        pltpu.VMEM((H, DK, DV), F32),
                        pltpu.VMEM((H, NC, C, DK), F32),
                        pltpu.VMEM((H, NC, DK, DK), F32),
                        pltpu.VMEM((H, NC, DK, DV), F32),
                        pltpu.VMEM((H, NC, C, DV), F32),
                        pltpu.VMEM((H, NC, 1, LANES), F32),
                        pltpu.VMEM((H, NC, C, DV), F32)],
        compiler_params=_cparams(("parallel", "arbitrary")),
        name="gdn_mixer",
    )(proj3, proj3, proj3, proj3, proj3, proj3, proj3, proj3, cw,
      a_log.reshape(H, 1, 1), dt_bias.reshape(H, 1, 1), norm_w.reshape(1, 1, DV))


def _rwkv_kernel(*refs, has_vres):
    if has_vres:
        (r_ref, k_ref, v_ref, lo_ref, rp_ref, kp_ref, vp_ref, lop_ref, vf_ref,
         mu_ref, mul_ref, vec_ref, w2_ref, a2_ref, g2_ref, v2_ref,
         y_ref, buf_ref, bufl_ref, s_ref, rr_ref, pp_ref, qq_ref, yv_ref, egc_ref, yc_ref) = refs
    else:
        (r_ref, k_ref, v_ref, lo_ref, rp_ref, kp_ref, vp_ref, lop_ref,
         mu_ref, mul_ref, vec_ref, w2_ref, a2_ref, g2_ref,
         y_ref, vf_out_ref, buf_ref, bufl_ref, s_ref, rr_ref, pp_ref, qq_ref, yv_ref, egc_ref, yc_ref) = refs
    H, D, C = RWKV_HEADS, RWKV_HEAD, CHUNK
    TB = r_ref.shape[1]
    NC = TB // C
    first = pl.program_id(1) == 0

    @pl.when(first)
    def _():
        s_ref[...] = jnp.zeros_like(s_ref)

    def mix(cur_ref, prev_ref, mu, buf):
        cur = cur_ref[0]
        (sh,) = _shifted_rows(buf, cur, prev_ref[0], first, (1,))
        return cur + mu * (sh - cur)

    r = mix(r_ref, rp_ref, mu_ref[0:1], buf_ref)
    k = mix(k_ref, kp_ref, mu_ref[1:2], buf_ref)
    v = mix(v_ref, vp_ref, mu_ref[2:3], buf_ref)
    lo = mix(lo_ref, lop_ref, mul_ref[...], bufl_ref)
    hw, ha, hg = (lo[:, j * LORA_SEG:(j + 1) * LORA_SEG] for j in range(3))
    w0, a0, k_k, k_a, r_k, ln_w, ln_b = (vec_ref[j:j + 1] for j in range(7))

    w_log = -_softplus(-(w0 + jnp.dot(jnp.tanh(hw), w2_ref[...], preferred_element_type=F32))) - 0.5
    lw = -jnp.exp(w_log)
    a = _sigmoid(a0 + jnp.dot(ha, a2_ref[...], preferred_element_type=F32))
    gate = jnp.dot(_sigmoid(hg), g2_ref[...], preferred_element_type=F32)
    if has_vres:
        hv = lo[:, 3 * LORA_SEG:4 * LORA_SEG]
        v0 = vec_ref[7:8]
        v = v + (vf_ref[0] - v) * _sigmoid(v0 + jnp.dot(hv, v2_ref[...], preferred_element_type=F32))
    else:
        vf_out_ref[0] = v

    ri = lax.broadcasted_iota(jnp.int32, (C, C), 0)
    ci = lax.broadcasted_iota(jnp.int32, (C, C), 1)
    tril = (ri >= ci).astype(F32)
    gi = jnp.concatenate(
        [jnp.dot(tril, lw[c * C:(c + 1) * C], preferred_element_type=F32, precision=HIGHEST) for c in range(NC)], axis=0)

    kk = _split_heads(k * k_k, H, D)
    kk = kk * lax.rsqrt(jnp.sum(kk * kk, axis=-1, keepdims=True) + L2_EPS)
    k = k * (1.0 + (a - 1.0) * k_a)
    r_h = _split_heads(r, H, D)
    k_h = _split_heads(k, H, D)
    v_h = _split_heads(v, H, D)
    a_h = _split_heads(a, H, D)
    gi_h = _split_heads(gi, H, D)
    lw_h = _split_heads(lw, H, D)
    bonus = jnp.sum(r_h * k_h * _split_heads(r_k, H, D), axis=-1, keepdims=True) * v_h

    N = H * NC
    rc, kc, vc, kkc, ac, gic, lwc = (t.reshape(N, C, D) for t in (r_h, k_h, v_h, kk, a_h, gi_h, lw_h))
    g_end = gic[:, C - 1:C, :]
    e_in = jnp.exp(gic)
    e_neg = jnp.exp(-gic)
    e_tail = jnp.exp(g_end - gic)
    at = -kkc * jnp.exp(gic - lwc)
    bvec = kkc * ac
    rt = rc * e_in
    bt = bvec * e_neg
    kt = kc * e_neg
    strict = ri > ci
    causal = ri >= ci
    amat = _bmm_nt(jnp.concatenate([at, rt], axis=1), jnp.concatenate([bt, kt], axis=1))
    a_ab = jnp.where(strict, amat[:, :C, :C], 0.0)
    a_ak = jnp.where(strict, amat[:, :C, C:], 0.0)
    a_r = jnp.concatenate([jnp.where(causal, amat[:, C:, :C], 0.0), jnp.where(causal, amat[:, C:, C:], 0.0)], axis=2)
    tinv = _unit_lower_inverse(-a_ab)
    x2 = _bmm(tinv, jnp.concatenate([at, _bmm(a_ak, vc)], axis=2))
    xv = jnp.concatenate([x2, jnp.concatenate([jnp.zeros_like(vc), vc], axis=2)], axis=1)
    y2 = _bmm(a_r, xv)
    pq = _bmm_tn(xv, jnp.concatenate([bvec * e_tail, kc * e_tail], axis=1))
    rr_ref[...] = (rt + y2[:, :, :D]).reshape(H, NC, C, D)
    yv_ref[...] = y2[:, :, D:].reshape(H, NC, C, D)
    pp_ref[...] = pq[:, :D].reshape(H, NC, D, D)
    qq_ref[...] = pq[:, D:].reshape(H, NC, D, D)
    egc_ref[...] = jnp.exp(g_end).reshape(H, NC, 1, D)

    for c in range(NC):
        S = s_ref[...]
        yc_ref[:, c] = _bmm_nt(rr_ref[:, c], S) + yv_ref[:, c]
        s_ref[...] = S * egc_ref[:, c] + _bmm(S, pp_ref[:, c]) + qq_ref[:, c]

    y = yc_ref[...].reshape(H, TB, D)
    mean = jnp.mean(y, axis=-1, keepdims=True)
    yc = y - mean
    var = jnp.mean(yc * yc, axis=-1, keepdims=True)
    y = _merge_heads(yc * lax.rsqrt(var + RWKV_LN_EPS)) * ln_w + ln_b
    y_ref[0] = (y + _merge_heads(bonus)) * gate


def rwkv_mixer(proj3, mu, mu_lora, vecs, w2, a2, g2, v_first=None, v2=None, tb=256):
    B, S, _ = proj3.shape
    H, D, C = RWKV_HEADS, RWKV_HEAD, CHUNK
    W = RWKV_DIM
    WL = 4 * LORA_SEG
    NC = tb // C
    rows8 = tb // SUBLANES
    has_vres = v_first is not None

    def cur(col, width):
        return pl.BlockSpec((1, tb, width), lambda b, i, cb=col // width: (b, i, cb))

    def prev(col, width):
        return pl.BlockSpec((1, SUBLANES, width),
                            lambda b, i, cb=col // width: (b, jnp.maximum(i * rows8 - 1, 0), cb))

    in_specs = [cur(EV_R, W), cur(EV_RK, W), cur(EV_RV, W), cur(EV_LORA, WL),
                prev(EV_R, W), prev(EV_RK, W), prev(EV_RV, W), prev(EV_LORA, WL)]
    args = [proj3] * 8
    if has_vres:
        in_specs.append(pl.BlockSpec((1, tb, W), lambda b, i: (b, i, 0)))
        args.append(v_first)
    in_specs += [_resident(mu.shape), _resident(mu_lora.shape), _resident(vecs.shape),
                 _resident(w2.shape), _resident(a2.shape), _resident(g2.shape)]
    args += [mu, mu_lora, vecs, w2, a2, g2]
    if has_vres:
        in_specs.append(_resident(v2.shape))
        args.append(v2)
    out_block = pl.BlockSpec((1, tb, W), lambda b, i: (b, i, 0))
    out_sds = jax.ShapeDtypeStruct((B, S, W), F32)
    res = pl.pallas_call(
        functools.partial(_rwkv_kernel, has_vres=has_vres),
        grid=(B, S // tb),
        in_specs=in_specs,
        out_specs=out_block if has_vres else [out_block, out_block],
        out_shape=out_sds if has_vres else [out_sds, out_sds],
        scratch_shapes=[pltpu.VMEM((tb + SUBLANES, W), F32),
                        pltpu.VMEM((tb + SUBLANES, WL), F32),
                        pltpu.VMEM((H, D, D), F32),
                        pltpu.VMEM((H, NC, C, D), F32),
                        pltpu.VMEM((H, NC, D, D), F32),
                        pltpu.VMEM((H, NC, D, D), F32),
                        pltpu.VMEM((H, NC, C, D), F32),
                        pltpu.VMEM((H, NC, 1, D), F32),
                        pltpu.VMEM((H, NC, C, D), F32)],
        compiler_params=_cparams(("parallel", "arbitrary")),
        name="rwkv_mixer",
    )(*args)
    if has_vres:
        return res, v_first
    return res[0], res[1]


def _dil_attn_kernel(q_ref, kc_ref, kp_ref, vc_ref, vp_ref, o_ref, lse_ref, kbuf, vbuf):
    Dh, Bk = DIL_HEAD_DIM, ATT_BLOCK
    QB = q_ref.shape[2]
    has_prev = pl.program_id(2) > 0
    kbuf[0:Bk] = kp_ref[0, 0]
    kbuf[Bk:] = kc_ref[0, 0]
    vbuf[0:Bk] = vp_ref[0, 0]
    vbuf[Bk:] = vc_ref[0, 0]
    qi = lax.broadcasted_iota(jnp.int32, (Bk, 2 * Bk), 0)
    cj = lax.broadcasted_iota(jnp.int32, (Bk, 2 * Bk), 1)
    band = (cj >= qi) & (cj <= qi + Bk)
    lane = lax.broadcasted_iota(jnp.int32, (Bk, LANES), 1)
    low = lane < Dh
    low_kv = lax.broadcasted_iota(jnp.int32, (2 * Bk, LANES), 1) < Dh
    for j in range(QB // Bk):
        ok = band if j > 0 else band & ((cj >= Bk) | has_prev)
        lse_blk = jnp.zeros((Bk, LANES), F32)
        for hp in range(DIL_WIDTH // LANES):
            cs = slice(hp * LANES, (hp + 1) * LANES)
            q_pair = q_ref[0, 0, j * Bk:(j + 1) * Bk, cs]
            k_pair = kbuf[j * Bk:(j + 2) * Bk, cs]
            v_pair = vbuf[j * Bk:(j + 2) * Bk, cs]
            res = []
            for half in range(2):
                mine, mine_kv = (low, low_kv) if half == 0 else (~low, ~low_kv)
                qm = jnp.where(mine, q_pair, jnp.zeros_like(q_pair))
                s = lax.dot_general(qm, k_pair, (((1,), (1,)), ((), ())), preferred_element_type=F32)
                s = jnp.where(ok, s, -jnp.inf)
                m = jnp.max(s, axis=-1, keepdims=True)
                p = jnp.exp(s - m).astype(BF16)
                po = jnp.dot(p, jnp.where(mine_kv, v_pair, jnp.ones_like(v_pair)), preferred_element_type=F32)
                l = po[:, Dh:Dh + 1] if half == 0 else po[:, 0:1]
                res.append(po / l)
                lse_blk = jnp.where(lane == 2 * hp + half, m + jnp.log(l), lse_blk)
            o_ref[0, 0, j * Bk:(j + 1) * Bk, cs] = jnp.where(low, res[0], res[1]).astype(o_ref.dtype)
        lse_ref[0, 0, j * Bk:(j + 1) * Bk, :] = lse_blk


def dil_attention_group(q, k, v, qb=2 * ATT_BLOCK):
    B, d, L, W = q.shape
    nj = qb // ATT_BLOCK
    cur = pl.BlockSpec((1, 1, qb, W), lambda b, r, n: (b, r, n, 0))
    prev = pl.BlockSpec((1, 1, ATT_BLOCK, W), lambda b, r, n: (b, r, jnp.maximum(n * nj - 1, 0), 0))
    return pl.pallas_call(
        _dil_attn_kernel,
        grid=(B, d, L // qb),
        in_specs=[cur, cur, prev, cur, prev],
        out_specs=[cur, pl.BlockSpec((1, 1, qb, LANES), lambda b, r, n: (b, r, n, 0))],
        out_shape=[jax.ShapeDtypeStruct((B, d, L, W), BF16), jax.ShapeDtypeStruct((B, d, L, LANES), F32)],
        scratch_shapes=[pltpu.VMEM((ATT_BLOCK + qb, W), BF16), pltpu.VMEM((ATT_BLOCK + qb, W), BF16)],
        compiler_params=_cparams(("parallel", "parallel", "arbitrary")),
        name=f"dil_attn_d{d}",
    )(q, k, k, v, v)


def _out_proj_even_kernel(x_ref, ya_ref, yb_ref, wa_ref, wb_ref, o_ref):
    o_ref[...] = (x_ref[...]
                  + jnp.dot(ya_ref[...].astype(BF16), wa_ref[...], preferred_element_type=F32)
                  + jnp.dot(yb_ref[...].astype(BF16), wb_ref[...], preferred_element_type=F32))


def out_proj_even(x, ya, yb, wa, wb, tm=512):
    T, D = x.shape
    return pl.pallas_call(
        _out_proj_even_kernel,
        grid=(T // tm,),
        in_specs=[pl.BlockSpec((tm, D), lambda i: (i, 0)),
                  pl.BlockSpec((tm, ya.shape[1]), lambda i: (i, 0)),
                  pl.BlockSpec((tm, yb.shape[1]), lambda i: (i, 0)),
                  _resident(wa.shape), _resident(wb.shape)],
        out_specs=pl.BlockSpec((tm, D), lambda i: (i, 0)),
        out_shape=jax.ShapeDtypeStruct((T, D), F32),
        compiler_params=_cparams(("parallel",)),
        name="out_proj_even",
    )(x, ya, yb, wa, wb)


def _out_proj_odd_kernel(x_ref, o0_ref, o1_ref, o2_ref, l0_ref, l1_ref, l2_ref, e_ref, w_ref, out_ref,
                         a_ref, b_ref, la_ref, lb_ref):
    dils = [d for _, d in DIL_PATTERNS]
    lses = [_interleave_load(r, d, la_ref, lb_ref) for r, d in zip((l0_ref, l1_ref, l2_ref), dils)]
    m = jnp.maximum(jnp.maximum(lses[0], lses[1]), lses[2])
    es = [jnp.exp(l - m) for l in lses]
    inv = 1.0 / (es[0] + es[1] + es[2])
    expand = e_ref[...]
    mixed = None
    for e, o_ref, d in zip(es, (o0_ref, o1_ref, o2_ref), dils):
        wide = jnp.dot(e * inv, expand, preferred_element_type=F32, precision=HIGHEST)
        term = wide * _interleave_load(o_ref, d, a_ref, b_ref)
        mixed = term if mixed is None else mixed + term
    out_ref[0] = x_ref[0] + jnp.dot(mixed.astype(BF16), w_ref[...], preferred_element_type=F32)


def out_proj_odd(x3, outs, lses, w, tm=512):
    B, S, D = x3.shape
    W = DIL_WIDTH
    expand = np.zeros((LANES, W), np.float32)
    for h in range(DIL_HEADS):
        expand[h, h * DIL_HEAD_DIM:(h + 1) * DIL_HEAD_DIM] = 1.0
    row = pl.BlockSpec((1, tm, D), lambda b, i: (b, i, 0))

    def res(width):
        return [pl.BlockSpec((1, d, tm // d, width), lambda b, i: (b, 0, i, 0)) for _, d in DIL_PATTERNS]

    return pl.pallas_call(
        _out_proj_odd_kernel,
        grid=(B, S // tm),
        in_specs=[row, *res(W), *res(LANES), _resident((LANES, W)), _resident(w.shape)],
        out_specs=row,
        out_shape=jax.ShapeDtypeStruct((B, S, D), F32),
        scratch_shapes=[pltpu.VMEM((W // LANES, tm, LANES), F32), pltpu.VMEM((W // LANES, tm, LANES), F32),
                        pltpu.VMEM((1, tm, LANES), F32), pltpu.VMEM((1, tm, LANES), F32)],
        compiler_params=_cparams(("parallel", "parallel")),
        name="out_proj_odd",
    )(x3, *outs, *lses, jnp.asarray(expand), w)


def _moe_kernel(x_ref, g_ref, wr_ref, w1_ref, w3_ref, w2_ref, fg_ref, o_ref, h_ref, gate_ref, acc_ref, *, final_norm):
    e = pl.program_id(1)
    lane = lax.broadcasted_iota(jnp.int32, gate_ref.shape, 1)
    NG, EPG = N_GROUPS, EXPERTS_PER_GROUP

    @pl.when(e == 0)
    def _():
        h = _rmsnorm(x_ref[...], g_ref[...])
        h_ref[...] = h.astype(BF16)
        logits = jnp.dot(h, wr_ref[...], preferred_element_type=F32, precision=HIGHEST)
        big = jnp.int32(1 << 30)
        gl = jnp.where(lane < NG, logits, -jnp.inf)
        gmax = jnp.max(gl, axis=-1, keepdims=True)
        gsel = jnp.min(jnp.where(gl == gmax, lane, big), axis=-1, keepdims=True)
        p_group = 1.0 / jnp.sum(jnp.where(lane < NG, jnp.exp(logits - gmax), 0.0), axis=-1, keepdims=True)
        lo = NG + gsel * EPG
        el = jnp.where((lane >= lo) & (lane < lo + EPG), logits, -jnp.inf)
        v1 = jnp.max(el, axis=-1, keepdims=True)
        i1 = jnp.min(jnp.where(el == v1, lane, big), axis=-1, keepdims=True)
        el2 = jnp.where(lane == i1, -jnp.inf, el)
        v2 = jnp.max(el2, axis=-1, keepdims=True)
        i2 = jnp.min(jnp.where(el2 == v2, lane, big), axis=-1, keepdims=True)
        t = jnp.exp(v2 - v1)
        w_top1 = p_group / (1.0 + t)
        w_top2 = p_group * t / (1.0 + t)
        gate_ref[...] = jnp.where(lane == i1, w_top1, 0.0) + jnp.where(lane == i2, w_top2, 0.0)
        acc_ref[...] = jnp.zeros_like(acc_ref)

    hb = h_ref[...]
    a = jnp.dot(hb, w1_ref[0], preferred_element_type=F32)
    b = jnp.dot(hb, w3_ref[0], preferred_element_type=F32)
    gate = jnp.sum(jnp.where(lane == NG + e, gate_ref[...], 0.0), axis=-1, keepdims=True)
    hh = (a * _sigmoid(a)) * b * gate
    acc_ref[...] += jnp.dot(hh.astype(BF16), w2_ref[0], preferred_element_type=F32)

    @pl.when(e == pl.num_programs(1) - 1)
    def _():
        y = x_ref[...] + acc_ref[...]
        if final_norm:
            y = _rmsnorm(y, fg_ref[...])
        o_ref[...] = y


def moe_block(x, gain, w_router, w1, w3, w2, final_gain=None, tm=1024):
    T, D = x.shape
    NE, _, FF = w1.shape
    final_norm = final_gain is not None
    fg = (final_gain if final_norm else jnp.ones((D,), F32)).reshape(1, D)
    return pl.pallas_call(
        functools.partial(_moe_kernel, final_norm=final_norm),
        grid=(T // tm, NE),
        in_specs=[pl.BlockSpec((tm, D), lambda i, e: (i, 0)),
                  _resident((1, D)), _resident((D, LANES)),
                  pl.BlockSpec((1, D, FF), lambda i, e: (e, 0, 0)),
                  pl.BlockSpec((1, D, FF), lambda i, e: (e, 0, 0)),
                  pl.BlockSpec((1, FF, D), lambda i, e: (e, 0, 0)),
                  _resident((1, D))],
        out_specs=pl.BlockSpec((tm, D), lambda i, e: (i, 0)),
        out_shape=jax.ShapeDtypeStruct((T, D), F32),
        scratch_shapes=[pltpu.VMEM((tm, D), BF16), pltpu.VMEM((tm, LANES), F32), pltpu.VMEM((tm, D), F32)],
        compiler_params=_cparams(("parallel", "arbitrary")),
        name="moe_block",
    )(x, gain.reshape(1, D), w_router, w1, w3, w2, fg)


def _place(cols, total):
    lead = cols[0][1].shape[:-1]
    out = jnp.zeros(lead + (total,), F32)
    for off, arr in cols:
        out = out.at[..., off:off + arr.shape[-1]].set(arr)
    return out


def _even_layout(t, vres=None):
    gq, gk, gv, gz, ga, gb, rr, rk, rv, lw, la, lg = jnp.split(
        t, np.cumsum([512, 512, 512, 512, 4, 4, 512, 512, 512, 32, 32, 96])[:-1].tolist(), axis=-1)
    cols = [(EV_Q, gq), (EV_K, gk), (EV_V, gv), (EV_Z, gz), (EV_AB, ga), (EV_AB + GDN_HEADS, gb),
            (EV_R, rr), (EV_RK, rk), (EV_RV, rv),
            (EV_LORA, lw), (EV_LORA + LORA_SEG, la), (EV_LORA + 2 * LORA_SEG, lg)]
    if vres is not None:
        cols.append((EV_LORA + 3 * LORA_SEG, vres))
    return _place(cols, EV_NPAD)


def _pad_rows(w, rows):
    return jnp.zeros((rows, w.shape[1]), F32).at[:w.shape[0]].set(w)


def kernel(x, positions, ev_norm, ev_w_in, rwkv_vres_down, ev_w_out, gdn_conv_w, gdn_A_log, gdn_dt_bias, gdn_norm,
           rwkv_mu, rwkv_w0, rwkv_w2, rwkv_a0, rwkv_a2, rwkv_g2, rwkv_k_k, rwkv_k_a, rwkv_r_k, rwkv_ln_w, rwkv_ln_b,
           rwkv_vres_mu, rwkv_v0, rwkv_v2, od_norm, od_w_in, od_w_out, ffn_norm, moe_w_group, moe_w_expert,
           moe_w1, moe_w3, moe_w2, final_norm):
    B, S, D = x.shape
    T = B * S
    depth = ffn_norm.shape[0]
    xf = x.reshape(T, D)
    cos, sin = rope_tables(positions.reshape(T))
    n_groups = len(DIL_PATTERNS)
    v_first = None
    for layer in range(depth):
        i = layer // 2
        if layer % 2 == 0:
            vres_w = None if i == 0 else rwkv_vres_down[i - 1]
            w_in = _even_layout(ev_w_in[i], vres_w).astype(BF16)
            proj3 = norm_proj(xf, ev_norm[i], w_in).reshape(B, S, EV_NPAD)
            ya = gdn_mixer(proj3, gdn_conv_w[i], gdn_A_log[i], gdn_dt_bias[i], gdn_norm[i])
            mu_r, mu_k, mu_v, mu_w, mu_a, mu_g = jnp.split(
                rwkv_mu[i], np.cumsum([RWKV_DIM, RWKV_DIM, RWKV_DIM, 32, 32])[:].tolist())
            mu = jnp.stack([mu_r, mu_k, mu_v])
            lora_cols = [(0, mu_w), (LORA_SEG, mu_a), (2 * LORA_SEG, mu_g)]
            if i > 0:
                lora_cols.append((3 * LORA_SEG, rwkv_vres_mu[i - 1]))
            mu_lora = _place(lora_cols, 4 * LORA_SEG).reshape(1, 4 * LORA_SEG)
            vec_rows = [rwkv_w0[i], rwkv_a0[i], rwkv_k_k[i], rwkv_k_a[i], rwkv_r_k[i].reshape(-1), rwkv_ln_w[i], rwkv_ln_b[i]]
            if i > 0:
                vec_rows.append(rwkv_v0[i - 1])
            vecs = jnp.stack(vec_rows)
            w2 = _pad_rows(rwkv_w2[i], LORA_SEG)
            a2 = _pad_rows(rwkv_a2[i], LORA_SEG)
            g2 = _pad_rows(rwkv_g2[i], LORA_SEG)
            if i == 0:
                yb, v_first = rwkv_mixer(proj3, mu, mu_lora, vecs, w2, a2, g2)
            else:
                yb, _ = rwkv_mixer(proj3, mu, mu_lora, vecs, w2, a2, g2, v_first, _pad_rows(rwkv_v2[i - 1], LORA_SEG))
            w_out = ev_w_out[i].astype(BF16)
            na = GDN_HEADS * GDN_DV
            xf = out_proj_even(xf, ya.reshape(T, na), yb.reshape(T, RWKV_DIM), w_out[:na], w_out[na:])
        else:
            w_in = od_w_in[i].astype(BF16)
            qkv = norm_proj_rope(xf.reshape(B, S, D), od_norm[i], w_in, cos.reshape(B, S, LANES),
                                 sin.reshape(B, S, LANES), DIL_HEAD_DIM ** -0.5)
            outs, lses = [], []
            for gi, (window, dilation) in enumerate(DIL_PATTERNS):
                assert window // dilation == ATT_BLOCK and (S // dilation) % (2 * ATT_BLOCK) == 0
                o, lse = dil_attention_group(qkv[gi], qkv[n_groups + gi], qkv[2 * n_groups + gi])
                outs.append(o)
                lses.append(lse)
            xf = out_proj_odd(xf.reshape(B, S, D), outs, lses, od_w_out[i].astype(BF16)).reshape(T, D)
        w_router = _place([(0, moe_w_group[layer])]
                          + [(N_GROUPS + g * EXPERTS_PER_GROUP, moe_w_expert[layer, g]) for g in range(N_GROUPS)], LANES)
        xf = moe_block(xf, ffn_norm[layer], w_router, moe_w1[layer].astype(BF16), moe_w3[layer].astype(BF16),
                       moe_w2[layer].astype(BF16), final_norm if layer == depth - 1 else None)
    return xf.reshape(B, S, D)
```

```python
import functools
import math

import jax
import jax.numpy as jnp
import numpy as np
from jax import lax
from jax.experimental import pallas as pl
from jax.experimental.pallas import tpu as pltpu

F32 = jnp.float32
BF16 = jnp.bfloat16
HIGHEST = lax.Precision.HIGHEST

RMS_EPS = 1e-6
L2_EPS = 1e-6
RWKV_LN_EPS = 64e-5
ROPE_THETA = 10000.0

LANES = 128
SUBLANES = 8
VMEM_LIMIT = 56 * 1024 * 1024

GDN_HEADS = 4
GDN_DK = 128
GDN_DV = 128
GDN_CONV = 4
RWKV_HEADS = 8
RWKV_HEAD = 64
RWKV_DIM = RWKV_HEADS * RWKV_HEAD
LORA_SEG = LANES
DIL_PATTERNS = ((128, 1), (512, 4), (2048, 16))
DIL_HEADS = 8
DIL_HEAD_DIM = 64
DIL_WIDTH = DIL_HEADS * DIL_HEAD_DIM
ATT_BLOCK = 128
N_GROUPS = 4
EXPERTS_PER_GROUP = 4
N_EXPERTS = N_GROUPS * EXPERTS_PER_GROUP
CHUNK = 64

EV_Q, EV_K, EV_V, EV_Z = 0, 512, 1024, 1536
EV_R, EV_RK, EV_RV, EV_LORA = 2048, 2560, 3072, 3584
EV_AB = EV_LORA + 4 * LORA_SEG
EV_NPAD = EV_AB + LANES


def _cparams(sem):
    return pltpu.CompilerParams(dimension_semantics=sem, vmem_limit_bytes=VMEM_LIMIT)


def _sigmoid(x):
    return 1.0 / (1.0 + jnp.exp(-x))


def _softplus(x):
    return jnp.maximum(x, 0.0) + jnp.log(1.0 + jnp.exp(-jnp.abs(x)))


def _rmsnorm(x, gain):
    return x * lax.rsqrt(jnp.mean(x * x, axis=-1, keepdims=True) + RMS_EPS) * gain


def _resident(shape):
    nd = len(shape)
    return pl.BlockSpec(shape, lambda *_: (0,) * nd, pipeline_mode=pl.Buffered(1))


def _rope_table_kernel(pos_ref, freq_ref, sign_ref, cos_ref, sin_ref):
    ang = pos_ref[...].astype(F32) * freq_ref[...]
    cos_ref[...] = jnp.cos(ang)
    sin_ref[...] = jnp.sin(ang) * sign_ref[...]


def rope_tables(positions_flat, tm=1024):
    T = positions_flat.shape[0]
    half = DIL_HEAD_DIM // 2
    inv_freq = ROPE_THETA ** (-jnp.arange(half, dtype=F32) * 2.0 / DIL_HEAD_DIM)
    lane = np.arange(LANES)
    freq = inv_freq[lane % half][None, :]
    sign = jnp.asarray(np.where(lane % DIL_HEAD_DIM < half, -1.0, 1.0), F32)[None, :]
    return pl.pallas_call(
        _rope_table_kernel,
        grid=(T // tm,),
        in_specs=[pl.BlockSpec((tm, 1), lambda i: (i, 0)), _resident((1, LANES)), _resident((1, LANES))],
        out_specs=[pl.BlockSpec((tm, LANES), lambda i: (i, 0))] * 2,
        out_shape=[jax.ShapeDtypeStruct((T, LANES), F32)] * 2,
        compiler_params=_cparams(("parallel",)),
        name="rope_tables",
    )(positions_flat.reshape(T, 1), freq, sign)


def _norm_proj_kernel(x_ref, g_ref, w_ref, o_ref):
    h = _rmsnorm(x_ref[...], g_ref[...]).astype(BF16)
    o_ref[...] = jnp.dot(h, w_ref[...], preferred_element_type=F32)


def norm_proj(x, gain, w, tm=256):
    T, D = x.shape
    N = w.shape[1]
    return pl.pallas_call(
        _norm_proj_kernel,
        grid=(T // tm,),
        in_specs=[pl.BlockSpec((tm, D), lambda i: (i, 0)), _resident((1, D)), _resident((D, N))],
        out_specs=pl.BlockSpec((tm, N), lambda i: (i, 0)),
        out_shape=jax.ShapeDtypeStruct((T, N), F32),
        compiler_params=_cparams(("parallel",)),
        name="norm_proj",
    )(x, gain.reshape(1, D), w)


def _deinterleave_store(out_ref, val, d, a_ref, b_ref):
    tm = val.shape[0]
    if d == 1:
        out_ref[0, 0] = val.astype(out_ref.dtype)
        return
    quarter = tm // 4
    for j in range(val.shape[1] // LANES):
        cs = slice(j * LANES, (j + 1) * LANES)
        a_ref[j] = val[:, cs]
        if d == 4:
            for r in range(4):
                out_ref[0, r, :, cs] = a_ref[j, pl.ds(r, quarter, stride=4), :].astype(out_ref.dtype)
            continue
        assert d == 16
        for r1 in range(4):
            b_ref[j, r1 * quarter:(r1 + 1) * quarter, :] = a_ref[j, pl.ds(r1, quarter, stride=4), :]
        for r1 in range(4):
            for r2 in range(4):
                out_ref[0, r1 + 4 * r2, :, cs] = (
                    b_ref[j, pl.ds(r1 * quarter + r2, quarter // 4, stride=4), :].astype(out_ref.dtype))


def _interleave_load(in_ref, d, a_ref, b_ref):
    if d == 1:
        return in_ref[0, 0].astype(F32)
    n_tiles, tm, _ = a_ref.shape
    quarter = tm // 4
    for j in range(n_tiles):
        cs = slice(j * LANES, (j + 1) * LANES)
        if d == 4:
            for r in range(4):
                a_ref[j, pl.ds(r, quarter, stride=4), :] = in_ref[0, r, :, cs].astype(F32)
            continue
        assert d == 16
        for r1 in range(4):
            for r2 in range(4):
                b_ref[j, pl.ds(r1 * quarter + r2, quarter // 4, stride=4), :] = in_ref[0, r1 + 4 * r2, :, cs].astype(F32)
        for r1 in range(4):
            a_ref[j, pl.ds(r1, quarter, stride=4), :] = b_ref[j, r1 * quarter:(r1 + 1) * quarter, :]
    return jnp.concatenate([a_ref[j] for j in range(n_tiles)], axis=-1)


def _norm_proj_rope_kernel(x_ref, g_ref, w_ref, cos_ref, sin_ref, *rest, q_scale):
    out_refs, (a_ref, b_ref) = rest[:-2], rest[-2:]
    n_groups = len(DIL_PATTERNS)
    h = _rmsnorm(x_ref[0], g_ref[...]).astype(BF16)
    cos = cos_ref[0]
    sin = sin_ref[0]
    lane = lax.broadcasted_iota(jnp.int32, cos.shape, 1)
    first_half = (lane % DIL_HEAD_DIM) < (DIL_HEAD_DIM // 2)
    W = DIL_WIDTH
    for c, out_ref in enumerate(out_refs):
        which, g = divmod(c, n_groups)
        acc = jnp.dot(h, w_ref[:, c * W:(c + 1) * W], preferred_element_type=F32)
        if which < 2:
            parts = []
            for j in range(W // LANES):
                blk = acc[:, j * LANES:(j + 1) * LANES]
                rot = jnp.where(first_half, pltpu.roll(blk, LANES - 32, axis=1), pltpu.roll(blk, 32, axis=1))
                out = blk * cos + rot * sin
                parts.append(out * q_scale if which == 0 else out)
            acc = jnp.concatenate(parts, axis=-1)
        _deinterleave_store(out_ref, acc, DIL_PATTERNS[g][1], a_ref, b_ref)


def norm_proj_rope(x3, gain, w, cos3, sin3, q_scale, tm=512):
    B, S, D = x3.shape
    W = DIL_WIDTH
    out_specs, out_shape = [], []
    for _ in range(3):
        for _, d in DIL_PATTERNS:
            out_specs.append(pl.BlockSpec((1, d, tm // d, W), lambda b, i: (b, 0, i, 0)))
            out_shape.append(jax.ShapeDtypeStruct((B, d, S // d, W), BF16))
    return pl.pallas_call(
        functools.partial(_norm_proj_rope_kernel, q_scale=q_scale),
        grid=(B, S // tm),
        in_specs=[pl.BlockSpec((1, tm, D), lambda b, i: (b, i, 0)), _resident((1, D)), _resident(w.shape),
                  pl.BlockSpec((1, tm, LANES), lambda b, i: (b, i, 0)),
                  pl.BlockSpec((1, tm, LANES), lambda b, i: (b, i, 0))],
        out_specs=out_specs,
        out_shape=out_shape,
        scratch_shapes=[pltpu.VMEM((W // LANES, tm, LANES), F32), pltpu.VMEM((W // LANES, tm, LANES), F32)],
        compiler_params=_cparams(("parallel", "parallel")),
        name="norm_proj_rope",
    )(x3, gain.reshape(1, D), w, cos3, sin3)


def _bmm(a, b):
    return jnp.einsum('nik,nkj->nij', a.astype(BF16), b.astype(BF16), preferred_element_type=F32)


def _bmm_nt(a, b):
    return jnp.einsum('nik,njk->nij', a.astype(BF16), b.astype(BF16), preferred_element_type=F32)


def _bmm_tn(a, b):
    return jnp.einsum('nci,ncj->nij', a.astype(BF16), b.astype(BF16), preferred_element_type=F32)


def _unit_lower_inverse(L):
    C = L.shape[-1]
    ri = lax.broadcasted_iota(jnp.int32, (C, C), 0)
    ci = lax.broadcasted_iota(jnp.int32, (C, C), 1)
    same16 = (ri // 16) == (ci // 16)
    same32 = (ri // 32) == (ci // 32)
    eye = (ri == ci).astype(F32)
    Ld = jnp.where(same16, L, 0.0)
    X = eye - Ld
    P = Ld
    for _ in range(3):
        P = _bmm(P, P)
        X = X + _bmm(X, P)
    for off in (jnp.where(same32 & ~same16, L, 0.0), jnp.where(~same32, L, 0.0)):
        X = X - _bmm(_bmm(X, off), X)
    return X


def _shifted_rows(buf_ref, cur, prev, first, shifts):
    TB = cur.shape[0]
    buf_ref[0:SUBLANES, :] = jnp.where(first, 0.0, prev)
    buf_ref[SUBLANES:, :] = cur
    return [buf_ref[SUBLANES - s:SUBLANES - s + TB, :] for s in shifts]


def _split_heads(x, n_heads, width):
    return jnp.stack([x[:, h * width:(h + 1) * width] for h in range(n_heads)], axis=0)


def _merge_heads(x):
    return jnp.concatenate([x[h] for h in range(x.shape[0])], axis=-1)


def _gdn_kernel(q_ref, k_ref, v_ref, z_ref, ab_ref, qp_ref, kp_ref, vp_ref, cw_ref, alog_ref, dtb_ref, nw_ref,
                o_ref, buf_ref, s_ref, qd_ref, pg_ref, qg_ref, ov_ref, egl_ref, oc_ref):
    H, DK, DV, C = GDN_HEADS, GDN_DK, GDN_DV, CHUNK
    TB = q_ref.shape[1]
    NC = TB // C
    first = pl.program_id(1) == 0

    @pl.when(first)
    def _():
        s_ref[...] = jnp.zeros_like(s_ref)

    def conv_silu(cur_ref, prev_ref, j):
        cur = cur_ref[0]
        x3, x2, x1 = _shifted_rows(buf_ref, cur, prev_ref[0], first, (3, 2, 1))
        w = cw_ref[j]
        y = x3 * w[0:1] + x2 * w[1:2] + x1 * w[2:3] + cur * w[3:4]
        return y * _sigmoid(y)

    q = _split_heads(conv_silu(q_ref, qp_ref, 0), H, DK)
    k = _split_heads(conv_silu(k_ref, kp_ref, 1), H, DK)
    v = _split_heads(conv_silu(v_ref, vp_ref, 2), H, DV)
    q = q * lax.rsqrt(jnp.sum(q * q, axis=-1, keepdims=True) + L2_EPS) * (DK ** -0.5)
    k = k * lax.rsqrt(jnp.sum(k * k, axis=-1, keepdims=True) + L2_EPS)
    ab = ab_ref[0]
    a = jnp.stack([ab[:, h:h + 1] for h in range(H)], axis=0)
    b = jnp.stack([ab[:, H + h:H + h + 1] for h in range(H)], axis=0)
    beta = _sigmoid(b)
    g = -jnp.exp(alog_ref[...]) * _softplus(a + dtb_ref[...])

    N = H * NC
    q = q.reshape(N, C, DK)
    k = k.reshape(N, C, DK)
    v = v.reshape(N, C, DV)
    beta = beta.reshape(N, C, 1)
    g = g.reshape(N, C, 1)

    ri = lax.broadcasted_iota(jnp.int32, (C, C), 0)
    ci = lax.broadcasted_iota(jnp.int32, (C, C), 1)
    causal = ri >= ci
    strict = ri > ci
    gb = jnp.broadcast_to(g, (N, C, C))
    g_row = jnp.sum(jnp.where(ri == ci, gb, 0.0), axis=1, keepdims=True)
    gc_col = jnp.sum(jnp.where(causal, jnp.broadcast_to(g_row, (N, C, C)), 0.0), axis=2, keepdims=True)
    gc_row = jnp.sum(jnp.where(ri <= ci, gb, 0.0), axis=1, keepdims=True)
    decay = jnp.where(causal, jnp.exp(jnp.where(causal, gc_col - gc_row, 0.0)), 0.0)

    kb = k * beta
    vb = v * beta
    kq = _bmm_nt(jnp.concatenate([kb, q], axis=1), k)
    L = jnp.where(strict, kq[:, :C] * decay, 0.0)
    aqk = kq[:, C:] * decay
    tinv = _unit_lower_inverse(L)
    egc = jnp.exp(gc_col)
    uw = _bmm(tinv, jnp.concatenate([vb, kb * egc], axis=2))
    auw = _bmm(aqk, uw)
    g_last = gc_col[:, C - 1:C, :]
    k_dec = k * jnp.exp(g_last - gc_col)
    kuw = _bmm_tn(k_dec, uw)
    qd_ref[...] = (q * egc - auw[:, :, DV:]).reshape(H, NC, C, DK)
    ov_ref[...] = auw[:, :, :DV].reshape(H, NC, C, DV)
    pg_ref[...] = kuw[:, :, DV:].reshape(H, NC, DK, DK)
    qg_ref[...] = kuw[:, :, :DV].reshape(H, NC, DK, DV)
    egl_ref[...] = jnp.broadcast_to(jnp.exp(g_last), (N, 1, LANES)).reshape(H, NC, 1, LANES)

    for c in range(NC):
        S = s_ref[...]
        oc_ref[:, c] = _bmm(qd_ref[:, c], S) + ov_ref[:, c]
        s_ref[...] = S * egl_ref[:, c] - _bmm(pg_ref[:, c], S) + qg_ref[:, c]

    o = oc_ref[...].reshape(H, TB, DV)
    o = _rmsnorm(o, nw_ref[...])
    z = _split_heads(z_ref[0], H, DV)
    o_ref[0] = _merge_heads(o * (z * _sigmoid(z)))


def gdn_mixer(proj3, conv_w, a_log, dt_bias, norm_w, tb=512):
    B, S, _ = proj3.shape
    H, DK, DV, C = GDN_HEADS, GDN_DK, GDN_DV, CHUNK
    W = H * DK
    NC = tb // C
    nq, nk, nv, nz, nab = EV_Q // W, EV_K // W, EV_V // W, EV_Z // W, EV_AB // LANES
    rows8 = tb // SUBLANES

    def cur(cb):
        return pl.BlockSpec((1, tb, W), lambda b, i, cb=cb: (b, i, cb))

    def prev(cb):
        return pl.BlockSpec((1, SUBLANES, W), lambda b, i, cb=cb: (b, jnp.maximum(i * rows8 - 1, 0), cb))

    cw = conv_w.reshape(GDN_CONV, 3, W).transpose(1, 0, 2)
    return pl.pallas_call(
        _gdn_kernel,
        grid=(B, S // tb),
        in_specs=[cur(nq), cur(nk), cur(nv), cur(nz),
                  pl.BlockSpec((1, tb, LANES), lambda b, i: (b, i, nab)),
                  prev(nq), prev(nk), prev(nv),
                  _resident((3, GDN_CONV, W)), _resident((H, 1, 1)), _resident((H, 1, 1)), _resident((1, 1, DV))],
        out_specs=pl.BlockSpec((1, tb, H * DV), lambda b, i: (b, i, 0)),
        out_shape=jax.ShapeDtypeStruct((B, S, H * DV), F32),
        scratch_shapes=[pltpu.VMEM((tb + SUBLANES, W), F32),
                        pltpu.VMEM((H, DK, DV), F32),
                        pltpu.VMEM((H, NC, C, DK), F32),
                        pltpu.VMEM((H, NC, DK, DK), F32),
                        pltpu.VMEM((H, NC, DK, DV), F32),
                        pltpu.VMEM((H, NC, C, DV), F32),
                        pltpu.VMEM((H, NC, 1, LANES), F32),
                        pltpu.VMEM((H, NC, C, DV), F32)],
        compiler_params=_cparams(("parallel", "arbitrary")),
        name="gdn_mixer",
    )(proj3, proj3, proj3, proj3, proj3, proj3, proj3, proj3, cw,
      a_log.reshape(H, 1, 1), dt_bias.reshape(H, 1, 1), norm_w.reshape(1, 1, DV))


def _rwkv_kernel(*refs, has_vres):
    if has_vres:
        (r_ref, k_ref, v_ref, lo_ref, rp_ref, kp_ref, vp_ref, lop_ref, vf_ref,
         mu_ref, mul_ref, vec_ref, w2_ref, a2_ref, g2_ref, v2_ref,
         y_ref, buf_ref, bufl_ref, s_ref, rr_ref, pp_ref, qq_ref, yv_ref, egc_ref, yc_ref) = refs
    else:
        (r_ref, k_ref, v_ref, lo_ref, rp_ref, kp_ref, vp_ref, lop_ref,
         mu_ref, mul_ref, vec_ref, w2_ref, a2_ref, g2_ref,
         y_ref, vf_out_ref, buf_ref, bufl_ref, s_ref, rr_ref, pp_ref, qq_ref, yv_ref, egc_ref, yc_ref) = refs
    H, D, C = RWKV_HEADS, RWKV_HEAD, CHUNK
    TB = r_ref.shape[1]
    NC = TB // C
    first = pl.program_id(1) == 0

    @pl.when(first)
    def _():
        s_ref[...] = jnp.zeros_like(s_ref)

    def mix(cur_ref, prev_ref, mu, buf):
        cur = cur_ref[0]
        (sh,) = _shifted_rows(buf, cur, prev_ref[0], first, (1,))
        return cur + mu * (sh - cur)

    r = mix(r_ref, rp_ref, mu_ref[0:1], buf_ref)
    k = mix(k_ref, kp_ref, mu_ref[1:2], buf_ref)
    v = mix(v_ref, vp_ref, mu_ref[2:3], buf_ref)
    lo = mix(lo_ref, lop_ref, mul_ref[...], bufl_ref)
    hw, ha, hg = (lo[:, j * LORA_SEG:(j + 1) * LORA_SEG] for j in range(3))
    w0, a0, k_k, k_a, r_k, ln_w, ln_b = (vec_ref[j:j + 1] for j in range(7))

    w_log = -_softplus(-(w0 + jnp.dot(jnp.tanh(hw), w2_ref[...], preferred_element_type=F32))) - 0.5
    lw = -jnp.exp(w_log)
    a = _sigmoid(a0 + jnp.dot(ha, a2_ref[...], preferred_element_type=F32))
    gate = jnp.dot(_sigmoid(hg), g2_ref[...], preferred_element_type=F32)
    if has_vres:
        hv = lo[:, 3 * LORA_SEG:4 * LORA_SEG]
        v0 = vec_ref[7:8]
        v = v + (vf_ref[0] - v) * _sigmoid(v0 + jnp.dot(hv, v2_ref[...], preferred_element_type=F32))
    else:
        vf_out_ref[0] = v

    ri = lax.broadcasted_iota(jnp.int32, (C, C), 0)
    ci = lax.broadcasted_iota(jnp.int32, (C, C), 1)
    tril = (ri >= ci).astype(F32)
    gi = jnp.concatenate(
        [jnp.dot(tril, lw[c * C:(c + 1) * C], preferred_element_type=F32, precision=HIGHEST) for c in range(NC)], axis=0)

    kk = _split_heads(k * k_k, H, D)
    kk = kk * lax.rsqrt(jnp.sum(kk * kk, axis=-1, keepdims=True) + L2_EPS)
    k = k * (1.0 + (a - 1.0) * k_a)
    r_h = _split_heads(r, H, D)
    k_h = _split_heads(k, H, D)
    v_h = _split_heads(v, H, D)
    a_h = _split_heads(a, H, D)
    gi_h = _split_heads(gi, H, D)
    lw_h = _split_heads(lw, H, D)
    bonus = jnp.sum(r_h * k_h * _split_heads(r_k, H, D), axis=-1, keepdims=True) * v_h

    N = H * NC
    rc, kc, vc, kkc, ac, gic, lwc = (t.reshape(N, C, D) for t in (r_h, k_h, v_h, kk, a_h, gi_h, lw_h))
    g_end = gic[:, C - 1:C, :]
    e_in = jnp.exp(gic)
    e_neg = jnp.exp(-gic)
    e_tail = jnp.exp(g_end - gic)
    at = -kkc * jnp.exp(gic - lwc)
    bvec = kkc * ac
    rt = rc * e_in
    bt = bvec * e_neg
    kt = kc * e_neg
    strict = ri > ci
    causal = ri >= ci
    amat = _bmm_nt(jnp.concatenate([at, rt], axis=1), jnp.concatenate([bt, kt], axis=1))
    a_ab = jnp.where(strict, amat[:, :C, :C], 0.0)
    a_ak = jnp.where(strict, amat[:, :C, C:], 0.0)
    a_r = jnp.concatenate([jnp.where(causal, amat[:, C:, :C], 0.0), jnp.where(causal, amat[:, C:, C:], 0.0)], axis=2)
    tinv = _unit_lower_inverse(-a_ab)
    x2 = _bmm(tinv, jnp.concatenate([at, _bmm(a_ak, vc)], axis=2))
    xv = jnp.concatenate([x2, jnp.concatenate([jnp.zeros_like(vc), vc], axis=2)], axis=1)
    y2 = _bmm(a_r, xv)
    pq = _bmm_tn(xv, jnp.concatenate([bvec * e_tail, kc * e_tail], axis=1))
    rr_ref[...] = (rt + y2[:, :, :D]).reshape(H, NC, C, D)
    yv_ref[...] = y2[:, :, D:].reshape(H, NC, C, D)
    pp_ref[...] = pq[:, :D].reshape(H, NC, D, D)
    qq_ref[...] = pq[:, D:].reshape(H, NC, D, D)
    egc_ref[...] = jnp.exp(g_end).reshape(H, NC, 1, D)

    for c in range(NC):
        S = s_ref[...]
        yc_ref[:, c] = _bmm_nt(rr_ref[:, c], S) + yv_ref[:, c]
        s_ref[...] = S * egc_ref[:, c] + _bmm(S, pp_ref[:, c]) + qq_ref[:, c]

    y = yc_ref[...].reshape(H, TB, D)
    mean = jnp.mean(y, axis=-1, keepdims=True)
    yc = y - mean
    var = jnp.mean(yc * yc, axis=-1, keepdims=True)
    y = _merge_heads(yc * lax.rsqrt(var + RWKV_LN_EPS)) * ln_w + ln_b
    y_ref[0] = (y + _merge_heads(bonus)) * gate


def rwkv_mixer(proj3, mu, mu_lora, vecs, w2, a2, g2, v_first=None, v2=None, tb=256):
    B, S, _ = proj3.shape
    H, D, C = RWKV_HEADS, RWKV_HEAD, CHUNK
    W = RWKV_DIM
    WL = 4 * LORA_SEG
    NC = tb // C
    rows8 = tb // SUBLANES
    has_vres = v_first is not None

    def cur(col, width):
        return pl.BlockSpec((1, tb, width), lambda b, i, cb=col // width: (b, i, cb))

    def prev(col, width):
        return pl.BlockSpec((1, SUBLANES, width),
                            lambda b, i, cb=col // width: (b, jnp.maximum(i * rows8 - 1, 0), cb))

    in_specs = [cur(EV_R, W), cur(EV_RK, W), cur(EV_RV, W), cur(EV_LORA, WL),
                prev(EV_R, W), prev(EV_RK, W), prev(EV_RV, W), prev(EV_LORA, WL)]
    args = [proj3] * 8
    if has_vres:
        in_specs.append(pl.BlockSpec((1, tb, W), lambda b, i: (b, i, 0)))
        args.append(v_first)
    in_specs += [_resident(mu.shape), _resident(mu_lora.shape), _resident(vecs.shape),
                 _resident(w2.shape), _resident(a2.shape), _resident(g2.shape)]
    args += [mu, mu_lora, vecs, w2, a2, g2]
    if has_vres:
        in_specs.append(_resident(v2.shape))
        args.append(v2)
    out_block = pl.BlockSpec((1, tb, W), lambda b, i: (b, i, 0))
    out_sds = jax.ShapeDtypeStruct((B, S, W), F32)
    res = pl.pallas_call(
        functools.partial(_rwkv_kernel, has_vres=has_vres),
        grid=(B, S // tb),
        in_specs=in_specs,
        out_specs=out_block if has_vres else [out_block, out_block],
        out_shape=out_sds if has_vres else [out_sds, out_sds],
        scratch_shapes=[pltpu.VMEM((tb + SUBLANES, W), F32),
                        pltpu.VMEM((tb + SUBLANES, WL), F32),
                        pltpu.VMEM((H, D, D), F32),
                        pltpu.VMEM((H, NC, C, D), F32),
                        pltpu.VMEM((H, NC, D, D), F32),
                        pltpu.VMEM((H, NC, D, D), F32),
                        pltpu.VMEM((H, NC, C, D), F32),
                        pltpu.VMEM((H, NC, 1, D), F32),
                        pltpu.VMEM((H, NC, C, D), F32)],
        compiler_params=_cparams(("parallel", "arbitrary")),
        name="rwkv_mixer",
    )(*args)
    if has_vres:
        return res, v_first
    return res[0], res[1]


def _dil_attn_kernel(q_ref, kc_ref, kp_ref, vc_ref, vp_ref, o_ref, lse_ref, kbuf, vbuf):
    Dh, Bk = DIL_HEAD_DIM, ATT_BLOCK
    QB = q_ref.shape[2]
    has_prev = pl.program_id(2) > 0
    kbuf[0:Bk] = kp_ref[0, 0]
    kbuf[Bk:] = kc_ref[0, 0]
    vbuf[0:Bk] = vp_ref[0, 0]
    vbuf[Bk:] = vc_ref[0, 0]
    qi = lax.broadcasted_iota(jnp.int32, (Bk, 2 * Bk), 0)
    cj = lax.broadcasted_iota(jnp.int32, (Bk, 2 * Bk), 1)
    band = (cj >= qi) & (cj <= qi + Bk)
    lane = lax.broadcasted_iota(jnp.int32, (Bk, LANES), 1)
    low = lane < Dh
    ones = jnp.ones((2 * Bk, LANES), BF16)
    NP = DIL_WIDTH // LANES

    def pairs(x):
        return jnp.stack([x[:, hp * LANES:(hp + 1) * LANES] for hp in range(NP)], axis=0)

    for j in range(QB // Bk):
        ok = band if j > 0 else band & ((cj >= Bk) | has_prev)
        ok2 = jnp.concatenate([ok, ok], axis=0)
        q_pairs = pairs(q_ref[0, 0, j * Bk:(j + 1) * Bk, :])
        zero = jnp.zeros_like(q_pairs)
        q2 = jnp.concatenate([jnp.where(low, q_pairs, zero), jnp.where(low, zero, q_pairs)], axis=1)
        k_pairs = pairs(kbuf[j * Bk:(j + 2) * Bk, :])
        v_pairs = pairs(vbuf[j * Bk:(j + 2) * Bk, :])
        s = jnp.einsum('hqd,hkd->hqk', q2, k_pairs, preferred_element_type=F32)
        s = jnp.where(ok2, s, -jnp.inf)
        m = jnp.max(s, axis=-1, keepdims=True)
        p = jnp.exp2(s - m).astype(BF16)
        po = jnp.einsum('hqk,hkd->hqd', p, v_pairs, preferred_element_type=F32)
        l = jnp.dot(p.reshape(NP * 2 * Bk, 2 * Bk), ones, preferred_element_type=F32)
        l = l.reshape(NP, 2 * Bk, LANES)
        r = po / l
        o_pairs = jnp.where(low, r[:, :Bk], r[:, Bk:])
        o_ref[0, 0, j * Bk:(j + 1) * Bk, :] = jnp.concatenate([o_pairs[hp] for hp in range(NP)], axis=-1).astype(o_ref.dtype)
        lse = m + jnp.log2(l)
        lse_blk = jnp.zeros((Bk, LANES), F32)
        for hp in range(NP):
            for half in range(2):
                lse_blk = jnp.where(lane == 2 * hp + half, lse[hp, half * Bk:(half + 1) * Bk], lse_blk)
        lse_ref[0, 0, j * Bk:(j + 1) * Bk, :] = lse_blk


def dil_attention_group(q, k, v, qb=2 * ATT_BLOCK):
    B, d, L, W = q.shape
    nj = qb // ATT_BLOCK
    cur = pl.BlockSpec((1, 1, qb, W), lambda b, r, n: (b, r, n, 0))
    prev = pl.BlockSpec((1, 1, ATT_BLOCK, W), lambda b, r, n: (b, r, jnp.maximum(n * nj - 1, 0), 0))
    return pl.pallas_call(
        _dil_attn_kernel,
        grid=(B, d, L // qb),
        in_specs=[cur, cur, prev, cur, prev],
        out_specs=[cur, pl.BlockSpec((1, 1, qb, LANES), lambda b, r, n: (b, r, n, 0))],
        out_shape=[jax.ShapeDtypeStruct((B, d, L, W), BF16), jax.ShapeDtypeStruct((B, d, L, LANES), F32)],
        scratch_shapes=[pltpu.VMEM((ATT_BLOCK + qb, W), BF16), pltpu.VMEM((ATT_BLOCK + qb, W), BF16)],
        compiler_params=_cparams(("parallel", "parallel", "arbitrary")),
        name=f"dil_attn_d{d}",
    )(q, k, k, v, v)


def _out_proj_even_kernel(x_ref, ya_ref, yb_ref, wa_ref, wb_ref, o_ref):
    o_ref[...] = (x_ref[...]
                  + jnp.dot(ya_ref[...].astype(BF16), wa_ref[...], preferred_element_type=F32)
                  + jnp.dot(yb_ref[...].astype(BF16), wb_ref[...], preferred_element_type=F32))


def out_proj_even(x, ya, yb, wa, wb, tm=512):
    T, D = x.shape
    return pl.pallas_call(
        _out_proj_even_kernel,
        grid=(T // tm,),
        in_specs=[pl.BlockSpec((tm, D), lambda i: (i, 0)),
                  pl.BlockSpec((tm, ya.shape[1]), lambda i: (i, 0)),
                  pl.BlockSpec((tm, yb.shape[1]), lambda i: (i, 0)),
                  _resident(wa.shape), _resident(wb.shape)],
        out_specs=pl.BlockSpec((tm, D), lambda i: (i, 0)),
        out_shape=jax.ShapeDtypeStruct((T, D), F32),
        compiler_params=_cparams(("parallel",)),
        name="out_proj_even",
    )(x, ya, yb, wa, wb)


def _out_proj_odd_kernel(x_ref, o0_ref, o1_ref, o2_ref, l0_ref, l1_ref, l2_ref, e_ref, w_ref, out_ref,
                         a_ref, b_ref, la_ref, lb_ref):
    dils = [d for _, d in DIL_PATTERNS]
    lses = [_interleave_load(r, d, la_ref, lb_ref) for r, d in zip((l0_ref, l1_ref, l2_ref), dils)]
    m = jnp.maximum(jnp.maximum(lses[0], lses[1]), lses[2])
    es = [jnp.exp2(l - m) for l in lses]
    inv = 1.0 / (es[0] + es[1] + es[2])
    expand = e_ref[...]
    mixed = None
    for e, o_ref, d in zip(es, (o0_ref, o1_ref, o2_ref), dils):
        wide = jnp.dot(e * inv, expand, preferred_element_type=F32, precision=HIGHEST)
        term = wide * _interleave_load(o_ref, d, a_ref, b_ref)
        mixed = term if mixed is None else mixed + term
    out_ref[0] = x_ref[0] + jnp.dot(mixed.astype(BF16), w_ref[...], preferred_element_type=F32)


def out_proj_odd(x3, outs, lses, w, tm=512):
    B, S, D = x3.shape
    W = DIL_WIDTH
    expand = np.zeros((LANES, W), np.float32)
    for h in range(DIL_HEADS):
        expand[h, h * DIL_HEAD_DIM:(h + 1) * DIL_HEAD_DIM] = 1.0
    row = pl.BlockSpec((1, tm, D), lambda b, i: (b, i, 0))

    def res(width):
        return [pl.BlockSpec((1, d, tm // d, width), lambda b, i: (b, 0, i, 0)) for _, d in DIL_PATTERNS]

    return pl.pallas_call(
        _out_proj_odd_kernel,
        grid=(B, S // tm),
        in_specs=[row, *res(W), *res(LANES), _resident((LANES, W)), _resident(w.shape)],
        out_specs=row,
        out_shape=jax.ShapeDtypeStruct((B, S, D), F32),
        scratch_shapes=[pltpu.VMEM((W // LANES, tm, LANES), F32), pltpu.VMEM((W // LANES, tm, LANES), F32),
                        pltpu.VMEM((1, tm, LANES), F32), pltpu.VMEM((1, tm, LANES), F32)],
        compiler_params=_cparams(("parallel", "parallel")),
        name="out_proj_odd",
    )(x3, *outs, *lses, jnp.asarray(expand), w)


def _moe_kernel(x_ref, g_ref, wr_ref, w1_ref, w3_ref, w2_ref, fg_ref, o_ref, hh_ref, *, final_norm):
    NG, EPG = N_GROUPS, EXPERTS_PER_GROUP
    NE, _, FF = w1_ref.shape
    x = x_ref[...]
    h = _rmsnorm(x, g_ref[...])
    hb = h.astype(BF16)
    h_lo = (h - hb.astype(F32)).astype(BF16)
    lg = jnp.dot(hb, wr_ref[...], preferred_element_type=F32)
    logits = lg[:, :LANES] + lg[:, LANES:] + jnp.dot(h_lo, wr_ref[:, :LANES], preferred_element_type=F32)
    lane = lax.broadcasted_iota(jnp.int32, logits.shape, 1)
    big = jnp.int32(1 << 30)
    gl = jnp.where(lane < NG, logits, -jnp.inf)
    gmax = jnp.max(gl, axis=-1, keepdims=True)
    gsel = jnp.min(jnp.where(gl == gmax, lane, big), axis=-1, keepdims=True)
    p_group = 1.0 / jnp.sum(jnp.where(lane < NG, jnp.exp(logits - gmax), 0.0), axis=-1, keepdims=True)
    lo = NG + gsel * EPG
    el = jnp.where((lane >= lo) & (lane < lo + EPG), logits, -jnp.inf)
    v1 = jnp.max(el, axis=-1, keepdims=True)
    i1 = jnp.min(jnp.where(el == v1, lane, big), axis=-1, keepdims=True)
    el2 = jnp.where(lane == i1, -jnp.inf, el)
    v2 = jnp.max(el2, axis=-1, keepdims=True)
    i2 = jnp.min(jnp.where(el2 == v2, lane, big), axis=-1, keepdims=True)
    t = jnp.exp(v2 - v1)
    gates = jnp.where(lane == i1, p_group / (1.0 + t), 0.0) + jnp.where(lane == i2, p_group * t / (1.0 + t), 0.0)

    for e in range(NE):
        a = jnp.dot(hb, w1_ref[e], preferred_element_type=F32)
        b = jnp.dot(hb, w3_ref[e], preferred_element_type=F32)
        hh_ref[:, e * FF:(e + 1) * FF] = ((a * _sigmoid(a)) * b * gates[:, NG + e:NG + e + 1]).astype(BF16)
    y = x + jnp.dot(hh_ref[...], w2_ref[...], preferred_element_type=F32)
    if final_norm:
        y = _rmsnorm(y, fg_ref[...])
    o_ref[...] = y


def moe_block(x, gain, w_router, w1, w3, w2, final_gain=None, tm=512):
    T, D = x.shape
    NE, _, FF = w1.shape
    final_norm = final_gain is not None
    fg = (final_gain if final_norm else jnp.ones((D,), F32)).reshape(1, D)
    return pl.pallas_call(
        functools.partial(_moe_kernel, final_norm=final_norm),
        grid=(T // tm,),
        in_specs=[pl.BlockSpec((tm, D), lambda i: (i, 0)),
                  _resident((1, D)), _resident(w_router.shape),
                  _resident(w1.shape), _resident(w3.shape), _resident(w2.shape), _resident((1, D))],
        out_specs=pl.BlockSpec((tm, D), lambda i: (i, 0)),
        out_shape=jax.ShapeDtypeStruct((T, D), F32),
        scratch_shapes=[pltpu.VMEM((tm, NE * FF), BF16)],
        compiler_params=_cparams(("parallel",)),
        name="moe_block",
    )(x, gain.reshape(1, D), w_router, w1, w3, w2, fg)


def _router_weights(w_group, w_expert):
    w = _place([(0, w_group)] + [(N_GROUPS + g * EXPERTS_PER_GROUP, w_expert[g]) for g in range(N_GROUPS)], LANES)
    hi = w.astype(BF16)
    lo = (w - hi.astype(F32)).astype(BF16)
    return jnp.concatenate([hi, lo], axis=1)


def _place(cols, total):
    lead = cols[0][1].shape[:-1]
    parts, pos = [], 0
    for off, arr in sorted(cols, key=lambda c: c[0]):
        if off > pos:
            parts.append(jnp.zeros(lead + (off - pos,), F32))
        parts.append(arr)
        pos = off + arr.shape[-1]
    if total > pos:
        parts.append(jnp.zeros(lead + (total - pos,), F32))
    return jnp.concatenate(parts, axis=-1)


def _even_layout(t, vres=None):
    gq, gk, gv, gz, ga, gb, rr, rk, rv, lw, la, lg = jnp.split(
        t, np.cumsum([512, 512, 512, 512, 4, 4, 512, 512, 512, 32, 32, 96])[:-1].tolist(), axis=-1)
    cols = [(EV_Q, gq), (EV_K, gk), (EV_V, gv), (EV_Z, gz), (EV_AB, ga), (EV_AB + GDN_HEADS, gb),
            (EV_R, rr), (EV_RK, rk), (EV_RV, rv),
            (EV_LORA, lw), (EV_LORA + LORA_SEG, la), (EV_LORA + 2 * LORA_SEG, lg)]
    if vres is not None:
        cols.append((EV_LORA + 3 * LORA_SEG, vres))
    return _place(cols, EV_NPAD)


def _pad_rows(w, rows):
    return jnp.concatenate([w, jnp.zeros((rows - w.shape[0], w.shape[1]), F32)], axis=0)


def kernel(x, positions, ev_norm, ev_w_in, rwkv_vres_down, ev_w_out, gdn_conv_w, gdn_A_log, gdn_dt_bias, gdn_norm,
           rwkv_mu, rwkv_w0, rwkv_w2, rwkv_a0, rwkv_a2, rwkv_g2, rwkv_k_k, rwkv_k_a, rwkv_r_k, rwkv_ln_w, rwkv_ln_b,
           rwkv_vres_mu, rwkv_v0, rwkv_v2, od_norm, od_w_in, od_w_out, ffn_norm, moe_w_group, moe_w_expert,
           moe_w1, moe_w3, moe_w2, final_norm):
    B, S, D = x.shape
    T = B * S
    depth = ffn_norm.shape[0]
    xf = x.reshape(T, D)
    cos, sin = rope_tables(positions.reshape(T))
    n_groups = len(DIL_PATTERNS)
    v_first = None
    for layer in range(depth):
        i = layer // 2
        if layer % 2 == 0:
            vres_w = None if i == 0 else rwkv_vres_down[i - 1]
            w_in = _even_layout(ev_w_in[i], vres_w).astype(BF16)
            proj3 = norm_proj(xf, ev_norm[i], w_in).reshape(B, S, EV_NPAD)
            ya = gdn_mixer(proj3, gdn_conv_w[i], gdn_A_log[i], gdn_dt_bias[i], gdn_norm[i])
            mu_r, mu_k, mu_v, mu_w, mu_a, mu_g = jnp.split(
                rwkv_mu[i], np.cumsum([RWKV_DIM, RWKV_DIM, RWKV_DIM, 32, 32])[:].tolist())
            mu = jnp.stack([mu_r, mu_k, mu_v])
            lora_cols = [(0, mu_w), (LORA_SEG, mu_a), (2 * LORA_SEG, mu_g)]
            if i > 0:
                lora_cols.append((3 * LORA_SEG, rwkv_vres_mu[i - 1]))
            mu_lora = _place(lora_cols, 4 * LORA_SEG).reshape(1, 4 * LORA_SEG)
            vec_rows = [rwkv_w0[i], rwkv_a0[i], rwkv_k_k[i], rwkv_k_a[i], rwkv_r_k[i].reshape(-1), rwkv_ln_w[i], rwkv_ln_b[i]]
            if i > 0:
                vec_rows.append(rwkv_v0[i - 1])
            vecs = jnp.stack(vec_rows)
            w2 = _pad_rows(rwkv_w2[i], LORA_SEG)
            a2 = _pad_rows(rwkv_a2[i], LORA_SEG)
            g2 = _pad_rows(rwkv_g2[i], LORA_SEG)
            if i == 0:
                yb, v_first = rwkv_mixer(proj3, mu, mu_lora, vecs, w2, a2, g2)
            else:
                yb, _ = rwkv_mixer(proj3, mu, mu_lora, vecs, w2, a2, g2, v_first, _pad_rows(rwkv_v2[i - 1], LORA_SEG))
            w_out = ev_w_out[i].astype(BF16)
            na = GDN_HEADS * GDN_DV
            xf = out_proj_even(xf, ya.reshape(T, na), yb.reshape(T, RWKV_DIM), w_out[:na], w_out[na:])
        else:
            w_in = od_w_in[i].astype(BF16)
            qkv = norm_proj_rope(xf.reshape(B, S, D), od_norm[i], w_in, cos.reshape(B, S, LANES),
                                 sin.reshape(B, S, LANES), DIL_HEAD_DIM ** -0.5 * math.log2(math.e))
            outs, lses = [], []
            for gi, (window, dilation) in enumerate(DIL_PATTERNS):
                assert window // dilation == ATT_BLOCK and (S // dilation) % (2 * ATT_BLOCK) == 0
                o, lse = dil_attention_group(qkv[gi], qkv[n_groups + gi], qkv[2 * n_groups + gi])
                outs.append(o)
                lses.append(lse)
            xf = out_proj_odd(xf.reshape(B, S, D), outs, lses, od_w_out[i].astype(BF16)).reshape(T, D)
        w_router = _router_weights(moe_w_group[layer], moe_w_expert[layer])
        w2_flat = moe_w2[layer].astype(BF16).reshape(-1, D)
        xf = moe_block(xf, ffn_norm[layer], w_router, moe_w1[layer].astype(BF16), moe_w3[layer].astype(BF16),
                       w2_flat, final_norm if layer == depth - 1 else None)
    return xf.reshape(B, S, D)
```

```python
import functools
import math

import jax
import jax.numpy as jnp
import numpy as np
from jax import lax
from jax.experimental import pallas as pl
from jax.experimental.pallas import tpu as pltpu

F32 = jnp.float32
BF16 = jnp.bfloat16
HIGHEST = lax.Precision.HIGHEST

RMS_EPS = 1e-6
L2_EPS = 1e-6
RWKV_LN_EPS = 64e-5
ROPE_THETA = 10000.0

LANES = 128
SUBLANES = 8
VMEM_LIMIT = 56 * 1024 * 1024

GDN_HEADS = 4
GDN_DK = 128
GDN_DV = 128
GDN_CONV = 4
RWKV_HEADS = 8
RWKV_HEAD = 64
RWKV_DIM = RWKV_HEADS * RWKV_HEAD
LORA_SEG = LANES
DIL_PATTERNS = ((128, 1), (512, 4), (2048, 16))
DIL_HEADS = 8
DIL_HEAD_DIM = 64
DIL_WIDTH = DIL_HEADS * DIL_HEAD_DIM
ATT_BLOCK = 128
N_GROUPS = 4
EXPERTS_PER_GROUP = 4
N_EXPERTS = N_GROUPS * EXPERTS_PER_GROUP
CHUNK = 64

EV_Q, EV_K, EV_V, EV_Z = 0, 512, 1024, 1536
EV_R, EV_RK, EV_RV, EV_LORA = 2048, 2560, 3072, 3584
EV_AB = EV_LORA + 4 * LORA_SEG
EV_NPAD = EV_AB + LANES


def _cparams(sem):
    return pltpu.CompilerParams(dimension_semantics=sem, vmem_limit_bytes=VMEM_LIMIT)


def _sigmoid(x):
    return 1.0 / (1.0 + jnp.exp(-x))


def _softplus(x):
    return jnp.maximum(x, 0.0) + jnp.log(1.0 + jnp.exp(-jnp.abs(x)))


def _rmsnorm(x, gain):
    return x * lax.rsqrt(jnp.mean(x * x, axis=-1, keepdims=True) + RMS_EPS) * gain


def _resident(shape):
    nd = len(shape)
    return pl.BlockSpec(shape, lambda *_: (0,) * nd, pipeline_mode=pl.Buffered(1))


def _rope_table_kernel(pos_ref, freq_ref, sign_ref, cos_ref, sin_ref):
    ang = pos_ref[...].astype(F32) * freq_ref[...]
    cos_ref[...] = jnp.cos(ang)
    sin_ref[...] = jnp.sin(ang) * sign_ref[...]


def rope_tables(positions_flat, tm=1024):
    T = positions_flat.shape[0]
    half = DIL_HEAD_DIM // 2
    inv_freq = ROPE_THETA ** (-jnp.arange(half, dtype=F32) * 2.0 / DIL_HEAD_DIM)
    lane = np.arange(LANES)
    freq = inv_freq[lane % half][None, :]
    sign = jnp.asarray(np.where(lane % DIL_HEAD_DIM < half, -1.0, 1.0), F32)[None, :]
    return pl.pallas_call(
        _rope_table_kernel,
        grid=(T // tm,),
        in_specs=[pl.BlockSpec((tm, 1), lambda i: (i, 0)), _resident((1, LANES)), _resident((1, LANES))],
        out_specs=[pl.BlockSpec((tm, LANES), lambda i: (i, 0))] * 2,
        out_shape=[jax.ShapeDtypeStruct((T, LANES), F32)] * 2,
        compiler_params=_cparams(("parallel",)),
        name="rope_tables",
    )(positions_flat.reshape(T, 1), freq, sign)


def _norm_proj_kernel(x_ref, g_ref, w_ref, o_ref):
    h = _rmsnorm(x_ref[...], g_ref[...]).astype(BF16)
    o_ref[...] = jnp.dot(h, w_ref[...], preferred_element_type=F32)


def norm_proj(x, gain, w, tm=256):
    T, D = x.shape
    N = w.shape[1]
    return pl.pallas_call(
        _norm_proj_kernel,
        grid=(T // tm,),
        in_specs=[pl.BlockSpec((tm, D), lambda i: (i, 0)), _resident((1, D)), _resident((D, N))],
        out_specs=pl.BlockSpec((tm, N), lambda i: (i, 0)),
        out_shape=jax.ShapeDtypeStruct((T, N), F32),
        compiler_params=_cparams(("parallel",)),
        name="norm_proj",
    )(x, gain.reshape(1, D), w)


def _deinterleave_store(out_ref, val, d, a_ref, b_ref):
    tm = val.shape[0]
    if d == 1:
        out_ref[0, 0] = val.astype(out_ref.dtype)
        return
    quarter = tm // 4
    for j in range(val.shape[1] // LANES):
        cs = slice(j * LANES, (j + 1) * LANES)
        a_ref[j] = val[:, cs]
        if d == 4:
            for r in range(4):
                out_ref[0, r, :, cs] = a_ref[j, pl.ds(r, quarter, stride=4), :].astype(out_ref.dtype)
            continue
        assert d == 16
        for r1 in range(4):
            b_ref[j, r1 * quarter:(r1 + 1) * quarter, :] = a_ref[j, pl.ds(r1, quarter, stride=4), :]
        for r1 in range(4):
            for r2 in range(4):
                out_ref[0, r1 + 4 * r2, :, cs] = (
                    b_ref[j, pl.ds(r1 * quarter + r2, quarter // 4, stride=4), :].astype(out_ref.dtype))


def _interleave_load(in_ref, d, a_ref, b_ref):
    if d == 1:
        return in_ref[0, 0].astype(F32)
    n_tiles, tm, _ = a_ref.shape
    quarter = tm // 4
    for j in range(n_tiles):
        cs = slice(j * LANES, (j + 1) * LANES)
        if d == 4:
            for r in range(4):
                a_ref[j, pl.ds(r, quarter, stride=4), :] = in_ref[0, r, :, cs].astype(F32)
            continue
        assert d == 16
        for r1 in range(4):
            for r2 in range(4):
                b_ref[j, pl.ds(r1 * quarter + r2, quarter // 4, stride=4), :] = in_ref[0, r1 + 4 * r2, :, cs].astype(F32)
        for r1 in range(4):
            a_ref[j, pl.ds(r1, quarter, stride=4), :] = b_ref[j, r1 * quarter:(r1 + 1) * quarter, :]
    return jnp.concatenate([a_ref[j] for j in range(n_tiles)], axis=-1)


def _norm_proj_rope_kernel(x_ref, g_ref, w_ref, cos_ref, sin_ref, *rest, q_scale):
    out_refs, (a_ref, b_ref) = rest[:-2], rest[-2:]
    n_groups = len(DIL_PATTERNS)
    h = _rmsnorm(x_ref[0], g_ref[...]).astype(BF16)
    cos = cos_ref[0]
    sin = sin_ref[0]
    lane = lax.broadcasted_iota(jnp.int32, cos.shape, 1)
    first_half = (lane % DIL_HEAD_DIM) < (DIL_HEAD_DIM // 2)
    W = DIL_WIDTH
    for c, out_ref in enumerate(out_refs):
        which, g = divmod(c, n_groups)
        acc = jnp.dot(h, w_ref[:, c * W:(c + 1) * W], preferred_element_type=F32)
        if which < 2:
            parts = []
            for j in range(W // LANES):
                blk = acc[:, j * LANES:(j + 1) * LANES]
                rot = jnp.where(first_half, pltpu.roll(blk, LANES - 32, axis=1), pltpu.roll(blk, 32, axis=1))
                out = blk * cos + rot * sin
                parts.append(out * q_scale if which == 0 else out)
            acc = jnp.concatenate(parts, axis=-1)
        _deinterleave_store(out_ref, acc, DIL_PATTERNS[g][1], a_ref, b_ref)


def norm_proj_rope(x3, gain, w, cos3, sin3, q_scale, tm=512):
    B, S, D = x3.shape
    W = DIL_WIDTH
    out_specs, out_shape = [], []
    for _ in range(3):
        for _, d in DIL_PATTERNS:
            out_specs.append(pl.BlockSpec((1, d, tm // d, W), lambda b, i: (b, 0, i, 0)))
            out_shape.append(jax.ShapeDtypeStruct((B, d, S // d, W), BF16))
    return pl.pallas_call(
        functools.partial(_norm_proj_rope_kernel, q_scale=q_scale),
        grid=(B, S // tm),
        in_specs=[pl.BlockSpec((1, tm, D), lambda b, i: (b, i, 0)), _resident((1, D)), _resident(w.shape),
                  pl.BlockSpec((1, tm, LANES), lambda b, i: (b, i, 0)),
                  pl.BlockSpec((1, tm, LANES), lambda b, i: (b, i, 0))],
        out_specs=out_specs,
        out_shape=out_shape,
        scratch_shapes=[pltpu.VMEM((W // LANES, tm, LANES), F32), pltpu.VMEM((W // LANES, tm, LANES), F32)],
        compiler_params=_cparams(("parallel", "parallel")),
        name="norm_proj_rope",
    )(x3, gain.reshape(1, D), w, cos3, sin3)


def _bmm(a, b):
    return jnp.einsum('nik,nkj->nij', a.astype(BF16), b.astype(BF16), preferred_element_type=F32)


def _bmm_nt(a, b):
    return jnp.einsum('nik,njk->nij', a.astype(BF16), b.astype(BF16), preferred_element_type=F32)


def _bmm_tn(a, b):
    return jnp.einsum('nci,ncj->nij', a.astype(BF16), b.astype(BF16), preferred_element_type=F32)


def _unit_lower_inverse(L):
    C = L.shape[-1]
    ri = lax.broadcasted_iota(jnp.int32, (C, C), 0)
    ci = lax.broadcasted_iota(jnp.int32, (C, C), 1)
    same16 = (ri // 16) == (ci // 16)
    same32 = (ri // 32) == (ci // 32)
    eye = (ri == ci).astype(F32)
    Ld = jnp.where(same16, L, 0.0)
    X = eye - Ld
    P = Ld
    for _ in range(3):
        P = _bmm(P, P)
        X = X + _bmm(X, P)
    for off in (jnp.where(same32 & ~same16, L, 0.0), jnp.where(~same32, L, 0.0)):
        X = X - _bmm(_bmm(X, off), X)
    return X


def _shifted_rows(buf_ref, cur, prev, first, shifts):
    TB = cur.shape[0]
    buf_ref[0:SUBLANES, :] = jnp.where(first, 0.0, prev)
    buf_ref[SUBLANES:, :] = cur
    return [buf_ref[SUBLANES - s:SUBLANES - s + TB, :] for s in shifts]


def _split_heads(x, n_heads, width):
    return jnp.stack([x[:, h * width:(h + 1) * width] for h in range(n_heads)], axis=0)


def _merge_heads(x):
    return jnp.concatenate([x[h] for h in range(x.shape[0])], axis=-1)


def _gdn_kernel(q_ref, k_ref, v_ref, z_ref, ab_ref, qp_ref, kp_ref, vp_ref, cw_ref, alog_ref, dtb_ref, nw_ref,
                o_ref, buf_ref, s_ref, qd_ref, pg_ref, qg_ref, ov_ref, egl_ref, oc_ref):
    H, DK, DV, C = GDN_HEADS, GDN_DK, GDN_DV, CHUNK
    TB = q_ref.shape[1]
    NC = TB // C
    first = pl.program_id(1) == 0

    @pl.when(first)
    def _():
        s_ref[...] = jnp.zeros_like(s_ref)

    def conv_silu(cur_ref, prev_ref, j):
        cur = cur_ref[0]
        x3, x2, x1 = _shifted_rows(buf_ref, cur, prev_ref[0], first, (3, 2, 1))
        w = cw_ref[j]
        y = x3 * w[0:1] + x2 * w[1:2] + x1 * w[2:3] + cur * w[3:4]
        return y * _sigmoid(y)

    q = _split_heads(conv_silu(q_ref, qp_ref, 0), H, DK)
    k = _split_heads(conv_silu(k_ref, kp_ref, 1), H, DK)
    v = _split_heads(conv_silu(v_ref, vp_ref, 2), H, DV)
    q = q * lax.rsqrt(jnp.sum(q * q, axis=-1, keepdims=True) + L2_EPS) * (DK ** -0.5)
    k = k * lax.rsqrt(jnp.sum(k * k, axis=-1, keepdims=True) + L2_EPS)
    ab = ab_ref[0]
    a = jnp.stack([ab[:, h:h + 1] for h in range(H)], axis=0)
    b = jnp.stack([ab[:, H + h:H + h + 1] for h in range(H)], axis=0)
    beta = _sigmoid(b)
    g = -jnp.exp(alog_ref[...]) * _softplus(a + dtb_ref[...])

    N = H * NC
    q = q.reshape(N, C, DK)
    k = k.reshape(N, C, DK)
    v = v.reshape(N, C, DV)
    beta = beta.reshape(N, C, 1)
    g = g.reshape(N, C, 1)

    ri = lax.broadcasted_iota(jnp.int32, (C, C), 0)
    ci = lax.broadcasted_iota(jnp.int32, (C, C), 1)
    causal = ri >= ci
    strict = ri > ci
    gb = jnp.broadcast_to(g, (N, C, C))
    g_row = jnp.sum(jnp.where(ri == ci, gb, 0.0), axis=1, keepdims=True)
    gc_col = jnp.sum(jnp.where(causal, jnp.broadcast_to(g_row, (N, C, C)), 0.0), axis=2, keepdims=True)
    gc_row = jnp.sum(jnp.where(ri <= ci, gb, 0.0), axis=1, keepdims=True)
    decay = jnp.where(causal, jnp.exp(jnp.where(causal, gc_col - gc_row, 0.0)), 0.0)

    kb = k * beta
    vb = v * beta
    kq = _bmm_nt(jnp.concatenate([kb, q], axis=1), k)
    L = jnp.where(strict, kq[:, :C] * decay, 0.0)
    aqk = kq[:, C:] * decay
    tinv = _unit_lower_inverse(L)
    egc = jnp.exp(gc_col)
    uw = _bmm(tinv, jnp.concatenate([vb, kb * egc], axis=2))
    auw = _bmm(aqk, uw)
    g_last = gc_col[:, C - 1:C, :]
    k_dec = k * jnp.exp(g_last - gc_col)
    kuw = _bmm_tn(k_dec, uw)
    qd_ref[...] = (q * egc - auw[:, :, DV:]).reshape(H, NC, C, DK)
    ov_ref[...] = auw[:, :, :DV].reshape(H, NC, C, DV)
    pg_ref[...] = kuw[:, :, DV:].reshape(H, NC, DK, DK)
    qg_ref[...] = kuw[:, :, :DV].reshape(H, NC, DK, DV)
    egl_ref[...] = jnp.broadcast_to(jnp.exp(g_last), (N, 1, LANES)).reshape(H, NC, 1, LANES)

    for c in range(NC):
        S = s_ref[...]
        oc_ref[:, c] = _bmm(qd_ref[:, c], S) + ov_ref[:, c]
        s_ref[...] = S * egl_ref[:, c] - _bmm(pg_ref[:, c], S) + qg_ref[:, c]

    o = oc_ref[...].reshape(H, TB, DV)
    o = _rmsnorm(o, nw_ref[...])
    z = _split_heads(z_ref[0], H, DV)
    o_ref[0] = _merge_heads(o * (z * _sigmoid(z)))


def gdn_mixer(proj3, conv_w, a_log, dt_bias, norm_w, tb=512):
    B, S, _ = proj3.shape
    H, DK, DV, C = GDN_HEADS, GDN_DK, GDN_DV, CHUNK
    W = H * DK
    NC = tb // C
    nq, nk, nv, nz, nab = EV_Q // W, EV_K // W, EV_V // W, EV_Z // W, EV_AB // LANES
    rows8 = tb // SUBLANES

    def cur(cb):
        return pl.BlockSpec((1, tb, W), lambda b, i, cb=cb: (b, i, cb))

    def prev(cb):
        return pl.BlockSpec((1, SUBLANES, W), lambda b, i, cb=cb: (b, jnp.maximum(i * rows8 - 1, 0), cb))

    cw = conv_w.reshape(GDN_CONV, 3, W).transpose(1, 0, 2)
    return pl.pallas_call(
        _gdn_kernel,
        grid=(B, S // tb),
        in_specs=[cur(nq), cur(nk), cur(nv), cur(nz),
                  pl.BlockSpec((1, tb, LANES), lambda b, i: (b, i, nab)),
                  prev(nq), prev(nk), prev(nv),
                  _resident((3, GDN_CONV, W)), _resident((H, 1, 1)), _resident((H, 1, 1)), _resident((1, 1, DV))],
        out_specs=pl.BlockSpec((1, tb, H * DV), lambda b, i: (b, i, 0)),
        out_shape=jax.ShapeDtypeStruct((B, S, H * DV), F32),
        scratch_shapes=[pltpu.VMEM((tb + SUBLANES, W), F32),
                        pltpu.VMEM((H, DK, DV), F32),
                        pltpu.VMEM((H, NC, C, DK), F32),
                        pltpu.VMEM((H, NC, DK, DK), F32),
                        pltpu.VMEM((H, NC, DK, DV), F32),
                        pltpu.VMEM((H, NC, C, DV), F32),
                        pltpu.VMEM((H, NC, 1, LANES), F32),
                        pltpu.VMEM((H, NC, C, DV), F32)],
        compiler_params=_cparams(("parallel", "arbitrary")),
        name="gdn_mixer",
    )(proj3, proj3, proj3, proj3, proj3, proj3, proj3, proj3, cw,
      a_log.reshape(H, 1, 1), dt_bias.reshape(H, 1, 1), norm_w.reshape(1, 1, DV))


def _rwkv_kernel(*refs, has_vres):
    if has_vres:
        (r_ref, k_ref, v_ref, lo_ref, rp_ref, kp_ref, vp_ref, lop_ref, vf_ref,
         mu_ref, mul_ref, vec_ref, w2_ref, a2_ref, g2_ref, v2_ref,
         y_ref, buf_ref, bufl_ref, s_ref, rr_ref, pp_ref, qq_ref, yv_ref, egc_ref, yc_ref) = refs
    else:
        (r_ref, k_ref, v_ref, lo_ref, rp_ref, kp_ref, vp_ref, lop_ref,
         mu_ref, mul_ref, vec_ref, w2_ref, a2_ref, g2_ref,
         y_ref, vf_out_ref, buf_ref, bufl_ref, s_ref, rr_ref, pp_ref, qq_ref, yv_ref, egc_ref, yc_ref) = refs
    H, D, C = RWKV_HEADS, RWKV_HEAD, CHUNK
    TB = r_ref.shape[1]
    NC = TB // C
    first = pl.program_id(1) == 0

    @pl.when(first)
    def _():
        s_ref[...] = jnp.zeros_like(s_ref)

    def mix(cur_ref, prev_ref, mu, buf):
        cur = cur_ref[0]
        (sh,) = _shifted_rows(buf, cur, prev_ref[0], first, (1,))
        return cur + mu * (sh - cur)

    r = mix(r_ref, rp_ref, mu_ref[0:1], buf_ref)
    k = mix(k_ref, kp_ref, mu_ref[1:2], buf_ref)
    v = mix(v_ref, vp_ref, mu_ref[2:3], buf_ref)
    lo = mix(lo_ref, lop_ref, mul_ref[...], bufl_ref)
    hw, ha, hg = (lo[:, j * LORA_SEG:(j + 1) * LORA_SEG] for j in range(3))
    w0, a0, k_k, k_a, r_k, ln_w, ln_b = (vec_ref[j:j + 1] for j in range(7))

    w_log = -_softplus(-(w0 + jnp.dot(jnp.tanh(hw), w2_ref[...], preferred_element_type=F32))) - 0.5
    lw = -jnp.exp(w_log)
    a = _sigmoid(a0 + jnp.dot(ha, a2_ref[...], preferred_element_type=F32))
    gate = jnp.dot(_sigmoid(hg), g2_ref[...], preferred_element_type=F32)
    if has_vres:
        hv = lo[:, 3 * LORA_SEG:4 * LORA_SEG]
        v0 = vec_ref[7:8]
        v = v + (vf_ref[0] - v) * _sigmoid(v0 + jnp.dot(hv, v2_ref[...], preferred_element_type=F32))
    else:
        vf_out_ref[0] = v

    ri = lax.broadcasted_iota(jnp.int32, (C, C), 0)
    ci = lax.broadcasted_iota(jnp.int32, (C, C), 1)
    tril = (ri >= ci).astype(F32)
    gi = jnp.concatenate(
        [jnp.dot(tril, lw[c * C:(c + 1) * C], preferred_element_type=F32, precision=HIGHEST) for c in range(NC)], axis=0)

    kk = _split_heads(k * k_k, H, D)
    kk = kk * lax.rsqrt(jnp.sum(kk * kk, axis=-1, keepdims=True) + L2_EPS)
    k = k * (1.0 + (a - 1.0) * k_a)
    r_h = _split_heads(r, H, D)
    k_h = _split_heads(k, H, D)
    v_h = _split_heads(v, H, D)
    a_h = _split_heads(a, H, D)
    gi_h = _split_heads(gi, H, D)
    lw_h = _split_heads(lw, H, D)
    bonus = jnp.sum(r_h * k_h * _split_heads(r_k, H, D), axis=-1, keepdims=True) * v_h

    N = H * NC
    rc, kc, vc, kkc, ac, gic, lwc = (t.reshape(N, C, D) for t in (r_h, k_h, v_h, kk, a_h, gi_h, lw_h))
    g_end = gic[:, C - 1:C, :]
    e_in = jnp.exp(gic)
    e_neg = jnp.exp(-gic)
    e_tail = jnp.exp(g_end - gic)
    at = -kkc * jnp.exp(gic - lwc)
    bvec = kkc * ac
    rt = rc * e_in
    bt = bvec * e_neg
    kt = kc * e_neg
    strict = ri > ci
    causal = ri >= ci
    amat = _bmm_nt(jnp.concatenate([at, rt], axis=1), jnp.concatenate([bt, kt], axis=1))
    a_ab = jnp.where(strict, amat[:, :C, :C], 0.0)
    a_ak = jnp.where(strict, amat[:, :C, C:], 0.0)
    a_r = jnp.concatenate([jnp.where(causal, amat[:, C:, :C], 0.0), jnp.where(causal, amat[:, C:, C:], 0.0)], axis=2)
    tinv = _unit_lower_inverse(-a_ab)
    x2 = _bmm(tinv, jnp.concatenate([at, _bmm(a_ak, vc)], axis=2))
    xv = jnp.concatenate([x2, jnp.concatenate([jnp.zeros_like(vc), vc], axis=2)], axis=1)
    y2 = _bmm(a_r, xv)
    pq = _bmm_tn(xv, jnp.concatenate([bvec * e_tail, kc * e_tail], axis=1))
    rr_ref[...] = (rt + y2[:, :, :D]).reshape(H, NC, C, D)
    yv_ref[...] = y2[:, :, D:].reshape(H, NC, C, D)
    pp_ref[...] = pq[:, :D].reshape(H, NC, D, D)
    qq_ref[...] = pq[:, D:].reshape(H, NC, D, D)
    egc_ref[...] = jnp.exp(g_end).reshape(H, NC, 1, D)

    for c in range(NC):
        S = s_ref[...]
        yc_ref[:, c] = _bmm_nt(rr_ref[:, c], S) + yv_ref[:, c]
        s_ref[...] = S * egc_ref[:, c] + _bmm(S, pp_ref[:, c]) + qq_ref[:, c]

    y = yc_ref[...].reshape(H, TB, D)
    mean = jnp.mean(y, axis=-1, keepdims=True)
    yc = y - mean
    var = jnp.mean(yc * yc, axis=-1, keepdims=True)
    y = _merge_heads(yc * lax.rsqrt(var + RWKV_LN_EPS)) * ln_w + ln_b
    y_ref[0] = (y + _merge_heads(bonus)) * gate


def rwkv_mixer(proj3, mu, mu_lora, vecs, w2, a2, g2, v_first=None, v2=None, tb=256):
    B, S, _ = proj3.shape
    H, D, C = RWKV_HEADS, RWKV_HEAD, CHUNK
    W = RWKV_DIM
    WL = 4 * LORA_SEG
    NC = tb // C
    rows8 = tb // SUBLANES
    has_vres = v_first is not None

    def cur(col, width):
        return pl.BlockSpec((1, tb, width), lambda b, i, cb=col // width: (b, i, cb))

    def prev(col, width):
        return pl.BlockSpec((1, SUBLANES, width),
                            lambda b, i, cb=col // width: (b, jnp.maximum(i * rows8 - 1, 0), cb))

    in_specs = [cur(EV_R, W), cur(EV_RK, W), cur(EV_RV, W), cur(EV_LORA, WL),
                prev(EV_R, W), prev(EV_RK, W), prev(EV_RV, W), prev(EV_LORA, WL)]
    args = [proj3] * 8
    if has_vres:
        in_specs.append(pl.BlockSpec((1, tb, W), lambda b, i: (b, i, 0)))
        args.append(v_first)
    in_specs += [_resident(mu.shape), _resident(mu_lora.shape), _resident(vecs.shape),
                 _resident(w2.shape), _resident(a2.shape), _resident(g2.shape)]
    args += [mu, mu_lora, vecs, w2, a2, g2]
    if has_vres:
        in_specs.append(_resident(v2.shape))
        args.append(v2)
    out_block = pl.BlockSpec((1, tb, W), lambda b, i: (b, i, 0))
    out_sds = jax.ShapeDtypeStruct((B, S, W), F32)
    res = pl.pallas_call(
        functools.partial(_rwkv_kernel, has_vres=has_vres),
        grid=(B, S // tb),
        in_specs=in_specs,
        out_specs=out_block if has_vres else [out_block, out_block],
        out_shape=out_sds if has_vres else [out_sds, out_sds],
        scratch_shapes=[pltpu.VMEM((tb + SUBLANES, W), F32),
                        pltpu.VMEM((tb + SUBLANES, WL), F32),
                        pltpu.VMEM((H, D, D), F32),
                        pltpu.VMEM((H, NC, C, D), F32),
                        pltpu.VMEM((H, NC, D, D), F32),
                        pltpu.VMEM((H, NC, D, D), F32),
                        pltpu.VMEM((H, NC, C, D), F32),
                        pltpu.VMEM((H, NC, 1, D), F32),
                        pltpu.VMEM((H, NC, C, D), F32)],
        compiler_params=_cparams(("parallel", "arbitrary")),
        name="rwkv_mixer",
    )(*args)
    if has_vres:
        return res, v_first
    return res[0], res[1]


def _dil_attn_kernel(q_ref, kc_ref, kp_ref, vc_ref, vp_ref, o_ref, lse_ref, kbuf, vbuf):
    Dh, Bk = DIL_HEAD_DIM, ATT_BLOCK
    QB = q_ref.shape[2]
    has_prev = pl.program_id(2) > 0
    kbuf[0:Bk] = kp_ref[0, 0]
    kbuf[Bk:] = kc_ref[0, 0]
    vbuf[0:Bk] = vp_ref[0, 0]
    vbuf[Bk:] = vc_ref[0, 0]
    qi = lax.broadcasted_iota(jnp.int32, (Bk, 2 * Bk), 0)
    cj = lax.broadcasted_iota(jnp.int32, (Bk, 2 * Bk), 1)
    band = (cj >= qi) & (cj <= qi + Bk)
    lane = lax.broadcasted_iota(jnp.int32, (Bk, LANES), 1)
    low = lane < Dh
    ones = jnp.ones((2 * Bk, LANES), BF16)
    NP = DIL_WIDTH // LANES

    def pairs(x):
        return jnp.stack([x[:, hp * LANES:(hp + 1) * LANES] for hp in range(NP)], axis=0)

    for j in range(QB // Bk):
        ok = band if j > 0 else band & ((cj >= Bk) | has_prev)
        ok2 = jnp.concatenate([ok, ok], axis=0)
        q_pairs = pairs(q_ref[0, 0, j * Bk:(j + 1) * Bk, :])
        zero = jnp.zeros_like(q_pairs)
        q2 = jnp.concatenate([jnp.where(low, q_pairs, zero), jnp.where(low, zero, q_pairs)], axis=1)
        k_pairs = pairs(kbuf[j * Bk:(j + 2) * Bk, :])
        v_pairs = pairs(vbuf[j * Bk:(j + 2) * Bk, :])
        s = jnp.einsum('hqd,hkd->hqk', q2, k_pairs, preferred_element_type=F32)
        s = jnp.where(ok2, s, -jnp.inf)
        m = jnp.max(s, axis=-1, keepdims=True)
        p = jnp.exp2(s - m).astype(BF16)
        po = jnp.einsum('hqk,hkd->hqd', p, v_pairs, preferred_element_type=F32)
        l = jnp.dot(p.reshape(NP * 2 * Bk, 2 * Bk), ones, preferred_element_type=F32)
        l = l.reshape(NP, 2 * Bk, LANES)
        r = po / l
        o_pairs = jnp.where(low, r[:, :Bk], r[:, Bk:])
        o_ref[0, 0, j * Bk:(j + 1) * Bk, :] = jnp.concatenate([o_pairs[hp] for hp in range(NP)], axis=-1).astype(o_ref.dtype)
        lse = m + jnp.log2(l)
        lse_blk = jnp.zeros((Bk, LANES), F32)
        for hp in range(NP):
            for half in range(2):
                lse_blk = jnp.where(lane == 2 * hp + half, lse[hp, half * Bk:(half + 1) * Bk], lse_blk)
        lse_ref[0, 0, j * Bk:(j + 1) * Bk, :] = lse_blk


def dil_attention_group(q, k, v, qb=2 * ATT_BLOCK):
    B, d, L, W = q.shape
    nj = qb // ATT_BLOCK
    cur = pl.BlockSpec((1, 1, qb, W), lambda b, r, n: (b, r, n, 0))
    prev = pl.BlockSpec((1, 1, ATT_BLOCK, W), lambda b, r, n: (b, r, jnp.maximum(n * nj - 1, 0), 0))
    return pl.pallas_call(
        _dil_attn_kernel,
        grid=(B, d, L // qb),
        in_specs=[cur, cur, prev, cur, prev],
        out_specs=[cur, pl.BlockSpec((1, 1, qb, LANES), lambda b, r, n: (b, r, n, 0))],
        out_shape=[jax.ShapeDtypeStruct((B, d, L, W), BF16), jax.ShapeDtypeStruct((B, d, L, LANES), F32)],
        scratch_shapes=[pltpu.VMEM((ATT_BLOCK + qb, W), BF16), pltpu.VMEM((ATT_BLOCK + qb, W), BF16)],
        compiler_params=_cparams(("parallel", "parallel", "arbitrary")),
        name=f"dil_attn_d{d}",
    )(q, k, k, v, v)


def _out_proj_even_kernel(x_ref, ya_ref, yb_ref, wa_ref, wb_ref, o_ref):
    o_ref[...] = (x_ref[...]
                  + jnp.dot(ya_ref[...].astype(BF16), wa_ref[...], preferred_element_type=F32)
                  + jnp.dot(yb_ref[...].astype(BF16), wb_ref[...], preferred_element_type=F32))


def out_proj_even(x, ya, yb, wa, wb, tm=512):
    T, D = x.shape
    return pl.pallas_call(
        _out_proj_even_kernel,
        grid=(T // tm,),
        in_specs=[pl.BlockSpec((tm, D), lambda i: (i, 0)),
                  pl.BlockSpec((tm, ya.shape[1]), lambda i: (i, 0)),
                  pl.BlockSpec((tm, yb.shape[1]), lambda i: (i, 0)),
                  _resident(wa.shape), _resident(wb.shape)],
        out_specs=pl.BlockSpec((tm, D), lambda i: (i, 0)),
        out_shape=jax.ShapeDtypeStruct((T, D), F32),
        compiler_params=_cparams(("parallel",)),
        name="out_proj_even",
    )(x, ya, yb, wa, wb)


def _out_proj_odd_kernel(x_ref, o0_ref, o1_ref, o2_ref, l0_ref, l1_ref, l2_ref, e_ref, w_ref, out_ref,
                         a_ref, b_ref, la_ref, lb_ref):
    dils = [d for _, d in DIL_PATTERNS]
    lses = [_interleave_load(r, d, la_ref, lb_ref) for r, d in zip((l0_ref, l1_ref, l2_ref), dils)]
    m = jnp.maximum(jnp.maximum(lses[0], lses[1]), lses[2])
    es = [jnp.exp2(l - m) for l in lses]
    inv = 1.0 / (es[0] + es[1] + es[2])
    expand = e_ref[...]
    mixed = None
    for e, o_ref, d in zip(es, (o0_ref, o1_ref, o2_ref), dils):
        wide = jnp.dot(e * inv, expand, preferred_element_type=F32, precision=HIGHEST)
        term = wide * _interleave_load(o_ref, d, a_ref, b_ref)
        mixed = term if mixed is None else mixed + term
    out_ref[0] = x_ref[0] + jnp.dot(mixed.astype(BF16), w_ref[...], preferred_element_type=F32)


def out_proj_odd(x3, outs, lses, w, tm=512):
    B, S, D = x3.shape
    W = DIL_WIDTH
    expand = np.zeros((LANES, W), np.float32)
    for h in range(DIL_HEADS):
        expand[h, h * DIL_HEAD_DIM:(h + 1) * DIL_HEAD_DIM] = 1.0
    row = pl.BlockSpec((1, tm, D), lambda b, i: (b, i, 0))

    def res(width):
        return [pl.BlockSpec((1, d, tm // d, width), lambda b, i: (b, 0, i, 0)) for _, d in DIL_PATTERNS]

    return pl.pallas_call(
        _out_proj_odd_kernel,
        grid=(B, S // tm),
        in_specs=[row, *res(W), *res(LANES), _resident((LANES, W)), _resident(w.shape)],
        out_specs=row,
        out_shape=jax.ShapeDtypeStruct((B, S, D), F32),
        scratch_shapes=[pltpu.VMEM((W // LANES, tm, LANES), F32), pltpu.VMEM((W // LANES, tm, LANES), F32),
                        pltpu.VMEM((1, tm, LANES), F32), pltpu.VMEM((1, tm, LANES), F32)],
        compiler_params=_cparams(("parallel", "parallel")),
        name="out_proj_odd",
    )(x3, *outs, *lses, jnp.asarray(expand), w)


MOE_TM = 512
MOE_RB = 128
MOE_NBLK = (MOE_TM + N_GROUPS * (MOE_RB - 1)) // MOE_RB
ROUTE_ROWS = 16
LOGIT_ROWS = 24


def _hi_lo(v):
    hi = v.astype(BF16)
    return hi, (v - hi.astype(F32)).astype(BF16)


def _moe_router_kernel(x_ref, g_ref, wr_ref, route_ref, nblk_ref):
    NG, EPG, RB = N_GROUPS, EXPERTS_PER_GROUP, MOE_RB
    tm = x_ref.shape[0]
    h = _rmsnorm(x_ref[...], g_ref[...])
    hb, h_lo = _hi_lo(h)
    nt = (((1,), (1,)), ((), ()))
    lg = lax.dot_general(wr_ref[...], hb, nt, preferred_element_type=F32)
    lg = lg[:LANES] + lg[LANES:] + lax.dot_general(wr_ref[:LANES], h_lo, nt, preferred_element_type=F32)
    L = lg[:LOGIT_ROWS]
    r = lax.broadcasted_iota(jnp.int32, L.shape, 0)
    big = jnp.int32(1 << 30)
    gl = jnp.where(r < NG, L, -jnp.inf)
    gmax = jnp.max(gl, axis=0, keepdims=True)
    gsel = jnp.min(jnp.where(gl == gmax, r, big), axis=0, keepdims=True)
    p_group = 1.0 / jnp.sum(jnp.where(r < NG, jnp.exp(L - gmax), 0.0), axis=0, keepdims=True)
    lo = NG + gsel * EPG
    el = jnp.where((r >= lo) & (r < lo + EPG), L, -jnp.inf)
    v1 = jnp.max(el, axis=0, keepdims=True)
    i1 = jnp.min(jnp.where(el == v1, r, big), axis=0, keepdims=True)
    el2 = jnp.where(r == i1, -jnp.inf, el)
    v2 = jnp.max(el2, axis=0, keepdims=True)
    i2 = jnp.min(jnp.where(el2 == v2, r, big), axis=0, keepdims=True)
    t = jnp.exp(v2 - v1)
    gates = jnp.where(r == i1, p_group / (1.0 + t), 0.0) + jnp.where(r == i2, p_group * t / (1.0 + t), 0.0)
    gate4 = [jnp.sum(jnp.where(r == lo + j, gates, 0.0), axis=0, keepdims=True) for j in range(EPG)]

    r8 = lax.broadcasted_iota(jnp.int32, (SUBLANES, tm), 0)
    onehot = (r8 == gsel).astype(F32)
    ti = lax.broadcasted_iota(jnp.int32, (tm, tm), 0)
    tj = lax.broadcasted_iota(jnp.int32, (tm, tm), 1)
    before = (ti < tj).astype(BF16)
    rank = jnp.dot(onehot.astype(BF16), before, preferred_element_type=F32)
    count = jnp.sum(onehot, axis=1, keepdims=True).astype(jnp.int32)
    assert RB & (RB - 1) == 0
    nblk = lax.shift_right_logical(count + (RB - 1), jnp.int32(RB.bit_length() - 1))
    padded = (nblk * RB).astype(F32)
    start, starts = jnp.zeros((1, 1), F32), []
    for g in range(NG):
        starts.append(start)
        start = start + padded[g:g + 1]
    start8 = jnp.concatenate(starts + [jnp.zeros((SUBLANES - NG, 1), F32)], axis=0)
    dest = jnp.sum(onehot * (start8 + rank), axis=0, keepdims=True)
    zeros7 = jnp.zeros((SUBLANES - 1, tm), F32)
    route_ref[0] = jnp.concatenate([dest, zeros7] + gate4 + [jnp.zeros((SUBLANES - EPG, tm), F32)], axis=0)
    nblk_ref[0] = jnp.broadcast_to(nblk, (SUBLANES, LANES))


def _moe_ffn_kernel(nblk_ref, x_ref, g_ref, route_ref, w1_ref, w3_ref, w2_ref, fg_ref, o_ref,
                    p_ref, xs_ref, gs_ref, hh_ref, ys_ref, *, final_norm):
    NG, EPG, RB, NBLK = N_GROUPS, EXPERTS_PER_GROUP, MOE_RB, MOE_NBLK
    FF = w1_ref.shape[2]
    tm = x_ref.shape[0]
    i = pl.program_id(0)
    x = x_ref[...]
    hb = _rmsnorm(x, g_ref[...]).astype(BF16)
    route = route_ref[0]
    dest = route[0:1].astype(jnp.int32)
    rows = lax.broadcasted_iota(jnp.int32, (NBLK * RB, tm), 0)
    p_ref[...] = jnp.where(rows == dest, 1.0, 0.0).astype(BF16)
    xs_ref[...] = jnp.dot(p_ref[...], hb, preferred_element_type=F32).astype(BF16)
    g4 = route[SUBLANES:2 * SUBLANES]
    g_hi = g4.astype(BF16).astype(F32)
    g2 = jnp.concatenate([g_hi, g4 - g_hi, jnp.zeros((LANES - 2 * SUBLANES, tm), F32)], axis=0).astype(BF16)
    gs_ref[...] = lax.dot_general(p_ref[...], g2, (((1,), (1,)), ((), ())), preferred_element_type=F32)

    ends, acc = [], jnp.int32(0)
    for g in range(NG):
        acc = acc + nblk_ref[i * NG + g]
        ends.append(acc)
    for b in range(NBLK):
        rs = slice(b * RB, (b + 1) * RB)
        grp = sum((b >= e).astype(jnp.int32) for e in ends[:-1])

        @pl.when(b < ends[-1])
        def _():
            xr = xs_ref[rs]
            gate = gs_ref[rs]
            for j in range(EPG):
                a = jnp.dot(xr, w1_ref[grp * EPG + j], preferred_element_type=F32)
                u = jnp.dot(xr, w3_ref[grp * EPG + j], preferred_element_type=F32)
                gj = gate[:, j:j + 1] + gate[:, SUBLANES + j:SUBLANES + j + 1]
                hh_ref[:, j * FF:(j + 1) * FF] = ((a * _sigmoid(a)) * u * gj).astype(BF16)
            ys_ref[rs] = jnp.dot(hh_ref[...], w2_ref[grp], preferred_element_type=F32).astype(BF16)

        @pl.when(b >= ends[-1])
        def _():
            ys_ref[rs] = jnp.zeros((RB, ys_ref.shape[1]), BF16)

    y = x + lax.dot_general(p_ref[...], ys_ref[...], (((0,), (0,)), ((), ())), preferred_element_type=F32)
    if final_norm:
        y = _rmsnorm(y, fg_ref[...])
    o_ref[...] = y


def moe_block(x, gain, w_router, w1, w3, w2, final_gain=None):
    T, D = x.shape
    NE, _, FF = w1.shape
    tm, RB, NBLK = MOE_TM, MOE_RB, MOE_NBLK
    nt = T // tm
    route, nblk = pl.pallas_call(
        _moe_router_kernel,
        grid=(nt,),
        in_specs=[pl.BlockSpec((tm, D), lambda i: (i, 0)), _resident((1, D)), _resident(w_router.shape)],
        out_specs=[pl.BlockSpec((1, ROUTE_ROWS, tm), lambda i: (i, 0, 0)),
                   pl.BlockSpec((1, SUBLANES, LANES), lambda i: (i, 0, 0))],
        out_shape=[jax.ShapeDtypeStruct((nt, ROUTE_ROWS, tm), F32),
                   jax.ShapeDtypeStruct((nt, SUBLANES, LANES), jnp.int32)],
        compiler_params=_cparams(("parallel",)),
        name="moe_router",
    )(x, gain.reshape(1, D), w_router)
    nblk_flat = nblk[:, :N_GROUPS, 0].reshape(-1)
    final_norm = final_gain is not None
    fg = (final_gain if final_norm else jnp.ones((D,), F32)).reshape(1, D)

    def res(shape):
        nd = len(shape)
        return pl.BlockSpec(shape, lambda i, nb: (0,) * nd, pipeline_mode=pl.Buffered(1))

    return pl.pallas_call(
        functools.partial(_moe_ffn_kernel, final_norm=final_norm),
        grid_spec=pltpu.PrefetchScalarGridSpec(
            num_scalar_prefetch=1,
            grid=(nt,),
            in_specs=[pl.BlockSpec((tm, D), lambda i, nb: (i, 0)), res((1, D)),
                      pl.BlockSpec((1, ROUTE_ROWS, tm), lambda i, nb: (i, 0, 0)),
                      res(w1.shape), res(w3.shape), res(w2.shape), res((1, D))],
            out_specs=pl.BlockSpec((tm, D), lambda i, nb: (i, 0)),
            scratch_shapes=[pltpu.VMEM((NBLK * RB, tm), BF16), pltpu.VMEM((NBLK * RB, D), BF16),
                            pltpu.VMEM((NBLK * RB, LANES), F32), pltpu.VMEM((RB, EXPERTS_PER_GROUP * FF), BF16),
                            pltpu.VMEM((NBLK * RB, D), BF16)]),
        out_shape=jax.ShapeDtypeStruct((T, D), F32),
        compiler_params=_cparams(("parallel",)),
        name="moe_ffn",
    )(nblk_flat, x, gain.reshape(1, D), route, w1, w3, w2, fg)


def _router_weights(w_group, w_expert):
    w = _place([(0, w_group)] + [(N_GROUPS + g * EXPERTS_PER_GROUP, w_expert[g]) for g in range(N_GROUPS)], LANES).T
    return jnp.concatenate(_hi_lo(w), axis=0)


def _place(cols, total):
    lead = cols[0][1].shape[:-1]
    parts, pos = [], 0
    for off, arr in sorted(cols, key=lambda c: c[0]):
        if off > pos:
            parts.append(jnp.zeros(lead + (off - pos,), F32))
        parts.append(arr)
        pos = off + arr.shape[-1]
    if total > pos:
        parts.append(jnp.zeros(lead + (total - pos,), F32))
    return jnp.concatenate(parts, axis=-1)


def _even_layout(t, vres=None):
    gq, gk, gv, gz, ga, gb, rr, rk, rv, lw, la, lg = jnp.split(
        t, np.cumsum([512, 512, 512, 512, 4, 4, 512, 512, 512, 32, 32, 96])[:-1].tolist(), axis=-1)
    cols = [(EV_Q, gq), (EV_K, gk), (EV_V, gv), (EV_Z, gz), (EV_AB, ga), (EV_AB + GDN_HEADS, gb),
            (EV_R, rr), (EV_RK, rk), (EV_RV, rv),
            (EV_LORA, lw), (EV_LORA + LORA_SEG, la), (EV_LORA + 2 * LORA_SEG, lg)]
    if vres is not None:
        cols.append((EV_LORA + 3 * LORA_SEG, vres))
    return _place(cols, EV_NPAD)


def _pad_rows(w, rows):
    return jnp.concatenate([w, jnp.zeros((rows - w.shape[0], w.shape[1]), F32)], axis=0)


def kernel(x, positions, ev_norm, ev_w_in, rwkv_vres_down, ev_w_out, gdn_conv_w, gdn_A_log, gdn_dt_bias, gdn_norm,
           rwkv_mu, rwkv_w0, rwkv_w2, rwkv_a0, rwkv_a2, rwkv_g2, rwkv_k_k, rwkv_k_a, rwkv_r_k, rwkv_ln_w, rwkv_ln_b,
           rwkv_vres_mu, rwkv_v0, rwkv_v2, od_norm, od_w_in, od_w_out, ffn_norm, moe_w_group, moe_w_expert,
           moe_w1, moe_w3, moe_w2, final_norm):
    B, S, D = x.shape
    T = B * S
    depth = ffn_norm.shape[0]
    xf = x.reshape(T, D)
    cos, sin = rope_tables(positions.reshape(T))
    n_groups = len(DIL_PATTERNS)
    v_first = None
    for layer in range(depth):
        i = layer // 2
        if layer % 2 == 0:
            vres_w = None if i == 0 else rwkv_vres_down[i - 1]
            w_in = _even_layout(ev_w_in[i], vres_w).astype(BF16)
            proj3 = norm_proj(xf, ev_norm[i], w_in).reshape(B, S, EV_NPAD)
            ya = gdn_mixer(proj3, gdn_conv_w[i], gdn_A_log[i], gdn_dt_bias[i], gdn_norm[i])
            mu_r, mu_k, mu_v, mu_w, mu_a, mu_g = jnp.split(
                rwkv_mu[i], np.cumsum([RWKV_DIM, RWKV_DIM, RWKV_DIM, 32, 32])[:].tolist())
            mu = jnp.stack([mu_r, mu_k, mu_v])
            lora_cols = [(0, mu_w), (LORA_SEG, mu_a), (2 * LORA_SEG, mu_g)]
            if i > 0:
                lora_cols.append((3 * LORA_SEG, rwkv_vres_mu[i - 1]))
            mu_lora = _place(lora_cols, 4 * LORA_SEG).reshape(1, 4 * LORA_SEG)
            vec_rows = [rwkv_w0[i], rwkv_a0[i], rwkv_k_k[i], rwkv_k_a[i], rwkv_r_k[i].reshape(-1), rwkv_ln_w[i], rwkv_ln_b[i]]
            if i > 0:
                vec_rows.append(rwkv_v0[i - 1])
            vecs = jnp.stack(vec_rows)
            w2 = _pad_rows(rwkv_w2[i], LORA_SEG)
            a2 = _pad_rows(rwkv_a2[i], LORA_SEG)
            g2 = _pad_rows(rwkv_g2[i], LORA_SEG)
            if i == 0:
                yb, v_first = rwkv_mixer(proj3, mu, mu_lora, vecs, w2, a2, g2)
            else:
                yb, _ = rwkv_mixer(proj3, mu, mu_lora, vecs, w2, a2, g2, v_first, _pad_rows(rwkv_v2[i - 1], LORA_SEG))
            w_out = ev_w_out[i].astype(BF16)
            na = GDN_HEADS * GDN_DV
            xf = out_proj_even(xf, ya.reshape(T, na), yb.reshape(T, RWKV_DIM), w_out[:na], w_out[na:])
        else:
            w_in = od_w_in[i].astype(BF16)
            qkv = norm_proj_rope(xf.reshape(B, S, D), od_norm[i], w_in, cos.reshape(B, S, LANES),
                                 sin.reshape(B, S, LANES), DIL_HEAD_DIM ** -0.5 * math.log2(math.e))
            outs, lses = [], []
            for gi, (window, dilation) in enumerate(DIL_PATTERNS):
                assert window // dilation == ATT_BLOCK and (S // dilation) % (2 * ATT_BLOCK) == 0
                o, lse = dil_attention_group(qkv[gi], qkv[n_groups + gi], qkv[2 * n_groups + gi])
                outs.append(o)
                lses.append(lse)
            xf = out_proj_odd(xf.reshape(B, S, D), outs, lses, od_w_out[i].astype(BF16)).reshape(T, D)
        w_router = _router_weights(moe_w_group[layer], moe_w_expert[layer])
        w2_grouped = moe_w2[layer].astype(BF16).reshape(N_GROUPS, -1, D)
        xf = moe_block(xf, ffn_norm[layer], w_router, moe_w1[layer].astype(BF16), moe_w3[layer].astype(BF16),
                       w2_grouped, final_norm if layer == depth - 1 else None)
    return xf.reshape(B, S, D)
```

```python
import functools
import math

import jax
import jax.numpy as jnp
import numpy as np
from jax import lax
from jax.experimental import pallas as pl
from jax.experimental.pallas import tpu as pltpu

F32 = jnp.float32
BF16 = jnp.bfloat16

RMS_EPS = 1e-6
L2_EPS = 1e-6
RWKV_LN_EPS = 64e-5
ROPE_THETA = 10000.0

LANES = 128
SUBLANES = 8
VMEM_LIMIT = 56 * 1024 * 1024

GDN_HEADS = 4
GDN_DK = 128
GDN_DV = 128
GDN_CONV = 4
RWKV_HEADS = 8
RWKV_HEAD = 64
RWKV_DIM = RWKV_HEADS * RWKV_HEAD
LORA_SEG = LANES
DIL_PATTERNS = ((128, 1), (512, 4), (2048, 16))
DIL_HEADS = 8
DIL_HEAD_DIM = 64
DIL_WIDTH = DIL_HEADS * DIL_HEAD_DIM
ATT_BLOCK = 128
N_GROUPS = 4
EXPERTS_PER_GROUP = 4
N_EXPERTS = N_GROUPS * EXPERTS_PER_GROUP
CHUNK = 64

EV_Q, EV_K, EV_V, EV_Z = 0, 512, 1024, 1536
EV_R, EV_RK, EV_RV, EV_LORA = 2048, 2560, 3072, 3584
EV_AB = EV_LORA + 4 * LORA_SEG
EV_NPAD = EV_AB + LANES


def _cparams(sem):
    return pltpu.CompilerParams(dimension_semantics=sem, vmem_limit_bytes=VMEM_LIMIT)


def _sigmoid(x):
    return 0.5 * jnp.tanh(0.5 * x) + 0.5


def _softplus(x):
    return jnp.maximum(x, 0.0) + jnp.log(1.0 + jnp.exp(-jnp.abs(x)))


def _rmsnorm(x, gain):
    return x * lax.rsqrt(jnp.mean(x * x, axis=-1, keepdims=True) + RMS_EPS) * gain


def _hi_lo(v):
    hi = v.astype(BF16)
    return hi, (v - hi.astype(F32)).astype(BF16)


def _group_sum(x, block_ones):
    hi, lo = _hi_lo(x)
    return (jnp.dot(hi, block_ones, preferred_element_type=F32)
            + jnp.dot(lo, block_ones, preferred_element_type=F32))


def _block_ones(width, group):
    idx = np.arange(width) // group
    return jnp.asarray(idx[:, None] == idx[None, :], BF16)


def _resident(shape):
    nd = len(shape)
    return pl.BlockSpec(shape, lambda *_: (0,) * nd, pipeline_mode=pl.Buffered(1))


def _rope_table_kernel(pos_ref, freq_ref, sign_ref, cos_ref, sin_ref):
    ang = pos_ref[...].astype(F32) * freq_ref[...]
    cos_ref[...] = jnp.cos(ang)
    sin_ref[...] = jnp.sin(ang) * sign_ref[...]


def rope_tables(positions_flat, tm=1024):
    T = positions_flat.shape[0]
    half = DIL_HEAD_DIM // 2
    inv_freq = ROPE_THETA ** (-jnp.arange(half, dtype=F32) * 2.0 / DIL_HEAD_DIM)
    lane = np.arange(LANES)
    freq = inv_freq[lane % half][None, :]
    sign = jnp.asarray(np.where(lane % DIL_HEAD_DIM < half, -1.0, 1.0), F32)[None, :]
    return pl.pallas_call(
        _rope_table_kernel,
        grid=(T // tm,),
        in_specs=[pl.BlockSpec((tm, 1), lambda i: (i, 0)), _resident((1, LANES)), _resident((1, LANES))],
        out_specs=[pl.BlockSpec((tm, LANES), lambda i: (i, 0))] * 2,
        out_shape=[jax.ShapeDtypeStruct((T, LANES), F32)] * 2,
        compiler_params=_cparams(("parallel",)),
        name="rope_tables",
    )(positions_flat.reshape(T, 1), freq, sign)


def _norm_proj_kernel(x_ref, g_ref, w_ref, o_ref):
    h = _rmsnorm(x_ref[...], g_ref[...]).astype(BF16)
    o_ref[...] = jnp.dot(h, w_ref[...], preferred_element_type=F32)


def norm_proj(x, gain, w, tm=512):
    T, D = x.shape
    N = w.shape[1]
    return pl.pallas_call(
        _norm_proj_kernel,
        grid=(T // tm,),
        in_specs=[pl.BlockSpec((tm, D), lambda i: (i, 0)), _resident((1, D)), _resident((D, N))],
        out_specs=pl.BlockSpec((tm, N), lambda i: (i, 0)),
        out_shape=jax.ShapeDtypeStruct((T, N), F32),
        compiler_params=_cparams(("parallel",)),
        name="norm_proj",
    )(x, gain.reshape(1, D), w)


def _deinterleave_store(out_ref, val, d, a_ref, b_ref):
    tm = val.shape[0]
    if d == 1:
        out_ref[0, 0] = val.astype(out_ref.dtype)
        return
    quarter = tm // 4
    for j in range(val.shape[1] // LANES):
        cs = slice(j * LANES, (j + 1) * LANES)
        a_ref[j] = val[:, cs]
        if d == 4:
            for r in range(4):
                out_ref[0, r, :, cs] = a_ref[j, pl.ds(r, quarter, stride=4), :].astype(out_ref.dtype)
            continue
        assert d == 16
        for r1 in range(4):
            b_ref[j, r1 * quarter:(r1 + 1) * quarter, :] = a_ref[j, pl.ds(r1, quarter, stride=4), :]
        for r1 in range(4):
            for r2 in range(4):
                out_ref[0, r1 + 4 * r2, :, cs] = (
                    b_ref[j, pl.ds(r1 * quarter + r2, quarter // 4, stride=4), :].astype(out_ref.dtype))


def _interleave_load(in_ref, d, a_ref, b_ref):
    if d == 1:
        return in_ref[0, 0].astype(F32)
    n_tiles, tm, _ = a_ref.shape
    quarter = tm // 4
    for j in range(n_tiles):
        cs = slice(j * LANES, (j + 1) * LANES)
        if d == 4:
            for r in range(4):
                a_ref[j, pl.ds(r, quarter, stride=4), :] = in_ref[0, r, :, cs].astype(F32)
            continue
        assert d == 16
        for r1 in range(4):
            for r2 in range(4):
                b_ref[j, pl.ds(r1 * quarter + r2, quarter // 4, stride=4), :] = in_ref[0, r1 + 4 * r2, :, cs].astype(F32)
        for r1 in range(4):
            a_ref[j, pl.ds(r1, quarter, stride=4), :] = b_ref[j, r1 * quarter:(r1 + 1) * quarter, :]
    return jnp.concatenate([a_ref[j] for j in range(n_tiles)], axis=-1)


def _norm_proj_rope_kernel(x_ref, g_ref, w_ref, cos_ref, sin_ref, *rest, q_scale):
    out_refs, (a_ref, b_ref) = rest[:-2], rest[-2:]
    n_groups = len(DIL_PATTERNS)
    h = _rmsnorm(x_ref[0], g_ref[...]).astype(BF16)
    cos = cos_ref[0]
    sin = sin_ref[0]
    lane = lax.broadcasted_iota(jnp.int32, cos.shape, 1)
    first_half = (lane % DIL_HEAD_DIM) < (DIL_HEAD_DIM // 2)
    W = DIL_WIDTH
    for c, out_ref in enumerate(out_refs):
        which, g = divmod(c, n_groups)
        acc = jnp.dot(h, w_ref[:, c * W:(c + 1) * W], preferred_element_type=F32)
        if which < 2:
            parts = []
            for j in range(W // LANES):
                blk = acc[:, j * LANES:(j + 1) * LANES]
                rot = jnp.where(first_half, pltpu.roll(blk, LANES - 32, axis=1), pltpu.roll(blk, 32, axis=1))
                out = blk * cos + rot * sin
                parts.append(out * q_scale if which == 0 else out)
            acc = jnp.concatenate(parts, axis=-1)
        _deinterleave_store(out_ref, acc, DIL_PATTERNS[g][1], a_ref, b_ref)


def norm_proj_rope(x3, gain, w, cos3, sin3, q_scale, tm=512):
    B, S, D = x3.shape
    W = DIL_WIDTH
    out_specs, out_shape = [], []
    for _ in range(3):
        for _, d in DIL_PATTERNS:
            out_specs.append(pl.BlockSpec((1, d, tm // d, W), lambda b, i: (b, 0, i, 0)))
            out_shape.append(jax.ShapeDtypeStruct((B, d, S // d, W), BF16))
    return pl.pallas_call(
        functools.partial(_norm_proj_rope_kernel, q_scale=q_scale),
        grid=(B, S // tm),
        in_specs=[pl.BlockSpec((1, tm, D), lambda b, i: (b, i, 0)), _resident((1, D)), _resident(w.shape),
                  pl.BlockSpec((1, tm, LANES), lambda b, i: (b, i, 0)),
                  pl.BlockSpec((1, tm, LANES), lambda b, i: (b, i, 0))],
        out_specs=out_specs,
        out_shape=out_shape,
        scratch_shapes=[pltpu.VMEM((W // LANES, tm, LANES), F32), pltpu.VMEM((W // LANES, tm, LANES), F32)],
        compiler_params=_cparams(("parallel", "parallel")),
        name="norm_proj_rope",
    )(x3, gain.reshape(1, D), w, cos3, sin3)


def _bmm(a, b):
    return jnp.einsum('nik,nkj->nij', a.astype(BF16), b.astype(BF16), preferred_element_type=F32)


def _bmm_nt(a, b):
    return jnp.einsum('nik,njk->nij', a.astype(BF16), b.astype(BF16), preferred_element_type=F32)


def _bmm_tn(a, b):
    return jnp.einsum('nci,ncj->nij', a.astype(BF16), b.astype(BF16), preferred_element_type=F32)


def _unit_lower_inverse(L):
    C = L.shape[-1]
    ri = lax.broadcasted_iota(jnp.int32, (C, C), 0)
    ci = lax.broadcasted_iota(jnp.int32, (C, C), 1)
    same16 = (ri // 16) == (ci // 16)
    same32 = (ri // 32) == (ci // 32)
    eye = (ri == ci).astype(F32)
    Ld = jnp.where(same16, L, 0.0)
    X = eye - Ld
    P = Ld
    for _ in range(3):
        P = _bmm(P, P)
        X = X + _bmm(X, P)
    for off in (jnp.where(same32 & ~same16, L, 0.0), jnp.where(~same32, L, 0.0)):
        X = X - _bmm(_bmm(X, off), X)
    return X


def _shifted_rows(buf_ref, cur, prev, first, shifts):
    TB = cur.shape[0]
    buf_ref[0:SUBLANES, :] = jnp.where(first, 0.0, prev)
    buf_ref[SUBLANES:, :] = cur
    return [buf_ref[SUBLANES - s:SUBLANES - s + TB, :] for s in shifts]


def _split_heads(x, n_heads, width):
    return jnp.stack([x[:, h * width:(h + 1) * width] for h in range(n_heads)], axis=0)


def _merge_heads(x):
    return jnp.concatenate([x[h] for h in range(x.shape[0])], axis=-1)


def _gdn_kernel(q_ref, k_ref, v_ref, z_ref, ab_ref, qp_ref, kp_ref, vp_ref, cw_ref, alog_ref, dtb_ref, nw_ref,
                o_ref, buf_ref, s_ref, qd_ref, pg_ref, qg_ref, ov_ref, egl_ref, oc_ref):
    H, DK, DV, C = GDN_HEADS, GDN_DK, GDN_DV, CHUNK
    TB = q_ref.shape[1]
    NC = TB // C
    first = pl.program_id(1) == 0

    @pl.when(first)
    def _():
        s_ref[...] = jnp.zeros_like(s_ref)

    def conv_silu(cur_ref, prev_ref, j):
        cur = cur_ref[0]
        x3, x2, x1 = _shifted_rows(buf_ref, cur, prev_ref[0], first, (3, 2, 1))
        w = cw_ref[j]
        y = x3 * w[0:1] + x2 * w[1:2] + x1 * w[2:3] + cur * w[3:4]
        return y * _sigmoid(y)

    q = _split_heads(conv_silu(q_ref, qp_ref, 0), H, DK)
    k = _split_heads(conv_silu(k_ref, kp_ref, 1), H, DK)
    v = _split_heads(conv_silu(v_ref, vp_ref, 2), H, DV)
    q = q * lax.rsqrt(jnp.sum(q * q, axis=-1, keepdims=True) + L2_EPS) * (DK ** -0.5)
    k = k * lax.rsqrt(jnp.sum(k * k, axis=-1, keepdims=True) + L2_EPS)
    ab = ab_ref[0]
    a = jnp.stack([ab[:, h:h + 1] for h in range(H)], axis=0)
    b = jnp.stack([ab[:, H + h:H + h + 1] for h in range(H)], axis=0)
    beta = _sigmoid(b)
    g = -jnp.exp(alog_ref[...]) * _softplus(a + dtb_ref[...])

    N = H * NC
    q = q.reshape(N, C, DK)
    k = k.reshape(N, C, DK)
    v = v.reshape(N, C, DV)
    beta = beta.reshape(N, C, 1)
    g = g.reshape(N, C, 1)

    ri = lax.broadcasted_iota(jnp.int32, (C, C), 0)
    ci = lax.broadcasted_iota(jnp.int32, (C, C), 1)
    causal = ri >= ci
    strict = ri > ci
    gb = jnp.broadcast_to(g, (N, C, C))
    g_row = jnp.sum(jnp.where(ri == ci, gb, 0.0), axis=1, keepdims=True)
    gc_col = jnp.sum(jnp.where(causal, jnp.broadcast_to(g_row, (N, C, C)), 0.0), axis=2, keepdims=True)
    gc_row = jnp.sum(jnp.where(ri <= ci, gb, 0.0), axis=1, keepdims=True)
    decay = jnp.where(causal, jnp.exp(jnp.where(causal, gc_col - gc_row, 0.0)), 0.0)

    kb = k * beta
    vb = v * beta
    kq = _bmm_nt(jnp.concatenate([kb, q], axis=1), k)
    L = jnp.where(strict, kq[:, :C] * decay, 0.0)
    aqk = kq[:, C:] * decay
    tinv = _unit_lower_inverse(L)
    egc = jnp.exp(gc_col)
    uw = _bmm(tinv, jnp.concatenate([vb, kb * egc], axis=2))
    auw = _bmm(aqk, uw)
    g_last = gc_col[:, C - 1:C, :]
    k_dec = k * jnp.exp(g_last - gc_col)
    kuw = _bmm_tn(k_dec, uw)
    qd_ref[...] = (q * egc - auw[:, :, DV:]).reshape(H, NC, C, DK)
    ov_ref[...] = auw[:, :, :DV].reshape(H, NC, C, DV)
    pg_ref[...] = kuw[:, :, DV:].reshape(H, NC, DK, DK)
    qg_ref[...] = kuw[:, :, :DV].reshape(H, NC, DK, DV)
    egl_ref[...] = jnp.broadcast_to(jnp.exp(g_last), (N, 1, LANES)).reshape(H, NC, 1, LANES)

    for c in range(NC):
        S = s_ref[...]
        oc_ref[:, c] = _bmm(qd_ref[:, c], S) + ov_ref[:, c]
        s_ref[...] = S * egl_ref[:, c] - _bmm(pg_ref[:, c], S) + qg_ref[:, c]

    o = oc_ref[...].reshape(H, TB, DV)
    o = o * lax.rsqrt(jnp.mean(o * o, axis=-1, keepdims=True) + RMS_EPS) * nw_ref[...]
    z = _split_heads(z_ref[0], H, DV)
    o_ref[0] = _merge_heads(o * (z * _sigmoid(z)))


def gdn_mixer(proj3, conv_w, a_log, dt_bias, norm_w, tb=512):
    B, S, _ = proj3.shape
    H, DK, DV, C = GDN_HEADS, GDN_DK, GDN_DV, CHUNK
    W = H * DK
    NC = tb // C
    nq, nk, nv, nz, nab = EV_Q // W, EV_K // W, EV_V // W, EV_Z // W, EV_AB // LANES
    rows8 = tb // SUBLANES

    def cur(cb):
        return pl.BlockSpec((1, tb, W), lambda b, i, cb=cb: (b, i, cb))

    def prev(cb):
        return pl.BlockSpec((1, SUBLANES, W), lambda b, i, cb=cb: (b, jnp.maximum(i * rows8 - 1, 0), cb))

    cw = conv_w.reshape(GDN_CONV, 3, W).transpose(1, 0, 2)
    return pl.pallas_call(
        _gdn_kernel,
        grid=(B, S // tb),
        in_specs=[cur(nq), cur(nk), cur(nv), cur(nz),
                  pl.BlockSpec((1, tb, LANES), lambda b, i: (b, i, nab)),
                  prev(nq), prev(nk), prev(nv),
                  _resident((3, GDN_CONV, W)), _resident((H, 1, 1)), _resident((H, 1, 1)),
                  _resident((1, 1, DV))],
        out_specs=pl.BlockSpec((1, tb, H * DV), lambda b, i: (b, i, 0)),
        out_shape=jax.ShapeDtypeStruct((B, S, H * DV), F32),
        scratch_shapes=[pltpu.VMEM((tb + SUBLANES, W), F32),
                        pltpu.VMEM((H, DK, DV), F32),
                        pltpu.VMEM((H, NC, C, DK), F32),
                        pltpu.VMEM((H, NC, DK, DK), F32),
                        pltpu.VMEM((H, NC, DK, DV), F32),
                        pltpu.VMEM((H, NC, C, DV), F32),
                        pltpu.VMEM((H, NC, 1, LANES), F32),
                        pltpu.VMEM((H, NC, C, DV), F32)],
        compiler_params=_cparams(("parallel", "arbitrary")),
        name="gdn_mixer",
    )(proj3, proj3, proj3, proj3, proj3, proj3, proj3, proj3, cw,
      a_log.reshape(H, 1, 1), dt_bias.reshape(H, 1, 1), norm_w.reshape(1, 1, DV))


def _rwkv_kernel(*refs, has_vres):
    if has_vres:
        (r_ref, k_ref, v_ref, lo_ref, rp_ref, kp_ref, vp_ref, lop_ref, vf_ref,
         mu_ref, mul_ref, vec_ref, w2_ref, a2_ref, g2_ref, hs_ref, v2_ref,
         y_ref, buf_ref, bufl_ref, s_ref, rr_ref, pp_ref, qq_ref, yv_ref, egc_ref, yc_ref) = refs
    else:
        (r_ref, k_ref, v_ref, lo_ref, rp_ref, kp_ref, vp_ref, lop_ref,
         mu_ref, mul_ref, vec_ref, w2_ref, a2_ref, g2_ref, hs_ref,
         y_ref, vf_out_ref, buf_ref, bufl_ref, s_ref, rr_ref, pp_ref, qq_ref, yv_ref, egc_ref, yc_ref) = refs
    H, D, C = RWKV_HEADS, RWKV_HEAD, CHUNK
    TB = r_ref.shape[1]
    NC = TB // C
    first = pl.program_id(1) == 0

    @pl.when(first)
    def _():
        s_ref[...] = jnp.zeros_like(s_ref)

    def mix(cur_ref, prev_ref, mu, buf):
        cur = cur_ref[0]
        (sh,) = _shifted_rows(buf, cur, prev_ref[0], first, (1,))
        return cur + mu * (sh - cur)

    r = mix(r_ref, rp_ref, mu_ref[0:1], buf_ref)
    k = mix(k_ref, kp_ref, mu_ref[1:2], buf_ref)
    v = mix(v_ref, vp_ref, mu_ref[2:3], buf_ref)
    lo = mix(lo_ref, lop_ref, mul_ref[...], bufl_ref)
    hw, ha, hg = (lo[:, j * LORA_SEG:(j + 1) * LORA_SEG] for j in range(3))
    w0, a0, k_k, k_a, r_k, ln_w, ln_b = (vec_ref[j:j + 1] for j in range(7))

    w_log = -_softplus(-(w0 + jnp.dot(jnp.tanh(hw), w2_ref[...], preferred_element_type=F32))) - 0.5
    lw = -jnp.exp(w_log)
    a = _sigmoid(a0 + jnp.dot(ha, a2_ref[...], preferred_element_type=F32))
    gate = jnp.dot(_sigmoid(hg), g2_ref[...], preferred_element_type=F32)
    if has_vres:
        hv = lo[:, 3 * LORA_SEG:4 * LORA_SEG]
        v0 = vec_ref[7:8]
        v = v + (vf_ref[0] - v) * _sigmoid(v0 + jnp.dot(hv, v2_ref[...], preferred_element_type=F32))
    else:
        vf_out_ref[0] = v

    ri = lax.broadcasted_iota(jnp.int32, (C, C), 0)
    ci = lax.broadcasted_iota(jnp.int32, (C, C), 1)
    tril = (ri >= ci).astype(BF16)
    lw_hi = lw.astype(BF16)
    lw_rest = lw - lw_hi.astype(F32)
    lw_mid = lw_rest.astype(BF16)
    lw_lo = (lw_rest - lw_mid.astype(F32)).astype(BF16)
    gi = jnp.concatenate(
        [sum(jnp.dot(tril, part[c * C:(c + 1) * C], preferred_element_type=F32) for part in (lw_hi, lw_mid, lw_lo))
         for c in range(NC)], axis=0)

    hs = hs_ref[...]
    kk = k * k_k
    kk = kk * lax.rsqrt(_group_sum(kk * kk, hs) + L2_EPS)
    k = k * (1.0 + (a - 1.0) * k_a)
    bonus = _group_sum(r * k * r_k, hs) * v

    def chunks(t):
        return t.reshape(NC, C, H * D)

    def heads(t):
        return jnp.concatenate([t[:, :, h * D:(h + 1) * D] for h in range(H)], axis=0)

    gi3, kk3, k3 = chunks(gi), chunks(kk), chunks(k)
    g_end = gi3[:, C - 1:C, :]
    e_neg = jnp.exp(-gi3)
    e_tail = jnp.exp(g_end - gi3)
    bvec = kk3 * chunks(a)
    N = H * NC
    at = heads(-kk3 * jnp.exp(gi3 - chunks(lw)))
    rt = heads(chunks(r) * jnp.exp(gi3))
    bt = heads(bvec * e_neg)
    kt = heads(k3 * e_neg)
    b_dec = heads(bvec * e_tail)
    k_dec = heads(k3 * e_tail)
    vc = heads(chunks(v))
    strict = ri > ci
    causal = ri >= ci
    amat = _bmm_nt(jnp.concatenate([at, rt], axis=1), jnp.concatenate([bt, kt], axis=1))
    a_ab = jnp.where(strict, amat[:, :C, :C], 0.0)
    a_ak = jnp.where(strict, amat[:, :C, C:], 0.0)
    a_r = jnp.concatenate([jnp.where(causal, amat[:, C:, :C], 0.0), jnp.where(causal, amat[:, C:, C:], 0.0)], axis=2)
    tinv = _unit_lower_inverse(-a_ab)
    x2 = _bmm(tinv, jnp.concatenate([at, _bmm(a_ak, vc)], axis=2))
    xv = jnp.concatenate([x2, jnp.concatenate([jnp.zeros_like(vc), vc], axis=2)], axis=1)
    y2 = _bmm(a_r, xv)
    pq = _bmm_tn(xv, jnp.concatenate([b_dec, k_dec], axis=1))
    rr_ref[...] = (rt + y2[:, :, :D]).reshape(H, NC, C, D)
    yv_ref[...] = y2[:, :, D:].reshape(H, NC, C, D)
    pp_ref[...] = pq[:, :D].reshape(H, NC, D, D)
    qq_ref[...] = pq[:, D:].reshape(H, NC, D, D)
    egc_ref[...] = heads(jnp.exp(g_end)).reshape(H, NC, 1, D)

    for c in range(NC):
        S = s_ref[...]
        yc_ref[:, c] = _bmm_nt(rr_ref[:, c], S) + yv_ref[:, c]
        s_ref[...] = S * egc_ref[:, c] + _bmm(S, pp_ref[:, c]) + qq_ref[:, c]

    y = _merge_heads(yc_ref[...].reshape(H, TB, D))
    yc = y - _group_sum(y, hs) * (1.0 / D)
    var = _group_sum(yc * yc, hs) * (1.0 / D)
    y = yc * lax.rsqrt(var + RWKV_LN_EPS) * ln_w + ln_b
    y_ref[0] = (y + bonus) * gate


def rwkv_mixer(proj3, mu, mu_lora, vecs, w2, a2, g2, v_first=None, v2=None, tb=256):
    B, S, _ = proj3.shape
    H, D, C = RWKV_HEADS, RWKV_HEAD, CHUNK
    W = RWKV_DIM
    WL = 4 * LORA_SEG
    NC = tb // C
    rows8 = tb // SUBLANES
    has_vres = v_first is not None

    def cur(col, width):
        return pl.BlockSpec((1, tb, width), lambda b, i, cb=col // width: (b, i, cb))

    def prev(col, width):
        return pl.BlockSpec((1, SUBLANES, width),
                            lambda b, i, cb=col // width: (b, jnp.maximum(i * rows8 - 1, 0), cb))

    in_specs = [cur(EV_R, W), cur(EV_RK, W), cur(EV_RV, W), cur(EV_LORA, WL),
                prev(EV_R, W), prev(EV_RK, W), prev(EV_RV, W), prev(EV_LORA, WL)]
    args = [proj3] * 8
    if has_vres:
        in_specs.append(pl.BlockSpec((1, tb, W), lambda b, i: (b, i, 0)))
        args.append(v_first)
    in_specs += [_resident(mu.shape), _resident(mu_lora.shape), _resident(vecs.shape),
                 _resident(w2.shape), _resident(a2.shape), _resident(g2.shape), _resident((W, W))]
    args += [mu, mu_lora, vecs, w2, a2, g2, _block_ones(W, D)]
    if has_vres:
        in_specs.append(_resident(v2.shape))
        args.append(v2)
    out_block = pl.BlockSpec((1, tb, W), lambda b, i: (b, i, 0))
    out_sds = jax.ShapeDtypeStruct((B, S, W), F32)
    res = pl.pallas_call(
        functools.partial(_rwkv_kernel, has_vres=has_vres),
        grid=(B, S // tb),
        in_specs=in_specs,
        out_specs=out_block if has_vres else [out_block, out_block],
        out_shape=out_sds if has_vres else [out_sds, out_sds],
        scratch_shapes=[pltpu.VMEM((tb + SUBLANES, W), F32),
                        pltpu.VMEM((tb + SUBLANES, WL), F32),
                        pltpu.VMEM((H, D, D), F32),
                        pltpu.VMEM((H, NC, C, D), F32),
                        pltpu.VMEM((H, NC, D, D), F32),
                        pltpu.VMEM((H, NC, D, D), F32),
                        pltpu.VMEM((H, NC, C, D), F32),
                        pltpu.VMEM((H, NC, 1, D), F32),
                        pltpu.VMEM((H, NC, C, D), F32)],
        compiler_params=_cparams(("parallel", "arbitrary")),
        name="rwkv_mixer",
    )(*args)
    if has_vres:
        return res, v_first
    return res[0], res[1]


def _dil_attn_kernel(q_ref, kc_ref, kp_ref, vc_ref, vp_ref, o_ref, lse_ref, kbuf, vbuf):
    Dh, Bk = DIL_HEAD_DIM, ATT_BLOCK
    QB = q_ref.shape[2]
    has_prev = pl.program_id(2) > 0
    kbuf[0:Bk] = kp_ref[0, 0]
    kbuf[Bk:] = kc_ref[0, 0]
    vbuf[0:Bk] = vp_ref[0, 0]
    vbuf[Bk:] = vc_ref[0, 0]
    qi = lax.broadcasted_iota(jnp.int32, (Bk, 2 * Bk), 0)
    cj = lax.broadcasted_iota(jnp.int32, (Bk, 2 * Bk), 1)
    band = (cj >= qi) & (cj <= qi + Bk)
    lane = lax.broadcasted_iota(jnp.int32, (Bk, LANES), 1)
    low = lane < Dh
    ones = jnp.ones((2 * Bk, LANES), BF16)
    NP = DIL_WIDTH // LANES

    def pairs(x):
        return jnp.stack([x[:, hp * LANES:(hp + 1) * LANES] for hp in range(NP)], axis=0)

    for j in range(QB // Bk):
        ok = band if j > 0 else band & ((cj >= Bk) | has_prev)
        ok2 = jnp.concatenate([ok, ok], axis=0)
        q_pairs = pairs(q_ref[0, 0, j * Bk:(j + 1) * Bk, :])
        zero = jnp.zeros_like(q_pairs)
        q2 = jnp.concatenate([jnp.where(low, q_pairs, zero), jnp.where(low, zero, q_pairs)], axis=1)
        k_pairs = pairs(kbuf[j * Bk:(j + 2) * Bk, :])
        v_pairs = pairs(vbuf[j * Bk:(j + 2) * Bk, :])
        s = jnp.einsum('hqd,hkd->hqk', q2, k_pairs, preferred_element_type=F32)
        s = jnp.where(ok2, s, -jnp.inf)
        m = jnp.max(s, axis=-1, keepdims=True)
        p = jnp.exp2(s - m).astype(BF16)
        po = jnp.einsum('hqk,hkd->hqd', p, v_pairs, preferred_element_type=F32)
        l = jnp.dot(p.reshape(NP * 2 * Bk, 2 * Bk), ones, preferred_element_type=F32)
        l = l.reshape(NP, 2 * Bk, LANES)
        r = po / l
        o_pairs = jnp.where(low, r[:, :Bk], r[:, Bk:])
        o_ref[0, 0, j * Bk:(j + 1) * Bk, :] = jnp.concatenate([o_pairs[hp] for hp in range(NP)], axis=-1).astype(o_ref.dtype)
        lse = m + jnp.log2(l)
        lse_blk = jnp.zeros((Bk, LANES), F32)
        for hp in range(NP):
            for half in range(2):
                lse_blk = jnp.where(lane == 2 * hp + half, lse[hp, half * Bk:(half + 1) * Bk], lse_blk)
        lse_ref[0, 0, j * Bk:(j + 1) * Bk, :] = lse_blk


def dil_attention_group(q, k, v, qb=2 * ATT_BLOCK):
    B, d, L, W = q.shape
    nj = qb // ATT_BLOCK
    cur = pl.BlockSpec((1, 1, qb, W), lambda b, r, n: (b, r, n, 0))
    prev = pl.BlockSpec((1, 1, ATT_BLOCK, W), lambda b, r, n: (b, r, jnp.maximum(n * nj - 1, 0), 0))
    return pl.pallas_call(
        _dil_attn_kernel,
        grid=(B, d, L // qb),
        in_specs=[cur, cur, prev, cur, prev],
        out_specs=[cur, pl.BlockSpec((1, 1, qb, LANES), lambda b, r, n: (b, r, n, 0))],
        out_shape=[jax.ShapeDtypeStruct((B, d, L, W), BF16), jax.ShapeDtypeStruct((B, d, L, LANES), F32)],
        scratch_shapes=[pltpu.VMEM((ATT_BLOCK + qb, W), BF16), pltpu.VMEM((ATT_BLOCK + qb, W), BF16)],
        compiler_params=_cparams(("parallel", "parallel", "arbitrary")),
        name=f"dil_attn_d{d}",
    )(q, k, k, v, v)


def _out_proj_even_kernel(x_ref, ya_ref, yb_ref, wa_ref, wb_ref, o_ref):
    o_ref[...] = (x_ref[...]
                  + jnp.dot(ya_ref[...].astype(BF16), wa_ref[...], preferred_element_type=F32)
                  + jnp.dot(yb_ref[...].astype(BF16), wb_ref[...], preferred_element_type=F32))


def out_proj_even(x, ya, yb, wa, wb, tm=512):
    T, D = x.shape
    return pl.pallas_call(
        _out_proj_even_kernel,
        grid=(T // tm,),
        in_specs=[pl.BlockSpec((tm, D), lambda i: (i, 0)),
                  pl.BlockSpec((tm, ya.shape[1]), lambda i: (i, 0)),
                  pl.BlockSpec((tm, yb.shape[1]), lambda i: (i, 0)),
                  _resident(wa.shape), _resident(wb.shape)],
        out_specs=pl.BlockSpec((tm, D), lambda i: (i, 0)),
        out_shape=jax.ShapeDtypeStruct((T, D), F32),
        compiler_params=_cparams(("parallel",)),
        name="out_proj_even",
    )(x, ya, yb, wa, wb)


def _out_proj_odd_kernel(x_ref, o0_ref, o1_ref, o2_ref, l0_ref, l1_ref, l2_ref, e_ref, w_ref, out_ref,
                         a_ref, b_ref, la_ref, lb_ref):
    dils = [d for _, d in DIL_PATTERNS]
    lses = [_interleave_load(r, d, la_ref, lb_ref) for r, d in zip((l0_ref, l1_ref, l2_ref), dils)]
    m = jnp.maximum(jnp.maximum(lses[0], lses[1]), lses[2])
    es = [jnp.exp2(l - m) for l in lses]
    inv = 1.0 / (es[0] + es[1] + es[2])
    expand = e_ref[...]
    mixed = None
    for e, o_ref, d in zip(es, (o0_ref, o1_ref, o2_ref), dils):
        term = _group_sum(e * inv, expand) * _interleave_load(o_ref, d, a_ref, b_ref)
        mixed = term if mixed is None else mixed + term
    out_ref[0] = x_ref[0] + jnp.dot(mixed.astype(BF16), w_ref[...], preferred_element_type=F32)


def out_proj_odd(x3, outs, lses, w, tm=512):
    B, S, D = x3.shape
    W = DIL_WIDTH
    expand = np.zeros((LANES, W), np.float32)
    for h in range(DIL_HEADS):
        expand[h, h * DIL_HEAD_DIM:(h + 1) * DIL_HEAD_DIM] = 1.0
    row = pl.BlockSpec((1, tm, D), lambda b, i: (b, i, 0))

    def res(width):
        return [pl.BlockSpec((1, d, tm // d, width), lambda b, i: (b, 0, i, 0)) for _, d in DIL_PATTERNS]

    return pl.pallas_call(
        _out_proj_odd_kernel,
        grid=(B, S // tm),
        in_specs=[row, *res(W), *res(LANES), _resident((LANES, W)), _resident(w.shape)],
        out_specs=row,
        out_shape=jax.ShapeDtypeStruct((B, S, D), F32),
        scratch_shapes=[pltpu.VMEM((W // LANES, tm, LANES), F32), pltpu.VMEM((W // LANES, tm, LANES), F32),
                        pltpu.VMEM((1, tm, LANES), F32), pltpu.VMEM((1, tm, LANES), F32)],
        compiler_params=_cparams(("parallel", "parallel")),
        name="out_proj_odd",
    )(x3, *outs, *lses, jnp.asarray(expand, BF16), w)


MOE_TM = 512
MOE_RB = 128
MOE_NBLK = (MOE_TM + N_GROUPS * (MOE_RB - 1)) // MOE_RB
LOGIT_ROWS = 24


def _moe_route(h, wr_ref):
    NG, EPG, RB = N_GROUPS, EXPERTS_PER_GROUP, MOE_RB
    tm = h.shape[0]
    hb, h_lo = _hi_lo(h)
    nt = (((1,), (1,)), ((), ()))
    lg = lax.dot_general(wr_ref[...], hb, nt, preferred_element_type=F32)
    lg = lg[:LANES] + lg[LANES:] + lax.dot_general(wr_ref[:LANES], h_lo, nt, preferred_element_type=F32)
    L = lg[:LOGIT_ROWS]
    r = lax.broadcasted_iota(jnp.int32, L.shape, 0)
    big = jnp.int32(1 << 30)
    gl = jnp.where(r < NG, L, -jnp.inf)
    gmax = jnp.max(gl, axis=0, keepdims=True)
    gsel = jnp.min(jnp.where(gl == gmax, r, big), axis=0, keepdims=True)
    p_group = 1.0 / jnp.sum(jnp.where(r < NG, jnp.exp(L - gmax), 0.0), axis=0, keepdims=True)
    lo = NG + gsel * EPG
    el = jnp.where((r >= lo) & (r < lo + EPG), L, -jnp.inf)
    v1 = jnp.max(el, axis=0, keepdims=True)
    i1 = jnp.min(jnp.where(el == v1, r, big), axis=0, keepdims=True)
    el2 = jnp.where(r == i1, -jnp.inf, el)
    v2 = jnp.max(el2, axis=0, keepdims=True)
    i2 = jnp.min(jnp.where(el2 == v2, r, big), axis=0, keepdims=True)
    t = jnp.exp(v2 - v1)
    gates = jnp.where(r == i1, p_group / (1.0 + t), 0.0) + jnp.where(r == i2, p_group * t / (1.0 + t), 0.0)
    gate4 = [jnp.sum(jnp.where(r == lo + j, gates, 0.0), axis=0, keepdims=True) for j in range(EPG)]

    r8 = lax.broadcasted_iota(jnp.int32, (SUBLANES, tm), 0)
    onehot = (r8 == gsel).astype(F32)
    ti = lax.broadcasted_iota(jnp.int32, (tm, tm), 0)
    tj = lax.broadcasted_iota(jnp.int32, (tm, tm), 1)
    before = (ti < tj).astype(BF16)
    rank = jnp.dot(onehot.astype(BF16), before, preferred_element_type=F32)
    count = jnp.sum(onehot, axis=1, keepdims=True).astype(jnp.int32)
    assert RB & (RB - 1) == 0
    nblk = lax.shift_right_logical(count + (RB - 1), jnp.int32(RB.bit_length() - 1))
    padded = (nblk * RB).astype(F32)
    start, starts = jnp.zeros((1, 1), F32), []
    for g in range(NG):
        starts.append(start)
        start = start + padded[g:g + 1]
    start8 = jnp.concatenate(starts + [jnp.zeros((SUBLANES - NG, 1), F32)], axis=0)
    dest = jnp.sum(onehot * (start8 + rank), axis=0, keepdims=True)
    return dest, gate4, nblk


def _moe_kernel(x_ref, g_ref, wr_ref, w1_ref, w3_ref, w2_ref, fg_ref, o_ref,
                p_ref, xs_ref, gs_ref, hh_ref, ys_ref, *, final_norm):
    NG, EPG, RB, NBLK = N_GROUPS, EXPERTS_PER_GROUP, MOE_RB, MOE_NBLK
    FF = w1_ref.shape[2]
    tm = x_ref.shape[0]
    x = x_ref[...]
    h = _rmsnorm(x, g_ref[...])
    dest, gate4, nblk = _moe_route(h, wr_ref)
    rows = lax.broadcasted_iota(jnp.int32, (NBLK * RB, tm), 0)
    p_ref[...] = jnp.where(rows == dest.astype(jnp.int32), 1.0, 0.0).astype(BF16)
    xs_ref[...] = jnp.dot(p_ref[...], h.astype(BF16), preferred_element_type=F32).astype(BF16)
    g4 = jnp.concatenate(gate4 + [jnp.zeros((SUBLANES - EPG, tm), F32)], axis=0)
    g_hi = g4.astype(BF16).astype(F32)
    g2 = jnp.concatenate([g_hi, g4 - g_hi, jnp.zeros((LANES - 2 * SUBLANES, tm), F32)], axis=0).astype(BF16)
    gs_ref[...] = lax.dot_general(p_ref[...], g2, (((1,), (1,)), ((), ())), preferred_element_type=F32)

    ends, acc = [], jnp.int32(0)
    for g in range(NG):
        acc = acc + jnp.sum(nblk[g:g + 1])
        ends.append(acc)
    for b in range(NBLK):
        rs = slice(b * RB, (b + 1) * RB)
        grp = sum((b >= e).astype(jnp.int32) for e in ends[:-1])

        @pl.when(b < ends[-1])
        def _():
            xr = xs_ref[rs]
            gate = gs_ref[rs]
            for j in range(EPG):
                a = jnp.dot(xr, w1_ref[grp * EPG + j], preferred_element_type=F32)
                u = jnp.dot(xr, w3_ref[grp * EPG + j], preferred_element_type=F32)
                gj = gate[:, j:j + 1] + gate[:, SUBLANES + j:SUBLANES + j + 1]
                hh_ref[:, j * FF:(j + 1) * FF] = ((a * _sigmoid(a)) * u * gj).astype(BF16)
            ys_ref[rs] = jnp.dot(hh_ref[...], w2_ref[grp], preferred_element_type=F32).astype(BF16)

        @pl.when(b >= ends[-1])
        def _():
            ys_ref[rs] = jnp.zeros((RB, ys_ref.shape[1]), BF16)

    y = x + lax.dot_general(p_ref[...], ys_ref[...], (((0,), (0,)), ((), ())), preferred_element_type=F32)
    if final_norm:
        y = _rmsnorm(y, fg_ref[...])
    o_ref[...] = y


def moe_block(x, gain, w_router, w1, w3, w2, final_gain=None):
    T, D = x.shape
    NE, _, FF = w1.shape
    tm, RB, NBLK = MOE_TM, MOE_RB, MOE_NBLK
    final_norm = final_gain is not None
    fg = (final_gain if final_norm else jnp.ones((D,), F32)).reshape(1, D)
    return pl.pallas_call(
        functools.partial(_moe_kernel, final_norm=final_norm),
        grid=(T // tm,),
        in_specs=[pl.BlockSpec((tm, D), lambda i: (i, 0)), _resident((1, D)), _resident(w_router.shape),
                  _resident(w1.shape), _resident(w3.shape), _resident(w2.shape), _resident((1, D))],
        out_specs=pl.BlockSpec((tm, D), lambda i: (i, 0)),
        out_shape=jax.ShapeDtypeStruct((T, D), F32),
        scratch_shapes=[pltpu.VMEM((NBLK * RB, tm), BF16), pltpu.VMEM((NBLK * RB, D), BF16),
                        pltpu.VMEM((NBLK * RB, LANES), F32), pltpu.VMEM((RB, EXPERTS_PER_GROUP * FF), BF16),
                        pltpu.VMEM((NBLK * RB, D), BF16)],
        compiler_params=_cparams(("parallel",)),
        name="moe_block",
    )(x, gain.reshape(1, D), w_router, w1, w3, w2, fg)


def _router_weights(w_group, w_expert):
    w = _place([(0, w_group)] + [(N_GROUPS + g * EXPERTS_PER_GROUP, w_expert[g]) for g in range(N_GROUPS)], LANES).T
    return jnp.concatenate(_hi_lo(w), axis=0)


def _place(cols, total):
    lead = cols[0][1].shape[:-1]
    parts, pos = [], 0
    for off, arr in sorted(cols, key=lambda c: c[0]):
        if off > pos:
            parts.append(jnp.zeros(lead + (off - pos,), F32))
        parts.append(arr)
        pos = off + arr.shape[-1]
    if total > pos:
        parts.append(jnp.zeros(lead + (total - pos,), F32))
    return jnp.concatenate(parts, axis=-1)


def _even_layout(t, vres=None):
    gq, gk, gv, gz, ga, gb, rr, rk, rv, lw, la, lg = jnp.split(
        t, np.cumsum([512, 512, 512, 512, 4, 4, 512, 512, 512, 32, 32, 96])[:-1].tolist(), axis=-1)
    cols = [(EV_Q, gq), (EV_K, gk), (EV_V, gv), (EV_Z, gz), (EV_AB, ga), (EV_AB + GDN_HEADS, gb),
            (EV_R, rr), (EV_RK, rk), (EV_RV, rv),
            (EV_LORA, lw), (EV_LORA + LORA_SEG, la), (EV_LORA + 2 * LORA_SEG, lg)]
    if vres is not None:
        cols.append((EV_LORA + 3 * LORA_SEG, vres))
    return _place(cols, EV_NPAD)


def _pad_rows(w, rows):
    return jnp.concatenate([w, jnp.zeros((rows - w.shape[0], w.shape[1]), F32)], axis=0)


def kernel(x, positions, ev_norm, ev_w_in, rwkv_vres_down, ev_w_out, gdn_conv_w, gdn_A_log, gdn_dt_bias, gdn_norm,
           rwkv_mu, rwkv_w0, rwkv_w2, rwkv_a0, rwkv_a2, rwkv_g2, rwkv_k_k, rwkv_k_a, rwkv_r_k, rwkv_ln_w, rwkv_ln_b,
           rwkv_vres_mu, rwkv_v0, rwkv_v2, od_norm, od_w_in, od_w_out, ffn_norm, moe_w_group, moe_w_expert,
           moe_w1, moe_w3, moe_w2, final_norm):
    B, S, D = x.shape
    T = B * S
    depth = ffn_norm.shape[0]
    xf = x.reshape(T, D)
    cos, sin = rope_tables(positions.reshape(T))
    n_groups = len(DIL_PATTERNS)
    v_first = None
    for layer in range(depth):
        i = layer // 2
        if layer % 2 == 0:
            vres_w = None if i == 0 else rwkv_vres_down[i - 1]
            w_in = _even_layout(ev_w_in[i], vres_w).astype(BF16)
            proj3 = norm_proj(xf, ev_norm[i], w_in).reshape(B, S, EV_NPAD)
            ya = gdn_mixer(proj3, gdn_conv_w[i], gdn_A_log[i], gdn_dt_bias[i], gdn_norm[i])
            mu_r, mu_k, mu_v, mu_w, mu_a, mu_g = jnp.split(
                rwkv_mu[i], np.cumsum([RWKV_DIM, RWKV_DIM, RWKV_DIM, 32, 32])[:].tolist())
            mu = jnp.stack([mu_r, mu_k, mu_v])
            lora_cols = [(0, mu_w), (LORA_SEG, mu_a), (2 * LORA_SEG, mu_g)]
            if i > 0:
                lora_cols.append((3 * LORA_SEG, rwkv_vres_mu[i - 1]))
            mu_lora = _place(lora_cols, 4 * LORA_SEG).reshape(1, 4 * LORA_SEG)
            vec_rows = [rwkv_w0[i], rwkv_a0[i], rwkv_k_k[i], rwkv_k_a[i], rwkv_r_k[i].reshape(-1), rwkv_ln_w[i], rwkv_ln_b[i]]
            if i > 0:
                vec_rows.append(rwkv_v0[i - 1])
            vecs = jnp.stack(vec_rows)
            w2 = _pad_rows(rwkv_w2[i], LORA_SEG)
            a2 = _pad_rows(rwkv_a2[i], LORA_SEG)
            g2 = _pad_rows(rwkv_g2[i], LORA_SEG)
            if i == 0:
                yb, v_first = rwkv_mixer(proj3, mu, mu_lora, vecs, w2, a2, g2)
            else:
                yb, _ = rwkv_mixer(proj3, mu, mu_lora, vecs, w2, a2, g2, v_first, _pad_rows(rwkv_v2[i - 1], LORA_SEG))
            w_out = ev_w_out[i].astype(BF16)
            na = GDN_HEADS * GDN_DV
            xf = out_proj_even(xf, ya.reshape(T, na), yb.reshape(T, RWKV_DIM), w_out[:na], w_out[na:])
        else:
            w_in = od_w_in[i].astype(BF16)
            qkv = norm_proj_rope(xf.reshape(B, S, D), od_norm[i], w_in, cos.reshape(B, S, LANES),
                                 sin.reshape(B, S, LANES), DIL_HEAD_DIM ** -0.5 * math.log2(math.e))
            outs, lses = [], []
            for gi, (window, dilation) in enumerate(DIL_PATTERNS):
                assert window // dilation == ATT_BLOCK and (S // dilation) % (2 * ATT_BLOCK) == 0
                o, lse = dil_attention_group(qkv[gi], qkv[n_groups + gi], qkv[2 * n_groups + gi])
                outs.append(o)
                lses.append(lse)
            xf = out_proj_odd(xf.reshape(B, S, D), outs, lses, od_w_out[i].astype(BF16)).reshape(T, D)
        w_router = _router_weights(moe_w_group[layer], moe_w_expert[layer])
        w2_grouped = moe_w2[layer].astype(BF16).reshape(N_GROUPS, -1, D)
        xf = moe_block(xf, ffn_norm[layer], w_router, moe_w1[layer].astype(BF16), moe_w3[layer].astype(BF16),
                       w2_grouped, final_norm if layer == depth - 1 else None)
    return xf.reshape(B, S, D)
```

```python
import functools
import math

import jax
import jax.numpy as jnp
import numpy as np
from jax import lax
from jax.experimental import pallas as pl
from jax.experimental.pallas import tpu as pltpu

F32 = jnp.float32
BF16 = jnp.bfloat16

RMS_EPS = 1e-6
L2_EPS = 1e-6
RWKV_LN_EPS = 64e-5
ROPE_THETA = 10000.0

LANES = 128
SUBLANES = 8
VMEM_LIMIT = 56 * 1024 * 1024

GDN_HEADS = 4
GDN_DK = 128
GDN_DV = 128
GDN_CONV = 4
RWKV_HEADS = 8
RWKV_HEAD = 64
RWKV_DIM = RWKV_HEADS * RWKV_HEAD
LORA_SEG = LANES
DIL_PATTERNS = ((128, 1), (512, 4), (2048, 16))
DIL_HEADS = 8
DIL_HEAD_DIM = 64
DIL_WIDTH = DIL_HEADS * DIL_HEAD_DIM
ATT_BLOCK = 128
N_GROUPS = 4
EXPERTS_PER_GROUP = 4
N_EXPERTS = N_GROUPS * EXPERTS_PER_GROUP
CHUNK = 64

EV_Q, EV_K, EV_V, EV_Z = 0, 512, 1024, 1536
EV_R, EV_RK, EV_RV, EV_LORA = 2048, 2560, 3072, 3584
EV_AB = EV_LORA + 4 * LORA_SEG
EV_NPAD = EV_AB + LANES


def _cparams(sem):
    return pltpu.CompilerParams(dimension_semantics=sem, vmem_limit_bytes=VMEM_LIMIT)


def _sigmoid(x):
    return 0.5 * jnp.tanh(0.5 * x) + 0.5


def _softplus(x):
    return jnp.maximum(x, 0.0) + jnp.log(1.0 + jnp.exp(-jnp.abs(x)))


def _rmsnorm(x, gain):
    return x * lax.rsqrt(jnp.mean(x * x, axis=-1, keepdims=True) + RMS_EPS) * gain


def _hi_lo(v):
    hi = v.astype(BF16)
    return hi, (v - hi.astype(F32)).astype(BF16)


def _group_sum(x, block_ones):
    hi, lo = _hi_lo(x)
    return (jnp.dot(hi, block_ones, preferred_element_type=F32)
            + jnp.dot(lo, block_ones, preferred_element_type=F32))


def _block_ones(width, group):
    idx = np.arange(width) // group
    return jnp.asarray(idx[:, None] == idx[None, :], BF16)


def _resident(shape):
    nd = len(shape)
    return pl.BlockSpec(shape, lambda *_: (0,) * nd, pipeline_mode=pl.Buffered(1))


def _rope_table_kernel(pos_ref, freq_ref, sign_ref, cos_ref, sin_ref):
    ang = pos_ref[...].astype(F32) * freq_ref[...]
    cos_ref[...] = jnp.cos(ang)
    sin_ref[...] = jnp.sin(ang) * sign_ref[...]


def rope_tables(positions_flat, tm=1024):
    T = positions_flat.shape[0]
    half = DIL_HEAD_DIM // 2
    inv_freq = ROPE_THETA ** (-jnp.arange(half, dtype=F32) * 2.0 / DIL_HEAD_DIM)
    lane = np.arange(LANES)
    freq = inv_freq[lane % half][None, :]
    sign = jnp.asarray(np.where(lane % DIL_HEAD_DIM < half, -1.0, 1.0), F32)[None, :]
    return pl.pallas_call(
        _rope_table_kernel,
        grid=(T // tm,),
        in_specs=[pl.BlockSpec((tm, 1), lambda i: (i, 0)), _resident((1, LANES)), _resident((1, LANES))],
        out_specs=[pl.BlockSpec((tm, LANES), lambda i: (i, 0))] * 2,
        out_shape=[jax.ShapeDtypeStruct((T, LANES), F32)] * 2,
        compiler_params=_cparams(("parallel",)),
        name="rope_tables",
    )(positions_flat.reshape(T, 1), freq, sign)


def _norm_proj_kernel(x_ref, g_ref, w_ref, o_ref):
    h = _rmsnorm(x_ref[...], g_ref[...]).astype(BF16)
    o_ref[...] = jnp.dot(h, w_ref[...], preferred_element_type=F32)


def norm_proj(x, gain, w, tm=512):
    T, D = x.shape
    N = w.shape[1]
    return pl.pallas_call(
        _norm_proj_kernel,
        grid=(T // tm,),
        in_specs=[pl.BlockSpec((tm, D), lambda i: (i, 0)), _resident((1, D)), _resident((D, N))],
        out_specs=pl.BlockSpec((tm, N), lambda i: (i, 0)),
        out_shape=jax.ShapeDtypeStruct((T, N), F32),
        compiler_params=_cparams(("parallel",)),
        name="norm_proj",
    )(x, gain.reshape(1, D), w)


def _deinterleave_store(out_ref, val, d, a_ref, b_ref):
    tm = val.shape[0]
    if d == 1:
        out_ref[0, 0] = val.astype(out_ref.dtype)
        return
    quarter = tm // 4
    for j in range(val.shape[1] // LANES):
        cs = slice(j * LANES, (j + 1) * LANES)
        a_ref[j] = val[:, cs]
        if d == 4:
            for r in range(4):
                out_ref[0, r, :, cs] = a_ref[j, pl.ds(r, quarter, stride=4), :].astype(out_ref.dtype)
            continue
        assert d == 16
        for r1 in range(4):
            b_ref[j, r1 * quarter:(r1 + 1) * quarter, :] = a_ref[j, pl.ds(r1, quarter, stride=4), :]
        for r1 in range(4):
            for r2 in range(4):
                out_ref[0, r1 + 4 * r2, :, cs] = (
                    b_ref[j, pl.ds(r1 * quarter + r2, quarter // 4, stride=4), :].astype(out_ref.dtype))


def _interleave_load(in_ref, d, a_ref, b_ref):
    if d == 1:
        return in_ref[0, 0].astype(F32)
    n_tiles, tm, _ = a_ref.shape
    quarter = tm // 4
    for j in range(n_tiles):
        cs = slice(j * LANES, (j + 1) * LANES)
        if d == 4:
            for r in range(4):
                a_ref[j, pl.ds(r, quarter, stride=4), :] = in_ref[0, r, :, cs].astype(F32)
            continue
        assert d == 16
        for r1 in range(4):
            for r2 in range(4):
                b_ref[j, pl.ds(r1 * quarter + r2, quarter // 4, stride=4), :] = in_ref[0, r1 + 4 * r2, :, cs].astype(F32)
        for r1 in range(4):
            a_ref[j, pl.ds(r1, quarter, stride=4), :] = b_ref[j, r1 * quarter:(r1 + 1) * quarter, :]
    return jnp.concatenate([a_ref[j] for j in range(n_tiles)], axis=-1)


def _norm_proj_rope_kernel(x_ref, g_ref, w_ref, cos_ref, sin_ref, *rest, q_scale):
    out_refs, (a_ref, b_ref) = rest[:-2], rest[-2:]
    n_groups = len(DIL_PATTERNS)
    h = _rmsnorm(x_ref[0], g_ref[...]).astype(BF16)
    cos = cos_ref[0]
    sin = sin_ref[0]
    lane = lax.broadcasted_iota(jnp.int32, cos.shape, 1)
    first_half = (lane % DIL_HEAD_DIM) < (DIL_HEAD_DIM // 2)
    W = DIL_WIDTH
    for c, out_ref in enumerate(out_refs):
        which, g = divmod(c, n_groups)
        acc = jnp.dot(h, w_ref[:, c * W:(c + 1) * W], preferred_element_type=F32)
        if which < 2:
            parts = []
            for j in range(W // LANES):
                blk = acc[:, j * LANES:(j + 1) * LANES]
                rot = jnp.where(first_half, pltpu.roll(blk, LANES - 32, axis=1), pltpu.roll(blk, 32, axis=1))
                out = blk * cos + rot * sin
                parts.append(out * q_scale if which == 0 else out)
            acc = jnp.concatenate(parts, axis=-1)
        _deinterleave_store(out_ref, acc, DIL_PATTERNS[g][1], a_ref, b_ref)


def norm_proj_rope(x3, gain, w, cos3, sin3, q_scale, tm=512):
    B, S, D = x3.shape
    W = DIL_WIDTH
    out_specs, out_shape = [], []
    for _ in range(3):
        for _, d in DIL_PATTERNS:
            out_specs.append(pl.BlockSpec((1, d, tm // d, W), lambda b, i: (b, 0, i, 0)))
            out_shape.append(jax.ShapeDtypeStruct((B, d, S // d, W), BF16))
    return pl.pallas_call(
        functools.partial(_norm_proj_rope_kernel, q_scale=q_scale),
        grid=(B, S // tm),
        in_specs=[pl.BlockSpec((1, tm, D), lambda b, i: (b, i, 0)), _resident((1, D)), _resident(w.shape),
                  pl.BlockSpec((1, tm, LANES), lambda b, i: (b, i, 0)),
                  pl.BlockSpec((1, tm, LANES), lambda b, i: (b, i, 0))],
        out_specs=out_specs,
        out_shape=out_shape,
        scratch_shapes=[pltpu.VMEM((W // LANES, tm, LANES), F32), pltpu.VMEM((W // LANES, tm, LANES), F32)],
        compiler_params=_cparams(("parallel", "parallel")),
        name="norm_proj_rope",
    )(x3, gain.reshape(1, D), w, cos3, sin3)


def _bmm(a, b):
    return jnp.einsum('nik,nkj->nij', a.astype(BF16), b.astype(BF16), preferred_element_type=F32)


def _bmm_nt(a, b):
    return jnp.einsum('nik,njk->nij', a.astype(BF16), b.astype(BF16), preferred_element_type=F32)


def _bmm_tn(a, b):
    return jnp.einsum('nci,ncj->nij', a.astype(BF16), b.astype(BF16), preferred_element_type=F32)


def _unit_lower_inverse(L):
    C = L.shape[-1]
    ri = lax.broadcasted_iota(jnp.int32, (C, C), 0)
    ci = lax.broadcasted_iota(jnp.int32, (C, C), 1)
    same16 = (ri // 16) == (ci // 16)
    same32 = (ri // 32) == (ci // 32)
    eye = (ri == ci).astype(F32)
    Ld = jnp.where(same16, L, 0.0)
    X = eye - Ld
    P = Ld
    for _ in range(3):
        P = _bmm(P, P)
        X = X + _bmm(X, P)
    for off in (jnp.where(same32 & ~same16, L, 0.0), jnp.where(~same32, L, 0.0)):
        X = X - _bmm(_bmm(X, off), X)
    return X


def _shifted_rows(buf_ref, cur, prev, first, shifts):
    TB = cur.shape[0]
    buf_ref[0:SUBLANES, :] = jnp.where(first, 0.0, prev)
    buf_ref[SUBLANES:, :] = cur
    return [buf_ref[SUBLANES - s:SUBLANES - s + TB, :] for s in shifts]


def _split_heads(x, n_heads, width):
    return jnp.stack([x[:, h * width:(h + 1) * width] for h in range(n_heads)], axis=0)


def _merge_heads(x):
    return jnp.concatenate([x[h] for h in range(x.shape[0])], axis=-1)


def _gdn_kernel(q_ref, k_ref, v_ref, z_ref, ab_ref, qp_ref, kp_ref, vp_ref, cw_ref, alog_ref, dtb_ref, nw_ref,
                o_ref, buf_ref, s_ref, qd_ref, pg_ref, qg_ref, ov_ref, egl_ref, oc_ref):
    H, DK, DV, C = GDN_HEADS, GDN_DK, GDN_DV, CHUNK
    TB = q_ref.shape[1]
    NC = TB // C
    first = pl.program_id(1) == 0

    @pl.when(first)
    def _():
        s_ref[...] = jnp.zeros_like(s_ref)

    def conv_silu(cur_ref, prev_ref, j):
        cur = cur_ref[0]
        x3, x2, x1 = _shifted_rows(buf_ref, cur, prev_ref[0], first, (3, 2, 1))
        w = cw_ref[j]
        y = x3 * w[0:1] + x2 * w[1:2] + x1 * w[2:3] + cur * w[3:4]
        return y * _sigmoid(y)

    q = _split_heads(conv_silu(q_ref, qp_ref, 0), H, DK)
    k = _split_heads(conv_silu(k_ref, kp_ref, 1), H, DK)
    v = _split_heads(conv_silu(v_ref, vp_ref, 2), H, DV)
    q = q * lax.rsqrt(jnp.sum(q * q, axis=-1, keepdims=True) + L2_EPS) * (DK ** -0.5)
    k = k * lax.rsqrt(jnp.sum(k * k, axis=-1, keepdims=True) + L2_EPS)
    ab = ab_ref[0]
    a = jnp.stack([ab[:, h:h + 1] for h in range(H)], axis=0)
    b = jnp.stack([ab[:, H + h:H + h + 1] for h in range(H)], axis=0)
    beta = _sigmoid(b)
    g = -jnp.exp(alog_ref[...]) * _softplus(a + dtb_ref[...])

    N = H * NC
    q = q.reshape(N, C, DK)
    k = k.reshape(N, C, DK)
    v = v.reshape(N, C, DV)
    beta = beta.reshape(N, C, 1)
    g = g.reshape(N, C, 1)

    ri = lax.broadcasted_iota(jnp.int32, (C, C), 0)
    ci = lax.broadcasted_iota(jnp.int32, (C, C), 1)
    causal = ri >= ci
    strict = ri > ci
    gb = jnp.broadcast_to(g, (N, C, C))
    g_row = jnp.sum(jnp.where(ri == ci, gb, 0.0), axis=1, keepdims=True)
    gc_col = jnp.sum(jnp.where(causal, jnp.broadcast_to(g_row, (N, C, C)), 0.0), axis=2, keepdims=True)
    gc_row = jnp.sum(jnp.where(ri <= ci, gb, 0.0), axis=1, keepdims=True)
    decay = jnp.where(causal, jnp.exp(jnp.where(causal, gc_col - gc_row, 0.0)), 0.0)

    kb = k * beta
    vb = v * beta
    kq = _bmm_nt(jnp.concatenate([kb, q], axis=1), k)
    L = jnp.where(strict, kq[:, :C] * decay, 0.0)
    aqk = kq[:, C:] * decay
    tinv = _unit_lower_inverse(L)
    egc = jnp.exp(gc_col)
    uw = _bmm(tinv, jnp.concatenate([vb, kb * egc], axis=2))
    auw = _bmm(aqk, uw)
    g_last = gc_col[:, C - 1:C, :]
    k_dec = k * jnp.exp(g_last - gc_col)
    kuw = _bmm_tn(k_dec, uw)
    qd_ref[...] = (q * egc - auw[:, :, DV:]).reshape(H, NC, C, DK)
    ov_ref[...] = auw[:, :, :DV].reshape(H, NC, C, DV)
    pg_ref[...] = kuw[:, :, DV:].reshape(H, NC, DK, DK)
    qg_ref[...] = kuw[:, :, :DV].reshape(H, NC, DK, DV)
    egl_ref[...] = jnp.broadcast_to(jnp.exp(g_last), (N, 1, LANES)).reshape(H, NC, 1, LANES)

    for c in range(NC):
        S = s_ref[...]
        oc_ref[:, c] = _bmm(qd_ref[:, c], S) + ov_ref[:, c]
        s_ref[...] = S * egl_ref[:, c] - _bmm(pg_ref[:, c], S) + qg_ref[:, c]

    o = oc_ref[...].reshape(H, TB, DV)
    o = o * lax.rsqrt(jnp.mean(o * o, axis=-1, keepdims=True) + RMS_EPS) * nw_ref[...]
    z = _split_heads(z_ref[0], H, DV)
    o_ref[0] = _merge_heads(o * (z * _sigmoid(z)))


def gdn_mixer(proj3, conv_w, a_log, dt_bias, norm_w, tb=512):
    B, S, _ = proj3.shape
    H, DK, DV, C = GDN_HEADS, GDN_DK, GDN_DV, CHUNK
    W = H * DK
    NC = tb // C
    nq, nk, nv, nz, nab = EV_Q // W, EV_K // W, EV_V // W, EV_Z // W, EV_AB // LANES
    rows8 = tb // SUBLANES

    def cur(cb):
        return pl.BlockSpec((1, tb, W), lambda b, i, cb=cb: (b, i, cb))

    def prev(cb):
        return pl.BlockSpec((1, SUBLANES, W), lambda b, i, cb=cb: (b, jnp.maximum(i * rows8 - 1, 0), cb))

    cw = conv_w.reshape(GDN_CONV, 3, W).transpose(1, 0, 2)
    return pl.pallas_call(
        _gdn_kernel,
        grid=(B, S // tb),
        in_specs=[cur(nq), cur(nk), cur(nv), cur(nz),
                  pl.BlockSpec((1, tb, LANES), lambda b, i: (b, i, nab)),
                  prev(nq), prev(nk), prev(nv),
                  _resident((3, GDN_CONV, W)), _resident((H, 1, 1)), _resident((H, 1, 1)),
                  _resident((1, 1, DV))],
        out_specs=pl.BlockSpec((1, tb, H * DV), lambda b, i: (b, i, 0)),
        out_shape=jax.ShapeDtypeStruct((B, S, H * DV), F32),
        scratch_shapes=[pltpu.VMEM((tb + SUBLANES, W), F32),
                        pltpu.VMEM((H, DK, DV), F32),
                        pltpu.VMEM((H, NC, C, DK), F32),
                        pltpu.VMEM((H, NC, DK, DK), F32),
                        pltpu.VMEM((H, NC, DK, DV), F32),
                        pltpu.VMEM((H, NC, C, DV), F32),
                        pltpu.VMEM((H, NC, 1, LANES), F32),
                        pltpu.VMEM((H, NC, C, DV), F32)],
        compiler_params=_cparams(("parallel", "arbitrary")),
        name="gdn_mixer",
    )(proj3, proj3, proj3, proj3, proj3, proj3, proj3, proj3, cw,
      a_log.reshape(H, 1, 1), dt_bias.reshape(H, 1, 1), norm_w.reshape(1, 1, DV))


def _rwkv_kernel(*refs, has_vres):
    if has_vres:
        (r_ref, k_ref, v_ref, lo_ref, rp_ref, kp_ref, vp_ref, lop_ref, vf_ref,
         mu_ref, mul_ref, vec_ref, w2_ref, a2_ref, g2_ref, v2_ref,
         y_ref, buf_ref, bufl_ref, s_ref, rr_ref, pp_ref, qq_ref, yv_ref, egc_ref, yc_ref) = refs
    else:
        (r_ref, k_ref, v_ref, lo_ref, rp_ref, kp_ref, vp_ref, lop_ref,
         mu_ref, mul_ref, vec_ref, w2_ref, a2_ref, g2_ref,
         y_ref, vf_out_ref, buf_ref, bufl_ref, s_ref, rr_ref, pp_ref, qq_ref, yv_ref, egc_ref, yc_ref) = refs
    H, D, C = RWKV_HEADS, RWKV_HEAD, CHUNK
    TB = r_ref.shape[1]
    NC = TB // C
    first = pl.program_id(1) == 0

    @pl.when(first)
    def _():
        s_ref[...] = jnp.zeros_like(s_ref)

    def mix(cur_ref, prev_ref, mu, buf):
        cur = cur_ref[0]
        (sh,) = _shifted_rows(buf, cur, prev_ref[0], first, (1,))
        return cur + mu * (sh - cur)

    r = mix(r_ref, rp_ref, mu_ref[0:1], buf_ref)
    k = mix(k_ref, kp_ref, mu_ref[1:2], buf_ref)
    v = mix(v_ref, vp_ref, mu_ref[2:3], buf_ref)
    lo = mix(lo_ref, lop_ref, mul_ref[...], bufl_ref)
    hw, ha, hg = (lo[:, j * LORA_SEG:(j + 1) * LORA_SEG] for j in range(3))
    w0, a0, k_k, k_a, r_k, ln_w, ln_b = (vec_ref[j:j + 1] for j in range(7))

    w_log = -_softplus(-(w0 + jnp.dot(jnp.tanh(hw), w2_ref[...], preferred_element_type=F32))) - 0.5
    lw = -jnp.exp(w_log)
    a = _sigmoid(a0 + jnp.dot(ha, a2_ref[...], preferred_element_type=F32))
    gate = jnp.dot(_sigmoid(hg), g2_ref[...], preferred_element_type=F32)
    if has_vres:
        hv = lo[:, 3 * LORA_SEG:4 * LORA_SEG]
        v0 = vec_ref[7:8]
        v = v + (vf_ref[0] - v) * _sigmoid(v0 + jnp.dot(hv, v2_ref[...], preferred_element_type=F32))
    else:
        vf_out_ref[0] = v

    ri = lax.broadcasted_iota(jnp.int32, (C, C), 0)
    ci = lax.broadcasted_iota(jnp.int32, (C, C), 1)
    tril = (ri >= ci).astype(BF16)
    lw_hi = lw.astype(BF16)
    lw_rest = lw - lw_hi.astype(F32)
    lw_mid = lw_rest.astype(BF16)
    lw_lo = (lw_rest - lw_mid.astype(F32)).astype(BF16)
    gi = jnp.concatenate(
        [sum(jnp.dot(tril, part[c * C:(c + 1) * C], preferred_element_type=F32) for part in (lw_hi, lw_mid, lw_lo))
         for c in range(NC)], axis=0)

    kk = _split_heads(k * k_k, H, D)
    kk = kk * lax.rsqrt(jnp.sum(kk * kk, axis=-1, keepdims=True) + L2_EPS)
    k = k * (1.0 + (a - 1.0) * k_a)
    r_h = _split_heads(r, H, D)
    k_h = _split_heads(k, H, D)
    v_h = _split_heads(v, H, D)
    a_h = _split_heads(a, H, D)
    gi_h = _split_heads(gi, H, D)
    lw_h = _split_heads(lw, H, D)
    bonus = jnp.sum(r_h * k_h * _split_heads(r_k, H, D), axis=-1, keepdims=True) * v_h

    N = H * NC
    rc, kc, vc, kkc, ac, gic, lwc = (t.reshape(N, C, D) for t in (r_h, k_h, v_h, kk, a_h, gi_h, lw_h))
    g_end = gic[:, C - 1:C, :]
    e_neg = jnp.exp(-gic)
    e_tail = jnp.exp(g_end - gic)
    at = -kkc * jnp.exp(gic - lwc)
    bvec = kkc * ac
    rt = rc * jnp.exp(gic)
    bt = bvec * e_neg
    kt = kc * e_neg
    b_dec = bvec * e_tail
    k_dec = kc * e_tail
    strict = ri > ci
    causal = ri >= ci
    amat = _bmm_nt(jnp.concatenate([at, rt], axis=1), jnp.concatenate([bt, kt], axis=1))
    a_ab = jnp.where(strict, amat[:, :C, :C], 0.0)
    a_ak = jnp.where(strict, amat[:, :C, C:], 0.0)
    a_r = jnp.concatenate([jnp.where(causal, amat[:, C:, :C], 0.0), jnp.where(causal, amat[:, C:, C:], 0.0)], axis=2)
    tinv = _unit_lower_inverse(-a_ab)
    x2 = _bmm(tinv, jnp.concatenate([at, _bmm(a_ak, vc)], axis=2))
    xv = jnp.concatenate([x2, jnp.concatenate([jnp.zeros_like(vc), vc], axis=2)], axis=1)
    y2 = _bmm(a_r, xv)
    pq = _bmm_tn(xv, jnp.concatenate([b_dec, k_dec], axis=1))
    rr_ref[...] = (rt + y2[:, :, :D]).reshape(H, NC, C, D)
    yv_ref[...] = y2[:, :, D:].reshape(H, NC, C, D)
    pp_ref[...] = pq[:, :D].reshape(H, NC, D, D)
    qq_ref[...] = pq[:, D:].reshape(H, NC, D, D)
    egc_ref[...] = jnp.exp(g_end).reshape(H, NC, 1, D)

    for c in range(NC):
        S = s_ref[...]
        yc_ref[:, c] = _bmm_nt(rr_ref[:, c], S) + yv_ref[:, c]
        s_ref[...] = S * egc_ref[:, c] + _bmm(S, pp_ref[:, c]) + qq_ref[:, c]

    y = yc_ref[...].reshape(H, TB, D)
    mean = jnp.mean(y, axis=-1, keepdims=True)
    yc = y - mean
    var = jnp.mean(yc * yc, axis=-1, keepdims=True)
    y = _merge_heads(yc * lax.rsqrt(var + RWKV_LN_EPS)) * ln_w + ln_b
    y_ref[0] = (y + _merge_heads(bonus)) * gate


def rwkv_mixer(proj3, mu, mu_lora, vecs, w2, a2, g2, v_first=None, v2=None, tb=256):
    B, S, _ = proj3.shape
    H, D, C = RWKV_HEADS, RWKV_HEAD, CHUNK
    W = RWKV_DIM
    WL = 4 * LORA_SEG
    NC = tb // C
    rows8 = tb // SUBLANES
    has_vres = v_first is not None

    def cur(col, width):
        return pl.BlockSpec((1, tb, width), lambda b, i, cb=col // width: (b, i, cb))

    def prev(col, width):
        return pl.BlockSpec((1, SUBLANES, width),
                            lambda b, i, cb=col // width: (b, jnp.maximum(i * rows8 - 1, 0), cb))

    in_specs = [cur(EV_R, W), cur(EV_RK, W), cur(EV_RV, W), cur(EV_LORA, WL),
                prev(EV_R, W), prev(EV_RK, W), prev(EV_RV, W), prev(EV_LORA, WL)]
    args = [proj3] * 8
    if has_vres:
        in_specs.append(pl.BlockSpec((1, tb, W), lambda b, i: (b, i, 0)))
        args.append(v_first)
    in_specs += [_resident(mu.shape), _resident(mu_lora.shape), _resident(vecs.shape),
                 _resident(w2.shape), _resident(a2.shape), _resident(g2.shape)]
    args += [mu, mu_lora, vecs, w2, a2, g2]
    if has_vres:
        in_specs.append(_resident(v2.shape))
        args.append(v2)
    out_block = pl.BlockSpec((1, tb, W), lambda b, i: (b, i, 0))
    out_sds = jax.ShapeDtypeStruct((B, S, W), F32)
    res = pl.pallas_call(
        functools.partial(_rwkv_kernel, has_vres=has_vres),
        grid=(B, S // tb),
        in_specs=in_specs,
        out_specs=out_block if has_vres else [out_block, out_block],
        out_shape=out_sds if has_vres else [out_sds, out_sds],
        scratch_shapes=[pltpu.VMEM((tb + SUBLANES, W), F32),
                        pltpu.VMEM((tb + SUBLANES, WL), F32),
                        pltpu.VMEM((H, D, D), F32),
                        pltpu.VMEM((H, NC, C, D), F32),
                        pltpu.VMEM((H, NC, D, D), F32),
                        pltpu.VMEM((H, NC, D, D), F32),
                        pltpu.VMEM((H, NC, C, D), F32),
                        pltpu.VMEM((H, NC, 1, D), F32),
                        pltpu.VMEM((H, NC, C, D), F32)],
        compiler_params=_cparams(("parallel", "arbitrary")),
        name="rwkv_mixer",
    )(*args)
    if has_vres:
        return res, v_first
    return res[0], res[1]


def _dil_attn_kernel(q_ref, kc_ref, kp_ref, vc_ref, vp_ref, o_ref, lse_ref, kbuf, vbuf):
    Dh, Bk = DIL_HEAD_DIM, ATT_BLOCK
    QB = q_ref.shape[2]
    has_prev = pl.program_id(2) > 0
    kbuf[0:Bk] = kp_ref[0, 0]
    kbuf[Bk:] = kc_ref[0, 0]
    vbuf[0:Bk] = vp_ref[0, 0]
    vbuf[Bk:] = vc_ref[0, 0]
    qi = lax.broadcasted_iota(jnp.int32, (Bk, 2 * Bk), 0)
    cj = lax.broadcasted_iota(jnp.int32, (Bk, 2 * Bk), 1)
    band = (cj >= qi) & (cj <= qi + Bk)
    lane = lax.broadcasted_iota(jnp.int32, (Bk, LANES), 1)
    low = lane < Dh
    low_kv = lax.broadcasted_iota(jnp.int32, (2 * Bk, LANES), 1) < Dh
    one_kv = jnp.ones((2 * Bk, LANES), BF16)
    NP = DIL_WIDTH // LANES

    def pairs(x):
        return jnp.stack([x[:, hp * LANES:(hp + 1) * LANES] for hp in range(NP)], axis=0)

    for j in range(QB // Bk):
        ok = band if j > 0 else band & ((cj >= Bk) | has_prev)
        ok2 = jnp.concatenate([ok, ok], axis=0)
        q_pairs = pairs(q_ref[0, 0, j * Bk:(j + 1) * Bk, :])
        zero = jnp.zeros_like(q_pairs)
        q2 = jnp.concatenate([jnp.where(low, q_pairs, zero), jnp.where(low, zero, q_pairs)], axis=1)
        k_pairs = pairs(kbuf[j * Bk:(j + 2) * Bk, :])
        v_pairs = pairs(vbuf[j * Bk:(j + 2) * Bk, :])
        s = jnp.einsum('hqd,hkd->hqk', q2, k_pairs, preferred_element_type=F32)
        s = jnp.where(ok2, s, -jnp.inf)
        m = jnp.max(s, axis=-1, keepdims=True).reshape(2 * NP, Bk, 1)
        p = jnp.exp2(s - m.reshape(NP, 2 * Bk, 1)).astype(BF16).reshape(2 * NP, Bk, 2 * Bk)
        v_ext = jnp.stack([jnp.where(low_kv, v_pairs, one_kv), jnp.where(low_kv, one_kv, v_pairs)], axis=1)
        po = jnp.einsum('hqk,hkd->hqd', p, v_ext.reshape(2 * NP, 2 * Bk, LANES), preferred_element_type=F32)
        swapped = pltpu.roll(po.reshape(2 * NP * Bk, LANES), Dh, axis=1).reshape(2 * NP, Bk, LANES)
        r = po / swapped
        o_ref[0, 0, j * Bk:(j + 1) * Bk, :] = jnp.concatenate(
            [jnp.where(low, r[2 * hp], r[2 * hp + 1]) for hp in range(NP)], axis=-1).astype(o_ref.dtype)
        lse_blk = jnp.zeros((Bk, LANES), F32)
        for h in range(2 * NP):
            lse_blk = jnp.where(lane == h, m[h] + jnp.log2((swapped if h % 2 == 0 else po)[h]), lse_blk)
        lse_ref[0, 0, j * Bk:(j + 1) * Bk, :] = lse_blk


def dil_attention_group(q, k, v, qb=2 * ATT_BLOCK):
    B, d, L, W = q.shape
    nj = qb // ATT_BLOCK
    cur = pl.BlockSpec((1, 1, qb, W), lambda b, r, n: (b, r, n, 0))
    prev = pl.BlockSpec((1, 1, ATT_BLOCK, W), lambda b, r, n: (b, r, jnp.maximum(n * nj - 1, 0), 0))
    return pl.pallas_call(
        _dil_attn_kernel,
        grid=(B, d, L // qb),
        in_specs=[cur, cur, prev, cur, prev],
        out_specs=[cur, pl.BlockSpec((1, 1, qb, LANES), lambda b, r, n: (b, r, n, 0))],
        out_shape=[jax.ShapeDtypeStruct((B, d, L, W), BF16), jax.ShapeDtypeStruct((B, d, L, LANES), F32)],
        scratch_shapes=[pltpu.VMEM((ATT_BLOCK + qb, W), BF16), pltpu.VMEM((ATT_BLOCK + qb, W), BF16)],
        compiler_params=_cparams(("parallel", "parallel", "arbitrary")),
        name=f"dil_attn_d{d}",
    )(q, k, k, v, v)


def _out_proj_even_kernel(x_ref, ya_ref, yb_ref, wa_ref, wb_ref, o_ref):
    o_ref[...] = (x_ref[...]
                  + jnp.dot(ya_ref[...].astype(BF16), wa_ref[...], preferred_element_type=F32)
                  + jnp.dot(yb_ref[...].astype(BF16), wb_ref[...], preferred_element_type=F32))


def out_proj_even(x, ya, yb, wa, wb, tm=512):
    T, D = x.shape
    return pl.pallas_call(
        _out_proj_even_kernel,
        grid=(T // tm,),
        in_specs=[pl.BlockSpec((tm, D), lambda i: (i, 0)),
                  pl.BlockSpec((tm, ya.shape[1]), lambda i: (i, 0)),
                  pl.BlockSpec((tm, yb.shape[1]), lambda i: (i, 0)),
                  _resident(wa.shape), _resident(wb.shape)],
        out_specs=pl.BlockSpec((tm, D), lambda i: (i, 0)),
        out_shape=jax.ShapeDtypeStruct((T, D), F32),
        compiler_params=_cparams(("parallel",)),
        name="out_proj_even",
    )(x, ya, yb, wa, wb)


def _out_proj_odd_kernel(x_ref, o0_ref, o1_ref, o2_ref, l0_ref, l1_ref, l2_ref, e_ref, w_ref, out_ref,
                         a_ref, b_ref, la_ref, lb_ref):
    dils = [d for _, d in DIL_PATTERNS]
    lses = [_interleave_load(r, d, la_ref, lb_ref) for r, d in zip((l0_ref, l1_ref, l2_ref), dils)]
    m = jnp.maximum(jnp.maximum(lses[0], lses[1]), lses[2])
    es = [jnp.exp2(l - m) for l in lses]
    inv = 1.0 / (es[0] + es[1] + es[2])
    expand = e_ref[...]
    mixed = None
    for e, o_ref, d in zip(es, (o0_ref, o1_ref, o2_ref), dils):
        term = _group_sum(e * inv, expand) * _interleave_load(o_ref, d, a_ref, b_ref)
        mixed = term if mixed is None else mixed + term
    out_ref[0] = x_ref[0] + jnp.dot(mixed.astype(BF16), w_ref[...], preferred_element_type=F32)


def out_proj_odd(x3, outs, lses, w, tm=512):
    B, S, D = x3.shape
    W = DIL_WIDTH
    expand = np.zeros((LANES, W), np.float32)
    for h in range(DIL_HEADS):
        expand[h, h * DIL_HEAD_DIM:(h + 1) * DIL_HEAD_DIM] = 1.0
    row = pl.BlockSpec((1, tm, D), lambda b, i: (b, i, 0))

    def res(width):
        return [pl.BlockSpec((1, d, tm // d, width), lambda b, i: (b, 0, i, 0)) for _, d in DIL_PATTERNS]

    return pl.pallas_call(
        _out_proj_odd_kernel,
        grid=(B, S // tm),
        in_specs=[row, *res(W), *res(LANES), _resident((LANES, W)), _resident(w.shape)],
        out_specs=row,
        out_shape=jax.ShapeDtypeStruct((B, S, D), F32),
        scratch_shapes=[pltpu.VMEM((W // LANES, tm, LANES), F32), pltpu.VMEM((W // LANES, tm, LANES), F32),
                        pltpu.VMEM((1, tm, LANES), F32), pltpu.VMEM((1, tm, LANES), F32)],
        compiler_params=_cparams(("parallel", "parallel")),
        name="out_proj_odd",
    )(x3, *outs, *lses, jnp.asarray(expand, BF16), w)


MOE_TM = 512
MOE_RB = 128
MOE_NBLK = (MOE_TM + N_GROUPS * (MOE_RB - 1)) // MOE_RB
LOGIT_ROWS = 24


def _moe_route(h, wr_ref):
    NG, EPG, RB = N_GROUPS, EXPERTS_PER_GROUP, MOE_RB
    tm = h.shape[0]
    hb, h_lo = _hi_lo(h)
    nt = (((1,), (1,)), ((), ()))
    lg = lax.dot_general(wr_ref[...], hb, nt, preferred_element_type=F32)
    lg = lg[:LANES] + lg[LANES:] + lax.dot_general(wr_ref[:LANES], h_lo, nt, preferred_element_type=F32)
    L = lg[:LOGIT_ROWS]
    r = lax.broadcasted_iota(jnp.int32, L.shape, 0)
    big = jnp.int32(1 << 30)
    gl = jnp.where(r < NG, L, -jnp.inf)
    gmax = jnp.max(gl, axis=0, keepdims=True)
    gsel = jnp.min(jnp.where(gl == gmax, r, big), axis=0, keepdims=True)
    p_group = 1.0 / jnp.sum(jnp.where(r < NG, jnp.exp(L - gmax), 0.0), axis=0, keepdims=True)
    lo = NG + gsel * EPG
    el = jnp.where((r >= lo) & (r < lo + EPG), L, -jnp.inf)
    v1 = jnp.max(el, axis=0, keepdims=True)
    i1 = jnp.min(jnp.where(el == v1, r, big), axis=0, keepdims=True)
    el2 = jnp.where(r == i1, -jnp.inf, el)
    v2 = jnp.max(el2, axis=0, keepdims=True)
    i2 = jnp.min(jnp.where(el2 == v2, r, big), axis=0, keepdims=True)
    t = jnp.exp(v2 - v1)
    gates = jnp.where(r == i1, p_group / (1.0 + t), 0.0) + jnp.where(r == i2, p_group * t / (1.0 + t), 0.0)
    gate4 = [jnp.sum(jnp.where(r == lo + j, gates, 0.0), axis=0, keepdims=True) for j in range(EPG)]

    r8 = lax.broadcasted_iota(jnp.int32, (SUBLANES, tm), 0)
    onehot = (r8 == gsel).astype(F32)
    ti = lax.broadcasted_iota(jnp.int32, (tm, tm), 0)
    tj = lax.broadcasted_iota(jnp.int32, (tm, tm), 1)
    before = (ti < tj).astype(BF16)
    rank = jnp.dot(onehot.astype(BF16), before, preferred_element_type=F32)
    count = jnp.sum(onehot, axis=1, keepdims=True).astype(jnp.int32)
    assert RB & (RB - 1) == 0
    nblk = lax.shift_right_logical(count + (RB - 1), jnp.int32(RB.bit_length() - 1))
    padded = (nblk * RB).astype(F32)
    start, starts = jnp.zeros((1, 1), F32), []
    for g in range(NG):
        starts.append(start)
        start = start + padded[g:g + 1]
    start8 = jnp.concatenate(starts + [jnp.zeros((SUBLANES - NG, 1), F32)], axis=0)
    dest = jnp.sum(onehot * (start8 + rank), axis=0, keepdims=True)
    return dest, gate4, nblk


def _moe_kernel(x_ref, g_ref, wr_ref, w1_ref, w3_ref, w2_ref, fg_ref, o_ref,
                p_ref, xs_ref, gs_ref, hh_ref, ys_ref, *, final_norm):
    NG, EPG, RB, NBLK = N_GROUPS, EXPERTS_PER_GROUP, MOE_RB, MOE_NBLK
    FF = w1_ref.shape[2]
    tm = x_ref.shape[0]
    x = x_ref[...]
    h = _rmsnorm(x, g_ref[...])
    dest, gate4, nblk = _moe_route(h, wr_ref)
    rows = lax.broadcasted_iota(jnp.int32, (NBLK * RB, tm), 0)
    p_ref[...] = jnp.where(rows == dest.astype(jnp.int32), 1.0, 0.0).astype(BF16)
    xs_ref[...] = jnp.dot(p_ref[...], h.astype(BF16), preferred_element_type=F32).astype(BF16)
    g4 = jnp.concatenate(gate4 + [jnp.zeros((SUBLANES - EPG, tm), F32)], axis=0)
    g_hi = g4.astype(BF16).astype(F32)
    g2 = jnp.concatenate([g_hi, g4 - g_hi, jnp.zeros((LANES - 2 * SUBLANES, tm), F32)], axis=0).astype(BF16)
    gs_ref[...] = lax.dot_general(p_ref[...], g2, (((1,), (1,)), ((), ())), preferred_element_type=F32)

    ends, acc = [], jnp.int32(0)
    for g in range(NG):
        acc = acc + jnp.sum(nblk[g:g + 1])
        ends.append(acc)
    for b in range(NBLK):
        rs = slice(b * RB, (b + 1) * RB)
        grp = sum((b >= e).astype(jnp.int32) for e in ends[:-1])

        @pl.when(b < ends[-1])
        def _():
            xr = xs_ref[rs]
            gate = gs_ref[rs]
            for j in range(EPG):
                a = jnp.dot(xr, w1_ref[grp * EPG + j], preferred_element_type=F32)
                u = jnp.dot(xr, w3_ref[grp * EPG + j], preferred_element_type=F32)
                gj = gate[:, j:j + 1] + gate[:, SUBLANES + j:SUBLANES + j + 1]
                hh_ref[:, j * FF:(j + 1) * FF] = ((a * _sigmoid(a)) * u * gj).astype(BF16)
            ys_ref[rs] = jnp.dot(hh_ref[...], w2_ref[grp], preferred_element_type=F32).astype(BF16)

        @pl.when(b >= ends[-1])
        def _():
            ys_ref[rs] = jnp.zeros((RB, ys_ref.shape[1]), BF16)

    y = x + lax.dot_general(p_ref[...], ys_ref[...], (((0,), (0,)), ((), ())), preferred_element_type=F32)
    if final_norm:
        y = _rmsnorm(y, fg_ref[...])
    o_ref[...] = y


def moe_block(x, gain, w_router, w1, w3, w2, final_gain=None):
    T, D = x.shape
    NE, _, FF = w1.shape
    tm, RB, NBLK = MOE_TM, MOE_RB, MOE_NBLK
    final_norm = final_gain is not None
    fg = (final_gain if final_norm else jnp.ones((D,), F32)).reshape(1, D)
    return pl.pallas_call(
        functools.partial(_moe_kernel, final_norm=final_norm),
        grid=(T // tm,),
        in_specs=[pl.BlockSpec((tm, D), lambda i: (i, 0)), _resident((1, D)), _resident(w_router.shape),
                  _resident(w1.shape), _resident(w3.shape), _resident(w2.shape), _resident((1, D))],
        out_specs=pl.BlockSpec((tm, D), lambda i: (i, 0)),
        out_shape=jax.ShapeDtypeStruct((T, D), F32),
        scratch_shapes=[pltpu.VMEM((NBLK * RB, tm), BF16), pltpu.VMEM((NBLK * RB, D), BF16),
                        pltpu.VMEM((NBLK * RB, LANES), F32), pltpu.VMEM((RB, EXPERTS_PER_GROUP * FF), BF16),
                        pltpu.VMEM((NBLK * RB, D), BF16)],
        compiler_params=_cparams(("parallel",)),
        name="moe_block",
    )(x, gain.reshape(1, D), w_router, w1, w3, w2, fg)


def _router_weights(w_group, w_expert):
    w = _place([(0, w_group)] + [(N_GROUPS + g * EXPERTS_PER_GROUP, w_expert[g]) for g in range(N_GROUPS)], LANES).T
    return jnp.concatenate(_hi_lo(w), axis=0)


def _place(cols, total):
    lead = cols[0][1].shape[:-1]
    parts, pos = [], 0
    for off, arr in sorted(cols, key=lambda c: c[0]):
        if off > pos:
            parts.append(jnp.zeros(lead + (off - pos,), F32))
        parts.append(arr)
        pos = off + arr.shape[-1]
    if total > pos:
        parts.append(jnp.zeros(lead + (total - pos,), F32))
    return jnp.concatenate(parts, axis=-1)


def _even_layout(t, vres=None):
    gq, gk, gv, gz, ga, gb, rr, rk, rv, lw, la, lg = jnp.split(
        t, np.cumsum([512, 512, 512, 512, 4, 4, 512, 512, 512, 32, 32, 96])[:-1].tolist(), axis=-1)
    cols = [(EV_Q, gq), (EV_K, gk), (EV_V, gv), (EV_Z, gz), (EV_AB, ga), (EV_AB + GDN_HEADS, gb),
            (EV_R, rr), (EV_RK, rk), (EV_RV, rv),
            (EV_LORA, lw), (EV_LORA + LORA_SEG, la), (EV_LORA + 2 * LORA_SEG, lg)]
    if vres is not None:
        cols.append((EV_LORA + 3 * LORA_SEG, vres))
    return _place(cols, EV_NPAD)


def _pad_rows(w, rows):
    return jnp.concatenate([w, jnp.zeros((rows - w.shape[0], w.shape[1]), F32)], axis=0)


def kernel(x, positions, ev_norm, ev_w_in, rwkv_vres_down, ev_w_out, gdn_conv_w, gdn_A_log, gdn_dt_bias, gdn_norm,
           rwkv_mu, rwkv_w0, rwkv_w2, rwkv_a0, rwkv_a2, rwkv_g2, rwkv_k_k, rwkv_k_a, rwkv_r_k, rwkv_ln_w, rwkv_ln_b,
           rwkv_vres_mu, rwkv_v0, rwkv_v2, od_norm, od_w_in, od_w_out, ffn_norm, moe_w_group, moe_w_expert,
           moe_w1, moe_w3, moe_w2, final_norm):
    B, S, D = x.shape
    T = B * S
    depth = ffn_norm.shape[0]
    xf = x.reshape(T, D)
    cos, sin = rope_tables(positions.reshape(T))
    n_groups = len(DIL_PATTERNS)
    v_first = None
    for layer in range(depth):
        i = layer // 2
        if layer % 2 == 0:
            vres_w = None if i == 0 else rwkv_vres_down[i - 1]
            w_in = _even_layout(ev_w_in[i], vres_w).astype(BF16)
            proj3 = norm_proj(xf, ev_norm[i], w_in).reshape(B, S, EV_NPAD)
            mu_r, mu_k, mu_v, mu_w, mu_a, mu_g = jnp.split(
                rwkv_mu[i], np.cumsum([RWKV_DIM, RWKV_DIM, RWKV_DIM, 32, 32])[:].tolist())
            mu = jnp.stack([mu_r, mu_k, mu_v])
            lora_cols = [(0, mu_w), (LORA_SEG, mu_a), (2 * LORA_SEG, mu_g)]
            if i > 0:
                lora_cols.append((3 * LORA_SEG, rwkv_vres_mu[i - 1]))
            mu_lora = _place(lora_cols, 4 * LORA_SEG).reshape(1, 4 * LORA_SEG)
            vec_rows = [rwkv_w0[i], rwkv_a0[i], rwkv_k_k[i], rwkv_k_a[i], rwkv_r_k[i].reshape(-1), rwkv_ln_w[i], rwkv_ln_b[i]]
            if i > 0:
                vec_rows.append(rwkv_v0[i - 1])
            vecs = jnp.stack(vec_rows)
            w2 = _pad_rows(rwkv_w2[i], LORA_SEG)
            a2 = _pad_rows(rwkv_a2[i], LORA_SEG)
            g2 = _pad_rows(rwkv_g2[i], LORA_SEG)
            ya = gdn_mixer(proj3, gdn_conv_w[i], gdn_A_log[i], gdn_dt_bias[i], gdn_norm[i])
            if i == 0:
                yb, v_first = rwkv_mixer(proj3, mu, mu_lora, vecs, w2, a2, g2)
            else:
                yb, _ = rwkv_mixer(proj3, mu, mu_lora, vecs, w2, a2, g2, v_first, _pad_rows(rwkv_v2[i - 1], LORA_SEG))
            w_out = ev_w_out[i].astype(BF16)
            na = GDN_HEADS * GDN_DV
            xf = out_proj_even(xf, ya.reshape(T, na), yb.reshape(T, RWKV_DIM), w_out[:na], w_out[na:])
        else:
            w_in = od_w_in[i].astype(BF16)
            qkv = norm_proj_rope(xf.reshape(B, S, D), od_norm[i], w_in, cos.reshape(B, S, LANES),
                                 sin.reshape(B, S, LANES), DIL_HEAD_DIM ** -0.5 * math.log2(math.e))
            outs, lses = [], []
            for gi, (window, dilation) in enumerate(DIL_PATTERNS):
                assert window // dilation == ATT_BLOCK and (S // dilation) % (2 * ATT_BLOCK) == 0
                o, lse = dil_attention_group(qkv[gi], qkv[n_groups + gi], qkv[2 * n_groups + gi])
                outs.append(o)
                lses.append(lse)
            xf = out_proj_odd(xf.reshape(B, S, D), outs, lses, od_w_out[i].astype(BF16)).reshape(T, D)
        w_router = _router_weights(moe_w_group[layer], moe_w_expert[layer])
        w2_grouped = moe_w2[layer].astype(BF16).reshape(N_GROUPS, -1, D)
        xf = moe_block(xf, ffn_norm[layer], w_router, moe_w1[layer].astype(BF16), moe_w3[layer].astype(BF16),
                       w2_grouped, final_norm if layer == depth - 1 else None)
    return xf.reshape(B, S, D)
```

```python
import functools
import math

import jax
import jax.numpy as jnp
import numpy as np
from jax import lax
from jax.experimental import pallas as pl
from jax.experimental.pallas import tpu as pltpu

F32 = jnp.float32
BF16 = jnp.bfloat16

RMS_EPS = 1e-6
L2_EPS = 1e-6
RWKV_LN_EPS = 64e-5
ROPE_THETA = 10000.0

LANES = 128
SUBLANES = 8
VMEM_LIMIT = 56 * 1024 * 1024

GDN_HEADS = 4
GDN_DK = 128
GDN_DV = 128
GDN_CONV = 4
RWKV_HEADS = 8
RWKV_HEAD = 64
RWKV_DIM = RWKV_HEADS * RWKV_HEAD
LORA_SEG = LANES
DIL_PATTERNS = ((128, 1), (512, 4), (2048, 16))
DIL_HEADS = 8
DIL_HEAD_DIM = 64
DIL_WIDTH = DIL_HEADS * DIL_HEAD_DIM
ATT_BLOCK = 128
N_GROUPS = 4
EXPERTS_PER_GROUP = 4
N_EXPERTS = N_GROUPS * EXPERTS_PER_GROUP
CHUNK = 64

EV_Q, EV_K, EV_V, EV_Z = 0, 512, 1024, 1536
EV_R, EV_RK, EV_RV, EV_LORA = 2048, 2560, 3072, 3584
EV_AB = EV_LORA + 4 * LORA_SEG
EV_NPAD = EV_AB + LANES


def _cparams(sem):
    return pltpu.CompilerParams(dimension_semantics=sem, vmem_limit_bytes=VMEM_LIMIT)


def _sigmoid(x):
    return 0.5 * jnp.tanh(0.5 * x) + 0.5


def _softplus(x):
    return jnp.maximum(x, 0.0) + jnp.log(1.0 + jnp.exp(-jnp.abs(x)))


def _rmsnorm(x, gain):
    return x * lax.rsqrt(jnp.mean(x * x, axis=-1, keepdims=True) + RMS_EPS) * gain


def _hi_lo(v):
    hi = v.astype(BF16)
    return hi, (v - hi.astype(F32)).astype(BF16)


def _group_sum(x, block_ones):
    hi, lo = _hi_lo(x)
    return (jnp.dot(hi, block_ones, preferred_element_type=F32)
            + jnp.dot(lo, block_ones, preferred_element_type=F32))


def _block_ones(width, group):
    idx = np.arange(width) // group
    return jnp.asarray(idx[:, None] == idx[None, :], BF16)


def _resident(shape):
    nd = len(shape)
    return pl.BlockSpec(shape, lambda *_: (0,) * nd, pipeline_mode=pl.Buffered(1))


def _rope_table_kernel(pos_ref, freq_ref, sign_ref, cos_ref, sin_ref):
    ang = pos_ref[...].astype(F32) * freq_ref[...]
    cos_ref[...] = jnp.cos(ang)
    sin_ref[...] = jnp.sin(ang) * sign_ref[...]


def rope_tables(positions_flat, tm=1024):
    T = positions_flat.shape[0]
    half = DIL_HEAD_DIM // 2
    inv_freq = ROPE_THETA ** (-jnp.arange(half, dtype=F32) * 2.0 / DIL_HEAD_DIM)
    lane = np.arange(LANES)
    freq = inv_freq[lane % half][None, :]
    sign = jnp.asarray(np.where(lane % DIL_HEAD_DIM < half, -1.0, 1.0), F32)[None, :]
    return pl.pallas_call(
        _rope_table_kernel,
        grid=(T // tm,),
        in_specs=[pl.BlockSpec((tm, 1), lambda i: (i, 0)), _resident((1, LANES)), _resident((1, LANES))],
        out_specs=[pl.BlockSpec((tm, LANES), lambda i: (i, 0))] * 2,
        out_shape=[jax.ShapeDtypeStruct((T, LANES), F32)] * 2,
        compiler_params=_cparams(("parallel",)),
        name="rope_tables",
    )(positions_flat.reshape(T, 1), freq, sign)


def _norm_proj_kernel(x_ref, g_ref, w_ref, o_ref):
    h = _rmsnorm(x_ref[...], g_ref[...]).astype(BF16)
    o_ref[...] = jnp.dot(h, w_ref[...], preferred_element_type=F32)


def norm_proj(x, gain, w, tm=512):
    T, D = x.shape
    N = w.shape[1]
    return pl.pallas_call(
        _norm_proj_kernel,
        grid=(T // tm,),
        in_specs=[pl.BlockSpec((tm, D), lambda i: (i, 0)), _resident((1, D)), _resident((D, N))],
        out_specs=pl.BlockSpec((tm, N), lambda i: (i, 0)),
        out_shape=jax.ShapeDtypeStruct((T, N), F32),
        compiler_params=_cparams(("parallel",)),
        name="norm_proj",
    )(x, gain.reshape(1, D), w)


def _deinterleave_store(out_ref, val, d, a_ref, b_ref):
    tm = val.shape[0]
    if d == 1:
        out_ref[0, 0] = val.astype(out_ref.dtype)
        return
    quarter = tm // 4
    for j in range(val.shape[1] // LANES):
        cs = slice(j * LANES, (j + 1) * LANES)
        a_ref[j] = val[:, cs]
        if d == 4:
            for r in range(4):
                out_ref[0, r, :, cs] = a_ref[j, pl.ds(r, quarter, stride=4), :].astype(out_ref.dtype)
            continue
        assert d == 16
        for r1 in range(4):
            b_ref[j, r1 * quarter:(r1 + 1) * quarter, :] = a_ref[j, pl.ds(r1, quarter, stride=4), :]
        for r1 in range(4):
            for r2 in range(4):
                out_ref[0, r1 + 4 * r2, :, cs] = (
                    b_ref[j, pl.ds(r1 * quarter + r2, quarter // 4, stride=4), :].astype(out_ref.dtype))


def _interleave_load(in_ref, d, a_ref, b_ref):
    if d == 1:
        return in_ref[0, 0].astype(F32)
    n_tiles, tm, _ = a_ref.shape
    quarter = tm // 4
    for j in range(n_tiles):
        cs = slice(j * LANES, (j + 1) * LANES)
        if d == 4:
            for r in range(4):
                a_ref[j, pl.ds(r, quarter, stride=4), :] = in_ref[0, r, :, cs].astype(F32)
            continue
        assert d == 16
        for r1 in range(4):
            for r2 in range(4):
                b_ref[j, pl.ds(r1 * quarter + r2, quarter // 4, stride=4), :] = in_ref[0, r1 + 4 * r2, :, cs].astype(F32)
        for r1 in range(4):
            a_ref[j, pl.ds(r1, quarter, stride=4), :] = b_ref[j, r1 * quarter:(r1 + 1) * quarter, :]
    return jnp.concatenate([a_ref[j] for j in range(n_tiles)], axis=-1)


def _norm_proj_rope_kernel(x_ref, g_ref, w_ref, cos_ref, sin_ref, *rest, q_scale):
    out_refs, (a_ref, b_ref) = rest[:-2], rest[-2:]
    n_groups = len(DIL_PATTERNS)
    h = _rmsnorm(x_ref[0], g_ref[...]).astype(BF16)
    cos = cos_ref[0]
    sin = sin_ref[0]
    lane = lax.broadcasted_iota(jnp.int32, cos.shape, 1)
    first_half = (lane % DIL_HEAD_DIM) < (DIL_HEAD_DIM // 2)
    W = DIL_WIDTH
    for c, out_ref in enumerate(out_refs):
        which, g = divmod(c, n_groups)
        acc = jnp.dot(h, w_ref[:, c * W:(c + 1) * W], preferred_element_type=F32)
        if which < 2:
            parts = []
            for j in range(W // LANES):
                blk = acc[:, j * LANES:(j + 1) * LANES]
                rot = jnp.where(first_half, pltpu.roll(blk, LANES - 32, axis=1), pltpu.roll(blk, 32, axis=1))
                out = blk * cos + rot * sin
                parts.append(out * q_scale if which == 0 else out)
            acc = jnp.concatenate(parts, axis=-1)
        _deinterleave_store(out_ref, acc, DIL_PATTERNS[g][1], a_ref, b_ref)


def norm_proj_rope(x3, gain, w, cos3, sin3, q_scale, tm=512):
    B, S, D = x3.shape
    W = DIL_WIDTH
    out_specs, out_shape = [], []
    for _ in range(3):
        for _, d in DIL_PATTERNS:
            out_specs.append(pl.BlockSpec((1, d, tm // d, W), lambda b, i: (b, 0, i, 0)))
            out_shape.append(jax.ShapeDtypeStruct((B, d, S // d, W), BF16))
    return pl.pallas_call(
        functools.partial(_norm_proj_rope_kernel, q_scale=q_scale),
        grid=(B, S // tm),
        in_specs=[pl.BlockSpec((1, tm, D), lambda b, i: (b, i, 0)), _resident((1, D)), _resident(w.shape),
                  pl.BlockSpec((1, tm, LANES), lambda b, i: (b, i, 0)),
                  pl.BlockSpec((1, tm, LANES), lambda b, i: (b, i, 0))],
        out_specs=out_specs,
        out_shape=out_shape,
        scratch_shapes=[pltpu.VMEM((W // LANES, tm, LANES), F32), pltpu.VMEM((W // LANES, tm, LANES), F32)],
        compiler_params=_cparams(("parallel", "parallel")),
        name="norm_proj_rope",
    )(x3, gain.reshape(1, D), w, cos3, sin3)


def _bmm(a, b):
    return jnp.einsum('nik,nkj->nij', a.astype(BF16), b.astype(BF16), preferred_element_type=F32)


def _bmm_nt(a, b):
    return jnp.einsum('nik,njk->nij', a.astype(BF16), b.astype(BF16), preferred_element_type=F32)


def _bmm_tn(a, b):
    return jnp.einsum('nci,ncj->nij', a.astype(BF16), b.astype(BF16), preferred_element_type=F32)


def _unit_lower_inverse(L):
    C = L.shape[-1]
    ri = lax.broadcasted_iota(jnp.int32, (C, C), 0)
    ci = lax.broadcasted_iota(jnp.int32, (C, C), 1)
    same16 = (ri // 16) == (ci // 16)
    same32 = (ri // 32) == (ci // 32)
    eye = (ri == ci).astype(F32)
    Ld = jnp.where(same16, L, 0.0)
    X = eye - Ld
    P = Ld
    for _ in range(3):
        P = _bmm(P, P)
        X = X + _bmm(X, P)
    for off in (jnp.where(same32 & ~same16, L, 0.0), jnp.where(~same32, L, 0.0)):
        X = X - _bmm(_bmm(X, off), X)
    return X


def _shifted_rows(buf_ref, cur, prev, first, shifts):
    TB = cur.shape[0]
    buf_ref[0:SUBLANES, :] = jnp.where(first, 0.0, prev)
    buf_ref[SUBLANES:, :] = cur
    return [buf_ref[SUBLANES - s:SUBLANES - s + TB, :] for s in shifts]


def _split_heads(x, n_heads, width):
    return jnp.stack([x[:, h * width:(h + 1) * width] for h in range(n_heads)], axis=0)


def _merge_heads(x):
    return jnp.concatenate([x[h] for h in range(x.shape[0])], axis=-1)


def _gdn_kernel(q_ref, k_ref, v_ref, z_ref, ab_ref, qp_ref, kp_ref, vp_ref, cw_ref, alog_ref, dtb_ref, nw_ref,
                o_ref, buf_ref, s_ref, qd_ref, pg_ref, qg_ref, ov_ref, egl_ref, oc_ref):
    H, DK, DV, C = GDN_HEADS, GDN_DK, GDN_DV, CHUNK
    TB = q_ref.shape[1]
    NC = TB // C
    first = pl.program_id(1) == 0

    @pl.when(first)
    def _():
        s_ref[...] = jnp.zeros_like(s_ref)

    def conv_silu(cur_ref, prev_ref, j):
        cur = cur_ref[0]
        x3, x2, x1 = _shifted_rows(buf_ref, cur, prev_ref[0], first, (3, 2, 1))
        w = cw_ref[j]
        y = x3 * w[0:1] + x2 * w[1:2] + x1 * w[2:3] + cur * w[3:4]
        return y * _sigmoid(y)

    q = _split_heads(conv_silu(q_ref, qp_ref, 0), H, DK)
    k = _split_heads(conv_silu(k_ref, kp_ref, 1), H, DK)
    v = _split_heads(conv_silu(v_ref, vp_ref, 2), H, DV)
    q = q * lax.rsqrt(jnp.sum(q * q, axis=-1, keepdims=True) + L2_EPS) * (DK ** -0.5)
    k = k * lax.rsqrt(jnp.sum(k * k, axis=-1, keepdims=True) + L2_EPS)
    ab = ab_ref[0]
    a = jnp.stack([ab[:, h:h + 1] for h in range(H)], axis=0)
    b = jnp.stack([ab[:, H + h:H + h + 1] for h in range(H)], axis=0)
    beta = _sigmoid(b)
    g = -jnp.exp(alog_ref[...]) * _softplus(a + dtb_ref[...])

    N = H * NC
    q = q.reshape(N, C, DK)
    k = k.reshape(N, C, DK)
    v = v.reshape(N, C, DV)
    beta = beta.reshape(N, C, 1)
    g = g.reshape(N, C, 1)

    ri = lax.broadcasted_iota(jnp.int32, (C, C), 0)
    ci = lax.broadcasted_iota(jnp.int32, (C, C), 1)
    causal = ri >= ci
    strict = ri > ci
    gb = jnp.broadcast_to(g, (N, C, C))
    g_row = jnp.sum(jnp.where(ri == ci, gb, 0.0), axis=1, keepdims=True)
    gc_col = jnp.sum(jnp.where(causal, jnp.broadcast_to(g_row, (N, C, C)), 0.0), axis=2, keepdims=True)
    gc_row = jnp.sum(jnp.where(ri <= ci, gb, 0.0), axis=1, keepdims=True)
    decay = jnp.where(causal, jnp.exp(jnp.where(causal, gc_col - gc_row, 0.0)), 0.0)

    kb = k * beta
    vb = v * beta
    kq = _bmm_nt(jnp.concatenate([kb, q], axis=1), k)
    L = jnp.where(strict, kq[:, :C] * decay, 0.0)
    aqk = kq[:, C:] * decay
    tinv = _unit_lower_inverse(L)
    egc = jnp.exp(gc_col)
    uw = _bmm(tinv, jnp.concatenate([vb, kb * egc], axis=2))
    auw = _bmm(aqk, uw)
    g_last = gc_col[:, C - 1:C, :]
    k_dec = k * jnp.exp(g_last - gc_col)
    kuw = _bmm_tn(k_dec, uw)
    qd_ref[...] = (q * egc - auw[:, :, DV:]).reshape(H, NC, C, DK)
    ov_ref[...] = auw[:, :, :DV].reshape(H, NC, C, DV)
    pg_ref[...] = kuw[:, :, DV:].reshape(H, NC, DK, DK)
    qg_ref[...] = kuw[:, :, :DV].reshape(H, NC, DK, DV)
    egl_ref[...] = jnp.broadcast_to(jnp.exp(g_last), (N, 1, LANES)).reshape(H, NC, 1, LANES)

    for c in range(NC):
        S = s_ref[...]
        oc_ref[:, c] = _bmm(qd_ref[:, c], S) + ov_ref[:, c]
        s_ref[...] = S * egl_ref[:, c] - _bmm(pg_ref[:, c], S) + qg_ref[:, c]

    o = oc_ref[...].reshape(H, TB, DV)
    o = o * lax.rsqrt(jnp.mean(o * o, axis=-1, keepdims=True) + RMS_EPS) * nw_ref[...]
    z = _split_heads(z_ref[0], H, DV)
    o_ref[0] = _merge_heads(o * (z * _sigmoid(z)))


def gdn_mixer(proj3, conv_w, a_log, dt_bias, norm_w, tb=512):
    B, S, _ = proj3.shape
    H, DK, DV, C = GDN_HEADS, GDN_DK, GDN_DV, CHUNK
    W = H * DK
    NC = tb // C
    nq, nk, nv, nz, nab = EV_Q // W, EV_K // W, EV_V // W, EV_Z // W, EV_AB // LANES
    rows8 = tb // SUBLANES

    def cur(cb):
        return pl.BlockSpec((1, tb, W), lambda b, i, cb=cb: (b, i, cb))

    def prev(cb):
        return pl.BlockSpec((1, SUBLANES, W), lambda b, i, cb=cb: (b, jnp.maximum(i * rows8 - 1, 0), cb))

    cw = conv_w.reshape(GDN_CONV, 3, W).transpose(1, 0, 2)
    return pl.pallas_call(
        _gdn_kernel,
        grid=(B, S // tb),
        in_specs=[cur(nq), cur(nk), cur(nv), cur(nz),
                  pl.BlockSpec((1, tb, LANES), lambda b, i: (b, i, nab)),
                  prev(nq), prev(nk), prev(nv),
                  _resident((3, GDN_CONV, W)), _resident((H, 1, 1)), _resident((H, 1, 1)),
                  _resident((1, 1, DV))],
        out_specs=pl.BlockSpec((1, tb, H * DV), lambda b, i: (b, i, 0)),
        out_shape=jax.ShapeDtypeStruct((B, S, H * DV), F32),
        scratch_shapes=[pltpu.VMEM((tb + SUBLANES, W), F32),
                        pltpu.VMEM((H, DK, DV), F32),
                        pltpu.VMEM((H, NC, C, DK), F32),
                        pltpu.VMEM((H, NC, DK, DK), F32),
                        pltpu.VMEM((H, NC, DK, DV), F32),
                        pltpu.VMEM((H, NC, C, DV), F32),
                        pltpu.VMEM((H, NC, 1, LANES), F32),
                        pltpu.VMEM((H, NC, C, DV), F32)],
        compiler_params=_cparams(("parallel", "arbitrary")),
        name="gdn_mixer",
    )(proj3, proj3, proj3, proj3, proj3, proj3, proj3, proj3, cw,
      a_log.reshape(H, 1, 1), dt_bias.reshape(H, 1, 1), norm_w.reshape(1, 1, DV))


def _rwkv_kernel(*refs, has_vres):
    if has_vres:
        (r_ref, k_ref, v_ref, lo_ref, rp_ref, kp_ref, vp_ref, lop_ref, vf_ref,
         mu_ref, mul_ref, vec_ref, w2_ref, a2_ref, g2_ref, v2_ref,
         y_ref, buf_ref, bufl_ref, s_ref, rr_ref, pp_ref, qq_ref, yv_ref, egc_ref, yc_ref) = refs
    else:
        (r_ref, k_ref, v_ref, lo_ref, rp_ref, kp_ref, vp_ref, lop_ref,
         mu_ref, mul_ref, vec_ref, w2_ref, a2_ref, g2_ref,
         y_ref, vf_out_ref, buf_ref, bufl_ref, s_ref, rr_ref, pp_ref, qq_ref, yv_ref, egc_ref, yc_ref) = refs
    H, D, C = RWKV_HEADS, RWKV_HEAD, CHUNK
    TB = r_ref.shape[1]
    NC = TB // C
    first = pl.program_id(1) == 0

    @pl.when(first)
    def _():
        s_ref[...] = jnp.zeros_like(s_ref)

    def mix(cur_ref, prev_ref, mu, buf):
        cur = cur_ref[0]
        (sh,) = _shifted_rows(buf, cur, prev_ref[0], first, (1,))
        return cur + mu * (sh - cur)

    r = mix(r_ref, rp_ref, mu_ref[0:1], buf_ref)
    k = mix(k_ref, kp_ref, mu_ref[1:2], buf_ref)
    v = mix(v_ref, vp_ref, mu_ref[2:3], buf_ref)
    lo = mix(lo_ref, lop_ref, mul_ref[...], bufl_ref)
    hw, ha, hg = (lo[:, j * LORA_SEG:(j + 1) * LORA_SEG] for j in range(3))
    w0, a0, k_k, k_a, r_k, ln_w, ln_b = (vec_ref[j:j + 1] for j in range(7))

    w_log = -_softplus(-(w0 + jnp.dot(jnp.tanh(hw), w2_ref[...], preferred_element_type=F32))) - 0.5
    lw = -jnp.exp(w_log)
    a = _sigmoid(a0 + jnp.dot(ha, a2_ref[...], preferred_element_type=F32))
    gate = jnp.dot(_sigmoid(hg), g2_ref[...], preferred_element_type=F32)
    if has_vres:
        hv = lo[:, 3 * LORA_SEG:4 * LORA_SEG]
        v0 = vec_ref[7:8]
        v = v + (vf_ref[0] - v) * _sigmoid(v0 + jnp.dot(hv, v2_ref[...], preferred_element_type=F32))
    else:
        vf_out_ref[0] = v

    ri = lax.broadcasted_iota(jnp.int32, (C, C), 0)
    ci = lax.broadcasted_iota(jnp.int32, (C, C), 1)
    tril = (ri >= ci).astype(BF16)
    lw_hi = lw.astype(BF16)
    lw_rest = lw - lw_hi.astype(F32)
    lw_mid = lw_rest.astype(BF16)
    lw_lo = (lw_rest - lw_mid.astype(F32)).astype(BF16)
    gi = jnp.concatenate(
        [sum(jnp.dot(tril, part[c * C:(c + 1) * C], preferred_element_type=F32) for part in (lw_hi, lw_mid, lw_lo))
         for c in range(NC)], axis=0)

    kk = _split_heads(k * k_k, H, D)
    kk = kk * lax.rsqrt(jnp.sum(kk * kk, axis=-1, keepdims=True) + L2_EPS)
    k = k * (1.0 + (a - 1.0) * k_a)
    r_h = _split_heads(r, H, D)
    k_h = _split_heads(k, H, D)
    v_h = _split_heads(v, H, D)
    a_h = _split_heads(a, H, D)
    gi_h = _split_heads(gi, H, D)
    lw_h = _split_heads(lw, H, D)
    bonus = jnp.sum(r_h * k_h * _split_heads(r_k, H, D), axis=-1, keepdims=True) * v_h

    N = H * NC
    rc, kc, vc, kkc, ac, gic, lwc = (t.reshape(N, C, D) for t in (r_h, k_h, v_h, kk, a_h, gi_h, lw_h))
    g_end = gic[:, C - 1:C, :]
    e_neg = jnp.exp(-gic)
    e_tail = jnp.exp(g_end - gic)
    at = -kkc * jnp.exp(gic - lwc)
    bvec = kkc * ac
    rt = rc * jnp.exp(gic)
    bt = bvec * e_neg
    kt = kc * e_neg
    b_dec = bvec * e_tail
    k_dec = kc * e_tail
    strict = ri > ci
    causal = ri >= ci
    amat = _bmm_nt(jnp.concatenate([at, rt], axis=1), jnp.concatenate([bt, kt], axis=1))
    a_ab = jnp.where(strict, amat[:, :C, :C], 0.0)
    a_ak = jnp.where(strict, amat[:, :C, C:], 0.0)
    a_r = jnp.concatenate([jnp.where(causal, amat[:, C:, :C], 0.0), jnp.where(causal, amat[:, C:, C:], 0.0)], axis=2)
    tinv = _unit_lower_inverse(-a_ab)
    x2 = _bmm(tinv, jnp.concatenate([at, _bmm(a_ak, vc)], axis=2))
    xv = jnp.concatenate([x2, jnp.concatenate([jnp.zeros_like(vc), vc], axis=2)], axis=1)
    y2 = _bmm(a_r, xv)
    pq = _bmm_tn(xv, jnp.concatenate([b_dec, k_dec], axis=1))
    rr_ref[...] = (rt + y2[:, :, :D]).reshape(H, NC, C, D)
    yv_ref[...] = y2[:, :, D:].reshape(H, NC, C, D)
    pp_ref[...] = pq[:, :D].reshape(H, NC, D, D)
    qq_ref[...] = pq[:, D:].reshape(H, NC, D, D)
    egc_ref[...] = jnp.exp(g_end).reshape(H, NC, 1, D)

    for c in range(NC):
        S = s_ref[...]
        yc_ref[:, c] = _bmm_nt(rr_ref[:, c], S) + yv_ref[:, c]
        s_ref[...] = S * egc_ref[:, c] + _bmm(S, pp_ref[:, c]) + qq_ref[:, c]

    y = yc_ref[...].reshape(H, TB, D)
    mean = jnp.mean(y, axis=-1, keepdims=True)
    yc = y - mean
    var = jnp.mean(yc * yc, axis=-1, keepdims=True)
    y = _merge_heads(yc * lax.rsqrt(var + RWKV_LN_EPS)) * ln_w + ln_b
    y_ref[0] = (y + _merge_heads(bonus)) * gate


def rwkv_mixer(proj3, mu, mu_lora, vecs, w2, a2, g2, v_first=None, v2=None, tb=256):
    B, S, _ = proj3.shape
    H, D, C = RWKV_HEADS, RWKV_HEAD, CHUNK
    W = RWKV_DIM
    WL = 4 * LORA_SEG
    NC = tb // C
    rows8 = tb // SUBLANES
    has_vres = v_first is not None

    def cur(col, width):
        return pl.BlockSpec((1, tb, width), lambda b, i, cb=col // width: (b, i, cb))

    def prev(col, width):
        return pl.BlockSpec((1, SUBLANES, width),
                            lambda b, i, cb=col // width: (b, jnp.maximum(i * rows8 - 1, 0), cb))

    in_specs = [cur(EV_R, W), cur(EV_RK, W), cur(EV_RV, W), cur(EV_LORA, WL),
                prev(EV_R, W), prev(EV_RK, W), prev(EV_RV, W), prev(EV_LORA, WL)]
    args = [proj3] * 8
    if has_vres:
        in_specs.append(pl.BlockSpec((1, tb, W), lambda b, i: (b, i, 0)))
        args.append(v_first)
    in_specs += [_resident(mu.shape), _resident(mu_lora.shape), _resident(vecs.shape),
                 _resident(w2.shape), _resident(a2.shape), _resident(g2.shape)]
    args += [mu, mu_lora, vecs, w2, a2, g2]
    if has_vres:
        in_specs.append(_resident(v2.shape))
        args.append(v2)
    out_block = pl.BlockSpec((1, tb, W), lambda b, i: (b, i, 0))
    out_sds = jax.ShapeDtypeStruct((B, S, W), F32)
    res = pl.pallas_call(
        functools.partial(_rwkv_kernel, has_vres=has_vres),
        grid=(B, S // tb),
        in_specs=in_specs,
        out_specs=out_block if has_vres else [out_block, out_block],
        out_shape=out_sds if has_vres else [out_sds, out_sds],
        scratch_shapes=[pltpu.VMEM((tb + SUBLANES, W), F32),
                        pltpu.VMEM((tb + SUBLANES, WL), F32),
                        pltpu.VMEM((H, D, D), F32),
                        pltpu.VMEM((H, NC, C, D), F32),
                        pltpu.VMEM((H, NC, D, D), F32),
                        pltpu.VMEM((H, NC, D, D), F32),
                        pltpu.VMEM((H, NC, C, D), F32),
                        pltpu.VMEM((H, NC, 1, D), F32),
                        pltpu.VMEM((H, NC, C, D), F32)],
        compiler_params=_cparams(("parallel", "arbitrary")),
        name="rwkv_mixer",
    )(*args)
    if has_vres:
        return res, v_first
    return res[0], res[1]


def _dil_attn_kernel(q_ref, kc_ref, kp_ref, vc_ref, vp_ref, o_ref, lse_ref, kbuf, vbuf):
    Dh, Bk = DIL_HEAD_DIM, ATT_BLOCK
    QB = q_ref.shape[2]
    has_prev = pl.program_id(2) > 0
    kbuf[0:Bk] = kp_ref[0, 0]
    kbuf[Bk:] = kc_ref[0, 0]
    vbuf[0:Bk] = vp_ref[0, 0]
    vbuf[Bk:] = vc_ref[0, 0]
    qi = lax.broadcasted_iota(jnp.int32, (Bk, 2 * Bk), 0)
    cj = lax.broadcasted_iota(jnp.int32, (Bk, 2 * Bk), 1)
    band = (cj >= qi) & (cj <= qi + Bk)
    lane = lax.broadcasted_iota(jnp.int32, (Bk, LANES), 1)
    low = lane < Dh
    low_kv = lax.broadcasted_iota(jnp.int32, (2 * Bk, LANES), 1) < Dh
    one_kv = jnp.ones((2 * Bk, LANES), BF16)
    NP = DIL_WIDTH // LANES

    def pairs(x):
        return jnp.stack([x[:, hp * LANES:(hp + 1) * LANES] for hp in range(NP)], axis=0)

    for j in range(QB // Bk):
        ok = band if j > 0 else band & ((cj >= Bk) | has_prev)
        ok2 = jnp.concatenate([ok, ok], axis=0)
        q_pairs = pairs(q_ref[0, 0, j * Bk:(j + 1) * Bk, :])
        zero = jnp.zeros_like(q_pairs)
        q2 = jnp.concatenate([jnp.where(low, q_pairs, zero), jnp.where(low, zero, q_pairs)], axis=1)
        k_pairs = pairs(kbuf[j * Bk:(j + 2) * Bk, :])
        v_pairs = pairs(vbuf[j * Bk:(j + 2) * Bk, :])
        s = jnp.einsum('hqd,hkd->hqk', q2, k_pairs, preferred_element_type=F32)
        s = jnp.where(ok2, s, -jnp.inf)
        m = jnp.max(s, axis=-1, keepdims=True).reshape(2 * NP, Bk, 1)
        p = jnp.exp2(s - m.reshape(NP, 2 * Bk, 1)).astype(BF16).reshape(2 * NP, Bk, 2 * Bk)
        v_ext = jnp.stack([jnp.where(low_kv, v_pairs, one_kv), jnp.where(low_kv, one_kv, v_pairs)], axis=1)
        po = jnp.einsum('hqk,hkd->hqd', p, v_ext.reshape(2 * NP, 2 * Bk, LANES), preferred_element_type=F32)
        swapped = pltpu.roll(po.reshape(2 * NP * Bk, LANES), Dh, axis=1).reshape(2 * NP, Bk, LANES)
        r = po / swapped
        o_ref[0, 0, j * Bk:(j + 1) * Bk, :] = jnp.concatenate(
            [jnp.where(low, r[2 * hp], r[2 * hp + 1]) for hp in range(NP)], axis=-1).astype(o_ref.dtype)
        lse_blk = jnp.zeros((Bk, LANES), F32)
        for h in range(2 * NP):
            lse_blk = jnp.where(lane == h, m[h] + jnp.log2((swapped if h % 2 == 0 else po)[h]), lse_blk)
        lse_ref[0, 0, j * Bk:(j + 1) * Bk, :] = lse_blk


def dil_attention_group(q, k, v, qb=2 * ATT_BLOCK):
    B, d, L, W = q.shape
    nj = qb // ATT_BLOCK
    cur = pl.BlockSpec((1, 1, qb, W), lambda b, r, n: (b, r, n, 0))
    prev = pl.BlockSpec((1, 1, ATT_BLOCK, W), lambda b, r, n: (b, r, jnp.maximum(n * nj - 1, 0), 0))
    return pl.pallas_call(
        _dil_attn_kernel,
        grid=(B, d, L // qb),
        in_specs=[cur, cur, prev, cur, prev],
        out_specs=[cur, pl.BlockSpec((1, 1, qb, LANES), lambda b, r, n: (b, r, n, 0))],
        out_shape=[jax.ShapeDtypeStruct((B, d, L, W), BF16), jax.ShapeDtypeStruct((B, d, L, LANES), F32)],
        scratch_shapes=[pltpu.VMEM((ATT_BLOCK + qb, W), BF16), pltpu.VMEM((ATT_BLOCK + qb, W), BF16)],
        compiler_params=_cparams(("parallel", "parallel", "arbitrary")),
        name=f"dil_attn_d{d}",
    )(q, k, k, v, v)


def _out_proj_even_kernel(x_ref, ya_ref, yb_ref, wa_ref, wb_ref, o_ref):
    o_ref[...] = (x_ref[...]
                  + jnp.dot(ya_ref[...].astype(BF16), wa_ref[...], preferred_element_type=F32)
                  + jnp.dot(yb_ref[...].astype(BF16), wb_ref[...], preferred_element_type=F32))


def out_proj_even(x, ya, yb, wa, wb, tm=512):
    T, D = x.shape
    return pl.pallas_call(
        _out_proj_even_kernel,
        grid=(T // tm,),
        in_specs=[pl.BlockSpec((tm, D), lambda i: (i, 0)),
                  pl.BlockSpec((tm, ya.shape[1]), lambda i: (i, 0)),
                  pl.BlockSpec((tm, yb.shape[1]), lambda i: (i, 0)),
                  _resident(wa.shape), _resident(wb.shape)],
        out_specs=pl.BlockSpec((tm, D), lambda i: (i, 0)),
        out_shape=jax.ShapeDtypeStruct((T, D), F32),
        compiler_params=_cparams(("parallel",)),
        name="out_proj_even",
    )(x, ya, yb, wa, wb)


def _out_proj_odd_kernel(x_ref, o0_ref, o1_ref, o2_ref, l0_ref, l1_ref, l2_ref, e_ref, w_ref, out_ref,
                         a_ref, b_ref, la_ref, lb_ref):
    dils = [d for _, d in DIL_PATTERNS]
    lses = [_interleave_load(r, d, la_ref, lb_ref) for r, d in zip((l0_ref, l1_ref, l2_ref), dils)]
    m = jnp.maximum(jnp.maximum(lses[0], lses[1]), lses[2])
    es = [jnp.exp2(l - m) for l in lses]
    inv = 1.0 / (es[0] + es[1] + es[2])
    expand = e_ref[...]
    mixed = None
    for e, o_ref, d in zip(es, (o0_ref, o1_ref, o2_ref), dils):
        term = _group_sum(e * inv, expand) * _interleave_load(o_ref, d, a_ref, b_ref)
        mixed = term if mixed is None else mixed + term
    out_ref[0] = x_ref[0] + jnp.dot(mixed.astype(BF16), w_ref[...], preferred_element_type=F32)


def out_proj_odd(x3, outs, lses, w, tm=512):
    B, S, D = x3.shape
    W = DIL_WIDTH
    expand = np.zeros((LANES, W), np.float32)
    for h in range(DIL_HEADS):
        expand[h, h * DIL_HEAD_DIM:(h + 1) * DIL_HEAD_DIM] = 1.0
    row = pl.BlockSpec((1, tm, D), lambda b, i: (b, i, 0))

    def res(width):
        return [pl.BlockSpec((1, d, tm // d, width), lambda b, i: (b, 0, i, 0)) for _, d in DIL_PATTERNS]

    return pl.pallas_call(
        _out_proj_odd_kernel,
        grid=(B, S // tm),
        in_specs=[row, *res(W), *res(LANES), _resident((LANES, W)), _resident(w.shape)],
        out_specs=row,
        out_shape=jax.ShapeDtypeStruct((B, S, D), F32),
        scratch_shapes=[pltpu.VMEM((W // LANES, tm, LANES), F32), pltpu.VMEM((W // LANES, tm, LANES), F32),
                        pltpu.VMEM((1, tm, LANES), F32), pltpu.VMEM((1, tm, LANES), F32)],
        compiler_params=_cparams(("parallel", "parallel")),
        name="out_proj_odd",
    )(x3, *outs, *lses, jnp.asarray(expand, BF16), w)


MOE_TM = 512
MOE_RB = 144
MOE_NBLK = (MOE_TM + N_GROUPS * (MOE_RB - 1)) // MOE_RB
LOGIT_ROWS = 24


def _moe_route(h, wr_ref):
    NG, EPG, RB = N_GROUPS, EXPERTS_PER_GROUP, MOE_RB
    tm = h.shape[0]
    hb, h_lo = _hi_lo(h)
    nt = (((1,), (1,)), ((), ()))
    lg = lax.dot_general(wr_ref[...], hb, nt, preferred_element_type=F32)
    lg = lg[:LANES] + lg[LANES:] + lax.dot_general(wr_ref[:LANES], h_lo, nt, preferred_element_type=F32)
    L = lg[:LOGIT_ROWS]
    r = lax.broadcasted_iota(jnp.int32, L.shape, 0)
    big = jnp.int32(1 << 30)
    gl = jnp.where(r < NG, L, -jnp.inf)
    gmax = jnp.max(gl, axis=0, keepdims=True)
    gsel = jnp.min(jnp.where(gl == gmax, r, big), axis=0, keepdims=True)
    p_group = 1.0 / jnp.sum(jnp.where(r < NG, jnp.exp(L - gmax), 0.0), axis=0, keepdims=True)
    lo = NG + gsel * EPG
    el = jnp.where((r >= lo) & (r < lo + EPG), L, -jnp.inf)
    v1 = jnp.max(el, axis=0, keepdims=True)
    i1 = jnp.min(jnp.where(el == v1, r, big), axis=0, keepdims=True)
    el2 = jnp.where(r == i1, -jnp.inf, el)
    v2 = jnp.max(el2, axis=0, keepdims=True)
    i2 = jnp.min(jnp.where(el2 == v2, r, big), axis=0, keepdims=True)
    t = jnp.exp(v2 - v1)
    gates = jnp.where(r == i1, p_group / (1.0 + t), 0.0) + jnp.where(r == i2, p_group * t / (1.0 + t), 0.0)
    gate4 = [jnp.sum(jnp.where(r == lo + j, gates, 0.0), axis=0, keepdims=True) for j in range(EPG)]

    r8 = lax.broadcasted_iota(jnp.int32, (SUBLANES, tm), 0)
    onehot = (r8 == gsel).astype(F32)
    ti = lax.broadcasted_iota(jnp.int32, (tm, tm), 0)
    tj = lax.broadcasted_iota(jnp.int32, (tm, tm), 1)
    before = (ti < tj).astype(BF16)
    rank = jnp.dot(onehot.astype(BF16), before, preferred_element_type=F32)
    count = jnp.sum(onehot, axis=1, keepdims=True).astype(jnp.int32)
    nblk = sum((count > j * RB).astype(jnp.int32) for j in range(pl.cdiv(tm, RB)))
    padded = (nblk * RB).astype(F32)
    start, starts = jnp.zeros((1, 1), F32), []
    for g in range(NG):
        starts.append(start)
        start = start + padded[g:g + 1]
    start8 = jnp.concatenate(starts + [jnp.zeros((SUBLANES - NG, 1), F32)], axis=0)
    dest = jnp.sum(onehot * (start8 + rank), axis=0, keepdims=True)
    return dest, gate4, nblk


def _moe_kernel(x_ref, g_ref, wr_ref, w1_ref, w3_ref, w2_ref, fg_ref, o_ref,
                p_ref, xs_ref, gs_ref, hh_ref, ys_ref, *, final_norm):
    NG, EPG, RB, NBLK = N_GROUPS, EXPERTS_PER_GROUP, MOE_RB, MOE_NBLK
    FF = w1_ref.shape[2]
    tm = x_ref.shape[0]
    x = x_ref[...]
    h = _rmsnorm(x, g_ref[...])
    dest, gate4, nblk = _moe_route(h, wr_ref)
    rows = lax.broadcasted_iota(jnp.int32, (NBLK * RB, tm), 0)
    p_ref[...] = jnp.where(rows == dest.astype(jnp.int32), 1.0, 0.0).astype(BF16)
    xs_ref[...] = jnp.dot(p_ref[...], h.astype(BF16), preferred_element_type=F32).astype(BF16)
    g4 = jnp.concatenate(gate4 + [jnp.zeros((SUBLANES - EPG, tm), F32)], axis=0)
    g_hi = g4.astype(BF16).astype(F32)
    g2 = jnp.concatenate([g_hi, g4 - g_hi, jnp.zeros((LANES - 2 * SUBLANES, tm), F32)], axis=0).astype(BF16)
    gs_ref[...] = lax.dot_general(p_ref[...], g2, (((1,), (1,)), ((), ())), preferred_element_type=F32)

    ends, acc = [], jnp.int32(0)
    for g in range(NG):
        acc = acc + jnp.sum(nblk[g:g + 1])
        ends.append(acc)
    for b in range(NBLK):
        rs = slice(b * RB, (b + 1) * RB)
        grp = sum((b >= e).astype(jnp.int32) for e in ends[:-1])

        @pl.when(b < ends[-1])
        def _():
            xr = xs_ref[rs]
            gate = gs_ref[rs]
            for j in range(EPG):
                a = jnp.dot(xr, w1_ref[grp * EPG + j], preferred_element_type=F32)
                u = jnp.dot(xr, w3_ref[grp * EPG + j], preferred_element_type=F32)
                gj = gate[:, j:j + 1] + gate[:, SUBLANES + j:SUBLANES + j + 1]
                hh_ref[:, j * FF:(j + 1) * FF] = ((a * _sigmoid(a)) * u * gj).astype(BF16)
            ys_ref[rs] = jnp.dot(hh_ref[...], w2_ref[grp], preferred_element_type=F32).astype(BF16)

        @pl.when(b >= ends[-1])
        def _():
            ys_ref[rs] = jnp.zeros((RB, ys_ref.shape[1]), BF16)

    y = x + lax.dot_general(p_ref[...], ys_ref[...], (((0,), (0,)), ((), ())), preferred_element_type=F32)
    if final_norm:
        y = _rmsnorm(y, fg_ref[...])
    o_ref[...] = y


def moe_block(x, gain, w_router, w1, w3, w2, final_gain=None):
    T, D = x.shape
    NE, _, FF = w1.shape
    tm, RB, NBLK = MOE_TM, MOE_RB, MOE_NBLK
    final_norm = final_gain is not None
    fg = (final_gain if final_norm else jnp.ones((D,), F32)).reshape(1, D)
    return pl.pallas_call(
        functools.partial(_moe_kernel, final_norm=final_norm),
        grid=(T // tm,),
        in_specs=[pl.BlockSpec((tm, D), lambda i: (i, 0)), _resident((1, D)), _resident(w_router.shape),
                  _resident(w1.shape), _resident(w3.shape), _resident(w2.shape), _resident((1, D))],
        out_specs=pl.BlockSpec((tm, D), lambda i: (i, 0)),
        out_shape=jax.ShapeDtypeStruct((T, D), F32),
        scratch_shapes=[pltpu.VMEM((NBLK * RB, tm), BF16), pltpu.VMEM((NBLK * RB, D), BF16),
                        pltpu.VMEM((NBLK * RB, LANES), F32), pltpu.VMEM((RB, EXPERTS_PER_GROUP * FF), BF16),
                        pltpu.VMEM((NBLK * RB, D), BF16)],
        compiler_params=_cparams(("parallel",)),
        name="moe_block",
    )(x, gain.reshape(1, D), w_router, w1, w3, w2, fg)


def _router_weights(w_group, w_expert):
    w = _place([(0, w_group)] + [(N_GROUPS + g * EXPERTS_PER_GROUP, w_expert[g]) for g in range(N_GROUPS)], LANES).T
    return jnp.concatenate(_hi_lo(w), axis=0)


def _place(cols, total):
    lead = cols[0][1].shape[:-1]
    parts, pos = [], 0
    for off, arr in sorted(cols, key=lambda c: c[0]):
        if off > pos:
            parts.append(jnp.zeros(lead + (off - pos,), F32))
        parts.append(arr)
        pos = off + arr.shape[-1]
    if total > pos:
        parts.append(jnp.zeros(lead + (total - pos,), F32))
    return jnp.concatenate(parts, axis=-1)


def _even_layout(t, vres=None):
    gq, gk, gv, gz, ga, gb, rr, rk, rv, lw, la, lg = jnp.split(
        t, np.cumsum([512, 512, 512, 512, 4, 4, 512, 512, 512, 32, 32, 96])[:-1].tolist(), axis=-1)
    cols = [(EV_Q, gq), (EV_K, gk), (EV_V, gv), (EV_Z, gz), (EV_AB, ga), (EV_AB + GDN_HEADS, gb),
            (EV_R, rr), (EV_RK, rk), (EV_RV, rv),
            (EV_LORA, lw), (EV_LORA + LORA_SEG, la), (EV_LORA + 2 * LORA_SEG, lg)]
    if vres is not None:
        cols.append((EV_LORA + 3 * LORA_SEG, vres))
    return _place(cols, EV_NPAD)


def _pad_rows(w, rows):
    return jnp.concatenate([w, jnp.zeros((rows - w.shape[0], w.shape[1]), F32)], axis=0)


def kernel(x, positions, ev_norm, ev_w_in, rwkv_vres_down, ev_w_out, gdn_conv_w, gdn_A_log, gdn_dt_bias, gdn_norm,
           rwkv_mu, rwkv_w0, rwkv_w2, rwkv_a0, rwkv_a2, rwkv_g2, rwkv_k_k, rwkv_k_a, rwkv_r_k, rwkv_ln_w, rwkv_ln_b,
           rwkv_vres_mu, rwkv_v0, rwkv_v2, od_norm, od_w_in, od_w_out, ffn_norm, moe_w_group, moe_w_expert,
           moe_w1, moe_w3, moe_w2, final_norm):
    B, S, D = x.shape
    T = B * S
    depth = ffn_norm.shape[0]
    xf = x.reshape(T, D)
    cos, sin = rope_tables(positions.reshape(T))
    n_groups = len(DIL_PATTERNS)
    v_first = None
    for layer in range(depth):
        i = layer // 2
        if layer % 2 == 0:
            vres_w = None if i == 0 else rwkv_vres_down[i - 1]
            w_in = _even_layout(ev_w_in[i], vres_w).astype(BF16)
            proj3 = norm_proj(xf, ev_norm[i], w_in).reshape(B, S, EV_NPAD)
            mu_r, mu_k, mu_v, mu_w, mu_a, mu_g = jnp.split(
                rwkv_mu[i], np.cumsum([RWKV_DIM, RWKV_DIM, RWKV_DIM, 32, 32])[:].tolist())
            mu = jnp.stack([mu_r, mu_k, mu_v])
            lora_cols = [(0, mu_w), (LORA_SEG, mu_a), (2 * LORA_SEG, mu_g)]
            if i > 0:
                lora_cols.append((3 * LORA_SEG, rwkv_vres_mu[i - 1]))
            mu_lora = _place(lora_cols, 4 * LORA_SEG).reshape(1, 4 * LORA_SEG)
            vec_rows = [rwkv_w0[i], rwkv_a0[i], rwkv_k_k[i], rwkv_k_a[i], rwkv_r_k[i].reshape(-1), rwkv_ln_w[i], rwkv_ln_b[i]]
            if i > 0:
                vec_rows.append(rwkv_v0[i - 1])
            vecs = jnp.stack(vec_rows)
            w2 = _pad_rows(rwkv_w2[i], LORA_SEG)
            a2 = _pad_rows(rwkv_a2[i], LORA_SEG)
            g2 = _pad_rows(rwkv_g2[i], LORA_SEG)
            ya = gdn_mixer(proj3, gdn_conv_w[i], gdn_A_log[i], gdn_dt_bias[i], gdn_norm[i])
            if i == 0:
                yb, v_first = rwkv_mixer(proj3, mu, mu_lora, vecs, w2, a2, g2)
            else:
                yb, _ = rwkv_mixer(proj3, mu, mu_lora, vecs, w2, a2, g2, v_first, _pad_rows(rwkv_v2[i - 1], LORA_SEG))
            w_out = ev_w_out[i].astype(BF16)
            na = GDN_HEADS * GDN_DV
            xf = out_proj_even(xf, ya.reshape(T, na), yb.reshape(T, RWKV_DIM), w_out[:na], w_out[na:])
        else:
            w_in = od_w_in[i].astype(BF16)
            qkv = norm_proj_rope(xf.reshape(B, S, D), od_norm[i], w_in, cos.reshape(B, S, LANES),
                                 sin.reshape(B, S, LANES), DIL_HEAD_DIM ** -0.5 * math.log2(math.e))
            outs, lses = [], []
            for gi, (window, dilation) in enumerate(DIL_PATTERNS):
                assert window // dilation == ATT_BLOCK and (S // dilation) % (2 * ATT_BLOCK) == 0
                o, lse = dil_attention_group(qkv[gi], qkv[n_groups + gi], qkv[2 * n_groups + gi])
                outs.append(o)
                lses.append(lse)
            xf = out_proj_odd(xf.reshape(B, S, D), outs, lses, od_w_out[i].astype(BF16)).reshape(T, D)
        w_router = _router_weights(moe_w_group[layer], moe_w_expert[layer])
        w2_grouped = moe_w2[layer].astype(BF16).reshape(N_GROUPS, -1, D)
        xf = moe_block(xf, ffn_norm[layer], w_router, moe_w1[layer].astype(BF16), moe_w3[layer].astype(BF16),
                       w2_grouped, final_norm if layer == depth - 1 else None)
    return xf.reshape(B, S, D)
```

```python
import functools
import math

import jax
import jax.numpy as jnp
import numpy as np
from jax import lax
from jax.experimental import pallas as pl
from jax.experimental.pallas import tpu as pltpu

F32 = jnp.float32
BF16 = jnp.bfloat16

RMS_EPS = 1e-6
L2_EPS = 1e-6
RWKV_LN_EPS = 64e-5
ROPE_THETA = 10000.0

LANES = 128
SUBLANES = 8
VMEM_LIMIT = 56 * 1024 * 1024

GDN_HEADS = 4
GDN_DK = 128
GDN_DV = 128
GDN_CONV = 4
RWKV_HEADS = 8
RWKV_HEAD = 64
RWKV_DIM = RWKV_HEADS * RWKV_HEAD
LORA_SEG = LANES
DIL_PATTERNS = ((128, 1), (512, 4), (2048, 16))
DIL_HEADS = 8
DIL_HEAD_DIM = 64
DIL_WIDTH = DIL_HEADS * DIL_HEAD_DIM
ATT_BLOCK = 128
N_GROUPS = 4
EXPERTS_PER_GROUP = 4
N_EXPERTS = N_GROUPS * EXPERTS_PER_GROUP
CHUNK = 64

EV_Q, EV_K, EV_V, EV_Z = 0, 512, 1024, 1536
EV_R, EV_RK, EV_RV, EV_LORA = 2048, 2560, 3072, 3584
EV_AB = EV_LORA + 4 * LORA_SEG
EV_NPAD = EV_AB + LANES


def _cparams(sem):
    return pltpu.CompilerParams(dimension_semantics=sem, vmem_limit_bytes=VMEM_LIMIT)


def _sigmoid(x):
    return 0.5 * jnp.tanh(0.5 * x) + 0.5


def _softplus(x):
    return jnp.maximum(x, 0.0) + jnp.log(1.0 + jnp.exp(-jnp.abs(x)))


def _rmsnorm(x, gain):
    return x * lax.rsqrt(jnp.mean(x * x, axis=-1, keepdims=True) + RMS_EPS) * gain


def _hi_lo(v):
    hi = v.astype(BF16)
    return hi, (v - hi.astype(F32)).astype(BF16)


def _group_sum(x, block_ones):
    hi, lo = _hi_lo(x)
    return (jnp.dot(hi, block_ones, preferred_element_type=F32)
            + jnp.dot(lo, block_ones, preferred_element_type=F32))


def _block_ones(width, group):
    idx = np.arange(width) // group
    return jnp.asarray(idx[:, None] == idx[None, :], BF16)


def _resident(shape):
    nd = len(shape)
    return pl.BlockSpec(shape, lambda *_: (0,) * nd, pipeline_mode=pl.Buffered(1))


def _rope_table_kernel(pos_ref, freq_ref, sign_ref, cos_ref, sin_ref):
    ang = pos_ref[...].astype(F32) * freq_ref[...]
    cos_ref[...] = jnp.cos(ang)
    sin_ref[...] = jnp.sin(ang) * sign_ref[...]


def rope_tables(positions_flat, tm=1024):
    T = positions_flat.shape[0]
    half = DIL_HEAD_DIM // 2
    inv_freq = ROPE_THETA ** (-jnp.arange(half, dtype=F32) * 2.0 / DIL_HEAD_DIM)
    lane = np.arange(LANES)
    freq = inv_freq[lane % half][None, :]
    sign = jnp.asarray(np.where(lane % DIL_HEAD_DIM < half, -1.0, 1.0), F32)[None, :]
    return pl.pallas_call(
        _rope_table_kernel,
        grid=(T // tm,),
        in_specs=[pl.BlockSpec((tm, 1), lambda i: (i, 0)), _resident((1, LANES)), _resident((1, LANES))],
        out_specs=[pl.BlockSpec((tm, LANES), lambda i: (i, 0))] * 2,
        out_shape=[jax.ShapeDtypeStruct((T, LANES), F32)] * 2,
        compiler_params=_cparams(("parallel",)),
        name="rope_tables",
    )(positions_flat.reshape(T, 1), freq, sign)


def _norm_proj_kernel(x_ref, g_ref, w_ref, o_ref):
    h = _rmsnorm(x_ref[...], g_ref[...]).astype(BF16)
    o_ref[...] = jnp.dot(h, w_ref[...], preferred_element_type=F32)


def norm_proj(x, gain, w, tm=512):
    T, D = x.shape
    N = w.shape[1]
    return pl.pallas_call(
        _norm_proj_kernel,
        grid=(T // tm,),
        in_specs=[pl.BlockSpec((tm, D), lambda i: (i, 0)), _resident((1, D)), _resident((D, N))],
        out_specs=pl.BlockSpec((tm, N), lambda i: (i, 0)),
        out_shape=jax.ShapeDtypeStruct((T, N), F32),
        compiler_params=_cparams(("parallel",)),
        name="norm_proj",
    )(x, gain.reshape(1, D), w)


def _deinterleave_store(out_ref, val, d, a_ref, b_ref):
    tm = val.shape[0]
    if d == 1:
        out_ref[0, 0] = val.astype(out_ref.dtype)
        return
    quarter = tm // 4
    for j in range(val.shape[1] // LANES):
        cs = slice(j * LANES, (j + 1) * LANES)
        a_ref[j] = val[:, cs]
        if d == 4:
            for r in range(4):
                out_ref[0, r, :, cs] = a_ref[j, pl.ds(r, quarter, stride=4), :].astype(out_ref.dtype)
            continue
        assert d == 16
        for r1 in range(4):
            b_ref[j, r1 * quarter:(r1 + 1) * quarter, :] = a_ref[j, pl.ds(r1, quarter, stride=4), :]
        for r1 in range(4):
            for r2 in range(4):
                out_ref[0, r1 + 4 * r2, :, cs] = (
                    b_ref[j, pl.ds(r1 * quarter + r2, quarter // 4, stride=4), :].astype(out_ref.dtype))


def _interleave_load(in_ref, d, a_ref, b_ref):
    if d == 1:
        return in_ref[0, 0].astype(F32)
    n_tiles, tm, _ = a_ref.shape
    quarter = tm // 4
    for j in range(n_tiles):
        cs = slice(j * LANES, (j + 1) * LANES)
        if d == 4:
            for r in range(4):
                a_ref[j, pl.ds(r, quarter, stride=4), :] = in_ref[0, r, :, cs].astype(F32)
            continue
        assert d == 16
        for r1 in range(4):
            for r2 in range(4):
                b_ref[j, pl.ds(r1 * quarter + r2, quarter // 4, stride=4), :] = in_ref[0, r1 + 4 * r2, :, cs].astype(F32)
        for r1 in range(4):
            a_ref[j, pl.ds(r1, quarter, stride=4), :] = b_ref[j, r1 * quarter:(r1 + 1) * quarter, :]
    return jnp.concatenate([a_ref[j] for j in range(n_tiles)], axis=-1)


def _norm_proj_rope_kernel(x_ref, g_ref, w_ref, cos_ref, sin_ref, *rest, q_scale):
    out_refs, (a_ref, b_ref) = rest[:-2], rest[-2:]
    n_groups = len(DIL_PATTERNS)
    h = _rmsnorm(x_ref[0], g_ref[...]).astype(BF16)
    cos = cos_ref[0]
    sin = sin_ref[0]
    lane = lax.broadcasted_iota(jnp.int32, cos.shape, 1)
    first_half = (lane % DIL_HEAD_DIM) < (DIL_HEAD_DIM // 2)
    W = DIL_WIDTH
    for c, out_ref in enumerate(out_refs):
        which, g = divmod(c, n_groups)
        acc = jnp.dot(h, w_ref[:, c * W:(c + 1) * W], preferred_element_type=F32)
        if which < 2:
            parts = []
            for j in range(W // LANES):
                blk = acc[:, j * LANES:(j + 1) * LANES]
                rot = jnp.where(first_half, pltpu.roll(blk, LANES - 32, axis=1), pltpu.roll(blk, 32, axis=1))
                out = blk * cos + rot * sin
                parts.append(out * q_scale if which == 0 else out)
            acc = jnp.concatenate(parts, axis=-1)
        _deinterleave_store(out_ref, acc, DIL_PATTERNS[g][1], a_ref, b_ref)


def norm_proj_rope(x3, gain, w, cos3, sin3, q_scale, tm=512):
    B, S, D = x3.shape
    W = DIL_WIDTH
    out_specs, out_shape = [], []
    for _ in range(3):
        for _, d in DIL_PATTERNS:
            out_specs.append(pl.BlockSpec((1, d, tm // d, W), lambda b, i: (b, 0, i, 0)))
            out_shape.append(jax.ShapeDtypeStruct((B, d, S // d, W), BF16))
    return pl.pallas_call(
        functools.partial(_norm_proj_rope_kernel, q_scale=q_scale),
        grid=(B, S // tm),
        in_specs=[pl.BlockSpec((1, tm, D), lambda b, i: (b, i, 0)), _resident((1, D)), _resident(w.shape),
                  pl.BlockSpec((1, tm, LANES), lambda b, i: (b, i, 0)),
                  pl.BlockSpec((1, tm, LANES), lambda b, i: (b, i, 0))],
        out_specs=out_specs,
        out_shape=out_shape,
        scratch_shapes=[pltpu.VMEM((W // LANES, tm, LANES), F32), pltpu.VMEM((W // LANES, tm, LANES), F32)],
        compiler_params=_cparams(("parallel", "parallel")),
        name="norm_proj_rope",
    )(x3, gain.reshape(1, D), w, cos3, sin3)


def _bmm(a, b):
    return jnp.einsum('nik,nkj->nij', a.astype(BF16), b.astype(BF16), preferred_element_type=F32)


def _bmm_nt(a, b):
    return jnp.einsum('nik,njk->nij', a.astype(BF16), b.astype(BF16), preferred_element_type=F32)


def _bmm_tn(a, b):
    return jnp.einsum('nci,ncj->nij', a.astype(BF16), b.astype(BF16), preferred_element_type=F32)


def _unit_lower_inverse(L):
    C = L.shape[-1]
    ri = lax.broadcasted_iota(jnp.int32, (C, C), 0)
    ci = lax.broadcasted_iota(jnp.int32, (C, C), 1)
    same16 = (ri // 16) == (ci // 16)
    same32 = (ri // 32) == (ci // 32)
    eye = (ri == ci).astype(F32)
    Ld = jnp.where(same16, L, 0.0)
    X = eye - Ld
    P = Ld
    for _ in range(3):
        P = _bmm(P, P)
        X = X + _bmm(X, P)
    for off in (jnp.where(same32 & ~same16, L, 0.0), jnp.where(~same32, L, 0.0)):
        X = X - _bmm(_bmm(X, off), X)
    return X


def _shifted_rows(buf_ref, cur, prev, first, shifts):
    TB = cur.shape[0]
    buf_ref[0:SUBLANES, :] = jnp.where(first, 0.0, prev)
    buf_ref[SUBLANES:, :] = cur
    return [buf_ref[SUBLANES - s:SUBLANES - s + TB, :] for s in shifts]


def _split_heads(x, n_heads, width):
    return jnp.stack([x[:, h * width:(h + 1) * width] for h in range(n_heads)], axis=0)


def _merge_heads(x):
    return jnp.concatenate([x[h] for h in range(x.shape[0])], axis=-1)


def _gdn_kernel(q_ref, k_ref, v_ref, z_ref, ab_ref, qp_ref, kp_ref, vp_ref, cw_ref, alog_ref, dtb_ref, nw_ref,
                o_ref, buf_ref, s_ref, qd_ref, pg_ref, qg_ref, ov_ref, egl_ref, oc_ref):
    H, DK, DV, C = GDN_HEADS, GDN_DK, GDN_DV, CHUNK
    TB = q_ref.shape[1]
    NC = TB // C
    first = pl.program_id(1) == 0

    @pl.when(first)
    def _():
        s_ref[...] = jnp.zeros_like(s_ref)

    def conv_silu(cur_ref, prev_ref, j):
        cur = cur_ref[0]
        x3, x2, x1 = _shifted_rows(buf_ref, cur, prev_ref[0], first, (3, 2, 1))
        w = cw_ref[j]
        y = x3 * w[0:1] + x2 * w[1:2] + x1 * w[2:3] + cur * w[3:4]
        return y * _sigmoid(y)

    q = _split_heads(conv_silu(q_ref, qp_ref, 0), H, DK)
    k = _split_heads(conv_silu(k_ref, kp_ref, 1), H, DK)
    v = _split_heads(conv_silu(v_ref, vp_ref, 2), H, DV)
    q = q * lax.rsqrt(jnp.sum(q * q, axis=-1, keepdims=True) + L2_EPS) * (DK ** -0.5)
    k = k * lax.rsqrt(jnp.sum(k * k, axis=-1, keepdims=True) + L2_EPS)
    ab = ab_ref[0]
    a = jnp.stack([ab[:, h:h + 1] for h in range(H)], axis=0)
    b = jnp.stack([ab[:, H + h:H + h + 1] for h in range(H)], axis=0)
    beta = _sigmoid(b)
    g = -jnp.exp(alog_ref[...]) * _softplus(a + dtb_ref[...])

    N = H * NC
    q = q.reshape(N, C, DK)
    k = k.reshape(N, C, DK)
    v = v.reshape(N, C, DV)
    beta = beta.reshape(N, C, 1)
    g = g.reshape(N, C, 1)

    ri = lax.broadcasted_iota(jnp.int32, (C, C), 0)
    ci = lax.broadcasted_iota(jnp.int32, (C, C), 1)
    causal = ri >= ci
    strict = ri > ci
    gb = jnp.broadcast_to(g, (N, C, C))
    g_row = jnp.sum(jnp.where(ri == ci, gb, 0.0), axis=1, keepdims=True)
    gc_col = jnp.sum(jnp.where(causal, jnp.broadcast_to(g_row, (N, C, C)), 0.0), axis=2, keepdims=True)
    gc_row = jnp.sum(jnp.where(ri <= ci, gb, 0.0), axis=1, keepdims=True)
    decay = jnp.where(causal, jnp.exp(jnp.where(causal, gc_col - gc_row, 0.0)), 0.0)

    kb = k * beta
    vb = v * beta
    kq = _bmm_nt(jnp.concatenate([kb, q], axis=1), k)
    L = jnp.where(strict, kq[:, :C] * decay, 0.0)
    aqk = kq[:, C:] * decay
    tinv = _unit_lower_inverse(L)
    egc = jnp.exp(gc_col)
    uw = _bmm(tinv, jnp.concatenate([vb, kb * egc], axis=2))
    auw = _bmm(aqk, uw)
    g_last = gc_col[:, C - 1:C, :]
    k_dec = k * jnp.exp(g_last - gc_col)
    kuw = _bmm_tn(k_dec, uw)
    qd_ref[...] = (q * egc - auw[:, :, DV:]).reshape(H, NC, C, DK)
    ov_ref[...] = auw[:, :, :DV].reshape(H, NC, C, DV)
    pg_ref[...] = kuw[:, :, DV:].reshape(H, NC, DK, DK)
    qg_ref[...] = kuw[:, :, :DV].reshape(H, NC, DK, DV)
    egl_ref[...] = jnp.broadcast_to(jnp.exp(g_last), (N, 1, LANES)).reshape(H, NC, 1, LANES)

    for c in range(NC):
        S = s_ref[...]
        oc_ref[:, c] = _bmm(qd_ref[:, c], S) + ov_ref[:, c]
        s_ref[...] = S * egl_ref[:, c] - _bmm(pg_ref[:, c], S) + qg_ref[:, c]

    o = oc_ref[...].reshape(H, TB, DV)
    o = o * lax.rsqrt(jnp.mean(o * o, axis=-1, keepdims=True) + RMS_EPS) * nw_ref[...]
    z = _split_heads(z_ref[0], H, DV)
    o_ref[0] = _merge_heads(o * (z * _sigmoid(z)))


def gdn_mixer(proj3, conv_w, a_log, dt_bias, norm_w, tb=512):
    B, S, _ = proj3.shape
    H, DK, DV, C = GDN_HEADS, GDN_DK, GDN_DV, CHUNK
    W = H * DK
    NC = tb // C
    nq, nk, nv, nz, nab = EV_Q // W, EV_K // W, EV_V // W, EV_Z // W, EV_AB // LANES
    rows8 = tb // SUBLANES

    def cur(cb):
        return pl.BlockSpec((1, tb, W), lambda b, i, cb=cb: (b, i, cb))

    def prev(cb):
        return pl.BlockSpec((1, SUBLANES, W), lambda b, i, cb=cb: (b, jnp.maximum(i * rows8 - 1, 0), cb))

    cw = conv_w.reshape(GDN_CONV, 3, W).transpose(1, 0, 2)
    return pl.pallas_call(
        _gdn_kernel,
        grid=(B, S // tb),
        in_specs=[cur(nq), cur(nk), cur(nv), cur(nz),
                  pl.BlockSpec((1, tb, LANES), lambda b, i: (b, i, nab)),
                  prev(nq), prev(nk), prev(nv),
                  _resident((3, GDN_CONV, W)), _resident((H, 1, 1)), _resident((H, 1, 1)),
                  _resident((1, 1, DV))],
        out_specs=pl.BlockSpec((1, tb, H * DV), lambda b, i: (b, i, 0)),
        out_shape=jax.ShapeDtypeStruct((B, S, H * DV), F32),
        scratch_shapes=[pltpu.VMEM((tb + SUBLANES, W), F32),
                        pltpu.VMEM((H, DK, DV), F32),
                        pltpu.VMEM((H, NC, C, DK), F32),
                        pltpu.VMEM((H, NC, DK, DK), F32),
                        pltpu.VMEM((H, NC, DK, DV), F32),
                        pltpu.VMEM((H, NC, C, DV), F32),
                        pltpu.VMEM((H, NC, 1, LANES), F32),
                        pltpu.VMEM((H, NC, C, DV), F32)],
        compiler_params=_cparams(("parallel", "arbitrary")),
        name="gdn_mixer",
    )(proj3, proj3, proj3, proj3, proj3, proj3, proj3, proj3, cw,
      a_log.reshape(H, 1, 1), dt_bias.reshape(H, 1, 1), norm_w.reshape(1, 1, DV))


def _rwkv_kernel(*refs, has_vres):
    if has_vres:
        (r_ref, k_ref, v_ref, lo_ref, rp_ref, kp_ref, vp_ref, lop_ref, vf_ref,
         mu_ref, mul_ref, vec_ref, w2_ref, a2_ref, g2_ref, v2_ref,
         y_ref, buf_ref, bufl_ref, s_ref, rr_ref, pp_ref, qq_ref, yv_ref, egc_ref, yc_ref) = refs
    else:
        (r_ref, k_ref, v_ref, lo_ref, rp_ref, kp_ref, vp_ref, lop_ref,
         mu_ref, mul_ref, vec_ref, w2_ref, a2_ref, g2_ref,
         y_ref, vf_out_ref, buf_ref, bufl_ref, s_ref, rr_ref, pp_ref, qq_ref, yv_ref, egc_ref, yc_ref) = refs
    H, D, C = RWKV_HEADS, RWKV_HEAD, CHUNK
    TB = r_ref.shape[1]
    NC = TB // C
    first = pl.program_id(1) == 0

    @pl.when(first)
    def _():
        s_ref[...] = jnp.zeros_like(s_ref)

    def mix(cur_ref, prev_ref, mu, buf):
        cur = cur_ref[0]
        (sh,) = _shifted_rows(buf, cur, prev_ref[0], first, (1,))
        return cur + mu * (sh - cur)

    r = mix(r_ref, rp_ref, mu_ref[0:1], buf_ref)
    k = mix(k_ref, kp_ref, mu_ref[1:2], buf_ref)
    v = mix(v_ref, vp_ref, mu_ref[2:3], buf_ref)
    lo = mix(lo_ref, lop_ref, mul_ref[...], bufl_ref)
    hw, ha, hg = (lo[:, j * LORA_SEG:(j + 1) * LORA_SEG] for j in range(3))
    w0, a0, k_k, k_a, r_k, ln_w, ln_b = (vec_ref[j:j + 1] for j in range(7))

    w_log = -_softplus(-(w0 + jnp.dot(jnp.tanh(hw), w2_ref[...], preferred_element_type=F32))) - 0.5
    lw = -jnp.exp(w_log)
    a = _sigmoid(a0 + jnp.dot(ha, a2_ref[...], preferred_element_type=F32))
    gate = jnp.dot(_sigmoid(hg), g2_ref[...], preferred_element_type=F32)
    if has_vres:
        hv = lo[:, 3 * LORA_SEG:4 * LORA_SEG]
        v0 = vec_ref[7:8]
        v = v + (vf_ref[0] - v) * _sigmoid(v0 + jnp.dot(hv, v2_ref[...], preferred_element_type=F32))
    else:
        vf_out_ref[0] = v

    ri = lax.broadcasted_iota(jnp.int32, (C, C), 0)
    ci = lax.broadcasted_iota(jnp.int32, (C, C), 1)
    tril = (ri >= ci).astype(BF16)
    lw_hi = lw.astype(BF16)
    lw_rest = lw - lw_hi.astype(F32)
    lw_mid = lw_rest.astype(BF16)
    lw_lo = (lw_rest - lw_mid.astype(F32)).astype(BF16)
    gi = jnp.concatenate(
        [sum(jnp.dot(tril, part[c * C:(c + 1) * C], preferred_element_type=F32) for part in (lw_hi, lw_mid, lw_lo))
         for c in range(NC)], axis=0)

    kk = _split_heads(k * k_k, H, D)
    kk = kk * lax.rsqrt(jnp.sum(kk * kk, axis=-1, keepdims=True) + L2_EPS)
    k = k * (1.0 + (a - 1.0) * k_a)
    r_h = _split_heads(r, H, D)
    k_h = _split_heads(k, H, D)
    v_h = _split_heads(v, H, D)
    a_h = _split_heads(a, H, D)
    gi_h = _split_heads(gi, H, D)
    lw_h = _split_heads(lw, H, D)
    bonus = jnp.sum(r_h * k_h * _split_heads(r_k, H, D), axis=-1, keepdims=True) * v_h

    N = H * NC
    rc, kc, vc, kkc, ac, gic, lwc = (t.reshape(N, C, D) for t in (r_h, k_h, v_h, kk, a_h, gi_h, lw_h))
    g_end = gic[:, C - 1:C, :]
    e_neg = jnp.exp(-gic)
    e_tail = jnp.exp(g_end - gic)
    at = -kkc * jnp.exp(gic - lwc)
    bvec = kkc * ac
    rt = rc * jnp.exp(gic)
    bt = bvec * e_neg
    kt = kc * e_neg
    b_dec = bvec * e_tail
    k_dec = kc * e_tail
    strict = ri > ci
    causal = ri >= ci
    amat = _bmm_nt(jnp.concatenate([at, rt], axis=1), jnp.concatenate([bt, kt], axis=1))
    a_ab = jnp.where(strict, amat[:, :C, :C], 0.0)
    a_ak = jnp.where(strict, amat[:, :C, C:], 0.0)
    a_r = jnp.concatenate([jnp.where(causal, amat[:, C:, :C], 0.0), jnp.where(causal, amat[:, C:, C:], 0.0)], axis=2)
    tinv = _unit_lower_inverse(-a_ab)
    x2 = _bmm(tinv, jnp.concatenate([at, _bmm(a_ak, vc)], axis=2))
    xv = jnp.concatenate([x2, jnp.concatenate([jnp.zeros_like(vc), vc], axis=2)], axis=1)
    y2 = _bmm(a_r, xv)
    pq = _bmm_tn(xv, jnp.concatenate([b_dec, k_dec], axis=1))
    rr_ref[...] = (rt + y2[:, :, :D]).reshape(H, NC, C, D)
    yv_ref[...] = y2[:, :, D:].reshape(H, NC, C, D)
    pp_ref[...] = pq[:, :D].reshape(H, NC, D, D)
    qq_ref[...] = pq[:, D:].reshape(H, NC, D, D)
    egc_ref[...] = jnp.exp(g_end).reshape(H, NC, 1, D)

    for c in range(NC):
        S = s_ref[...]
        yc_ref[:, c] = _bmm_nt(rr_ref[:, c], S) + yv_ref[:, c]
        s_ref[...] = S * egc_ref[:, c] + _bmm(S, pp_ref[:, c]) + qq_ref[:, c]

    y = yc_ref[...].reshape(H, TB, D)
    mean = jnp.mean(y, axis=-1, keepdims=True)
    yc = y - mean
    var = jnp.mean(yc * yc, axis=-1, keepdims=True)
    y = _merge_heads(yc * lax.rsqrt(var + RWKV_LN_EPS)) * ln_w + ln_b
    y_ref[0] = (y + _merge_heads(bonus)) * gate


def rwkv_mixer(proj3, mu, mu_lora, vecs, w2, a2, g2, v_first=None, v2=None, tb=256):
    B, S, _ = proj3.shape
    H, D, C = RWKV_HEADS, RWKV_HEAD, CHUNK
    W = RWKV_DIM
    WL = 4 * LORA_SEG
    NC = tb // C
    rows8 = tb // SUBLANES
    has_vres = v_first is not None

    def cur(col, width):
        return pl.BlockSpec((1, tb, width), lambda b, i, cb=col // width: (b, i, cb))

    def prev(col, width):
        return pl.BlockSpec((1, SUBLANES, width),
                            lambda b, i, cb=col // width: (b, jnp.maximum(i * rows8 - 1, 0), cb))

    in_specs = [cur(EV_R, W), cur(EV_RK, W), cur(EV_RV, W), cur(EV_LORA, WL),
                prev(EV_R, W), prev(EV_RK, W), prev(EV_RV, W), prev(EV_LORA, WL)]
    args = [proj3] * 8
    if has_vres:
        in_specs.append(pl.BlockSpec((1, tb, W), lambda b, i: (b, i, 0)))
        args.append(v_first)
    in_specs += [_resident(mu.shape), _resident(mu_lora.shape), _resident(vecs.shape),
                 _resident(w2.shape), _resident(a2.shape), _resident(g2.shape)]
    args += [mu, mu_lora, vecs, w2, a2, g2]
    if has_vres:
        in_specs.append(_resident(v2.shape))
        args.append(v2)
    out_block = pl.BlockSpec((1, tb, W), lambda b, i: (b, i, 0))
    out_sds = jax.ShapeDtypeStruct((B, S, W), F32)
    res = pl.pallas_call(
        functools.partial(_rwkv_kernel, has_vres=has_vres),
        grid=(B, S // tb),
        in_specs=in_specs,
        out_specs=out_block if has_vres else [out_block, out_block],
        out_shape=out_sds if has_vres else [out_sds, out_sds],
        scratch_shapes=[pltpu.VMEM((tb + SUBLANES, W), F32),
                        pltpu.VMEM((tb + SUBLANES, WL), F32),
                        pltpu.VMEM((H, D, D), F32),
                        pltpu.VMEM((H, NC, C, D), F32),
                        pltpu.VMEM((H, NC, D, D), F32),
                        pltpu.VMEM((H, NC, D, D), F32),
                        pltpu.VMEM((H, NC, C, D), F32),
                        pltpu.VMEM((H, NC, 1, D), F32),
                        pltpu.VMEM((H, NC, C, D), F32)],
        compiler_params=_cparams(("parallel", "arbitrary")),
        name="rwkv_mixer",
    )(*args)
    if has_vres:
        return res, v_first
    return res[0], res[1]


def _dil_attn_kernel(q_ref, kc_ref, kp_ref, vc_ref, vp_ref, o_ref, lse_ref, kbuf, vbuf):
    Dh, Bk = DIL_HEAD_DIM, ATT_BLOCK
    QB = q_ref.shape[2]
    has_prev = pl.program_id(2) > 0
    kbuf[0:Bk] = kp_ref[0, 0]
    kbuf[Bk:] = kc_ref[0, 0]
    vbuf[0:Bk] = vp_ref[0, 0]
    vbuf[Bk:] = vc_ref[0, 0]
    qi = lax.broadcasted_iota(jnp.int32, (Bk, 2 * Bk), 0)
    cj = lax.broadcasted_iota(jnp.int32, (Bk, 2 * Bk), 1)
    band = (cj >= qi) & (cj <= qi + Bk)
    lane = lax.broadcasted_iota(jnp.int32, (Bk, LANES), 1)
    low = lane < Dh
    low_kv = lax.broadcasted_iota(jnp.int32, (2 * Bk, LANES), 1) < Dh
    one_kv = jnp.ones((2 * Bk, LANES), BF16)
    NP = DIL_WIDTH // LANES

    def pairs(x):
        return jnp.stack([x[:, hp * LANES:(hp + 1) * LANES] for hp in range(NP)], axis=0)

    for j in range(QB // Bk):
        ok = band if j > 0 else band & ((cj >= Bk) | has_prev)
        ok2 = jnp.concatenate([ok, ok], axis=0)
        q_pairs = pairs(q_ref[0, 0, j * Bk:(j + 1) * Bk, :])
        zero = jnp.zeros_like(q_pairs)
        q2 = jnp.concatenate([jnp.where(low, q_pairs, zero), jnp.where(low, zero, q_pairs)], axis=1)
        k_pairs = pairs(kbuf[j * Bk:(j + 2) * Bk, :])
        v_pairs = pairs(vbuf[j * Bk:(j + 2) * Bk, :])
        s = jnp.einsum('hqd,hkd->hqk', q2, k_pairs, preferred_element_type=F32)
        s = jnp.where(ok2, s, -jnp.inf)
        m = jnp.max(s, axis=-1, keepdims=True).reshape(2 * NP, Bk, 1)
        p = jnp.exp2(s - m.reshape(NP, 2 * Bk, 1)).astype(BF16).reshape(2 * NP, Bk, 2 * Bk)
        v_ext = jnp.stack([jnp.where(low_kv, v_pairs, one_kv), jnp.where(low_kv, one_kv, v_pairs)], axis=1)
        po = jnp.einsum('hqk,hkd->hqd', p, v_ext.reshape(2 * NP, 2 * Bk, LANES), preferred_element_type=F32)
        swapped = pltpu.roll(po.reshape(2 * NP * Bk, LANES), Dh, axis=1).reshape(2 * NP, Bk, LANES)
        r = po / swapped
        o_ref[0, 0, j * Bk:(j + 1) * Bk, :] = jnp.concatenate(
            [jnp.where(low, r[2 * hp], r[2 * hp + 1]) for hp in range(NP)], axis=-1).astype(o_ref.dtype)
        lse_blk = jnp.zeros((Bk, LANES), F32)
        for h in range(2 * NP):
            lse_blk = jnp.where(lane == h, m[h] + jnp.log2((swapped if h % 2 == 0 else po)[h]), lse_blk)
        lse_ref[0, 0, j * Bk:(j + 1) * Bk, :] = lse_blk


def dil_attention_group(q, k, v):
    B, d, L, W = q.shape
    qb = min(4 * ATT_BLOCK, L)
    nj = qb // ATT_BLOCK
    cur = pl.BlockSpec((1, 1, qb, W), lambda b, r, n: (b, r, n, 0))
    prev = pl.BlockSpec((1, 1, ATT_BLOCK, W), lambda b, r, n: (b, r, jnp.maximum(n * nj - 1, 0), 0))
    return pl.pallas_call(
        _dil_attn_kernel,
        grid=(B, d, L // qb),
        in_specs=[cur, cur, prev, cur, prev],
        out_specs=[cur, pl.BlockSpec((1, 1, qb, LANES), lambda b, r, n: (b, r, n, 0))],
        out_shape=[jax.ShapeDtypeStruct((B, d, L, W), BF16), jax.ShapeDtypeStruct((B, d, L, LANES), F32)],
        scratch_shapes=[pltpu.VMEM((ATT_BLOCK + qb, W), BF16), pltpu.VMEM((ATT_BLOCK + qb, W), BF16)],
        compiler_params=_cparams(("parallel", "parallel", "arbitrary")),
        name=f"dil_attn_d{d}",
    )(q, k, k, v, v)


def _out_proj_odd_kernel(x_ref, o0_ref, o1_ref, o2_ref, l0_ref, l1_ref, l2_ref, e_ref, w_ref, out_ref,
                         a_ref, b_ref, la_ref, lb_ref):
    dils = [d for _, d in DIL_PATTERNS]
    lses = [_interleave_load(r, d, la_ref, lb_ref) for r, d in zip((l0_ref, l1_ref, l2_ref), dils)]
    m = jnp.maximum(jnp.maximum(lses[0], lses[1]), lses[2])
    es = [jnp.exp2(l - m) for l in lses]
    inv = 1.0 / (es[0] + es[1] + es[2])
    expand = e_ref[...]
    mixed = None
    for e, o_ref, d in zip(es, (o0_ref, o1_ref, o2_ref), dils):
        term = _group_sum(e * inv, expand) * _interleave_load(o_ref, d, a_ref, b_ref)
        mixed = term if mixed is None else mixed + term
    out_ref[0] = x_ref[0] + jnp.dot(mixed.astype(BF16), w_ref[...], preferred_element_type=F32)


def out_proj_odd(x3, outs, lses, w, tm=512):
    B, S, D = x3.shape
    W = DIL_WIDTH
    expand = np.zeros((LANES, W), np.float32)
    for h in range(DIL_HEADS):
        expand[h, h * DIL_HEAD_DIM:(h + 1) * DIL_HEAD_DIM] = 1.0
    row = pl.BlockSpec((1, tm, D), lambda b, i: (b, i, 0))

    def res(width):
        return [pl.BlockSpec((1, d, tm // d, width), lambda b, i: (b, 0, i, 0)) for _, d in DIL_PATTERNS]

    return pl.pallas_call(
        _out_proj_odd_kernel,
        grid=(B, S // tm),
        in_specs=[row, *res(W), *res(LANES), _resident((LANES, W)), _resident(w.shape)],
        out_specs=row,
        out_shape=jax.ShapeDtypeStruct((B, S, D), F32),
        scratch_shapes=[pltpu.VMEM((W // LANES, tm, LANES), F32), pltpu.VMEM((W // LANES, tm, LANES), F32),
                        pltpu.VMEM((1, tm, LANES), F32), pltpu.VMEM((1, tm, LANES), F32)],
        compiler_params=_cparams(("parallel", "parallel")),
        name="out_proj_odd",
    )(x3, *outs, *lses, jnp.asarray(expand, BF16), w)


MOE_TM = 512
MOE_RB = 144
MOE_NBLK = (MOE_TM + N_GROUPS * (MOE_RB - 1)) // MOE_RB
LOGIT_ROWS = 24


def _moe_route(h, wr_ref):
    NG, EPG, RB = N_GROUPS, EXPERTS_PER_GROUP, MOE_RB
    tm = h.shape[0]
    hb, h_lo = _hi_lo(h)
    nt = (((1,), (1,)), ((), ()))
    lg = lax.dot_general(wr_ref[...], hb, nt, preferred_element_type=F32)
    lg = lg[:LANES] + lg[LANES:] + lax.dot_general(wr_ref[:LANES], h_lo, nt, preferred_element_type=F32)
    L = lg[:LOGIT_ROWS]
    r = lax.broadcasted_iota(jnp.int32, L.shape, 0)
    big = jnp.int32(1 << 30)
    gl = jnp.where(r < NG, L, -jnp.inf)
    gmax = jnp.max(gl, axis=0, keepdims=True)
    gsel = jnp.min(jnp.where(gl == gmax, r, big), axis=0, keepdims=True)
    p_group = 1.0 / jnp.sum(jnp.where(r < NG, jnp.exp(L - gmax), 0.0), axis=0, keepdims=True)
    lo = NG + gsel * EPG
    el = jnp.where((r >= lo) & (r < lo + EPG), L, -jnp.inf)
    v1 = jnp.max(el, axis=0, keepdims=True)
    i1 = jnp.min(jnp.where(el == v1, r, big), axis=0, keepdims=True)
    el2 = jnp.where(r == i1, -jnp.inf, el)
    v2 = jnp.max(el2, axis=0, keepdims=True)
    i2 = jnp.min(jnp.where(el2 == v2, r, big), axis=0, keepdims=True)
    t = jnp.exp(v2 - v1)
    gates = jnp.where(r == i1, p_group / (1.0 + t), 0.0) + jnp.where(r == i2, p_group * t / (1.0 + t), 0.0)
    gate4 = [jnp.sum(jnp.where(r == lo + j, gates, 0.0), axis=0, keepdims=True) for j in range(EPG)]

    r8 = lax.broadcasted_iota(jnp.int32, (SUBLANES, tm), 0)
    onehot = (r8 == gsel).astype(F32)
    ti = lax.broadcasted_iota(jnp.int32, (tm, tm), 0)
    tj = lax.broadcasted_iota(jnp.int32, (tm, tm), 1)
    before = (ti < tj).astype(BF16)
    rank = jnp.dot(onehot.astype(BF16), before, preferred_element_type=F32)
    count = jnp.sum(onehot, axis=1, keepdims=True).astype(jnp.int32)
    nblk = sum((count > j * RB).astype(jnp.int32) for j in range(pl.cdiv(tm, RB)))
    padded = (nblk * RB).astype(F32)
    start, starts = jnp.zeros((1, 1), F32), []
    for g in range(NG):
        starts.append(start)
        start = start + padded[g:g + 1]
    start8 = jnp.concatenate(starts + [jnp.zeros((SUBLANES - NG, 1), F32)], axis=0)
    dest = jnp.sum(onehot * (start8 + rank), axis=0, keepdims=True)
    return dest, gate4, nblk


def _moe_kernel(x_ref, *refs, final_norm, mixer_proj):
    NG, EPG, RB, NBLK = N_GROUPS, EXPERTS_PER_GROUP, MOE_RB, MOE_NBLK
    x = x_ref[...]
    if mixer_proj:
        ya_ref, yb_ref, wa_ref, wb_ref = refs[:4]
        refs = refs[4:]
        x = (x + jnp.dot(ya_ref[...].astype(BF16), wa_ref[...], preferred_element_type=F32)
             + jnp.dot(yb_ref[...].astype(BF16), wb_ref[...], preferred_element_type=F32))
    g_ref, wr_ref, w1_ref, w3_ref, w2_ref, fg_ref, o_ref, p_ref, hb_ref, g2_ref, hh_ref, ys_ref = refs
    FF = w1_ref.shape[3]
    tm = x_ref.shape[0]
    h = _rmsnorm(x, g_ref[...])
    dest, gate4, nblk = _moe_route(h, wr_ref)
    rows = lax.broadcasted_iota(jnp.int32, (NBLK * RB, tm), 0)
    p_ref[...] = jnp.where(rows == dest.astype(jnp.int32), 1.0, 0.0).astype(BF16)
    hb_ref[...] = h.astype(BF16)
    g4 = jnp.concatenate(gate4 + [jnp.zeros((SUBLANES - EPG, tm), F32)], axis=0)
    g_hi = g4.astype(BF16).astype(F32)
    g2_ref[...] = jnp.concatenate([g_hi, g4 - g_hi, jnp.zeros((LANES - 2 * SUBLANES, tm), F32)], axis=0).astype(BF16)

    ends, acc = [], jnp.int32(0)
    for g in range(NG):
        acc = acc + jnp.sum(nblk[g:g + 1])
        ends.append(acc)
    for b in range(NBLK):
        rs = slice(b * RB, (b + 1) * RB)
        grp = sum((b >= e).astype(jnp.int32) for e in ends[:-1])

        @pl.when(b < ends[-1])
        def _():
            pb = p_ref[rs]
            xr = jnp.dot(pb, hb_ref[...], preferred_element_type=F32).astype(BF16)
            gate = lax.dot_general(pb, g2_ref[...], (((1,), (1,)), ((), ())), preferred_element_type=F32)
            for j in range(EPG):
                a = jnp.dot(xr, w1_ref[0, grp * EPG + j], preferred_element_type=F32)
                u = jnp.dot(xr, w3_ref[0, grp * EPG + j], preferred_element_type=F32)
                gj = gate[:, j:j + 1] + gate[:, SUBLANES + j:SUBLANES + j + 1]
                hh_ref[:, j * FF:(j + 1) * FF] = ((a * _sigmoid(a)) * u * gj).astype(BF16)
            ys_ref[rs] = jnp.dot(hh_ref[...], w2_ref[0, grp], preferred_element_type=F32).astype(BF16)

        @pl.when(b >= ends[-1])
        def _():
            ys_ref[rs] = jnp.zeros((RB, ys_ref.shape[1]), BF16)

    y = x + lax.dot_general(p_ref[...], ys_ref[...], (((0,), (0,)), ((), ())), preferred_element_type=F32)
    if final_norm:
        y = _rmsnorm(y, fg_ref[...])
    o_ref[...] = y


def moe_block(x, gain, w_router, w1, w3, w2, layer, final_gain=None, mixer=None):
    T, D = x.shape
    _, NE, _, FF = w1.shape
    tm, RB, NBLK = MOE_TM, MOE_RB, MOE_NBLK
    final_norm = final_gain is not None
    fg = (final_gain if final_norm else jnp.ones((D,), F32)).reshape(1, D)

    def of_layer(w):
        return pl.BlockSpec((1,) + w.shape[1:], lambda i: (layer, 0, 0, 0), pipeline_mode=pl.Buffered(1))

    mixer_specs, mixer_args = [], []
    if mixer is not None:
        ya, yb, wa, wb = mixer
        mixer_specs = [pl.BlockSpec((tm, ya.shape[1]), lambda i: (i, 0)), pl.BlockSpec((tm, yb.shape[1]), lambda i: (i, 0)),
                       _resident(wa.shape), _resident(wb.shape)]
        mixer_args = [ya, yb, wa, wb]
    return pl.pallas_call(
        functools.partial(_moe_kernel, final_norm=final_norm, mixer_proj=mixer is not None),
        grid=(T // tm,),
        in_specs=[pl.BlockSpec((tm, D), lambda i: (i, 0)), *mixer_specs, _resident((1, D)), _resident(w_router.shape),
                  of_layer(w1), of_layer(w3), of_layer(w2), _resident((1, D))],
        out_specs=pl.BlockSpec((tm, D), lambda i: (i, 0)),
        out_shape=jax.ShapeDtypeStruct((T, D), F32),
        scratch_shapes=[pltpu.VMEM((NBLK * RB, tm), BF16), pltpu.VMEM((tm, D), BF16),
                        pltpu.VMEM((LANES, tm), BF16), pltpu.VMEM((RB, EXPERTS_PER_GROUP * FF), BF16),
                        pltpu.VMEM((NBLK * RB, D), BF16)],
        compiler_params=_cparams(("parallel",)),
        name="moe_block",
    )(x, *mixer_args, gain.reshape(1, D), w_router, w1, w3, w2, fg)


def _router_weights(w_group, w_expert):
    w = _place([(0, w_group)] + [(N_GROUPS + g * EXPERTS_PER_GROUP, w_expert[g]) for g in range(N_GROUPS)], LANES).T
    return jnp.concatenate(_hi_lo(w), axis=0)


def _place(cols, total):
    lead = cols[0][1].shape[:-1]
    parts, pos = [], 0
    for off, arr in sorted(cols, key=lambda c: c[0]):
        if off > pos:
            parts.append(jnp.zeros(lead + (off - pos,), F32))
        parts.append(arr)
        pos = off + arr.shape[-1]
    if total > pos:
        parts.append(jnp.zeros(lead + (total - pos,), F32))
    return jnp.concatenate(parts, axis=-1)


def _even_layout(t, vres=None):
    gq, gk, gv, gz, ga, gb, rr, rk, rv, lw, la, lg = jnp.split(
        t, np.cumsum([512, 512, 512, 512, 4, 4, 512, 512, 512, 32, 32, 96])[:-1].tolist(), axis=-1)
    cols = [(EV_Q, gq), (EV_K, gk), (EV_V, gv), (EV_Z, gz), (EV_AB, ga), (EV_AB + GDN_HEADS, gb),
            (EV_R, rr), (EV_RK, rk), (EV_RV, rv),
            (EV_LORA, lw), (EV_LORA + LORA_SEG, la), (EV_LORA + 2 * LORA_SEG, lg)]
    if vres is not None:
        cols.append((EV_LORA + 3 * LORA_SEG, vres))
    return _place(cols, EV_NPAD)


def _pad_rows(w, rows):
    return jnp.concatenate([w, jnp.zeros((rows - w.shape[0], w.shape[1]), F32)], axis=0)


def kernel(x, positions, ev_norm, ev_w_in, rwkv_vres_down, ev_w_out, gdn_conv_w, gdn_A_log, gdn_dt_bias, gdn_norm,
           rwkv_mu, rwkv_w0, rwkv_w2, rwkv_a0, rwkv_a2, rwkv_g2, rwkv_k_k, rwkv_k_a, rwkv_r_k, rwkv_ln_w, rwkv_ln_b,
           rwkv_vres_mu, rwkv_v0, rwkv_v2, od_norm, od_w_in, od_w_out, ffn_norm, moe_w_group, moe_w_expert,
           moe_w1, moe_w3, moe_w2, final_norm):
    B, S, D = x.shape
    T = B * S
    depth = ffn_norm.shape[0]
    xf = x.reshape(T, D)
    cos, sin = rope_tables(positions.reshape(T))
    n_groups = len(DIL_PATTERNS)
    moe_w1_bf = moe_w1.astype(BF16)
    moe_w3_bf = moe_w3.astype(BF16)
    moe_w2_bf = moe_w2.astype(BF16).reshape(depth, N_GROUPS, -1, D)
    v_first = None
    for layer in range(depth):
        i = layer // 2
        if layer % 2 == 0:
            vres_w = None if i == 0 else rwkv_vres_down[i - 1]
            w_in = _even_layout(ev_w_in[i], vres_w).astype(BF16)
            proj3 = norm_proj(xf, ev_norm[i], w_in).reshape(B, S, EV_NPAD)
            mu_r, mu_k, mu_v, mu_w, mu_a, mu_g = jnp.split(
                rwkv_mu[i], np.cumsum([RWKV_DIM, RWKV_DIM, RWKV_DIM, 32, 32])[:].tolist())
            mu = jnp.stack([mu_r, mu_k, mu_v])
            lora_cols = [(0, mu_w), (LORA_SEG, mu_a), (2 * LORA_SEG, mu_g)]
            if i > 0:
                lora_cols.append((3 * LORA_SEG, rwkv_vres_mu[i - 1]))
            mu_lora = _place(lora_cols, 4 * LORA_SEG).reshape(1, 4 * LORA_SEG)
            vec_rows = [rwkv_w0[i], rwkv_a0[i], rwkv_k_k[i], rwkv_k_a[i], rwkv_r_k[i].reshape(-1), rwkv_ln_w[i], rwkv_ln_b[i]]
            if i > 0:
                vec_rows.append(rwkv_v0[i - 1])
            vecs = jnp.stack(vec_rows)
            w2 = _pad_rows(rwkv_w2[i], LORA_SEG)
            a2 = _pad_rows(rwkv_a2[i], LORA_SEG)
            g2 = _pad_rows(rwkv_g2[i], LORA_SEG)
            ya = gdn_mixer(proj3, gdn_conv_w[i], gdn_A_log[i], gdn_dt_bias[i], gdn_norm[i])
            if i == 0:
                yb, v_first = rwkv_mixer(proj3, mu, mu_lora, vecs, w2, a2, g2)
            else:
                yb, _ = rwkv_mixer(proj3, mu, mu_lora, vecs, w2, a2, g2, v_first, _pad_rows(rwkv_v2[i - 1], LORA_SEG))
            w_out = ev_w_out[i].astype(BF16)
            na = GDN_HEADS * GDN_DV
            mixer = (ya.reshape(T, na), yb.reshape(T, RWKV_DIM), w_out[:na], w_out[na:])
        else:
            mixer = None
            w_in = od_w_in[i].astype(BF16)
            qkv = norm_proj_rope(xf.reshape(B, S, D), od_norm[i], w_in, cos.reshape(B, S, LANES),
                                 sin.reshape(B, S, LANES), DIL_HEAD_DIM ** -0.5 * math.log2(math.e))
            outs, lses = [], []
            for gi, (window, dilation) in enumerate(DIL_PATTERNS):
                assert window // dilation == ATT_BLOCK and (S // dilation) % (2 * ATT_BLOCK) == 0
                o, lse = dil_attention_group(qkv[gi], qkv[n_groups + gi], qkv[2 * n_groups + gi])
                outs.append(o)
                lses.append(lse)
            xf = out_proj_odd(xf.reshape(B, S, D), outs, lses, od_w_out[i].astype(BF16)).reshape(T, D)
        w_router = _router_weights(moe_w_group[layer], moe_w_expert[layer])
        xf = moe_block(xf, ffn_norm[layer], w_router, moe_w1_bf, moe_w3_bf, moe_w2_bf, layer,
                       final_norm if layer == depth - 1 else None, mixer)
    return xf.reshape(B, S, D)
```

```python
import functools
import math

import jax
import jax.numpy as jnp
import numpy as np
from jax import lax
from jax.experimental import pallas as pl
from jax.experimental.pallas import tpu as pltpu

F32 = jnp.float32
BF16 = jnp.bfloat16

RMS_EPS = 1e-6
L2_EPS = 1e-6
RWKV_LN_EPS = 64e-5
ROPE_THETA = 10000.0

LANES = 128
SUBLANES = 8
VMEM_LIMIT = 56 * 1024 * 1024

GDN_HEADS = 4
GDN_DK = 128
GDN_DV = 128
GDN_CONV = 4
RWKV_HEADS = 8
RWKV_HEAD = 64
RWKV_DIM = RWKV_HEADS * RWKV_HEAD
LORA_SEG = LANES
DIL_PATTERNS = ((128, 1), (512, 4), (2048, 16))
DIL_HEADS = 8
DIL_HEAD_DIM = 64
DIL_WIDTH = DIL_HEADS * DIL_HEAD_DIM
ATT_BLOCK = 128
N_GROUPS = 4
EXPERTS_PER_GROUP = 4
N_EXPERTS = N_GROUPS * EXPERTS_PER_GROUP
CHUNK = 64

EV_Q, EV_K, EV_V, EV_Z = 0, 512, 1024, 1536
EV_R, EV_RK, EV_RV, EV_LORA = 2048, 2560, 3072, 3584
EV_AB = EV_LORA + 4 * LORA_SEG
EV_NPAD = EV_AB + LANES


def _cparams(sem):
    return pltpu.CompilerParams(dimension_semantics=sem, vmem_limit_bytes=VMEM_LIMIT)


def _sigmoid(x):
    return 0.5 * jnp.tanh(0.5 * x) + 0.5


def _softplus(x):
    return jnp.maximum(x, 0.0) + jnp.log(1.0 + jnp.exp(-jnp.abs(x)))


def _rmsnorm(x, gain):
    return x * lax.rsqrt(jnp.mean(x * x, axis=-1, keepdims=True) + RMS_EPS) * gain


def _hi_lo(v):
    hi = v.astype(BF16)
    return hi, (v - hi.astype(F32)).astype(BF16)


def _group_sum(x, block_ones):
    hi, lo = _hi_lo(x)
    return (jnp.dot(hi, block_ones, preferred_element_type=F32)
            + jnp.dot(lo, block_ones, preferred_element_type=F32))


def _resident(shape):
    nd = len(shape)
    return pl.BlockSpec(shape, lambda *_: (0,) * nd, pipeline_mode=pl.Buffered(1))


def _rope_table_kernel(pos_ref, freq_ref, sign_ref, cos_ref, sin_ref):
    ang = pos_ref[...].astype(F32) * freq_ref[...]
    cos_ref[...] = jnp.cos(ang)
    sin_ref[...] = jnp.sin(ang) * sign_ref[...]


def rope_tables(positions_flat, tm=1024):
    T = positions_flat.shape[0]
    half = DIL_HEAD_DIM // 2
    inv_freq = ROPE_THETA ** (-jnp.arange(half, dtype=F32) * 2.0 / DIL_HEAD_DIM)
    lane = np.arange(LANES)
    freq = inv_freq[lane % half][None, :]
    sign = jnp.asarray(np.where(lane % DIL_HEAD_DIM < half, -1.0, 1.0), F32)[None, :]
    return pl.pallas_call(
        _rope_table_kernel,
        grid=(T // tm,),
        in_specs=[pl.BlockSpec((tm, 1), lambda i: (i, 0)), _resident((1, LANES)), _resident((1, LANES))],
        out_specs=[pl.BlockSpec((tm, LANES), lambda i: (i, 0))] * 2,
        out_shape=[jax.ShapeDtypeStruct((T, LANES), F32)] * 2,
        compiler_params=_cparams(("parallel",)),
        name="rope_tables",
    )(positions_flat.reshape(T, 1), freq, sign)


def _norm_proj_kernel(x_ref, g_ref, w_ref, o_ref):
    h = _rmsnorm(x_ref[...], g_ref[...]).astype(BF16)
    o_ref[...] = jnp.dot(h, w_ref[...], preferred_element_type=F32)


def norm_proj(x, gain, w, tm=512):
    T, D = x.shape
    N = w.shape[1]
    return pl.pallas_call(
        _norm_proj_kernel,
        grid=(T // tm,),
        in_specs=[pl.BlockSpec((tm, D), lambda i: (i, 0)), _resident((1, D)), _resident((D, N))],
        out_specs=pl.BlockSpec((tm, N), lambda i: (i, 0)),
        out_shape=jax.ShapeDtypeStruct((T, N), F32),
        compiler_params=_cparams(("parallel",)),
        name="norm_proj",
    )(x, gain.reshape(1, D), w)


def _deinterleave_store(out_ref, val, d, a_ref, b_ref):
    tm = val.shape[0]
    if d == 1:
        out_ref[0, 0] = val.astype(out_ref.dtype)
        return
    quarter = tm // 4
    for j in range(val.shape[1] // LANES):
        cs = slice(j * LANES, (j + 1) * LANES)
        a_ref[j] = val[:, cs]
        if d == 4:
            for r in range(4):
                out_ref[0, r, :, cs] = a_ref[j, pl.ds(r, quarter, stride=4), :].astype(out_ref.dtype)
            continue
        assert d == 16
        for r1 in range(4):
            b_ref[j, r1 * quarter:(r1 + 1) * quarter, :] = a_ref[j, pl.ds(r1, quarter, stride=4), :]
        for r1 in range(4):
            for r2 in range(4):
                out_ref[0, r1 + 4 * r2, :, cs] = (
                    b_ref[j, pl.ds(r1 * quarter + r2, quarter // 4, stride=4), :].astype(out_ref.dtype))


def _interleave_load(in_ref, d, a_ref, b_ref):
    if d == 1:
        return in_ref[0, 0].astype(F32)
    n_tiles, tm, _ = a_ref.shape
    quarter = tm // 4
    for j in range(n_tiles):
        cs = slice(j * LANES, (j + 1) * LANES)
        if d == 4:
            for r in range(4):
                a_ref[j, pl.ds(r, quarter, stride=4), :] = in_ref[0, r, :, cs].astype(F32)
            continue
        assert d == 16
        for r1 in range(4):
            for r2 in range(4):
                b_ref[j, pl.ds(r1 * quarter + r2, quarter // 4, stride=4), :] = in_ref[0, r1 + 4 * r2, :, cs].astype(F32)
        for r1 in range(4):
            a_ref[j, pl.ds(r1, quarter, stride=4), :] = b_ref[j, r1 * quarter:(r1 + 1) * quarter, :]
    return jnp.concatenate([a_ref[j] for j in range(n_tiles)], axis=-1)


def _norm_proj_rope_kernel(x_ref, g_ref, w_ref, cos_ref, sin_ref, *rest, q_scale):
    out_refs, (a_ref, b_ref) = rest[:-2], rest[-2:]
    n_groups = len(DIL_PATTERNS)
    h = _rmsnorm(x_ref[0], g_ref[...]).astype(BF16)
    cos = cos_ref[0]
    sin = sin_ref[0]
    lane = lax.broadcasted_iota(jnp.int32, cos.shape, 1)
    first_half = (lane % DIL_HEAD_DIM) < (DIL_HEAD_DIM // 2)
    W = DIL_WIDTH
    for c, out_ref in enumerate(out_refs):
        which, g = divmod(c, n_groups)
        acc = jnp.dot(h, w_ref[:, c * W:(c + 1) * W], preferred_element_type=F32)
        if which < 2:
            parts = []
            for j in range(W // LANES):
                blk = acc[:, j * LANES:(j + 1) * LANES]
                rot = jnp.where(first_half, pltpu.roll(blk, LANES - 32, axis=1), pltpu.roll(blk, 32, axis=1))
                out = blk * cos + rot * sin
                parts.append(out * q_scale if which == 0 else out)
            acc = jnp.concatenate(parts, axis=-1)
        _deinterleave_store(out_ref, acc, DIL_PATTERNS[g][1], a_ref, b_ref)


def norm_proj_rope(x3, gain, w, cos3, sin3, q_scale, tm=512):
    B, S, D = x3.shape
    W = DIL_WIDTH
    out_specs, out_shape = [], []
    for _ in range(3):
        for _, d in DIL_PATTERNS:
            out_specs.append(pl.BlockSpec((1, d, tm // d, W), lambda b, i: (b, 0, i, 0)))
            out_shape.append(jax.ShapeDtypeStruct((B, d, S // d, W), BF16))
    return pl.pallas_call(
        functools.partial(_norm_proj_rope_kernel, q_scale=q_scale),
        grid=(B, S // tm),
        in_specs=[pl.BlockSpec((1, tm, D), lambda b, i: (b, i, 0)), _resident((1, D)), _resident(w.shape),
                  pl.BlockSpec((1, tm, LANES), lambda b, i: (b, i, 0)),
                  pl.BlockSpec((1, tm, LANES), lambda b, i: (b, i, 0))],
        out_specs=out_specs,
        out_shape=out_shape,
        scratch_shapes=[pltpu.VMEM((W // LANES, tm, LANES), F32), pltpu.VMEM((W // LANES, tm, LANES), F32)],
        compiler_params=_cparams(("parallel", "parallel")),
        name="norm_proj_rope",
    )(x3, gain.reshape(1, D), w, cos3, sin3)


def _bmm(a, b):
    return jnp.einsum('nik,nkj->nij', a.astype(BF16), b.astype(BF16), preferred_element_type=F32)


def _bmm_nt(a, b):
    return jnp.einsum('nik,njk->nij', a.astype(BF16), b.astype(BF16), preferred_element_type=F32)


def _bmm_tn(a, b):
    return jnp.einsum('nci,ncj->nij', a.astype(BF16), b.astype(BF16), preferred_element_type=F32)


def _unit_lower_inverse(L):
    C = L.shape[-1]
    ri = lax.broadcasted_iota(jnp.int32, (C, C), 0)
    ci = lax.broadcasted_iota(jnp.int32, (C, C), 1)
    same16 = (ri // 16) == (ci // 16)
    same32 = (ri // 32) == (ci // 32)
    eye = (ri == ci).astype(F32)
    Ld = jnp.where(same16, L, 0.0)
    X = eye - Ld
    P = Ld
    for _ in range(3):
        P = _bmm(P, P)
        X = X + _bmm(X, P)
    for off in (jnp.where(same32 & ~same16, L, 0.0), jnp.where(~same32, L, 0.0)):
        X = X - _bmm(_bmm(X, off), X)
    return X


def _shifted_rows(buf_ref, cur, prev, first, shifts):
    TB = cur.shape[0]
    buf_ref[0:SUBLANES, :] = jnp.where(first, 0.0, prev)
    buf_ref[SUBLANES:, :] = cur
    return [buf_ref[SUBLANES - s:SUBLANES - s + TB, :] for s in shifts]


def _split_heads(x, n_heads, width):
    return jnp.stack([x[:, h * width:(h + 1) * width] for h in range(n_heads)], axis=0)


def _merge_heads(x):
    return jnp.concatenate([x[h] for h in range(x.shape[0])], axis=-1)


def _gdn_kernel(q_ref, k_ref, v_ref, z_ref, ab_ref, qp_ref, kp_ref, vp_ref, cw_ref, alog_ref, dtb_ref, nw_ref,
                o_ref, buf_ref, s_ref, qd_ref, pg_ref, qg_ref, ov_ref, egl_ref, oc_ref):
    H, DK, DV, C = GDN_HEADS, GDN_DK, GDN_DV, CHUNK
    TB = q_ref.shape[1]
    NC = TB // C
    first = pl.program_id(1) == 0

    @pl.when(first)
    def _():
        s_ref[...] = jnp.zeros_like(s_ref)

    def conv_silu(cur_ref, prev_ref, j):
        cur = cur_ref[0]
        x3, x2, x1 = _shifted_rows(buf_ref, cur, prev_ref[0], first, (3, 2, 1))
        w = cw_ref[j]
        y = x3 * w[0:1] + x2 * w[1:2] + x1 * w[2:3] + cur * w[3:4]
        return y * _sigmoid(y)

    q = _split_heads(conv_silu(q_ref, qp_ref, 0), H, DK)
    k = _split_heads(conv_silu(k_ref, kp_ref, 1), H, DK)
    v = _split_heads(conv_silu(v_ref, vp_ref, 2), H, DV)
    q = q * lax.rsqrt(jnp.sum(q * q, axis=-1, keepdims=True) + L2_EPS) * (DK ** -0.5)
    k = k * lax.rsqrt(jnp.sum(k * k, axis=-1, keepdims=True) + L2_EPS)
    ab = ab_ref[0]
    a = jnp.stack([ab[:, h:h + 1] for h in range(H)], axis=0)
    b = jnp.stack([ab[:, H + h:H + h + 1] for h in range(H)], axis=0)
    beta = _sigmoid(b)
    g = -jnp.exp(alog_ref[...]) * _softplus(a + dtb_ref[...])

    N = H * NC
    q = q.reshape(N, C, DK)
    k = k.reshape(N, C, DK)
    v = v.reshape(N, C, DV)
    beta = beta.reshape(N, C, 1)
    g = g.reshape(N, C, 1)

    ri = lax.broadcasted_iota(jnp.int32, (C, C), 0)
    ci = lax.broadcasted_iota(jnp.int32, (C, C), 1)
    causal = ri >= ci
    strict = ri > ci
    gb = jnp.broadcast_to(g, (N, C, C))
    g_row = jnp.sum(jnp.where(ri == ci, gb, 0.0), axis=1, keepdims=True)
    gc_col = jnp.sum(jnp.where(causal, jnp.broadcast_to(g_row, (N, C, C)), 0.0), axis=2, keepdims=True)
    gc_row = jnp.sum(jnp.where(ri <= ci, gb, 0.0), axis=1, keepdims=True)
    decay = jnp.where(causal, jnp.exp(jnp.where(causal, gc_col - gc_row, 0.0)), 0.0)

    kb = k * beta
    vb = v * beta
    kq = _bmm_nt(jnp.concatenate([kb, q], axis=1), k)
    L = jnp.where(strict, kq[:, :C] * decay, 0.0)
    aqk = kq[:, C:] * decay
    tinv = _unit_lower_inverse(L)
    egc = jnp.exp(gc_col)
    uw = _bmm(tinv, jnp.concatenate([vb, kb * egc], axis=2))
    auw = _bmm(aqk, uw)
    g_last = gc_col[:, C - 1:C, :]
    k_dec = k * jnp.exp(g_last - gc_col)
    kuw = _bmm_tn(k_dec, uw)
    qd_ref[...] = (q * egc - auw[:, :, DV:]).reshape(H, NC, C, DK)
    ov_ref[...] = auw[:, :, :DV].reshape(H, NC, C, DV)
    pg_ref[...] = kuw[:, :, DV:].reshape(H, NC, DK, DK)
    qg_ref[...] = kuw[:, :, :DV].reshape(H, NC, DK, DV)
    egl_ref[...] = jnp.broadcast_to(jnp.exp(g_last), (N, 1, LANES)).reshape(H, NC, 1, LANES)

    for c in range(NC):
        S = s_ref[...]
        oc_ref[:, c] = _bmm(qd_ref[:, c], S) + ov_ref[:, c]
        s_ref[...] = S * egl_ref[:, c] - _bmm(pg_ref[:, c], S) + qg_ref[:, c]

    o = oc_ref[...].reshape(H, TB, DV)
    o = o * lax.rsqrt(jnp.mean(o * o, axis=-1, keepdims=True) + RMS_EPS) * nw_ref[...]
    z = _split_heads(z_ref[0], H, DV)
    o_ref[0] = _merge_heads(o * (z * _sigmoid(z)))


def gdn_mixer(proj3, conv_w, a_log, dt_bias, norm_w, tb=512):
    B, S, _ = proj3.shape
    H, DK, DV, C = GDN_HEADS, GDN_DK, GDN_DV, CHUNK
    W = H * DK
    NC = tb // C
    nq, nk, nv, nz, nab = EV_Q // W, EV_K // W, EV_V // W, EV_Z // W, EV_AB // LANES
    rows8 = tb // SUBLANES

    def cur(cb):
        return pl.BlockSpec((1, tb, W), lambda b, i, cb=cb: (b, i, cb))

    def prev(cb):
        return pl.BlockSpec((1, SUBLANES, W), lambda b, i, cb=cb: (b, jnp.maximum(i * rows8 - 1, 0), cb))

    cw = conv_w.reshape(GDN_CONV, 3, W).transpose(1, 0, 2)
    return pl.pallas_call(
        _gdn_kernel,
        grid=(B, S // tb),
        in_specs=[cur(nq), cur(nk), cur(nv), cur(nz),
                  pl.BlockSpec((1, tb, LANES), lambda b, i: (b, i, nab)),
                  prev(nq), prev(nk), prev(nv),
                  _resident((3, GDN_CONV, W)), _resident((H, 1, 1)), _resident((H, 1, 1)),
                  _resident((1, 1, DV))],
        out_specs=pl.BlockSpec((1, tb, H * DV), lambda b, i: (b, i, 0)),
        out_shape=jax.ShapeDtypeStruct((B, S, H * DV), F32),
        scratch_shapes=[pltpu.VMEM((tb + SUBLANES, W), F32),
                        pltpu.VMEM((H, DK, DV), F32),
                        pltpu.VMEM((H, NC, C, DK), F32),
                        pltpu.VMEM((H, NC, DK, DK), F32),
                        pltpu.VMEM((H, NC, DK, DV), F32),
                        pltpu.VMEM((H, NC, C, DV), F32),
                        pltpu.VMEM((H, NC, 1, LANES), F32),
                        pltpu.VMEM((H, NC, C, DV), F32)],
        compiler_params=_cparams(("parallel", "arbitrary")),
        name="gdn_mixer",
    )(proj3, proj3, proj3, proj3, proj3, proj3, proj3, proj3, cw,
      a_log.reshape(H, 1, 1), dt_bias.reshape(H, 1, 1), norm_w.reshape(1, 1, DV))


def _rwkv_kernel(*refs, has_vres):
    if has_vres:
        (r_ref, k_ref, v_ref, lo_ref, rp_ref, kp_ref, vp_ref, lop_ref, vf_ref,
         mu_ref, mul_ref, vec_ref, w2_ref, a2_ref, g2_ref, v2_ref,
         y_ref, buf_ref, bufl_ref, s_ref, rr_ref, pp_ref, qq_ref, yv_ref, egc_ref, yc_ref) = refs
    else:
        (r_ref, k_ref, v_ref, lo_ref, rp_ref, kp_ref, vp_ref, lop_ref,
         mu_ref, mul_ref, vec_ref, w2_ref, a2_ref, g2_ref,
         y_ref, vf_out_ref, buf_ref, bufl_ref, s_ref, rr_ref, pp_ref, qq_ref, yv_ref, egc_ref, yc_ref) = refs
    H, D, C = RWKV_HEADS, RWKV_HEAD, CHUNK
    TB = r_ref.shape[1]
    NC = TB // C
    first = pl.program_id(1) == 0

    @pl.when(first)
    def _():
        s_ref[...] = jnp.zeros_like(s_ref)

    def mix(cur_ref, prev_ref, mu, buf):
        cur = cur_ref[0]
        (sh,) = _shifted_rows(buf, cur, prev_ref[0], first, (1,))
        return cur + mu * (sh - cur)

    r = mix(r_ref, rp_ref, mu_ref[0:1], buf_ref)
    k = mix(k_ref, kp_ref, mu_ref[1:2], buf_ref)
    v = mix(v_ref, vp_ref, mu_ref[2:3], buf_ref)
    lo = mix(lo_ref, lop_ref, mul_ref[...], bufl_ref)
    hw, ha, hg = (lo[:, j * LORA_SEG:(j + 1) * LORA_SEG] for j in range(3))
    w0, a0, k_k, k_a, r_k, ln_w, ln_b = (vec_ref[j:j + 1] for j in range(7))

    w_log = -_softplus(-(w0 + jnp.dot(jnp.tanh(hw), w2_ref[...], preferred_element_type=F32))) - 0.5
    lw = -jnp.exp(w_log)
    a = _sigmoid(a0 + jnp.dot(ha, a2_ref[...], preferred_element_type=F32))
    gate = jnp.dot(_sigmoid(hg), g2_ref[...], preferred_element_type=F32)
    if has_vres:
        hv = lo[:, 3 * LORA_SEG:4 * LORA_SEG]
        v0 = vec_ref[7:8]
        v = v + (vf_ref[0] - v) * _sigmoid(v0 + jnp.dot(hv, v2_ref[...], preferred_element_type=F32))
    else:
        vf_out_ref[0] = v

    ri = lax.broadcasted_iota(jnp.int32, (C, C), 0)
    ci = lax.broadcasted_iota(jnp.int32, (C, C), 1)
    tril = (ri >= ci).astype(BF16)
    lw_hi = lw.astype(BF16)
    lw_rest = lw - lw_hi.astype(F32)
    lw_mid = lw_rest.astype(BF16)
    lw_lo = (lw_rest - lw_mid.astype(F32)).astype(BF16)
    gi = jnp.concatenate(
        [sum(jnp.dot(tril, part[c * C:(c + 1) * C], preferred_element_type=F32) for part in (lw_hi, lw_mid, lw_lo))
         for c in range(NC)], axis=0)

    kk = _split_heads(k * k_k, H, D)
    kk = kk * lax.rsqrt(jnp.sum(kk * kk, axis=-1, keepdims=True) + L2_EPS)
    k = k * (1.0 + (a - 1.0) * k_a)
    r_h = _split_heads(r, H, D)
    k_h = _split_heads(k, H, D)
    v_h = _split_heads(v, H, D)
    a_h = _split_heads(a, H, D)
    gi_h = _split_heads(gi, H, D)
    lw_h = _split_heads(lw, H, D)
    bonus = jnp.sum(r_h * k_h * _split_heads(r_k, H, D), axis=-1, keepdims=True) * v_h

    N = H * NC
    rc, kc, vc, kkc, ac, gic, lwc = (t.reshape(N, C, D) for t in (r_h, k_h, v_h, kk, a_h, gi_h, lw_h))
    g_end = gic[:, C - 1:C, :]
    e_neg = jnp.exp(-gic)
    e_tail = jnp.exp(g_end - gic)
    at = -kkc * jnp.exp(gic - lwc)
    bvec = kkc * ac
    rt = rc * jnp.exp(gic)
    bt = bvec * e_neg
    kt = kc * e_neg
    b_dec = bvec * e_tail
    k_dec = kc * e_tail
    strict = ri > ci
    causal = ri >= ci
    amat = _bmm_nt(jnp.concatenate([at, rt], axis=1), jnp.concatenate([bt, kt], axis=1))
    a_ab = jnp.where(strict, amat[:, :C, :C], 0.0)
    a_ak = jnp.where(strict, amat[:, :C, C:], 0.0)
    a_r = jnp.concatenate([jnp.where(causal, amat[:, C:, :C], 0.0), jnp.where(causal, amat[:, C:, C:], 0.0)], axis=2)
    tinv = _unit_lower_inverse(-a_ab)
    x2 = _bmm(tinv, jnp.concatenate([at, _bmm(a_ak, vc)], axis=2))
    xv = jnp.concatenate([x2, jnp.concatenate([jnp.zeros_like(vc), vc], axis=2)], axis=1)
    y2 = _bmm(a_r, xv)
    pq = _bmm_tn(xv, jnp.concatenate([b_dec, k_dec], axis=1))
    rr_ref[...] = (rt + y2[:, :, :D]).reshape(H, NC, C, D)
    yv_ref[...] = y2[:, :, D:].reshape(H, NC, C, D)
    pp_ref[...] = pq[:, :D].reshape(H, NC, D, D)
    qq_ref[...] = pq[:, D:].reshape(H, NC, D, D)
    egc_ref[...] = jnp.exp(g_end).reshape(H, NC, 1, D)

    for c in range(NC):
        S = s_ref[...]
        yc_ref[:, c] = _bmm_nt(rr_ref[:, c], S) + yv_ref[:, c]
        s_ref[...] = S * egc_ref[:, c] + _bmm(S, pp_ref[:, c]) + qq_ref[:, c]

    y = yc_ref[...].reshape(H, TB, D)
    mean = jnp.mean(y, axis=-1, keepdims=True)
    yc = y - mean
    var = jnp.mean(yc * yc, axis=-1, keepdims=True)
    y = _merge_heads(yc * lax.rsqrt(var + RWKV_LN_EPS)) * ln_w + ln_b
    y_ref[0] = (y + _merge_heads(bonus)) * gate


def rwkv_mixer(proj3, mu, mu_lora, vecs, w2, a2, g2, v_first=None, v2=None, tb=256):
    B, S, _ = proj3.shape
    H, D, C = RWKV_HEADS, RWKV_HEAD, CHUNK
    W = RWKV_DIM
    WL = 4 * LORA_SEG
    NC = tb // C
    rows8 = tb // SUBLANES
    has_vres = v_first is not None

    def cur(col, width):
        return pl.BlockSpec((1, tb, width), lambda b, i, cb=col // width: (b, i, cb))

    def prev(col, width):
        return pl.BlockSpec((1, SUBLANES, width),
                            lambda b, i, cb=col // width: (b, jnp.maximum(i * rows8 - 1, 0), cb))

    in_specs = [cur(EV_R, W), cur(EV_RK, W), cur(EV_RV, W), cur(EV_LORA, WL),
                prev(EV_R, W), prev(EV_RK, W), prev(EV_RV, W), prev(EV_LORA, WL)]
    args = [proj3] * 8
    if has_vres:
        in_specs.append(pl.BlockSpec((1, tb, W), lambda b, i: (b, i, 0)))
        args.append(v_first)
    in_specs += [_resident(mu.shape), _resident(mu_lora.shape), _resident(vecs.shape),
                 _resident(w2.shape), _resident(a2.shape), _resident(g2.shape)]
    args += [mu, mu_lora, vecs, w2, a2, g2]
    if has_vres:
        in_specs.append(_resident(v2.shape))
        args.append(v2)
    out_block = pl.BlockSpec((1, tb, W), lambda b, i: (b, i, 0))
    out_sds = jax.ShapeDtypeStruct((B, S, W), F32)
    res = pl.pallas_call(
        functools.partial(_rwkv_kernel, has_vres=has_vres),
        grid=(B, S // tb),
        in_specs=in_specs,
        out_specs=out_block if has_vres else [out_block, out_block],
        out_shape=out_sds if has_vres else [out_sds, out_sds],
        scratch_shapes=[pltpu.VMEM((tb + SUBLANES, W), F32),
                        pltpu.VMEM((tb + SUBLANES, WL), F32),
                        pltpu.VMEM((H, D, D), F32),
                        pltpu.VMEM((H, NC, C, D), F32),
                        pltpu.VMEM((H, NC, D, D), F32),
                        pltpu.VMEM((H, NC, D, D), F32),
                        pltpu.VMEM((H, NC, C, D), F32),
                        pltpu.VMEM((H, NC, 1, D), F32),
                        pltpu.VMEM((H, NC, C, D), F32)],
        compiler_params=_cparams(("parallel", "arbitrary")),
        name="rwkv_mixer",
    )(*args)
    if has_vres:
        return res, v_first
    return res[0], res[1]


def _dil_attn_kernel(q_ref, kc_ref, kp_ref, vc_ref, vp_ref, o_ref, lse_ref, kbuf, vbuf):
    Dh, Bk = DIL_HEAD_DIM, ATT_BLOCK
    QB = q_ref.shape[2]
    has_prev = pl.program_id(2) > 0
    kbuf[0:Bk] = kp_ref[0, 0]
    kbuf[Bk:] = kc_ref[0, 0]
    vbuf[0:Bk] = vp_ref[0, 0]
    vbuf[Bk:] = vc_ref[0, 0]
    qi = lax.broadcasted_iota(jnp.int32, (Bk, 2 * Bk), 0)
    cj = lax.broadcasted_iota(jnp.int32, (Bk, 2 * Bk), 1)
    band = (cj >= qi) & (cj <= qi + Bk)
    lane = lax.broadcasted_iota(jnp.int32, (Bk, LANES), 1)
    low = lane < Dh
    low_kv = lax.broadcasted_iota(jnp.int32, (2 * Bk, LANES), 1) < Dh
    one_kv = jnp.ones((2 * Bk, LANES), BF16)
    NP = DIL_WIDTH // LANES

    def pairs(x):
        return jnp.stack([x[:, hp * LANES:(hp + 1) * LANES] for hp in range(NP)], axis=0)

    for j in range(QB // Bk):
        ok = band if j > 0 else band & ((cj >= Bk) | has_prev)
        ok2 = jnp.concatenate([ok, ok], axis=0)
        q_pairs = pairs(q_ref[0, 0, j * Bk:(j + 1) * Bk, :])
        zero = jnp.zeros_like(q_pairs)
        q2 = jnp.concatenate([jnp.where(low, q_pairs, zero), jnp.where(low, zero, q_pairs)], axis=1)
        k_pairs = pairs(kbuf[j * Bk:(j + 2) * Bk, :])
        v_pairs = pairs(vbuf[j * Bk:(j + 2) * Bk, :])
        s = jnp.einsum('hqd,hkd->hqk', q2, k_pairs, preferred_element_type=F32)
        s = jnp.where(ok2, s, -jnp.inf)
        m = jnp.max(s, axis=-1, keepdims=True).reshape(2 * NP, Bk, 1)
        p = jnp.exp2(s - m.reshape(NP, 2 * Bk, 1)).astype(BF16).reshape(2 * NP, Bk, 2 * Bk)
        v_ext = jnp.stack([jnp.where(low_kv, v_pairs, one_kv), jnp.where(low_kv, one_kv, v_pairs)], axis=1)
        po = jnp.einsum('hqk,hkd->hqd', p, v_ext.reshape(2 * NP, 2 * Bk, LANES), preferred_element_type=F32)
        swapped = pltpu.roll(po.reshape(2 * NP * Bk, LANES), Dh, axis=1).reshape(2 * NP, Bk, LANES)
        r = po / swapped
        o_ref[0, 0, j * Bk:(j + 1) * Bk, :] = jnp.concatenate(
            [jnp.where(low, r[2 * hp], r[2 * hp + 1]) for hp in range(NP)], axis=-1).astype(o_ref.dtype)
        lse_blk = jnp.zeros((Bk, LANES), F32)
        for h in range(2 * NP):
            lse_blk = jnp.where(lane == h, m[h] + jnp.log2((swapped if h % 2 == 0 else po)[h]), lse_blk)
        lse_ref[0, 0, j * Bk:(j + 1) * Bk, :] = lse_blk


def dil_attention_group(q, k, v):
    B, d, L, W = q.shape
    qb = min(4 * ATT_BLOCK, L)
    nj = qb // ATT_BLOCK
    cur = pl.BlockSpec((1, 1, qb, W), lambda b, r, n: (b, r, n, 0))
    prev = pl.BlockSpec((1, 1, ATT_BLOCK, W), lambda b, r, n: (b, r, jnp.maximum(n * nj - 1, 0), 0))
    return pl.pallas_call(
        _dil_attn_kernel,
        grid=(B, d, L // qb),
        in_specs=[cur, cur, prev, cur, prev],
        out_specs=[cur, pl.BlockSpec((1, 1, qb, LANES), lambda b, r, n: (b, r, n, 0))],
        out_shape=[jax.ShapeDtypeStruct((B, d, L, W), BF16), jax.ShapeDtypeStruct((B, d, L, LANES), F32)],
        scratch_shapes=[pltpu.VMEM((ATT_BLOCK + qb, W), BF16), pltpu.VMEM((ATT_BLOCK + qb, W), BF16)],
        compiler_params=_cparams(("parallel", "parallel", "arbitrary")),
        name=f"dil_attn_d{d}",
    )(q, k, k, v, v)


def _merged_attention(o_refs, l_refs, e_ref, a_ref, b_ref, la_ref, lb_ref):
    dils = [d for _, d in DIL_PATTERNS]
    lses = [_interleave_load(r, d, la_ref, lb_ref) for r, d in zip(l_refs, dils)]
    m = jnp.maximum(jnp.maximum(lses[0], lses[1]), lses[2])
    es = [jnp.exp2(l - m) for l in lses]
    inv = 1.0 / (es[0] + es[1] + es[2])
    expand = e_ref[...]
    mixed = None
    for e, o_ref, d in zip(es, o_refs, dils):
        term = _group_sum(e * inv, expand) * _interleave_load(o_ref, d, a_ref, b_ref)
        mixed = term if mixed is None else mixed + term
    return mixed


def _head_expand_matrix():
    expand = np.zeros((LANES, DIL_WIDTH), np.float32)
    for h in range(DIL_HEADS):
        expand[h, h * DIL_HEAD_DIM:(h + 1) * DIL_HEAD_DIM] = 1.0
    return jnp.asarray(expand, BF16)


MOE_TM = 512
MOE_RB = 144
MOE_NBLK = (MOE_TM + N_GROUPS * (MOE_RB - 1)) // MOE_RB
LOGIT_ROWS = 24


def _moe_route(h, wr_ref):
    NG, EPG, RB = N_GROUPS, EXPERTS_PER_GROUP, MOE_RB
    tm = h.shape[0]
    hb, h_lo = _hi_lo(h)
    nt = (((1,), (1,)), ((), ()))
    lg = lax.dot_general(wr_ref[...], hb, nt, preferred_element_type=F32)
    lg = lg[:LANES] + lg[LANES:] + lax.dot_general(wr_ref[:LANES], h_lo, nt, preferred_element_type=F32)
    L = lg[:LOGIT_ROWS]
    r = lax.broadcasted_iota(jnp.int32, L.shape, 0)
    big = jnp.int32(1 << 30)
    gl = jnp.where(r < NG, L, -jnp.inf)
    gmax = jnp.max(gl, axis=0, keepdims=True)
    gsel = jnp.min(jnp.where(gl == gmax, r, big), axis=0, keepdims=True)
    p_group = 1.0 / jnp.sum(jnp.where(r < NG, jnp.exp(L - gmax), 0.0), axis=0, keepdims=True)
    lo = NG + gsel * EPG
    el = jnp.where((r >= lo) & (r < lo + EPG), L, -jnp.inf)
    v1 = jnp.max(el, axis=0, keepdims=True)
    i1 = jnp.min(jnp.where(el == v1, r, big), axis=0, keepdims=True)
    el2 = jnp.where(r == i1, -jnp.inf, el)
    v2 = jnp.max(el2, axis=0, keepdims=True)
    i2 = jnp.min(jnp.where(el2 == v2, r, big), axis=0, keepdims=True)
    t = jnp.exp(v2 - v1)
    gates = jnp.where(r == i1, p_group / (1.0 + t), 0.0) + jnp.where(r == i2, p_group * t / (1.0 + t), 0.0)
    gate4 = [jnp.sum(jnp.where(r == lo + j, gates, 0.0), axis=0, keepdims=True) for j in range(EPG)]

    r8 = lax.broadcasted_iota(jnp.int32, (SUBLANES, tm), 0)
    onehot = (r8 == gsel).astype(F32)
    ti = lax.broadcasted_iota(jnp.int32, (tm, tm), 0)
    tj = lax.broadcasted_iota(jnp.int32, (tm, tm), 1)
    before = (ti < tj).astype(BF16)
    rank = jnp.dot(onehot.astype(BF16), before, preferred_element_type=F32)
    count = jnp.sum(onehot, axis=1, keepdims=True).astype(jnp.int32)
    nblk = sum((count > j * RB).astype(jnp.int32) for j in range(pl.cdiv(tm, RB)))
    padded = (nblk * RB).astype(F32)
    start, starts = jnp.zeros((1, 1), F32), []
    for g in range(NG):
        starts.append(start)
        start = start + padded[g:g + 1]
    start8 = jnp.concatenate(starts + [jnp.zeros((SUBLANES - NG, 1), F32)], axis=0)
    dest = jnp.sum(onehot * (start8 + rank), axis=0, keepdims=True)
    return dest, gate4, nblk


def _moe_kernel(x_ref, *refs, final_norm, mixer_proj):
    NG, EPG, RB, NBLK = N_GROUPS, EXPERTS_PER_GROUP, MOE_RB, MOE_NBLK
    x = x_ref[...]
    if mixer_proj == "even":
        ya_ref, yb_ref, wa_ref, wb_ref = refs[:4]
        refs = refs[4:]
        x = (x + jnp.dot(ya_ref[...].astype(BF16), wa_ref[...], preferred_element_type=F32)
             + jnp.dot(yb_ref[...].astype(BF16), wb_ref[...], preferred_element_type=F32))
    elif mixer_proj == "odd":
        mixed = _merged_attention(refs[0:3], refs[3:6], refs[6], *refs[-4:])
        x = x + jnp.dot(mixed.astype(BF16), refs[7][...], preferred_element_type=F32)
        refs = refs[8:-4]
    g_ref, wr_ref, w1_ref, w3_ref, w2_ref, fg_ref, o_ref, p_ref, hb_ref, g2_ref, hh_ref, ys_ref = refs
    FF = w1_ref.shape[3]
    tm = x_ref.shape[0]
    h = _rmsnorm(x, g_ref[...])
    dest, gate4, nblk = _moe_route(h, wr_ref)
    rows = lax.broadcasted_iota(jnp.int32, (NBLK * RB, tm), 0)
    p_ref[...] = jnp.where(rows == dest.astype(jnp.int32), 1.0, 0.0).astype(BF16)
    hb_ref[...] = h.astype(BF16)
    g4 = jnp.concatenate(gate4 + [jnp.zeros((SUBLANES - EPG, tm), F32)], axis=0)
    g_hi = g4.astype(BF16).astype(F32)
    g2_ref[...] = jnp.concatenate([g_hi, g4 - g_hi, jnp.zeros((LANES - 2 * SUBLANES, tm), F32)], axis=0).astype(BF16)

    ends, acc = [], jnp.int32(0)
    for g in range(NG):
        acc = acc + jnp.sum(nblk[g:g + 1])
        ends.append(acc)
    for b in range(NBLK):
        rs = slice(b * RB, (b + 1) * RB)
        grp = sum((b >= e).astype(jnp.int32) for e in ends[:-1])

        @pl.when(b < ends[-1])
        def _():
            pb = p_ref[rs]
            xr = jnp.dot(pb, hb_ref[...], preferred_element_type=F32).astype(BF16)
            gate = lax.dot_general(pb, g2_ref[...], (((1,), (1,)), ((), ())), preferred_element_type=F32)
            for j in range(EPG):
                a = jnp.dot(xr, w1_ref[0, grp * EPG + j], preferred_element_type=F32)
                u = jnp.dot(xr, w3_ref[0, grp * EPG + j], preferred_element_type=F32)
                gj = gate[:, j:j + 1] + gate[:, SUBLANES + j:SUBLANES + j + 1]
                hh_ref[:, j * FF:(j + 1) * FF] = ((a * _sigmoid(a)) * u * gj).astype(BF16)
            ys_ref[rs] = jnp.dot(hh_ref[...], w2_ref[0, grp], preferred_element_type=F32).astype(BF16)

        @pl.when(b >= ends[-1])
        def _():
            ys_ref[rs] = jnp.zeros((RB, ys_ref.shape[1]), BF16)

    y = x + lax.dot_general(p_ref[...], ys_ref[...], (((0,), (0,)), ((), ())), preferred_element_type=F32)
    if final_norm:
        y = _rmsnorm(y, fg_ref[...])
    o_ref[...] = y


def moe_block(x, gain, w_router, w1, w3, w2, layer, final_gain=None, mixer=None):
    T, D = x.shape
    _, NE, _, FF = w1.shape
    tm, RB, NBLK = MOE_TM, MOE_RB, MOE_NBLK
    final_norm = final_gain is not None
    fg = (final_gain if final_norm else jnp.ones((D,), F32)).reshape(1, D)

    def of_layer(w):
        return pl.BlockSpec((1,) + w.shape[1:], lambda i: (layer, 0, 0, 0), pipeline_mode=pl.Buffered(1))

    mixer_specs, mixer_args, mixer_scratch = [], [], []
    kind = None if mixer is None else mixer[0]
    if kind == "even":
        _, ya, yb, wa, wb = mixer
        mixer_specs = [pl.BlockSpec((tm, ya.shape[1]), lambda i: (i, 0)), pl.BlockSpec((tm, yb.shape[1]), lambda i: (i, 0)),
                       _resident(wa.shape), _resident(wb.shape)]
        mixer_args = [ya, yb, wa, wb]
    elif kind == "odd":
        _, outs, lses, w_out = mixer
        tiles_per_seq = outs[0].shape[1] * outs[0].shape[2] // tm
        for width, arrs in ((DIL_WIDTH, outs), (LANES, lses)):
            for (_, d), arr in zip(DIL_PATTERNS, arrs):
                mixer_specs.append(pl.BlockSpec((1, d, tm // d, width),
                                                lambda i: (i // tiles_per_seq, 0, i % tiles_per_seq, 0)))
                mixer_args.append(arr)
        mixer_specs += [_resident((LANES, DIL_WIDTH)), _resident(w_out.shape)]
        mixer_args += [_head_expand_matrix(), w_out]
        n_tiles = DIL_WIDTH // LANES
        mixer_scratch = [pltpu.VMEM((n_tiles, tm, LANES), F32), pltpu.VMEM((n_tiles, tm, LANES), F32),
                         pltpu.VMEM((1, tm, LANES), F32), pltpu.VMEM((1, tm, LANES), F32)]
    return pl.pallas_call(
        functools.partial(_moe_kernel, final_norm=final_norm, mixer_proj=kind),
        grid=(T // tm,),
        in_specs=[pl.BlockSpec((tm, D), lambda i: (i, 0)), *mixer_specs, _resident((1, D)), _resident(w_router.shape),
                  of_layer(w1), of_layer(w3), of_layer(w2), _resident((1, D))],
        out_specs=pl.BlockSpec((tm, D), lambda i: (i, 0)),
        out_shape=jax.ShapeDtypeStruct((T, D), F32),
        scratch_shapes=[pltpu.VMEM((NBLK * RB, tm), BF16), pltpu.VMEM((tm, D), BF16),
                        pltpu.VMEM((LANES, tm), BF16), pltpu.VMEM((RB, EXPERTS_PER_GROUP * FF), BF16),
                        pltpu.VMEM((NBLK * RB, D), BF16), *mixer_scratch],
        compiler_params=_cparams(("parallel",)),
        name="moe_block",
    )(x, *mixer_args, gain.reshape(1, D), w_router, w1, w3, w2, fg)


def _router_weights(w_group, w_expert):
    w = _place([(0, w_group)] + [(N_GROUPS + g * EXPERTS_PER_GROUP, w_expert[g]) for g in range(N_GROUPS)], LANES).T
    return jnp.concatenate(_hi_lo(w), axis=0)


def _place(cols, total):
    lead = cols[0][1].shape[:-1]
    parts, pos = [], 0
    for off, arr in sorted(cols, key=lambda c: c[0]):
        if off > pos:
            parts.append(jnp.zeros(lead + (off - pos,), F32))
        parts.append(arr)
        pos = off + arr.shape[-1]
    if total > pos:
        parts.append(jnp.zeros(lead + (total - pos,), F32))
    return jnp.concatenate(parts, axis=-1)


def _even_layout(t, vres=None):
    gq, gk, gv, gz, ga, gb, rr, rk, rv, lw, la, lg = jnp.split(
        t, np.cumsum([512, 512, 512, 512, 4, 4, 512, 512, 512, 32, 32, 96])[:-1].tolist(), axis=-1)
    cols = [(EV_Q, gq), (EV_K, gk), (EV_V, gv), (EV_Z, gz), (EV_AB, ga), (EV_AB + GDN_HEADS, gb),
            (EV_R, rr), (EV_RK, rk), (EV_RV, rv),
            (EV_LORA, lw), (EV_LORA + LORA_SEG, la), (EV_LORA + 2 * LORA_SEG, lg)]
    if vres is not None:
        cols.append((EV_LORA + 3 * LORA_SEG, vres))
    return _place(cols, EV_NPAD)


def _pad_rows(w, rows):
    return jnp.concatenate([w, jnp.zeros((rows - w.shape[0], w.shape[1]), F32)], axis=0)


def kernel(x, positions, ev_norm, ev_w_in, rwkv_vres_down, ev_w_out, gdn_conv_w, gdn_A_log, gdn_dt_bias, gdn_norm,
           rwkv_mu, rwkv_w0, rwkv_w2, rwkv_a0, rwkv_a2, rwkv_g2, rwkv_k_k, rwkv_k_a, rwkv_r_k, rwkv_ln_w, rwkv_ln_b,
           rwkv_vres_mu, rwkv_v0, rwkv_v2, od_norm, od_w_in, od_w_out, ffn_norm, moe_w_group, moe_w_expert,
           moe_w1, moe_w3, moe_w2, final_norm):
    B, S, D = x.shape
    T = B * S
    depth = ffn_norm.shape[0]
    xf = x.reshape(T, D)
    cos, sin = rope_tables(positions.reshape(T))
    n_groups = len(DIL_PATTERNS)
    moe_w1_bf = moe_w1.astype(BF16)
    moe_w3_bf = moe_w3.astype(BF16)
    moe_w2_bf = moe_w2.astype(BF16).reshape(depth, N_GROUPS, -1, D)
    v_first = None
    for layer in range(depth):
        i = layer // 2
        if layer % 2 == 0:
            vres_w = None if i == 0 else rwkv_vres_down[i - 1]
            w_in = _even_layout(ev_w_in[i], vres_w).astype(BF16)
            proj3 = norm_proj(xf, ev_norm[i], w_in).reshape(B, S, EV_NPAD)
            mu_r, mu_k, mu_v, mu_w, mu_a, mu_g = jnp.split(
                rwkv_mu[i], np.cumsum([RWKV_DIM, RWKV_DIM, RWKV_DIM, 32, 32])[:].tolist())
            mu = jnp.stack([mu_r, mu_k, mu_v])
            lora_cols = [(0, mu_w), (LORA_SEG, mu_a), (2 * LORA_SEG, mu_g)]
            if i > 0:
                lora_cols.append((3 * LORA_SEG, rwkv_vres_mu[i - 1]))
            mu_lora = _place(lora_cols, 4 * LORA_SEG).reshape(1, 4 * LORA_SEG)
            vec_rows = [rwkv_w0[i], rwkv_a0[i], rwkv_k_k[i], rwkv_k_a[i], rwkv_r_k[i].reshape(-1), rwkv_ln_w[i], rwkv_ln_b[i]]
            if i > 0:
                vec_rows.append(rwkv_v0[i - 1])
            vecs = jnp.stack(vec_rows)
            w2 = _pad_rows(rwkv_w2[i], LORA_SEG)
            a2 = _pad_rows(rwkv_a2[i], LORA_SEG)
            g2 = _pad_rows(rwkv_g2[i], LORA_SEG)
            ya = gdn_mixer(proj3, gdn_conv_w[i], gdn_A_log[i], gdn_dt_bias[i], gdn_norm[i])
            if i == 0:
                yb, v_first = rwkv_mixer(proj3, mu, mu_lora, vecs, w2, a2, g2)
            else:
                yb, _ = rwkv_mixer(proj3, mu, mu_lora, vecs, w2, a2, g2, v_first, _pad_rows(rwkv_v2[i - 1], LORA_SEG))
            w_out = ev_w_out[i].astype(BF16)
            na = GDN_HEADS * GDN_DV
            mixer = ("even", ya.reshape(T, na), yb.reshape(T, RWKV_DIM), w_out[:na], w_out[na:])
        else:
            w_in = od_w_in[i].astype(BF16)
            qkv = norm_proj_rope(xf.reshape(B, S, D), od_norm[i], w_in, cos.reshape(B, S, LANES),
                                 sin.reshape(B, S, LANES), DIL_HEAD_DIM ** -0.5 * math.log2(math.e))
            outs, lses = [], []
            for gi, (window, dilation) in enumerate(DIL_PATTERNS):
                assert window // dilation == ATT_BLOCK and (S // dilation) % (2 * ATT_BLOCK) == 0
                o, lse = dil_attention_group(qkv[gi], qkv[n_groups + gi], qkv[2 * n_groups + gi])
                outs.append(o)
                lses.append(lse)
            mixer = ("odd", outs, lses, od_w_out[i].astype(BF16))
        w_router = _router_weights(moe_w_group[layer], moe_w_expert[layer])
        xf = moe_block(xf, ffn_norm[layer], w_router, moe_w1_bf, moe_w3_bf, moe_w2_bf, layer,
                       final_norm if layer == depth - 1 else None, mixer)
    return xf.reshape(B, S, D)
```

```python
import functools
import math

import jax
import jax.numpy as jnp
import numpy as np
from jax import lax
from jax.experimental import pallas as pl
from jax.experimental.pallas import tpu as pltpu

F32 = jnp.float32
BF16 = jnp.bfloat16

RMS_EPS = 1e-6
L2_EPS = 1e-6
RWKV_LN_EPS = 64e-5
ROPE_THETA = 10000.0

LANES = 128
SUBLANES = 8
VMEM_LIMIT = 56 * 1024 * 1024

GDN_HEADS = 4
GDN_DK = 128
GDN_DV = 128
GDN_CONV = 4
RWKV_HEADS = 8
RWKV_HEAD = 64
RWKV_DIM = RWKV_HEADS * RWKV_HEAD
LORA_SEG = LANES
DIL_PATTERNS = ((128, 1), (512, 4), (2048, 16))
DIL_HEADS = 8
DIL_HEAD_DIM = 64
DIL_WIDTH = DIL_HEADS * DIL_HEAD_DIM
ATT_BLOCK = 128
N_GROUPS = 4
EXPERTS_PER_GROUP = 4
N_EXPERTS = N_GROUPS * EXPERTS_PER_GROUP
CHUNK = 64

EV_Q, EV_K, EV_V, EV_Z = 0, 512, 1024, 1536
EV_R, EV_RK, EV_RV, EV_LORA = 2048, 2560, 3072, 3584
EV_AB = EV_LORA + 4 * LORA_SEG
EV_NPAD = EV_AB + LANES


def _cparams(sem):
    return pltpu.CompilerParams(dimension_semantics=sem, vmem_limit_bytes=VMEM_LIMIT)


def _sigmoid(x):
    return 0.5 * jnp.tanh(0.5 * x) + 0.5


def _softplus(x):
    return jnp.maximum(x, 0.0) + jnp.log(1.0 + jnp.exp(-jnp.abs(x)))


def _rmsnorm(x, gain):
    return x * lax.rsqrt(jnp.mean(x * x, axis=-1, keepdims=True) + RMS_EPS) * gain


def _hi_lo(v):
    hi = v.astype(BF16)
    return hi, (v - hi.astype(F32)).astype(BF16)


def _group_sum(x, block_ones):
    hi, lo = _hi_lo(x)
    return (jnp.dot(hi, block_ones, preferred_element_type=F32)
            + jnp.dot(lo, block_ones, preferred_element_type=F32))


def _resident(shape):
    nd = len(shape)
    return pl.BlockSpec(shape, lambda *_: (0,) * nd, pipeline_mode=pl.Buffered(1))


def _rope_table_kernel(pos_ref, freq_ref, sign_ref, cos_ref, sin_ref):
    ang = pos_ref[...].astype(F32) * freq_ref[...]
    cos_ref[...] = jnp.cos(ang)
    sin_ref[...] = jnp.sin(ang) * sign_ref[...]


def rope_tables(positions_flat, tm=1024):
    T = positions_flat.shape[0]
    half = DIL_HEAD_DIM // 2
    inv_freq = ROPE_THETA ** (-jnp.arange(half, dtype=F32) * 2.0 / DIL_HEAD_DIM)
    lane = np.arange(LANES)
    freq = inv_freq[lane % half][None, :]
    sign = jnp.asarray(np.where(lane % DIL_HEAD_DIM < half, -1.0, 1.0), F32)[None, :]
    return pl.pallas_call(
        _rope_table_kernel,
        grid=(T // tm,),
        in_specs=[pl.BlockSpec((tm, 1), lambda i: (i, 0)), _resident((1, LANES)), _resident((1, LANES))],
        out_specs=[pl.BlockSpec((tm, LANES), lambda i: (i, 0))] * 2,
        out_shape=[jax.ShapeDtypeStruct((T, LANES), F32)] * 2,
        compiler_params=_cparams(("parallel",)),
        name="rope_tables",
    )(positions_flat.reshape(T, 1), freq, sign)


def _norm_proj_kernel(x_ref, g_ref, w_ref, o_ref):
    h = _rmsnorm(x_ref[...], g_ref[...]).astype(BF16)
    o_ref[...] = jnp.dot(h, w_ref[...], preferred_element_type=F32)


def norm_proj(x, gain, w, tm=512):
    T, D = x.shape
    N = w.shape[1]
    return pl.pallas_call(
        _norm_proj_kernel,
        grid=(T // tm,),
        in_specs=[pl.BlockSpec((tm, D), lambda i: (i, 0)), _resident((1, D)), _resident((D, N))],
        out_specs=pl.BlockSpec((tm, N), lambda i: (i, 0)),
        out_shape=jax.ShapeDtypeStruct((T, N), F32),
        compiler_params=_cparams(("parallel",)),
        name="norm_proj",
    )(x, gain.reshape(1, D), w)


def _deinterleave_store(out_ref, val, d, a_ref, b_ref):
    tm = val.shape[0]
    if d == 1:
        out_ref[0, 0] = val.astype(out_ref.dtype)
        return
    quarter = tm // 4
    for j in range(val.shape[1] // LANES):
        cs = slice(j * LANES, (j + 1) * LANES)
        a_ref[j] = val[:, cs]
        if d == 4:
            for r in range(4):
                out_ref[0, r, :, cs] = a_ref[j, pl.ds(r, quarter, stride=4), :].astype(out_ref.dtype)
            continue
        assert d == 16
        for r1 in range(4):
            b_ref[j, r1 * quarter:(r1 + 1) * quarter, :] = a_ref[j, pl.ds(r1, quarter, stride=4), :]
        for r1 in range(4):
            for r2 in range(4):
                out_ref[0, r1 + 4 * r2, :, cs] = (
                    b_ref[j, pl.ds(r1 * quarter + r2, quarter // 4, stride=4), :].astype(out_ref.dtype))


def _interleave_load(in_ref, d, a_ref, b_ref):
    if d == 1:
        return in_ref[0, 0].astype(F32)
    n_tiles, tm, _ = a_ref.shape
    quarter = tm // 4
    for j in range(n_tiles):
        cs = slice(j * LANES, (j + 1) * LANES)
        if d == 4:
            for r in range(4):
                a_ref[j, pl.ds(r, quarter, stride=4), :] = in_ref[0, r, :, cs].astype(F32)
            continue
        assert d == 16
        for r1 in range(4):
            for r2 in range(4):
                b_ref[j, pl.ds(r1 * quarter + r2, quarter // 4, stride=4), :] = in_ref[0, r1 + 4 * r2, :, cs].astype(F32)
        for r1 in range(4):
            a_ref[j, pl.ds(r1, quarter, stride=4), :] = b_ref[j, r1 * quarter:(r1 + 1) * quarter, :]
    return jnp.concatenate([a_ref[j] for j in range(n_tiles)], axis=-1)


def _norm_proj_rope_kernel(x_ref, g_ref, w_ref, cos_ref, sin_ref, *rest, q_scale):
    out_refs, (a_ref, b_ref) = rest[:-2], rest[-2:]
    n_groups = len(DIL_PATTERNS)
    h = _rmsnorm(x_ref[0], g_ref[...]).astype(BF16)
    cos = cos_ref[0]
    sin = sin_ref[0]
    lane = lax.broadcasted_iota(jnp.int32, cos.shape, 1)
    first_half = (lane % DIL_HEAD_DIM) < (DIL_HEAD_DIM // 2)
    W = DIL_WIDTH
    for c, out_ref in enumerate(out_refs):
        which, g = divmod(c, n_groups)
        acc = jnp.dot(h, w_ref[:, c * W:(c + 1) * W], preferred_element_type=F32)
        if which < 2:
            parts = []
            for j in range(W // LANES):
                blk = acc[:, j * LANES:(j + 1) * LANES]
                rot = jnp.where(first_half, pltpu.roll(blk, LANES - 32, axis=1), pltpu.roll(blk, 32, axis=1))
                out = blk * cos + rot * sin
                parts.append(out * q_scale if which == 0 else out)
            acc = jnp.concatenate(parts, axis=-1)
        _deinterleave_store(out_ref, acc, DIL_PATTERNS[g][1], a_ref, b_ref)


def norm_proj_rope(x3, gain, w, cos3, sin3, q_scale, tm=1024):
    B, S, D = x3.shape
    W = DIL_WIDTH
    out_specs, out_shape = [], []
    for _ in range(3):
        for _, d in DIL_PATTERNS:
            out_specs.append(pl.BlockSpec((1, d, tm // d, W), lambda b, i: (b, 0, i, 0)))
            out_shape.append(jax.ShapeDtypeStruct((B, d, S // d, W), BF16))
    return pl.pallas_call(
        functools.partial(_norm_proj_rope_kernel, q_scale=q_scale),
        grid=(B, S // tm),
        in_specs=[pl.BlockSpec((1, tm, D), lambda b, i: (b, i, 0)), _resident((1, D)), _resident(w.shape),
                  pl.BlockSpec((1, tm, LANES), lambda b, i: (b, i, 0)),
                  pl.BlockSpec((1, tm, LANES), lambda b, i: (b, i, 0))],
        out_specs=out_specs,
        out_shape=out_shape,
        scratch_shapes=[pltpu.VMEM((W // LANES, tm, LANES), F32), pltpu.VMEM((W // LANES, tm, LANES), F32)],
        compiler_params=_cparams(("parallel", "parallel")),
        name="norm_proj_rope",
    )(x3, gain.reshape(1, D), w, cos3, sin3)


def _bmm(a, b):
    return jnp.einsum('nik,nkj->nij', a.astype(BF16), b.astype(BF16), preferred_element_type=F32)


def _bmm_nt(a, b):
    return jnp.einsum('nik,njk->nij', a.astype(BF16), b.astype(BF16), preferred_element_type=F32)


def _bmm_tn(a, b):
    return jnp.einsum('nci,ncj->nij', a.astype(BF16), b.astype(BF16), preferred_element_type=F32)


def _unit_lower_inverse(L):
    C = L.shape[-1]
    ri = lax.broadcasted_iota(jnp.int32, (C, C), 0)
    ci = lax.broadcasted_iota(jnp.int32, (C, C), 1)
    same16 = (ri // 16) == (ci // 16)
    same32 = (ri // 32) == (ci // 32)
    eye = (ri == ci).astype(F32)
    Ld = jnp.where(same16, L, 0.0)
    X = eye - Ld
    P = Ld
    for _ in range(3):
        P = _bmm(P, P)
        X = X + _bmm(X, P)
    for off in (jnp.where(same32 & ~same16, L, 0.0), jnp.where(~same32, L, 0.0)):
        X = X - _bmm(_bmm(X, off), X)
    return X


def _shifted_rows(buf_ref, cur, prev, first, shifts):
    TB = cur.shape[0]
    buf_ref[0:SUBLANES, :] = jnp.where(first, 0.0, prev)
    buf_ref[SUBLANES:, :] = cur
    return [buf_ref[SUBLANES - s:SUBLANES - s + TB, :] for s in shifts]


def _split_heads(x, n_heads, width):
    return jnp.stack([x[:, h * width:(h + 1) * width] for h in range(n_heads)], axis=0)


def _merge_heads(x):
    return jnp.concatenate([x[h] for h in range(x.shape[0])], axis=-1)


def _gdn_kernel(q_ref, k_ref, v_ref, z_ref, ab_ref, qp_ref, kp_ref, vp_ref, cw_ref, alog_ref, dtb_ref, nw_ref,
                o_ref, buf_ref, s_ref, qd_ref, pg_ref, qg_ref, ov_ref, egl_ref, oc_ref):
    H, DK, DV, C = GDN_HEADS, GDN_DK, GDN_DV, CHUNK
    TB = q_ref.shape[1]
    NC = TB // C
    first = pl.program_id(1) == 0

    @pl.when(first)
    def _():
        s_ref[...] = jnp.zeros_like(s_ref)

    def conv_silu(cur_ref, prev_ref, j):
        cur = cur_ref[0]
        x3, x2, x1 = _shifted_rows(buf_ref, cur, prev_ref[0], first, (3, 2, 1))
        w = cw_ref[j]
        y = x3 * w[0:1] + x2 * w[1:2] + x1 * w[2:3] + cur * w[3:4]
        return y * _sigmoid(y)

    q = _split_heads(conv_silu(q_ref, qp_ref, 0), H, DK)
    k = _split_heads(conv_silu(k_ref, kp_ref, 1), H, DK)
    v = _split_heads(conv_silu(v_ref, vp_ref, 2), H, DV)
    q = q * lax.rsqrt(jnp.sum(q * q, axis=-1, keepdims=True) + L2_EPS) * (DK ** -0.5)
    k = k * lax.rsqrt(jnp.sum(k * k, axis=-1, keepdims=True) + L2_EPS)
    ab = ab_ref[0]
    a = jnp.stack([ab[:, h:h + 1] for h in range(H)], axis=0)
    b = jnp.stack([ab[:, H + h:H + h + 1] for h in range(H)], axis=0)
    beta = _sigmoid(b)
    g = -jnp.exp(alog_ref[...]) * _softplus(a + dtb_ref[...])

    N = H * NC
    q = q.reshape(N, C, DK)
    k = k.reshape(N, C, DK)
    v = v.reshape(N, C, DV)
    beta = beta.reshape(N, C, 1)
    g = g.reshape(N, C, 1)

    ri = lax.broadcasted_iota(jnp.int32, (C, C), 0)
    ci = lax.broadcasted_iota(jnp.int32, (C, C), 1)
    causal = ri >= ci
    strict = ri > ci
    gb = jnp.broadcast_to(g, (N, C, C))
    g_row = jnp.sum(jnp.where(ri == ci, gb, 0.0), axis=1, keepdims=True)
    gc_col = jnp.sum(jnp.where(causal, jnp.broadcast_to(g_row, (N, C, C)), 0.0), axis=2, keepdims=True)
    gc_row = jnp.sum(jnp.where(ri <= ci, gb, 0.0), axis=1, keepdims=True)
    decay = jnp.where(causal, jnp.exp(jnp.where(causal, gc_col - gc_row, 0.0)), 0.0)

    kb = k * beta
    vb = v * beta
    kq = _bmm_nt(jnp.concatenate([kb, q], axis=1), k)
    L = jnp.where(strict, kq[:, :C] * decay, 0.0)
    aqk = kq[:, C:] * decay
    tinv = _unit_lower_inverse(L)
    egc = jnp.exp(gc_col)
    uw = _bmm(tinv, jnp.concatenate([vb, kb * egc], axis=2))
    auw = _bmm(aqk, uw)
    g_last = gc_col[:, C - 1:C, :]
    k_dec = k * jnp.exp(g_last - gc_col)
    kuw = _bmm_tn(k_dec, uw)
    qd_ref[...] = (q * egc - auw[:, :, DV:]).reshape(H, NC, C, DK)
    ov_ref[...] = auw[:, :, :DV].reshape(H, NC, C, DV)
    pg_ref[...] = kuw[:, :, DV:].reshape(H, NC, DK, DK)
    qg_ref[...] = kuw[:, :, :DV].reshape(H, NC, DK, DV)
    egl_ref[...] = jnp.broadcast_to(jnp.exp(g_last), (N, 1, LANES)).reshape(H, NC, 1, LANES)

    for c in range(NC):
        S = s_ref[...]
        oc_ref[:, c] = _bmm(qd_ref[:, c], S) + ov_ref[:, c]
        s_ref[...] = S * egl_ref[:, c] - _bmm(pg_ref[:, c], S) + qg_ref[:, c]

    o = oc_ref[...].reshape(H, TB, DV)
    o = o * lax.rsqrt(jnp.mean(o * o, axis=-1, keepdims=True) + RMS_EPS) * nw_ref[...]
    z = _split_heads(z_ref[0], H, DV)
    o_ref[0] = _merge_heads(o * (z * _sigmoid(z)))


def gdn_mixer(proj3, conv_w, a_log, dt_bias, norm_w, tb=512):
    B, S, _ = proj3.shape
    H, DK, DV, C = GDN_HEADS, GDN_DK, GDN_DV, CHUNK
    W = H * DK
    NC = tb // C
    nq, nk, nv, nz, nab = EV_Q // W, EV_K // W, EV_V // W, EV_Z // W, EV_AB // LANES
    rows8 = tb // SUBLANES

    def cur(cb):
        return pl.BlockSpec((1, tb, W), lambda b, i, cb=cb: (b, i, cb))

    def prev(cb):
        return pl.BlockSpec((1, SUBLANES, W), lambda b, i, cb=cb: (b, jnp.maximum(i * rows8 - 1, 0), cb))

    cw = conv_w.reshape(GDN_CONV, 3, W).transpose(1, 0, 2)
    return pl.pallas_call(
        _gdn_kernel,
        grid=(B, S // tb),
        in_specs=[cur(nq), cur(nk), cur(nv), cur(nz),
                  pl.BlockSpec((1, tb, LANES), lambda b, i: (b, i, nab)),
                  prev(nq), prev(nk), prev(nv),
                  _resident((3, GDN_CONV, W)), _resident((H, 1, 1)), _resident((H, 1, 1)),
                  _resident((1, 1, DV))],
        out_specs=pl.BlockSpec((1, tb, H * DV), lambda b, i: (b, i, 0)),
        out_shape=jax.ShapeDtypeStruct((B, S, H * DV), F32),
        scratch_shapes=[pltpu.VMEM((tb + SUBLANES, W), F32),
                        pltpu.VMEM((H, DK, DV), F32),
                        pltpu.VMEM((H, NC, C, DK), F32),
                        pltpu.VMEM((H, NC, DK, DK), F32),
                        pltpu.VMEM((H, NC, DK, DV), F32),
                        pltpu.VMEM((H, NC, C, DV), F32),
                        pltpu.VMEM((H, NC, 1, LANES), F32),
                        pltpu.VMEM((H, NC, C, DV), F32)],
        compiler_params=_cparams(("parallel", "arbitrary")),
        name="gdn_mixer",
    )(proj3, proj3, proj3, proj3, proj3, proj3, proj3, proj3, cw,
      a_log.reshape(H, 1, 1), dt_bias.reshape(H, 1, 1), norm_w.reshape(1, 1, DV))


def _rwkv_kernel(*refs, has_vres):
    if has_vres:
        (r_ref, k_ref, v_ref, lo_ref, rp_ref, kp_ref, vp_ref, lop_ref, vf_ref,
         mu_ref, mul_ref, vec_ref, w2_ref, a2_ref, g2_ref, v2_ref,
         y_ref, buf_ref, bufl_ref, s_ref, rr_ref, pp_ref, qq_ref, yv_ref, egc_ref, yc_ref) = refs
    else:
        (r_ref, k_ref, v_ref, lo_ref, rp_ref, kp_ref, vp_ref, lop_ref,
         mu_ref, mul_ref, vec_ref, w2_ref, a2_ref, g2_ref,
         y_ref, vf_out_ref, buf_ref, bufl_ref, s_ref, rr_ref, pp_ref, qq_ref, yv_ref, egc_ref, yc_ref) = refs
    H, D, C = RWKV_HEADS, RWKV_HEAD, CHUNK
    TB = r_ref.shape[1]
    NC = TB // C
    first = pl.program_id(1) == 0

    @pl.when(first)
    def _():
        s_ref[...] = jnp.zeros_like(s_ref)

    def mix(cur_ref, prev_ref, mu, buf):
        cur = cur_ref[0]
        (sh,) = _shifted_rows(buf, cur, prev_ref[0], first, (1,))
        return cur + mu * (sh - cur)

    r = mix(r_ref, rp_ref, mu_ref[0:1], buf_ref)
    k = mix(k_ref, kp_ref, mu_ref[1:2], buf_ref)
    v = mix(v_ref, vp_ref, mu_ref[2:3], buf_ref)
    lo = mix(lo_ref, lop_ref, mul_ref[...], bufl_ref)
    hw, ha, hg = (lo[:, j * LORA_SEG:(j + 1) * LORA_SEG] for j in range(3))
    w0, a0, k_k, k_a, r_k, ln_w, ln_b = (vec_ref[j:j + 1] for j in range(7))

    w_log = -_softplus(-(w0 + jnp.dot(jnp.tanh(hw), w2_ref[...], preferred_element_type=F32))) - 0.5
    lw = -jnp.exp(w_log)
    a = _sigmoid(a0 + jnp.dot(ha, a2_ref[...], preferred_element_type=F32))
    gate = jnp.dot(_sigmoid(hg), g2_ref[...], preferred_element_type=F32)
    if has_vres:
        hv = lo[:, 3 * LORA_SEG:4 * LORA_SEG]
        v0 = vec_ref[7:8]
        v = v + (vf_ref[0] - v) * _sigmoid(v0 + jnp.dot(hv, v2_ref[...], preferred_element_type=F32))
    else:
        vf_out_ref[0] = v

    ri = lax.broadcasted_iota(jnp.int32, (C, C), 0)
    ci = lax.broadcasted_iota(jnp.int32, (C, C), 1)
    tril = (ri >= ci).astype(BF16)
    lw_hi = lw.astype(BF16)
    lw_rest = lw - lw_hi.astype(F32)
    lw_mid = lw_rest.astype(BF16)
    lw_lo = (lw_rest - lw_mid.astype(F32)).astype(BF16)
    gi = jnp.concatenate(
        [sum(jnp.dot(tril, part[c * C:(c + 1) * C], preferred_element_type=F32) for part in (lw_hi, lw_mid, lw_lo))
         for c in range(NC)], axis=0)

    kk = _split_heads(k * k_k, H, D)
    kk = kk * lax.rsqrt(jnp.sum(kk * kk, axis=-1, keepdims=True) + L2_EPS)
    k = k * (1.0 + (a - 1.0) * k_a)
    r_h = _split_heads(r, H, D)
    k_h = _split_heads(k, H, D)
    v_h = _split_heads(v, H, D)
    a_h = _split_heads(a, H, D)
    gi_h = _split_heads(gi, H, D)
    lw_h = _split_heads(lw, H, D)
    bonus = jnp.sum(r_h * k_h * _split_heads(r_k, H, D), axis=-1, keepdims=True) * v_h

    N = H * NC
    rc, kc, vc, kkc, ac, gic, lwc = (t.reshape(N, C, D) for t in (r_h, k_h, v_h, kk, a_h, gi_h, lw_h))
    g_end = gic[:, C - 1:C, :]
    e_neg = jnp.exp(-gic)
    e_tail = jnp.exp(g_end - gic)
    at = -kkc * jnp.exp(gic - lwc)
    bvec = kkc * ac
    rt = rc * jnp.exp(gic)
    bt = bvec * e_neg
    kt = kc * e_neg
    b_dec = bvec * e_tail
    k_dec = kc * e_tail
    strict = ri > ci
    causal = ri >= ci
    amat = _bmm_nt(jnp.concatenate([at, rt], axis=1), jnp.concatenate([bt, kt], axis=1))
    a_ab = jnp.where(strict, amat[:, :C, :C], 0.0)
    a_ak = jnp.where(strict, amat[:, :C, C:], 0.0)
    a_r = jnp.concatenate([jnp.where(causal, amat[:, C:, :C], 0.0), jnp.where(causal, amat[:, C:, C:], 0.0)], axis=2)
    tinv = _unit_lower_inverse(-a_ab)
    x2 = _bmm(tinv, jnp.concatenate([at, _bmm(a_ak, vc)], axis=2))
    xv = jnp.concatenate([x2, jnp.concatenate([jnp.zeros_like(vc), vc], axis=2)], axis=1)
    y2 = _bmm(a_r, xv)
    pq = _bmm_tn(xv, jnp.concatenate([b_dec, k_dec], axis=1))
    rr_ref[...] = (rt + y2[:, :, :D]).reshape(H, NC, C, D)
    yv_ref[...] = y2[:, :, D:].reshape(H, NC, C, D)
    pp_ref[...] = pq[:, :D].reshape(H, NC, D, D)
    qq_ref[...] = pq[:, D:].reshape(H, NC, D, D)
    egc_ref[...] = jnp.exp(g_end).reshape(H, NC, 1, D)

    for c in range(NC):
        S = s_ref[...]
        yc_ref[:, c] = _bmm_nt(rr_ref[:, c], S) + yv_ref[:, c]
        s_ref[...] = S * egc_ref[:, c] + _bmm(S, pp_ref[:, c]) + qq_ref[:, c]

    y = yc_ref[...].reshape(H, TB, D)
    mean = jnp.mean(y, axis=-1, keepdims=True)
    yc = y - mean
    var = jnp.mean(yc * yc, axis=-1, keepdims=True)
    y = _merge_heads(yc * lax.rsqrt(var + RWKV_LN_EPS)) * ln_w + ln_b
    y_ref[0] = (y + _merge_heads(bonus)) * gate


def rwkv_mixer(proj3, mu, mu_lora, vecs, w2, a2, g2, v_first=None, v2=None, tb=256):
    B, S, _ = proj3.shape
    H, D, C = RWKV_HEADS, RWKV_HEAD, CHUNK
    W = RWKV_DIM
    WL = 4 * LORA_SEG
    NC = tb // C
    rows8 = tb // SUBLANES
    has_vres = v_first is not None

    def cur(col, width):
        return pl.BlockSpec((1, tb, width), lambda b, i, cb=col // width: (b, i, cb))

    def prev(col, width):
        return pl.BlockSpec((1, SUBLANES, width),
                            lambda b, i, cb=col // width: (b, jnp.maximum(i * rows8 - 1, 0), cb))

    in_specs = [cur(EV_R, W), cur(EV_RK, W), cur(EV_RV, W), cur(EV_LORA, WL),
                prev(EV_R, W), prev(EV_RK, W), prev(EV_RV, W), prev(EV_LORA, WL)]
    args = [proj3] * 8
    if has_vres:
        in_specs.append(pl.BlockSpec((1, tb, W), lambda b, i: (b, i, 0)))
        args.append(v_first)
    in_specs += [_resident(mu.shape), _resident(mu_lora.shape), _resident(vecs.shape),
                 _resident(w2.shape), _resident(a2.shape), _resident(g2.shape)]
    args += [mu, mu_lora, vecs, w2, a2, g2]
    if has_vres:
        in_specs.append(_resident(v2.shape))
        args.append(v2)
    out_block = pl.BlockSpec((1, tb, W), lambda b, i: (b, i, 0))
    out_sds = jax.ShapeDtypeStruct((B, S, W), F32)
    res = pl.pallas_call(
        functools.partial(_rwkv_kernel, has_vres=has_vres),
        grid=(B, S // tb),
        in_specs=in_specs,
        out_specs=out_block if has_vres else [out_block, out_block],
        out_shape=out_sds if has_vres else [out_sds, out_sds],
        scratch_shapes=[pltpu.VMEM((tb + SUBLANES, W), F32),
                        pltpu.VMEM((tb + SUBLANES, WL), F32),
                        pltpu.VMEM((H, D, D), F32),
                        pltpu.VMEM((H, NC, C, D), F32),
                        pltpu.VMEM((H, NC, D, D), F32),
                        pltpu.VMEM((H, NC, D, D), F32),
                        pltpu.VMEM((H, NC, C, D), F32),
                        pltpu.VMEM((H, NC, 1, D), F32),
                        pltpu.VMEM((H, NC, C, D), F32)],
        compiler_params=_cparams(("parallel", "arbitrary")),
        name="rwkv_mixer",
    )(*args)
    if has_vres:
        return res, v_first
    return res[0], res[1]


def _dil_attn_kernel(q_ref, kc_ref, kp_ref, vc_ref, vp_ref, o_ref, lse_ref, kbuf, vbuf):
    Dh, Bk = DIL_HEAD_DIM, ATT_BLOCK
    QB = q_ref.shape[2]
    has_prev = pl.program_id(2) > 0
    kbuf[0:Bk] = kp_ref[0, 0]
    kbuf[Bk:] = kc_ref[0, 0]
    vbuf[0:Bk] = vp_ref[0, 0]
    vbuf[Bk:] = vc_ref[0, 0]
    qi = lax.broadcasted_iota(jnp.int32, (Bk, 2 * Bk), 0)
    cj = lax.broadcasted_iota(jnp.int32, (Bk, 2 * Bk), 1)
    band = (cj >= qi) & (cj <= qi + Bk)
    lane = lax.broadcasted_iota(jnp.int32, (Bk, LANES), 1)
    low = lane < Dh
    low_kv = lax.broadcasted_iota(jnp.int32, (2 * Bk, LANES), 1) < Dh
    one_kv = jnp.ones((2 * Bk, LANES), BF16)
    NP = DIL_WIDTH // LANES

    def pairs(x):
        return jnp.stack([x[:, hp * LANES:(hp + 1) * LANES] for hp in range(NP)], axis=0)

    for j in range(QB // Bk):
        ok = band if j > 0 else band & ((cj >= Bk) | has_prev)
        ok2 = jnp.concatenate([ok, ok], axis=0)
        q_pairs = pairs(q_ref[0, 0, j * Bk:(j + 1) * Bk, :])
        zero = jnp.zeros_like(q_pairs)
        q2 = jnp.concatenate([jnp.where(low, q_pairs, zero), jnp.where(low, zero, q_pairs)], axis=1)
        k_pairs = pairs(kbuf[j * Bk:(j + 2) * Bk, :])
        v_pairs = pairs(vbuf[j * Bk:(j + 2) * Bk, :])
        s = jnp.einsum('hqd,hkd->hqk', q2, k_pairs, preferred_element_type=F32)
        s = jnp.where(ok2, s, -jnp.inf)
        m = jnp.max(s, axis=-1, keepdims=True).reshape(2 * NP, Bk, 1)
        p = jnp.exp2(s - m.reshape(NP, 2 * Bk, 1)).astype(BF16).reshape(2 * NP, Bk, 2 * Bk)
        v_ext = jnp.stack([jnp.where(low_kv, v_pairs, one_kv), jnp.where(low_kv, one_kv, v_pairs)], axis=1)
        po = jnp.einsum('hqk,hkd->hqd', p, v_ext.reshape(2 * NP, 2 * Bk, LANES), preferred_element_type=F32)
        swapped = pltpu.roll(po.reshape(2 * NP * Bk, LANES), Dh, axis=1).reshape(2 * NP, Bk, LANES)
        r = po / swapped
        o_ref[0, 0, j * Bk:(j + 1) * Bk, :] = jnp.concatenate(
            [jnp.where(low, r[2 * hp], r[2 * hp + 1]) for hp in range(NP)], axis=-1).astype(o_ref.dtype)
        lse_blk = jnp.zeros((Bk, LANES), F32)
        for h in range(2 * NP):
            lse_blk = jnp.where(lane == h, m[h] + jnp.log2((swapped if h % 2 == 0 else po)[h]), lse_blk)
        lse_ref[0, 0, j * Bk:(j + 1) * Bk, :] = lse_blk


def dil_attention_group(q, k, v):
    B, d, L, W = q.shape
    qb = min(4 * ATT_BLOCK, L)
    nj = qb // ATT_BLOCK
    cur = pl.BlockSpec((1, 1, qb, W), lambda b, r, n: (b, r, n, 0))
    prev = pl.BlockSpec((1, 1, ATT_BLOCK, W), lambda b, r, n: (b, r, jnp.maximum(n * nj - 1, 0), 0))
    return pl.pallas_call(
        _dil_attn_kernel,
        grid=(B, d, L // qb),
        in_specs=[cur, cur, prev, cur, prev],
        out_specs=[cur, pl.BlockSpec((1, 1, qb, LANES), lambda b, r, n: (b, r, n, 0))],
        out_shape=[jax.ShapeDtypeStruct((B, d, L, W), BF16), jax.ShapeDtypeStruct((B, d, L, LANES), F32)],
        scratch_shapes=[pltpu.VMEM((ATT_BLOCK + qb, W), BF16), pltpu.VMEM((ATT_BLOCK + qb, W), BF16)],
        compiler_params=_cparams(("parallel", "parallel", "arbitrary")),
        name=f"dil_attn_d{d}",
    )(q, k, k, v, v)


def _merged_attention(o_refs, l_refs, e_ref, a_ref, b_ref, la_ref, lb_ref):
    dils = [d for _, d in DIL_PATTERNS]
    lses = [_interleave_load(r, d, la_ref, lb_ref) for r, d in zip(l_refs, dils)]
    m = jnp.maximum(jnp.maximum(lses[0], lses[1]), lses[2])
    es = [jnp.exp2(l - m) for l in lses]
    inv = 1.0 / (es[0] + es[1] + es[2])
    expand = e_ref[...]
    mixed = None
    for e, o_ref, d in zip(es, o_refs, dils):
        term = _group_sum(e * inv, expand) * _interleave_load(o_ref, d, a_ref, b_ref)
        mixed = term if mixed is None else mixed + term
    return mixed


def _head_expand_matrix():
    expand = np.zeros((LANES, DIL_WIDTH), np.float32)
    for h in range(DIL_HEADS):
        expand[h, h * DIL_HEAD_DIM:(h + 1) * DIL_HEAD_DIM] = 1.0
    return jnp.asarray(expand, BF16)


MOE_TM = 512
MOE_RB = 144
MOE_NBLK = (MOE_TM + N_GROUPS * (MOE_RB - 1)) // MOE_RB
LOGIT_ROWS = 24


def _moe_route(h, wr_ref):
    NG, EPG, RB = N_GROUPS, EXPERTS_PER_GROUP, MOE_RB
    tm = h.shape[0]
    hb, h_lo = _hi_lo(h)
    nt = (((1,), (1,)), ((), ()))
    lg = lax.dot_general(wr_ref[...], hb, nt, preferred_element_type=F32)
    lg = lg[:LANES] + lg[LANES:] + lax.dot_general(wr_ref[:LANES], h_lo, nt, preferred_element_type=F32)
    L = lg[:LOGIT_ROWS]
    r = lax.broadcasted_iota(jnp.int32, L.shape, 0)
    big = jnp.int32(1 << 30)
    gl = jnp.where(r < NG, L, -jnp.inf)
    gmax = jnp.max(gl, axis=0, keepdims=True)
    gsel = jnp.min(jnp.where(gl == gmax, r, big), axis=0, keepdims=True)
    p_group = 1.0 / jnp.sum(jnp.where(r < NG, jnp.exp(L - gmax), 0.0), axis=0, keepdims=True)
    lo = NG + gsel * EPG
    el = jnp.where((r >= lo) & (r < lo + EPG), L, -jnp.inf)
    v1 = jnp.max(el, axis=0, keepdims=True)
    i1 = jnp.min(jnp.where(el == v1, r, big), axis=0, keepdims=True)
    el2 = jnp.where(r == i1, -jnp.inf, el)
    v2 = jnp.max(el2, axis=0, keepdims=True)
    i2 = jnp.min(jnp.where(el2 == v2, r, big), axis=0, keepdims=True)
    t = jnp.exp(v2 - v1)
    gates = jnp.where(r == i1, p_group / (1.0 + t), 0.0) + jnp.where(r == i2, p_group * t / (1.0 + t), 0.0)
    gate4 = [jnp.sum(jnp.where(r == lo + j, gates, 0.0), axis=0, keepdims=True) for j in range(EPG)]

    r8 = lax.broadcasted_iota(jnp.int32, (SUBLANES, tm), 0)
    onehot = (r8 == gsel).astype(F32)
    ti = lax.broadcasted_iota(jnp.int32, (tm, tm), 0)
    tj = lax.broadcasted_iota(jnp.int32, (tm, tm), 1)
    before = (ti < tj).astype(BF16)
    rank = jnp.dot(onehot.astype(BF16), before, preferred_element_type=F32)
    count = jnp.sum(onehot, axis=1, keepdims=True).astype(jnp.int32)
    nblk = sum((count > j * RB).astype(jnp.int32) for j in range(pl.cdiv(tm, RB)))
    padded = (nblk * RB).astype(F32)
    start, starts = jnp.zeros((1, 1), F32), []
    for g in range(NG):
        starts.append(start)
        start = start + padded[g:g + 1]
    start8 = jnp.concatenate(starts + [jnp.zeros((SUBLANES - NG, 1), F32)], axis=0)
    dest = jnp.sum(onehot * (start8 + rank), axis=0, keepdims=True)
    return dest, gate4, nblk


def _moe_kernel(x_ref, *refs, final_norm, mixer_proj):
    NG, EPG, RB, NBLK = N_GROUPS, EXPERTS_PER_GROUP, MOE_RB, MOE_NBLK
    x = x_ref[...]
    if mixer_proj == "even":
        ya_ref, yb_ref, wa_ref, wb_ref = refs[:4]
        refs = refs[4:]
        x = (x + jnp.dot(ya_ref[...].astype(BF16), wa_ref[...], preferred_element_type=F32)
             + jnp.dot(yb_ref[...].astype(BF16), wb_ref[...], preferred_element_type=F32))
    elif mixer_proj == "odd":
        mixed = _merged_attention(refs[0:3], refs[3:6], refs[6], *refs[-4:])
        x = x + jnp.dot(mixed.astype(BF16), refs[7][...], preferred_element_type=F32)
        refs = refs[8:-4]
    g_ref, wr_ref, w1_ref, w3_ref, w2_ref, fg_ref, o_ref, p_ref, hb_ref, g2_ref, hh_ref, ys_ref = refs
    FF = w1_ref.shape[3]
    tm = x_ref.shape[0]
    h = _rmsnorm(x, g_ref[...])
    dest, gate4, nblk = _moe_route(h, wr_ref)
    rows = lax.broadcasted_iota(jnp.int32, (NBLK * RB, tm), 0)
    p_ref[...] = jnp.where(rows == dest.astype(jnp.int32), 1.0, 0.0).astype(BF16)
    hb_ref[...] = h.astype(BF16)
    g4 = jnp.concatenate(gate4 + [jnp.zeros((SUBLANES - EPG, tm), F32)], axis=0)
    g_hi = g4.astype(BF16).astype(F32)
    g2_ref[...] = jnp.concatenate([g_hi, g4 - g_hi, jnp.zeros((LANES - 2 * SUBLANES, tm), F32)], axis=0).astype(BF16)

    ends, acc = [], jnp.int32(0)
    for g in range(NG):
        acc = acc + jnp.sum(nblk[g:g + 1])
        ends.append(acc)
    for b in range(NBLK):
        rs = slice(b * RB, (b + 1) * RB)
        grp = sum((b >= e).astype(jnp.int32) for e in ends[:-1])

        @pl.when(b < ends[-1])
        def _():
            pb = p_ref[rs]
            xr = jnp.dot(pb, hb_ref[...], preferred_element_type=F32).astype(BF16)
            gate = lax.dot_general(pb, g2_ref[...], (((1,), (1,)), ((), ())), preferred_element_type=F32)
            for j in range(EPG):
                a = jnp.dot(xr, w1_ref[0, grp * EPG + j], preferred_element_type=F32)
                u = jnp.dot(xr, w3_ref[0, grp * EPG + j], preferred_element_type=F32)
                gj = gate[:, j:j + 1] + gate[:, SUBLANES + j:SUBLANES + j + 1]
                hh_ref[:, j * FF:(j + 1) * FF] = ((a * _sigmoid(a)) * u * gj).astype(BF16)
            ys_ref[rs] = jnp.dot(hh_ref[...], w2_ref[0, grp], preferred_element_type=F32).astype(BF16)

        @pl.when(b >= ends[-1])
        def _():
            ys_ref[rs] = jnp.zeros((RB, ys_ref.shape[1]), BF16)

    y = x + lax.dot_general(p_ref[...], ys_ref[...], (((0,), (0,)), ((), ())), preferred_element_type=F32)
    if final_norm:
        y = _rmsnorm(y, fg_ref[...])
    o_ref[...] = y


def moe_block(x, gain, w_router, w1, w3, w2, layer, final_gain=None, mixer=None):
    T, D = x.shape
    _, NE, _, FF = w1.shape
    tm, RB, NBLK = MOE_TM, MOE_RB, MOE_NBLK
    final_norm = final_gain is not None
    fg = (final_gain if final_norm else jnp.ones((D,), F32)).reshape(1, D)

    def of_layer(w):
        return pl.BlockSpec((1,) + w.shape[1:], lambda i: (layer, 0, 0, 0), pipeline_mode=pl.Buffered(1))

    mixer_specs, mixer_args, mixer_scratch = [], [], []
    kind = None if mixer is None else mixer[0]
    if kind == "even":
        _, ya, yb, wa, wb = mixer
        mixer_specs = [pl.BlockSpec((tm, ya.shape[1]), lambda i: (i, 0)), pl.BlockSpec((tm, yb.shape[1]), lambda i: (i, 0)),
                       _resident(wa.shape), _resident(wb.shape)]
        mixer_args = [ya, yb, wa, wb]
    elif kind == "odd":
        _, outs, lses, w_out = mixer
        tiles_per_seq = outs[0].shape[1] * outs[0].shape[2] // tm
        for width, arrs in ((DIL_WIDTH, outs), (LANES, lses)):
            for (_, d), arr in zip(DIL_PATTERNS, arrs):
                mixer_specs.append(pl.BlockSpec((1, d, tm // d, width),
                                                lambda i: (i // tiles_per_seq, 0, i % tiles_per_seq, 0)))
                mixer_args.append(arr)
        mixer_specs += [_resident((LANES, DIL_WIDTH)), _resident(w_out.shape)]
        mixer_args += [_head_expand_matrix(), w_out]
        n_tiles = DIL_WIDTH // LANES
        mixer_scratch = [pltpu.VMEM((n_tiles, tm, LANES), F32), pltpu.VMEM((n_tiles, tm, LANES), F32),
                         pltpu.VMEM((1, tm, LANES), F32), pltpu.VMEM((1, tm, LANES), F32)]
    return pl.pallas_call(
        functools.partial(_moe_kernel, final_norm=final_norm, mixer_proj=kind),
        grid=(T // tm,),
        in_specs=[pl.BlockSpec((tm, D), lambda i: (i, 0)), *mixer_specs, _resident((1, D)), _resident(w_router.shape),
                  of_layer(w1), of_layer(w3), of_layer(w2), _resident((1, D))],
        out_specs=pl.BlockSpec((tm, D), lambda i: (i, 0)),
        out_shape=jax.ShapeDtypeStruct((T, D), F32),
        scratch_shapes=[pltpu.VMEM((NBLK * RB, tm), BF16), pltpu.VMEM((tm, D), BF16),
                        pltpu.VMEM((LANES, tm), BF16), pltpu.VMEM((RB, EXPERTS_PER_GROUP * FF), BF16),
                        pltpu.VMEM((NBLK * RB, D), BF16), *mixer_scratch],
        compiler_params=_cparams(("parallel",)),
        name="moe_block",
    )(x, *mixer_args, gain.reshape(1, D), w_router, w1, w3, w2, fg)


def _router_weights(w_group, w_expert):
    w = _place([(0, w_group)] + [(N_GROUPS + g * EXPERTS_PER_GROUP, w_expert[g]) for g in range(N_GROUPS)], LANES).T
    return jnp.concatenate(_hi_lo(w), axis=0)


def _place(cols, total):
    lead, dtype = cols[0][1].shape[:-1], cols[0][1].dtype
    parts, pos = [], 0
    for off, arr in sorted(cols, key=lambda c: c[0]):
        if off > pos:
            parts.append(jnp.zeros(lead + (off - pos,), dtype))
        parts.append(arr)
        pos = off + arr.shape[-1]
    if total > pos:
        parts.append(jnp.zeros(lead + (total - pos,), dtype))
    return jnp.concatenate(parts, axis=-1)


def _even_layout(t, vres=None):
    gq, gk, gv, gz, ga, gb, rr, rk, rv, lw, la, lg = jnp.split(
        t, np.cumsum([512, 512, 512, 512, 4, 4, 512, 512, 512, 32, 32, 96])[:-1].tolist(), axis=-1)
    cols = [(EV_Q, gq), (EV_K, gk), (EV_V, gv), (EV_Z, gz), (EV_AB, ga), (EV_AB + GDN_HEADS, gb),
            (EV_R, rr), (EV_RK, rk), (EV_RV, rv),
            (EV_LORA, lw), (EV_LORA + LORA_SEG, la), (EV_LORA + 2 * LORA_SEG, lg)]
    if vres is not None:
        cols.append((EV_LORA + 3 * LORA_SEG, vres))
    return _place(cols, EV_NPAD)


def _pad_rows(w, rows):
    return jnp.concatenate([w, jnp.zeros((rows - w.shape[0], w.shape[1]), F32)], axis=0)


def kernel(x, positions, ev_norm, ev_w_in, rwkv_vres_down, ev_w_out, gdn_conv_w, gdn_A_log, gdn_dt_bias, gdn_norm,
           rwkv_mu, rwkv_w0, rwkv_w2, rwkv_a0, rwkv_a2, rwkv_g2, rwkv_k_k, rwkv_k_a, rwkv_r_k, rwkv_ln_w, rwkv_ln_b,
           rwkv_vres_mu, rwkv_v0, rwkv_v2, od_norm, od_w_in, od_w_out, ffn_norm, moe_w_group, moe_w_expert,
           moe_w1, moe_w3, moe_w2, final_norm):
    B, S, D = x.shape
    T = B * S
    depth = ffn_norm.shape[0]
    xf = x.reshape(T, D)
    cos, sin = rope_tables(positions.reshape(T))
    n_groups = len(DIL_PATTERNS)
    moe_w1_bf = moe_w1.astype(BF16)
    moe_w3_bf = moe_w3.astype(BF16)
    moe_w2_bf = moe_w2.astype(BF16).reshape(depth, N_GROUPS, -1, D)
    v_first = None
    for layer in range(depth):
        i = layer // 2
        if layer % 2 == 0:
            vres_w = None if i == 0 else rwkv_vres_down[i - 1].astype(BF16)
            w_in = _even_layout(ev_w_in[i].astype(BF16), vres_w)
            proj3 = norm_proj(xf, ev_norm[i], w_in).reshape(B, S, EV_NPAD)
            mu_r, mu_k, mu_v, mu_w, mu_a, mu_g = jnp.split(
                rwkv_mu[i], np.cumsum([RWKV_DIM, RWKV_DIM, RWKV_DIM, 32, 32])[:].tolist())
            mu = jnp.stack([mu_r, mu_k, mu_v])
            lora_cols = [(0, mu_w), (LORA_SEG, mu_a), (2 * LORA_SEG, mu_g)]
            if i > 0:
                lora_cols.append((3 * LORA_SEG, rwkv_vres_mu[i - 1]))
            mu_lora = _place(lora_cols, 4 * LORA_SEG).reshape(1, 4 * LORA_SEG)
            vec_rows = [rwkv_w0[i], rwkv_a0[i], rwkv_k_k[i], rwkv_k_a[i], rwkv_r_k[i].reshape(-1), rwkv_ln_w[i], rwkv_ln_b[i]]
            if i > 0:
                vec_rows.append(rwkv_v0[i - 1])
            vecs = jnp.stack(vec_rows)
            w2 = _pad_rows(rwkv_w2[i], LORA_SEG)
            a2 = _pad_rows(rwkv_a2[i], LORA_SEG)
            g2 = _pad_rows(rwkv_g2[i], LORA_SEG)
            ya = gdn_mixer(proj3, gdn_conv_w[i], gdn_A_log[i], gdn_dt_bias[i], gdn_norm[i])
            if i == 0:
                yb, v_first = rwkv_mixer(proj3, mu, mu_lora, vecs, w2, a2, g2)
            else:
                yb, _ = rwkv_mixer(proj3, mu, mu_lora, vecs, w2, a2, g2, v_first, _pad_rows(rwkv_v2[i - 1], LORA_SEG))
            w_out = ev_w_out[i].astype(BF16)
            na = GDN_HEADS * GDN_DV
            mixer = ("even", ya.reshape(T, na), yb.reshape(T, RWKV_DIM), w_out[:na], w_out[na:])
        else:
            w_in = od_w_in[i].astype(BF16)
            qkv = norm_proj_rope(xf.reshape(B, S, D), od_norm[i], w_in, cos.reshape(B, S, LANES),
                                 sin.reshape(B, S, LANES), DIL_HEAD_DIM ** -0.5 * math.log2(math.e))
            outs, lses = [], []
            for gi, (window, dilation) in enumerate(DIL_PATTERNS):
                assert window // dilation == ATT_BLOCK and (S // dilation) % (2 * ATT_BLOCK) == 0
                o, lse = dil_attention_group(qkv[gi], qkv[n_groups + gi], qkv[2 * n_groups + gi])
                outs.append(o)
                lses.append(lse)
            mixer = ("odd", outs, lses, od_w_out[i].astype(BF16))
        w_router = _router_weights(moe_w_group[layer], moe_w_expert[layer])
        xf = moe_block(xf, ffn_norm[layer], w_router, moe_w1_bf, moe_w3_bf, moe_w2_bf, layer,
                       final_norm if layer == depth - 1 else None, mixer)
    return xf.reshape(B, S, D)
```

```python
import functools
import math

import jax
import jax.numpy as jnp
import numpy as np
from jax import lax
from jax.experimental import pallas as pl
from jax.experimental.pallas import tpu as pltpu

F32 = jnp.float32
BF16 = jnp.bfloat16

RMS_EPS = 1e-6
L2_EPS = 1e-6
RWKV_LN_EPS = 64e-5
ROPE_THETA = 10000.0

LANES = 128
SUBLANES = 8
VMEM_LIMIT = 56 * 1024 * 1024

GDN_HEADS = 4
GDN_DK = 128
GDN_DV = 128
GDN_CONV = 4
RWKV_HEADS = 8
RWKV_HEAD = 64
RWKV_DIM = RWKV_HEADS * RWKV_HEAD
DECAY_LORA, AAA_LORA, MV_LORA, GATE_LORA = 32, 32, 32, 96
DIL_PATTERNS = ((128, 1), (512, 4), (2048, 16))
DIL_HEADS = 8
DIL_HEAD_DIM = 64
DIL_WIDTH = DIL_HEADS * DIL_HEAD_DIM
ATT_BLOCK = 128
N_GROUPS = 4
EXPERTS_PER_GROUP = 4
N_EXPERTS = N_GROUPS * EXPERTS_PER_GROUP
CHUNK = 64

EV_Q, EV_K, EV_V, EV_Z = 0, 512, 1024, 1536
EV_R, EV_RK, EV_RV, EV_SMALL = 2048, 2560, 3072, 3584
SMALL_W, SMALL_A, SMALL_V = 0, DECAY_LORA, DECAY_LORA + AAA_LORA
SMALL_AB, SMALL_G = LANES, LANES + 2 * GDN_HEADS
EV_NPAD = EV_SMALL + 2 * LANES


def _cparams(sem):
    return pltpu.CompilerParams(dimension_semantics=sem, vmem_limit_bytes=VMEM_LIMIT)


def _sigmoid(x):
    return 0.5 * jnp.tanh(0.5 * x) + 0.5


def _softplus(x):
    return jnp.maximum(x, 0.0) + jnp.log(1.0 + jnp.exp(-jnp.abs(x)))


def _rmsnorm(x, gain):
    return x * lax.rsqrt(jnp.mean(x * x, axis=-1, keepdims=True) + RMS_EPS) * gain


def _hi_lo(v):
    hi = v.astype(BF16)
    return hi, (v - hi.astype(F32)).astype(BF16)


def _group_sum(x, block_ones):
    hi, lo = _hi_lo(x)
    return (jnp.dot(hi, block_ones, preferred_element_type=F32)
            + jnp.dot(lo, block_ones, preferred_element_type=F32))


def _resident(shape):
    nd = len(shape)
    return pl.BlockSpec(shape, lambda *_: (0,) * nd, pipeline_mode=pl.Buffered(1))


def _rope_table_kernel(pos_ref, freq_ref, sign_ref, cos_ref, sin_ref):
    ang = pos_ref[...].astype(F32) * freq_ref[...]
    cos_ref[...] = jnp.cos(ang)
    sin_ref[...] = jnp.sin(ang) * sign_ref[...]


def rope_tables(positions_flat, tm=1024):
    T = positions_flat.shape[0]
    half = DIL_HEAD_DIM // 2
    inv_freq = ROPE_THETA ** (-jnp.arange(half, dtype=F32) * 2.0 / DIL_HEAD_DIM)
    lane = np.arange(LANES)
    freq = inv_freq[lane % half][None, :]
    sign = jnp.asarray(np.where(lane % DIL_HEAD_DIM < half, -1.0, 1.0), F32)[None, :]
    return pl.pallas_call(
        _rope_table_kernel,
        grid=(T // tm,),
        in_specs=[pl.BlockSpec((tm, 1), lambda i: (i, 0)), _resident((1, LANES)), _resident((1, LANES))],
        out_specs=[pl.BlockSpec((tm, LANES), lambda i: (i, 0))] * 2,
        out_shape=[jax.ShapeDtypeStruct((T, LANES), F32)] * 2,
        compiler_params=_cparams(("parallel",)),
        name="rope_tables",
    )(positions_flat.reshape(T, 1), freq, sign)


def _norm_proj_kernel(x_ref, g_ref, w_ref, o_ref):
    h = _rmsnorm(x_ref[...], g_ref[...]).astype(BF16)
    o_ref[...] = jnp.dot(h, w_ref[...], preferred_element_type=F32)


def norm_proj(x, gain, w, tm=512):
    T, D = x.shape
    N = w.shape[1]
    return pl.pallas_call(
        _norm_proj_kernel,
        grid=(T // tm,),
        in_specs=[pl.BlockSpec((tm, D), lambda i: (i, 0)), _resident((1, D)), _resident((D, N))],
        out_specs=pl.BlockSpec((tm, N), lambda i: (i, 0)),
        out_shape=jax.ShapeDtypeStruct((T, N), F32),
        compiler_params=_cparams(("parallel",)),
        name="norm_proj",
    )(x, gain.reshape(1, D), w)


def _deinterleave_store(out_ref, val, d, a_ref, b_ref):
    tm = val.shape[0]
    if d == 1:
        out_ref[0, 0] = val.astype(out_ref.dtype)
        return
    quarter = tm // 4
    for j in range(val.shape[1] // LANES):
        cs = slice(j * LANES, (j + 1) * LANES)
        a_ref[j] = val[:, cs]
        if d == 4:
            for r in range(4):
                out_ref[0, r, :, cs] = a_ref[j, pl.ds(r, quarter, stride=4), :].astype(out_ref.dtype)
            continue
        assert d == 16
        for r1 in range(4):
            b_ref[j, r1 * quarter:(r1 + 1) * quarter, :] = a_ref[j, pl.ds(r1, quarter, stride=4), :]
        for r1 in range(4):
            for r2 in range(4):
                out_ref[0, r1 + 4 * r2, :, cs] = (
                    b_ref[j, pl.ds(r1 * quarter + r2, quarter // 4, stride=4), :].astype(out_ref.dtype))


def _interleave_load(in_ref, d, a_ref, b_ref):
    if d == 1:
        return in_ref[0, 0].astype(F32)
    n_tiles, tm, _ = a_ref.shape
    quarter = tm // 4
    for j in range(n_tiles):
        cs = slice(j * LANES, (j + 1) * LANES)
        if d == 4:
            for r in range(4):
                a_ref[j, pl.ds(r, quarter, stride=4), :] = in_ref[0, r, :, cs].astype(F32)
            continue
        assert d == 16
        for r1 in range(4):
            for r2 in range(4):
                b_ref[j, pl.ds(r1 * quarter + r2, quarter // 4, stride=4), :] = in_ref[0, r1 + 4 * r2, :, cs].astype(F32)
        for r1 in range(4):
            a_ref[j, pl.ds(r1, quarter, stride=4), :] = b_ref[j, r1 * quarter:(r1 + 1) * quarter, :]
    return jnp.concatenate([a_ref[j] for j in range(n_tiles)], axis=-1)


def _norm_proj_rope_kernel(x_ref, g_ref, w_ref, cos_ref, sin_ref, *rest, q_scale):
    out_refs, (a_ref, b_ref) = rest[:-2], rest[-2:]
    n_groups = len(DIL_PATTERNS)
    h = _rmsnorm(x_ref[0], g_ref[...]).astype(BF16)
    cos = cos_ref[0]
    sin = sin_ref[0]
    lane = lax.broadcasted_iota(jnp.int32, cos.shape, 1)
    first_half = (lane % DIL_HEAD_DIM) < (DIL_HEAD_DIM // 2)
    W = DIL_WIDTH
    for c, out_ref in enumerate(out_refs):
        which, g = divmod(c, n_groups)
        acc = jnp.dot(h, w_ref[:, c * W:(c + 1) * W], preferred_element_type=F32)
        if which < 2:
            parts = []
            for j in range(W // LANES):
                blk = acc[:, j * LANES:(j + 1) * LANES]
                rot = jnp.where(first_half, pltpu.roll(blk, LANES - 32, axis=1), pltpu.roll(blk, 32, axis=1))
                out = blk * cos + rot * sin
                parts.append(out * q_scale if which == 0 else out)
            acc = jnp.concatenate(parts, axis=-1)
        _deinterleave_store(out_ref, acc, DIL_PATTERNS[g][1], a_ref, b_ref)


def norm_proj_rope(x3, gain, w, cos3, sin3, q_scale, tm=1024):
    B, S, D = x3.shape
    W = DIL_WIDTH
    out_specs, out_shape = [], []
    for _ in range(3):
        for _, d in DIL_PATTERNS:
            out_specs.append(pl.BlockSpec((1, d, tm // d, W), lambda b, i: (b, 0, i, 0)))
            out_shape.append(jax.ShapeDtypeStruct((B, d, S // d, W), BF16))
    return pl.pallas_call(
        functools.partial(_norm_proj_rope_kernel, q_scale=q_scale),
        grid=(B, S // tm),
        in_specs=[pl.BlockSpec((1, tm, D), lambda b, i: (b, i, 0)), _resident((1, D)), _resident(w.shape),
                  pl.BlockSpec((1, tm, LANES), lambda b, i: (b, i, 0)),
                  pl.BlockSpec((1, tm, LANES), lambda b, i: (b, i, 0))],
        out_specs=out_specs,
        out_shape=out_shape,
        scratch_shapes=[pltpu.VMEM((W // LANES, tm, LANES), F32), pltpu.VMEM((W // LANES, tm, LANES), F32)],
        compiler_params=_cparams(("parallel", "parallel")),
        name="norm_proj_rope",
    )(x3, gain.reshape(1, D), w, cos3, sin3)


def _bmm(a, b):
    return jnp.einsum('nik,nkj->nij', a.astype(BF16), b.astype(BF16), preferred_element_type=F32)


def _bmm_nt(a, b):
    return jnp.einsum('nik,njk->nij', a.astype(BF16), b.astype(BF16), preferred_element_type=F32)


def _bmm_tn(a, b):
    return jnp.einsum('nci,ncj->nij', a.astype(BF16), b.astype(BF16), preferred_element_type=F32)


def _unit_lower_inverse(L):
    C = L.shape[-1]
    ri = lax.broadcasted_iota(jnp.int32, (C, C), 0)
    ci = lax.broadcasted_iota(jnp.int32, (C, C), 1)
    same16 = (ri // 16) == (ci // 16)
    same32 = (ri // 32) == (ci // 32)
    eye = (ri == ci).astype(F32)
    Ld = jnp.where(same16, L, 0.0)
    X = eye - Ld
    P = Ld
    for _ in range(3):
        P = _bmm(P, P)
        X = X + _bmm(X, P)
    for off in (jnp.where(same32 & ~same16, L, 0.0), jnp.where(~same32, L, 0.0)):
        X = X - _bmm(_bmm(X, off), X)
    return X


def _shifted_rows(buf_ref, cur, prev, first, shifts):
    TB = cur.shape[0]
    buf_ref[0:SUBLANES, :] = jnp.where(first, 0.0, prev)
    buf_ref[SUBLANES:, :] = cur
    return [buf_ref[SUBLANES - s:SUBLANES - s + TB, :] for s in shifts]


def _split_heads(x, n_heads, width):
    return jnp.stack([x[:, h * width:(h + 1) * width] for h in range(n_heads)], axis=0)


def _merge_heads(x):
    return jnp.concatenate([x[h] for h in range(x.shape[0])], axis=-1)


def _gdn_kernel(q_ref, k_ref, v_ref, z_ref, ab_ref, qp_ref, kp_ref, vp_ref, cw_ref, alog_ref, dtb_ref, nw_ref,
                o_ref, buf_ref, s_ref, qd_ref, pg_ref, qg_ref, ov_ref, egl_ref, oc_ref):
    H, DK, DV, C = GDN_HEADS, GDN_DK, GDN_DV, CHUNK
    TB = q_ref.shape[1]
    NC = TB // C
    first = pl.program_id(1) == 0

    @pl.when(first)
    def _():
        s_ref[...] = jnp.zeros_like(s_ref)

    def conv_silu(cur_ref, prev_ref, j):
        cur = cur_ref[0]
        x3, x2, x1 = _shifted_rows(buf_ref, cur, prev_ref[0], first, (3, 2, 1))
        w = cw_ref[j]
        y = x3 * w[0:1] + x2 * w[1:2] + x1 * w[2:3] + cur * w[3:4]
        return y * _sigmoid(y)

    q = _split_heads(conv_silu(q_ref, qp_ref, 0), H, DK)
    k = _split_heads(conv_silu(k_ref, kp_ref, 1), H, DK)
    v = _split_heads(conv_silu(v_ref, vp_ref, 2), H, DV)
    q = q * lax.rsqrt(jnp.sum(q * q, axis=-1, keepdims=True) + L2_EPS) * (DK ** -0.5)
    k = k * lax.rsqrt(jnp.sum(k * k, axis=-1, keepdims=True) + L2_EPS)
    ab = ab_ref[0]
    a = jnp.stack([ab[:, h:h + 1] for h in range(H)], axis=0)
    b = jnp.stack([ab[:, H + h:H + h + 1] for h in range(H)], axis=0)
    beta = _sigmoid(b)
    g = -jnp.exp(alog_ref[...]) * _softplus(a + dtb_ref[...])

    N = H * NC
    q = q.reshape(N, C, DK)
    k = k.reshape(N, C, DK)
    v = v.reshape(N, C, DV)
    beta = beta.reshape(N, C, 1)
    g = g.reshape(N, C, 1)

    ri = lax.broadcasted_iota(jnp.int32, (C, C), 0)
    ci = lax.broadcasted_iota(jnp.int32, (C, C), 1)
    causal = ri >= ci
    strict = ri > ci
    gb = jnp.broadcast_to(g, (N, C, C))
    g_row = jnp.sum(jnp.where(ri == ci, gb, 0.0), axis=1, keepdims=True)
    gc_col = jnp.sum(jnp.where(causal, jnp.broadcast_to(g_row, (N, C, C)), 0.0), axis=2, keepdims=True)
    gc_row = jnp.sum(jnp.where(ri <= ci, gb, 0.0), axis=1, keepdims=True)
    decay = jnp.where(causal, jnp.exp(jnp.where(causal, gc_col - gc_row, 0.0)), 0.0)

    kb = k * beta
    vb = v * beta
    kq = _bmm_nt(jnp.concatenate([kb, q], axis=1), k)
    L = jnp.where(strict, kq[:, :C] * decay, 0.0)
    aqk = kq[:, C:] * decay
    tinv = _unit_lower_inverse(L)
    egc = jnp.exp(gc_col)
    uw = _bmm(tinv, jnp.concatenate([vb, kb * egc], axis=2))
    auw = _bmm(aqk, uw)
    g_last = gc_col[:, C - 1:C, :]
    k_dec = k * jnp.exp(g_last - gc_col)
    kuw = _bmm_tn(k_dec, uw)
    qd_ref[...] = (q * egc - auw[:, :, DV:]).reshape(H, NC, C, DK)
    ov_ref[...] = auw[:, :, :DV].reshape(H, NC, C, DV)
    pg_ref[...] = kuw[:, :, DV:].reshape(H, NC, DK, DK)
    qg_ref[...] = kuw[:, :, :DV].reshape(H, NC, DK, DV)
    egl_ref[...] = jnp.broadcast_to(jnp.exp(g_last), (N, 1, LANES)).reshape(H, NC, 1, LANES)

    for c in range(NC):
        S = s_ref[...]
        oc_ref[:, c] = _bmm(qd_ref[:, c], S) + ov_ref[:, c]
        s_ref[...] = S * egl_ref[:, c] - _bmm(pg_ref[:, c], S) + qg_ref[:, c]

    o = oc_ref[...].reshape(H, TB, DV)
    o = o * lax.rsqrt(jnp.mean(o * o, axis=-1, keepdims=True) + RMS_EPS) * nw_ref[...]
    z = _split_heads(z_ref[0], H, DV)
    o_ref[0] = _merge_heads(o * (z * _sigmoid(z)))


def gdn_mixer(proj3, conv_w, a_log, dt_bias, norm_w, tb=512):
    B, S, _ = proj3.shape
    H, DK, DV, C = GDN_HEADS, GDN_DK, GDN_DV, CHUNK
    W = H * DK
    NC = tb // C
    nq, nk, nv, nz, nab = EV_Q // W, EV_K // W, EV_V // W, EV_Z // W, (EV_SMALL + SMALL_AB) // LANES
    rows8 = tb // SUBLANES

    def cur(cb):
        return pl.BlockSpec((1, tb, W), lambda b, i, cb=cb: (b, i, cb))

    def prev(cb):
        return pl.BlockSpec((1, SUBLANES, W), lambda b, i, cb=cb: (b, jnp.maximum(i * rows8 - 1, 0), cb))

    cw = conv_w.reshape(GDN_CONV, 3, W).transpose(1, 0, 2)
    return pl.pallas_call(
        _gdn_kernel,
        grid=(B, S // tb),
        in_specs=[cur(nq), cur(nk), cur(nv), cur(nz),
                  pl.BlockSpec((1, tb, LANES), lambda b, i: (b, i, nab)),
                  prev(nq), prev(nk), prev(nv),
                  _resident((3, GDN_CONV, W)), _resident((H, 1, 1)), _resident((H, 1, 1)),
                  _resident((1, 1, DV))],
        out_specs=pl.BlockSpec((1, tb, H * DV), lambda b, i: (b, i, 0)),
        out_shape=jax.ShapeDtypeStruct((B, S, H * DV), F32),
        scratch_shapes=[pltpu.VMEM((tb + SUBLANES, W), F32),
                        pltpu.VMEM((H, DK, DV), F32),
                        pltpu.VMEM((H, NC, C, DK), F32),
                        pltpu.VMEM((H, NC, DK, DK), F32),
                        pltpu.VMEM((H, NC, DK, DV), F32),
                        pltpu.VMEM((H, NC, C, DV), F32),
                        pltpu.VMEM((H, NC, 1, LANES), F32),
                        pltpu.VMEM((H, NC, C, DV), F32)],
        compiler_params=_cparams(("parallel", "arbitrary")),
        name="gdn_mixer",
    )(proj3, proj3, proj3, proj3, proj3, proj3, proj3, proj3, cw,
      a_log.reshape(H, 1, 1), dt_bias.reshape(H, 1, 1), norm_w.reshape(1, 1, DV))


def _rwkv_kernel(*refs, has_vres):
    if has_vres:
        (r_ref, k_ref, v_ref, lo_ref, rp_ref, kp_ref, vp_ref, lop_ref, vf_ref,
         mu_ref, mul_ref, vec_ref, w2_ref, a2_ref, g2_ref, v2_ref,
         y_ref, buf_ref, bufl_ref, s_ref, rr_ref, pp_ref, qq_ref, yv_ref, egc_ref, yc_ref) = refs
    else:
        (r_ref, k_ref, v_ref, lo_ref, rp_ref, kp_ref, vp_ref, lop_ref,
         mu_ref, mul_ref, vec_ref, w2_ref, a2_ref, g2_ref,
         y_ref, vf_out_ref, buf_ref, bufl_ref, s_ref, rr_ref, pp_ref, qq_ref, yv_ref, egc_ref, yc_ref) = refs
    H, D, C = RWKV_HEADS, RWKV_HEAD, CHUNK
    TB = r_ref.shape[1]
    NC = TB // C
    first = pl.program_id(1) == 0

    @pl.when(first)
    def _():
        s_ref[...] = jnp.zeros_like(s_ref)

    def mix(cur_ref, prev_ref, mu, buf):
        cur = cur_ref[0]
        (sh,) = _shifted_rows(buf, cur, prev_ref[0], first, (1,))
        return cur + mu * (sh - cur)

    r = mix(r_ref, rp_ref, mu_ref[0:1], buf_ref)
    k = mix(k_ref, kp_ref, mu_ref[1:2], buf_ref)
    v = mix(v_ref, vp_ref, mu_ref[2:3], buf_ref)
    lo = mix(lo_ref, lop_ref, mul_ref[...], bufl_ref)
    lo0, lo1 = lo[:, :LANES], lo[:, LANES:]
    w0, a0, k_k, k_a, r_k, ln_w, ln_b = (vec_ref[j:j + 1] for j in range(7))

    w_log = -_softplus(-(w0 + jnp.dot(jnp.tanh(lo0), w2_ref[...], preferred_element_type=F32))) - 0.5
    lw = -jnp.exp(w_log)
    a = _sigmoid(a0 + jnp.dot(lo0, a2_ref[...], preferred_element_type=F32))
    gate = jnp.dot(_sigmoid(lo1), g2_ref[...], preferred_element_type=F32)
    if has_vres:
        v0 = vec_ref[7:8]
        v = v + (vf_ref[0] - v) * _sigmoid(v0 + jnp.dot(lo0, v2_ref[...], preferred_element_type=F32))
    else:
        vf_out_ref[0] = v

    ri = lax.broadcasted_iota(jnp.int32, (C, C), 0)
    ci = lax.broadcasted_iota(jnp.int32, (C, C), 1)
    tril = (ri >= ci).astype(BF16)
    lw_hi = lw.astype(BF16)
    lw_rest = lw - lw_hi.astype(F32)
    lw_mid = lw_rest.astype(BF16)
    lw_lo = (lw_rest - lw_mid.astype(F32)).astype(BF16)
    gi = jnp.concatenate(
        [sum(jnp.dot(tril, part[c * C:(c + 1) * C], preferred_element_type=F32) for part in (lw_hi, lw_mid, lw_lo))
         for c in range(NC)], axis=0)

    kk = _split_heads(k * k_k, H, D)
    kk = kk * lax.rsqrt(jnp.sum(kk * kk, axis=-1, keepdims=True) + L2_EPS)
    k = k * (1.0 + (a - 1.0) * k_a)
    r_h = _split_heads(r, H, D)
    k_h = _split_heads(k, H, D)
    v_h = _split_heads(v, H, D)
    a_h = _split_heads(a, H, D)
    gi_h = _split_heads(gi, H, D)
    lw_h = _split_heads(lw, H, D)
    bonus = jnp.sum(r_h * k_h * _split_heads(r_k, H, D), axis=-1, keepdims=True) * v_h

    N = H * NC
    rc, kc, vc, kkc, ac, gic, lwc = (t.reshape(N, C, D) for t in (r_h, k_h, v_h, kk, a_h, gi_h, lw_h))
    g_end = gic[:, C - 1:C, :]
    e_neg = jnp.exp(-gic)
    e_tail = jnp.exp(g_end - gic)
    at = -kkc * jnp.exp(gic - lwc)
    bvec = kkc * ac
    rt = rc * jnp.exp(gic)
    bt = bvec * e_neg
    kt = kc * e_neg
    b_dec = bvec * e_tail
    k_dec = kc * e_tail
    strict = ri > ci
    causal = ri >= ci
    amat = _bmm_nt(jnp.concatenate([at, rt], axis=1), jnp.concatenate([bt, kt], axis=1))
    a_ab = jnp.where(strict, amat[:, :C, :C], 0.0)
    a_ak = jnp.where(strict, amat[:, :C, C:], 0.0)
    a_r = jnp.concatenate([jnp.where(causal, amat[:, C:, :C], 0.0), jnp.where(causal, amat[:, C:, C:], 0.0)], axis=2)
    tinv = _unit_lower_inverse(-a_ab)
    x2 = _bmm(tinv, jnp.concatenate([at, _bmm(a_ak, vc)], axis=2))
    xv = jnp.concatenate([x2, jnp.concatenate([jnp.zeros_like(vc), vc], axis=2)], axis=1)
    y2 = _bmm(a_r, xv)
    pq = _bmm_tn(xv, jnp.concatenate([b_dec, k_dec], axis=1))
    rr_ref[...] = (rt + y2[:, :, :D]).reshape(H, NC, C, D)
    yv_ref[...] = y2[:, :, D:].reshape(H, NC, C, D)
    pp_ref[...] = pq[:, :D].reshape(H, NC, D, D)
    qq_ref[...] = pq[:, D:].reshape(H, NC, D, D)
    egc_ref[...] = jnp.exp(g_end).reshape(H, NC, 1, D)

    for c in range(NC):
        S = s_ref[...]
        yc_ref[:, c] = _bmm_nt(rr_ref[:, c], S) + yv_ref[:, c]
        s_ref[...] = S * egc_ref[:, c] + _bmm(S, pp_ref[:, c]) + qq_ref[:, c]

    y = yc_ref[...].reshape(H, TB, D)
    mean = jnp.mean(y, axis=-1, keepdims=True)
    yc = y - mean
    var = jnp.mean(yc * yc, axis=-1, keepdims=True)
    y = _merge_heads(yc * lax.rsqrt(var + RWKV_LN_EPS)) * ln_w + ln_b
    y_ref[0] = (y + _merge_heads(bonus)) * gate


def rwkv_mixer(proj3, mu, mu_lora, vecs, w2, a2, g2, v_first=None, v2=None, tb=256):
    B, S, _ = proj3.shape
    H, D, C = RWKV_HEADS, RWKV_HEAD, CHUNK
    W = RWKV_DIM
    WL = 2 * LANES
    NC = tb // C
    rows8 = tb // SUBLANES
    has_vres = v_first is not None

    def cur(col, width):
        return pl.BlockSpec((1, tb, width), lambda b, i, cb=col // width: (b, i, cb))

    def prev(col, width):
        return pl.BlockSpec((1, SUBLANES, width),
                            lambda b, i, cb=col // width: (b, jnp.maximum(i * rows8 - 1, 0), cb))

    in_specs = [cur(EV_R, W), cur(EV_RK, W), cur(EV_RV, W), cur(EV_SMALL, WL),
                prev(EV_R, W), prev(EV_RK, W), prev(EV_RV, W), prev(EV_SMALL, WL)]
    args = [proj3] * 8
    if has_vres:
        in_specs.append(pl.BlockSpec((1, tb, W), lambda b, i: (b, i, 0)))
        args.append(v_first)
    in_specs += [_resident(mu.shape), _resident(mu_lora.shape), _resident(vecs.shape),
                 _resident(w2.shape), _resident(a2.shape), _resident(g2.shape)]
    args += [mu, mu_lora, vecs, w2, a2, g2]
    if has_vres:
        in_specs.append(_resident(v2.shape))
        args.append(v2)
    out_block = pl.BlockSpec((1, tb, W), lambda b, i: (b, i, 0))
    out_sds = jax.ShapeDtypeStruct((B, S, W), F32)
    res = pl.pallas_call(
        functools.partial(_rwkv_kernel, has_vres=has_vres),
        grid=(B, S // tb),
        in_specs=in_specs,
        out_specs=out_block if has_vres else [out_block, out_block],
        out_shape=out_sds if has_vres else [out_sds, out_sds],
        scratch_shapes=[pltpu.VMEM((tb + SUBLANES, W), F32),
                        pltpu.VMEM((tb + SUBLANES, WL), F32),
                        pltpu.VMEM((H, D, D), F32),
                        pltpu.VMEM((H, NC, C, D), F32),
                        pltpu.VMEM((H, NC, D, D), F32),
                        pltpu.VMEM((H, NC, D, D), F32),
                        pltpu.VMEM((H, NC, C, D), F32),
                        pltpu.VMEM((H, NC, 1, D), F32),
                        pltpu.VMEM((H, NC, C, D), F32)],
        compiler_params=_cparams(("parallel", "arbitrary")),
        name="rwkv_mixer",
    )(*args)
    if has_vres:
        return res, v_first
    return res[0], res[1]


def _dil_attn_kernel(q_ref, kc_ref, kp_ref, vc_ref, vp_ref, o_ref, lse_ref, kbuf, vbuf):
    Dh, Bk = DIL_HEAD_DIM, ATT_BLOCK
    QB = q_ref.shape[2]
    has_prev = pl.program_id(2) > 0
    kbuf[0:Bk] = kp_ref[0, 0]
    kbuf[Bk:] = kc_ref[0, 0]
    vbuf[0:Bk] = vp_ref[0, 0]
    vbuf[Bk:] = vc_ref[0, 0]
    qi = lax.broadcasted_iota(jnp.int32, (Bk, 2 * Bk), 0)
    cj = lax.broadcasted_iota(jnp.int32, (Bk, 2 * Bk), 1)
    band = (cj >= qi) & (cj <= qi + Bk)
    lane = lax.broadcasted_iota(jnp.int32, (Bk, LANES), 1)
    low = lane < Dh
    low_kv = lax.broadcasted_iota(jnp.int32, (2 * Bk, LANES), 1) < Dh
    one_kv = jnp.ones((2 * Bk, LANES), BF16)
    NP = DIL_WIDTH // LANES

    def pairs(x):
        return jnp.stack([x[:, hp * LANES:(hp + 1) * LANES] for hp in range(NP)], axis=0)

    for j in range(QB // Bk):
        ok = band if j > 0 else band & ((cj >= Bk) | has_prev)
        ok2 = jnp.concatenate([ok, ok], axis=0)
        q_pairs = pairs(q_ref[0, 0, j * Bk:(j + 1) * Bk, :])
        zero = jnp.zeros_like(q_pairs)
        q2 = jnp.concatenate([jnp.where(low, q_pairs, zero), jnp.where(low, zero, q_pairs)], axis=1)
        k_pairs = pairs(kbuf[j * Bk:(j + 2) * Bk, :])
        v_pairs = pairs(vbuf[j * Bk:(j + 2) * Bk, :])
        s = jnp.einsum('hqd,hkd->hqk', q2, k_pairs, preferred_element_type=F32)
        s = jnp.where(ok2, s, -jnp.inf)
        m = jnp.max(s, axis=-1, keepdims=True).reshape(2 * NP, Bk, 1)
        p = jnp.exp2(s - m.reshape(NP, 2 * Bk, 1)).astype(BF16).reshape(2 * NP, Bk, 2 * Bk)
        v_ext = jnp.stack([jnp.where(low_kv, v_pairs, one_kv), jnp.where(low_kv, one_kv, v_pairs)], axis=1)
        po = jnp.einsum('hqk,hkd->hqd', p, v_ext.reshape(2 * NP, 2 * Bk, LANES), preferred_element_type=F32)
        swapped = pltpu.roll(po.reshape(2 * NP * Bk, LANES), Dh, axis=1).reshape(2 * NP, Bk, LANES)
        r = po / swapped
        o_ref[0, 0, j * Bk:(j + 1) * Bk, :] = jnp.concatenate(
            [jnp.where(low, r[2 * hp], r[2 * hp + 1]) for hp in range(NP)], axis=-1).astype(o_ref.dtype)
        lse_blk = jnp.zeros((Bk, LANES), F32)
        for h in range(2 * NP):
            lse_blk = jnp.where(lane == h, m[h] + jnp.log2((swapped if h % 2 == 0 else po)[h]), lse_blk)
        lse_ref[0, 0, j * Bk:(j + 1) * Bk, :] = lse_blk


def dil_attention_group(q, k, v):
    B, d, L, W = q.shape
    qb = min(4 * ATT_BLOCK, L)
    nj = qb // ATT_BLOCK
    cur = pl.BlockSpec((1, 1, qb, W), lambda b, r, n: (b, r, n, 0))
    prev = pl.BlockSpec((1, 1, ATT_BLOCK, W), lambda b, r, n: (b, r, jnp.maximum(n * nj - 1, 0), 0))
    return pl.pallas_call(
        _dil_attn_kernel,
        grid=(B, d, L // qb),
        in_specs=[cur, cur, prev, cur, prev],
        out_specs=[cur, pl.BlockSpec((1, 1, qb, LANES), lambda b, r, n: (b, r, n, 0))],
        out_shape=[jax.ShapeDtypeStruct((B, d, L, W), BF16), jax.ShapeDtypeStruct((B, d, L, LANES), F32)],
        scratch_shapes=[pltpu.VMEM((ATT_BLOCK + qb, W), BF16), pltpu.VMEM((ATT_BLOCK + qb, W), BF16)],
        compiler_params=_cparams(("parallel", "parallel", "arbitrary")),
        name=f"dil_attn_d{d}",
    )(q, k, k, v, v)


def _merged_attention(o_refs, l_refs, e_ref, a_ref, b_ref, la_ref, lb_ref):
    dils = [d for _, d in DIL_PATTERNS]
    lses = [_interleave_load(r, d, la_ref, lb_ref) for r, d in zip(l_refs, dils)]
    m = jnp.maximum(jnp.maximum(lses[0], lses[1]), lses[2])
    es = [jnp.exp2(l - m) for l in lses]
    inv = 1.0 / (es[0] + es[1] + es[2])
    expand = e_ref[...]
    mixed = None
    for e, o_ref, d in zip(es, o_refs, dils):
        term = _group_sum(e * inv, expand) * _interleave_load(o_ref, d, a_ref, b_ref)
        mixed = term if mixed is None else mixed + term
    return mixed


def _head_expand_matrix():
    expand = np.zeros((LANES, DIL_WIDTH), np.float32)
    for h in range(DIL_HEADS):
        expand[h, h * DIL_HEAD_DIM:(h + 1) * DIL_HEAD_DIM] = 1.0
    return jnp.asarray(expand, BF16)


MOE_TM = 512
MOE_RB = 144
MOE_NBLK = (MOE_TM + N_GROUPS * (MOE_RB - 1)) // MOE_RB
LOGIT_ROWS = 24


def _moe_route(h, wr_ref):
    NG, EPG, RB = N_GROUPS, EXPERTS_PER_GROUP, MOE_RB
    tm = h.shape[0]
    hb, h_lo = _hi_lo(h)
    nt = (((1,), (1,)), ((), ()))
    lg = lax.dot_general(wr_ref[...], hb, nt, preferred_element_type=F32)
    lg = lg[:LANES] + lg[LANES:] + lax.dot_general(wr_ref[:LANES], h_lo, nt, preferred_element_type=F32)
    L = lg[:LOGIT_ROWS]
    r = lax.broadcasted_iota(jnp.int32, L.shape, 0)
    big = jnp.int32(1 << 30)
    gl = jnp.where(r < NG, L, -jnp.inf)
    gmax = jnp.max(gl, axis=0, keepdims=True)
    gsel = jnp.min(jnp.where(gl == gmax, r, big), axis=0, keepdims=True)
    p_group = 1.0 / jnp.sum(jnp.where(r < NG, jnp.exp(L - gmax), 0.0), axis=0, keepdims=True)
    lo = NG + gsel * EPG
    el = jnp.where((r >= lo) & (r < lo + EPG), L, -jnp.inf)
    v1 = jnp.max(el, axis=0, keepdims=True)
    i1 = jnp.min(jnp.where(el == v1, r, big), axis=0, keepdims=True)
    el2 = jnp.where(r == i1, -jnp.inf, el)
    v2 = jnp.max(el2, axis=0, keepdims=True)
    i2 = jnp.min(jnp.where(el2 == v2, r, big), axis=0, keepdims=True)
    t = jnp.exp(v2 - v1)
    gates = jnp.where(r == i1, p_group / (1.0 + t), 0.0) + jnp.where(r == i2, p_group * t / (1.0 + t), 0.0)
    gate4 = [jnp.sum(jnp.where(r == lo + j, gates, 0.0), axis=0, keepdims=True) for j in range(EPG)]

    r8 = lax.broadcasted_iota(jnp.int32, (SUBLANES, tm), 0)
    onehot = (r8 == gsel).astype(F32)
    ti = lax.broadcasted_iota(jnp.int32, (tm, tm), 0)
    tj = lax.broadcasted_iota(jnp.int32, (tm, tm), 1)
    before = (ti < tj).astype(BF16)
    rank = jnp.dot(onehot.astype(BF16), before, preferred_element_type=F32)
    count = jnp.sum(onehot, axis=1, keepdims=True).astype(jnp.int32)
    nblk = sum((count > j * RB).astype(jnp.int32) for j in range(pl.cdiv(tm, RB)))
    padded = (nblk * RB).astype(F32)
    start, starts = jnp.zeros((1, 1), F32), []
    for g in range(NG):
        starts.append(start)
        start = start + padded[g:g + 1]
    start8 = jnp.concatenate(starts + [jnp.zeros((SUBLANES - NG, 1), F32)], axis=0)
    dest = jnp.sum(onehot * (start8 + rank), axis=0, keepdims=True)
    return dest, gate4, nblk


def _moe_kernel(x_ref, *refs, final_norm, mixer_proj):
    NG, EPG, RB, NBLK = N_GROUPS, EXPERTS_PER_GROUP, MOE_RB, MOE_NBLK
    x = x_ref[...]
    if mixer_proj == "even":
        ya_ref, yb_ref, wa_ref, wb_ref = refs[:4]
        refs = refs[4:]
        x = (x + jnp.dot(ya_ref[...].astype(BF16), wa_ref[...], preferred_element_type=F32)
             + jnp.dot(yb_ref[...].astype(BF16), wb_ref[...], preferred_element_type=F32))
    elif mixer_proj == "odd":
        mixed = _merged_attention(refs[0:3], refs[3:6], refs[6], *refs[-4:])
        x = x + jnp.dot(mixed.astype(BF16), refs[7][...], preferred_element_type=F32)
        refs = refs[8:-4]
    g_ref, wr_ref, w1_ref, w3_ref, w2_ref, fg_ref, o_ref, p_ref, hb_ref, g2_ref, hh_ref, ys_ref = refs
    FF = w1_ref.shape[3]
    tm = x_ref.shape[0]
    h = _rmsnorm(x, g_ref[...])
    dest, gate4, nblk = _moe_route(h, wr_ref)
    rows = lax.broadcasted_iota(jnp.int32, (NBLK * RB, tm), 0)
    p_ref[...] = jnp.where(rows == dest.astype(jnp.int32), 1.0, 0.0).astype(BF16)
    hb_ref[...] = h.astype(BF16)
    g4 = jnp.concatenate(gate4 + [jnp.zeros((SUBLANES - EPG, tm), F32)], axis=0)
    g_hi = g4.astype(BF16).astype(F32)
    g2_ref[...] = jnp.concatenate([g_hi, g4 - g_hi, jnp.zeros((LANES - 2 * SUBLANES, tm), F32)], axis=0).astype(BF16)

    ends, acc = [], jnp.int32(0)
    for g in range(NG):
        acc = acc + jnp.sum(nblk[g:g + 1])
        ends.append(acc)
    for b in range(NBLK):
        rs = slice(b * RB, (b + 1) * RB)
        grp = sum((b >= e).astype(jnp.int32) for e in ends[:-1])

        @pl.when(b < ends[-1])
        def _():
            pb = p_ref[rs]
            xr = jnp.dot(pb, hb_ref[...], preferred_element_type=F32).astype(BF16)
            gate = lax.dot_general(pb, g2_ref[...], (((1,), (1,)), ((), ())), preferred_element_type=F32)
            for j in range(EPG):
                a = jnp.dot(xr, w1_ref[0, grp * EPG + j], preferred_element_type=F32)
                u = jnp.dot(xr, w3_ref[0, grp * EPG + j], preferred_element_type=F32)
                gj = gate[:, j:j + 1] + gate[:, SUBLANES + j:SUBLANES + j + 1]
                hh_ref[:, j * FF:(j + 1) * FF] = ((a * _sigmoid(a)) * u * gj).astype(BF16)
            ys_ref[rs] = jnp.dot(hh_ref[...], w2_ref[0, grp], preferred_element_type=F32).astype(BF16)

        @pl.when(b >= ends[-1])
        def _():
            ys_ref[rs] = jnp.zeros((RB, ys_ref.shape[1]), BF16)

    y = x + lax.dot_general(p_ref[...], ys_ref[...], (((0,), (0,)), ((), ())), preferred_element_type=F32)
    if final_norm:
        y = _rmsnorm(y, fg_ref[...])
    o_ref[...] = y


def moe_block(x, gain, w_router, w1, w3, w2, layer, final_gain=None, mixer=None):
    T, D = x.shape
    _, NE, _, FF = w1.shape
    tm, RB, NBLK = MOE_TM, MOE_RB, MOE_NBLK
    final_norm = final_gain is not None
    fg = (final_gain if final_norm else jnp.ones((D,), F32)).reshape(1, D)

    def of_layer(w):
        return pl.BlockSpec((1,) + w.shape[1:], lambda i: (layer, 0, 0, 0), pipeline_mode=pl.Buffered(1))

    mixer_specs, mixer_args, mixer_scratch = [], [], []
    kind = None if mixer is None else mixer[0]
    if kind == "even":
        _, ya, yb, wa, wb = mixer
        mixer_specs = [pl.BlockSpec((tm, ya.shape[1]), lambda i: (i, 0)), pl.BlockSpec((tm, yb.shape[1]), lambda i: (i, 0)),
                       _resident(wa.shape), _resident(wb.shape)]
        mixer_args = [ya, yb, wa, wb]
    elif kind == "odd":
        _, outs, lses, w_out = mixer
        tiles_per_seq = outs[0].shape[1] * outs[0].shape[2] // tm
        for width, arrs in ((DIL_WIDTH, outs), (LANES, lses)):
            for (_, d), arr in zip(DIL_PATTERNS, arrs):
                mixer_specs.append(pl.BlockSpec((1, d, tm // d, width),
                                                lambda i: (i // tiles_per_seq, 0, i % tiles_per_seq, 0)))
                mixer_args.append(arr)
        mixer_specs += [_resident((LANES, DIL_WIDTH)), _resident(w_out.shape)]
        mixer_args += [_head_expand_matrix(), w_out]
        n_tiles = DIL_WIDTH // LANES
        mixer_scratch = [pltpu.VMEM((n_tiles, tm, LANES), F32), pltpu.VMEM((n_tiles, tm, LANES), F32),
                         pltpu.VMEM((1, tm, LANES), F32), pltpu.VMEM((1, tm, LANES), F32)]
    return pl.pallas_call(
        functools.partial(_moe_kernel, final_norm=final_norm, mixer_proj=kind),
        grid=(T // tm,),
        in_specs=[pl.BlockSpec((tm, D), lambda i: (i, 0)), *mixer_specs, _resident((1, D)), _resident(w_router.shape),
                  of_layer(w1), of_layer(w3), of_layer(w2), _resident((1, D))],
        out_specs=pl.BlockSpec((tm, D), lambda i: (i, 0)),
        out_shape=jax.ShapeDtypeStruct((T, D), F32),
        scratch_shapes=[pltpu.VMEM((NBLK * RB, tm), BF16), pltpu.VMEM((tm, D), BF16),
                        pltpu.VMEM((LANES, tm), BF16), pltpu.VMEM((RB, EXPERTS_PER_GROUP * FF), BF16),
                        pltpu.VMEM((NBLK * RB, D), BF16), *mixer_scratch],
        compiler_params=_cparams(("parallel",)),
        name="moe_block",
    )(x, *mixer_args, gain.reshape(1, D), w_router, w1, w3, w2, fg)


def _router_weights(w_group, w_expert):
    w = _place([(0, w_group)] + [(N_GROUPS + g * EXPERTS_PER_GROUP, w_expert[g]) for g in range(N_GROUPS)], LANES).T
    return jnp.concatenate(_hi_lo(w), axis=0)


def _place(cols, total):
    lead, dtype = cols[0][1].shape[:-1], cols[0][1].dtype
    parts, pos = [], 0
    for off, arr in sorted(cols, key=lambda c: c[0]):
        if off > pos:
            parts.append(jnp.zeros(lead + (off - pos,), dtype))
        parts.append(arr)
        pos = off + arr.shape[-1]
    if total > pos:
        parts.append(jnp.zeros(lead + (total - pos,), dtype))
    return jnp.concatenate(parts, axis=-1)


def _even_layout(t, vres=None):
    gw = GDN_HEADS * GDN_DK
    sizes = [gw, gw, gw, gw, GDN_HEADS, GDN_HEADS, RWKV_DIM, RWKV_DIM, RWKV_DIM, DECAY_LORA, AAA_LORA, GATE_LORA]
    gq, gk, gv, gz, ga, gb, rr, rk, rv, lw, la, lg = jnp.split(t, np.cumsum(sizes)[:-1].tolist(), axis=-1)
    cols = [(EV_Q, gq), (EV_K, gk), (EV_V, gv), (EV_Z, gz), (EV_R, rr), (EV_RK, rk), (EV_RV, rv),
            (EV_SMALL + SMALL_W, lw), (EV_SMALL + SMALL_A, la), (EV_SMALL + SMALL_G, lg),
            (EV_SMALL + SMALL_AB, ga), (EV_SMALL + SMALL_AB + GDN_HEADS, gb)]
    if vres is not None:
        cols.append((EV_SMALL + SMALL_V, vres))
    return _place(cols, EV_NPAD)


def _lora_up(w, lane):
    return jnp.concatenate([jnp.zeros((lane, w.shape[1]), F32), w,
                            jnp.zeros((LANES - lane - w.shape[0], w.shape[1]), F32)], axis=0)


def kernel(x, positions, ev_norm, ev_w_in, rwkv_vres_down, ev_w_out, gdn_conv_w, gdn_A_log, gdn_dt_bias, gdn_norm,
           rwkv_mu, rwkv_w0, rwkv_w2, rwkv_a0, rwkv_a2, rwkv_g2, rwkv_k_k, rwkv_k_a, rwkv_r_k, rwkv_ln_w, rwkv_ln_b,
           rwkv_vres_mu, rwkv_v0, rwkv_v2, od_norm, od_w_in, od_w_out, ffn_norm, moe_w_group, moe_w_expert,
           moe_w1, moe_w3, moe_w2, final_norm):
    B, S, D = x.shape
    T = B * S
    depth = ffn_norm.shape[0]
    xf = x.reshape(T, D)
    cos, sin = rope_tables(positions.reshape(T))
    n_groups = len(DIL_PATTERNS)
    moe_w1_bf = moe_w1.astype(BF16)
    moe_w3_bf = moe_w3.astype(BF16)
    moe_w2_bf = moe_w2.astype(BF16).reshape(depth, N_GROUPS, -1, D)
    v_first = None
    for layer in range(depth):
        i = layer // 2
        if layer % 2 == 0:
            vres_w = None if i == 0 else rwkv_vres_down[i - 1].astype(BF16)
            w_in = _even_layout(ev_w_in[i].astype(BF16), vres_w)
            proj3 = norm_proj(xf, ev_norm[i], w_in).reshape(B, S, EV_NPAD)
            mu_r, mu_k, mu_v, mu_w, mu_a, mu_g = jnp.split(
                rwkv_mu[i], np.cumsum([RWKV_DIM, RWKV_DIM, RWKV_DIM, DECAY_LORA, AAA_LORA]).tolist())
            mu = jnp.stack([mu_r, mu_k, mu_v])
            lora_cols = [(SMALL_W, mu_w), (SMALL_A, mu_a), (SMALL_G, mu_g)]
            if i > 0:
                lora_cols.append((SMALL_V, rwkv_vres_mu[i - 1]))
            mu_lora = _place(lora_cols, 2 * LANES).reshape(1, 2 * LANES)
            vec_rows = [rwkv_w0[i], rwkv_a0[i], rwkv_k_k[i], rwkv_k_a[i], rwkv_r_k[i].reshape(-1), rwkv_ln_w[i], rwkv_ln_b[i]]
            if i > 0:
                vec_rows.append(rwkv_v0[i - 1])
            vecs = jnp.stack(vec_rows)
            w2 = _lora_up(rwkv_w2[i], SMALL_W)
            a2 = _lora_up(rwkv_a2[i], SMALL_A)
            g2 = _lora_up(rwkv_g2[i], SMALL_G - LANES)
            ya = gdn_mixer(proj3, gdn_conv_w[i], gdn_A_log[i], gdn_dt_bias[i], gdn_norm[i])
            if i == 0:
                yb, v_first = rwkv_mixer(proj3, mu, mu_lora, vecs, w2, a2, g2)
            else:
                yb, _ = rwkv_mixer(proj3, mu, mu_lora, vecs, w2, a2, g2, v_first, _lora_up(rwkv_v2[i - 1], SMALL_V))
            w_out = ev_w_out[i].astype(BF16)
            na = GDN_HEADS * GDN_DV
            mixer = ("even", ya.reshape(T, na), yb.reshape(T, RWKV_DIM), w_out[:na], w_out[na:])
        else:
            w_in = od_w_in[i].astype(BF16)
            qkv = norm_proj_rope(xf.reshape(B, S, D), od_norm[i], w_in, cos.reshape(B, S, LANES),
                                 sin.reshape(B, S, LANES), DIL_HEAD_DIM ** -0.5 * math.log2(math.e))
            outs, lses = [], []
            for gi, (window, dilation) in enumerate(DIL_PATTERNS):
                assert window // dilation == ATT_BLOCK and (S // dilation) % (2 * ATT_BLOCK) == 0
                o, lse = dil_attention_group(qkv[gi], qkv[n_groups + gi], qkv[2 * n_groups + gi])
                outs.append(o)
                lses.append(lse)
            mixer = ("odd", outs, lses, od_w_out[i].astype(BF16))
        w_router = _router_weights(moe_w_group[layer], moe_w_expert[layer])
        xf = moe_block(xf, ffn_norm[layer], w_router, moe_w1_bf, moe_w3_bf, moe_w2_bf, layer,
                       final_norm if layer == depth - 1 else None, mixer)
    return xf.reshape(B, S, D)
```

```python
import functools
import math

import jax
import jax.numpy as jnp
import numpy as np
from jax import lax
from jax.experimental import pallas as pl
from jax.experimental.pallas import tpu as pltpu

F32 = jnp.float32
BF16 = jnp.bfloat16

RMS_EPS = 1e-6
L2_EPS = 1e-6
RWKV_LN_EPS = 64e-5
ROPE_THETA = 10000.0

LANES = 128
SUBLANES = 8
VMEM_LIMIT = 56 * 1024 * 1024

GDN_HEADS = 4
GDN_DK = 128
GDN_DV = 128
GDN_CONV = 4
RWKV_HEADS = 8
RWKV_HEAD = 64
RWKV_DIM = RWKV_HEADS * RWKV_HEAD
DECAY_LORA, AAA_LORA, MV_LORA, GATE_LORA = 32, 32, 32, 96
DIL_PATTERNS = ((128, 1), (512, 4), (2048, 16))
DIL_HEADS = 8
DIL_HEAD_DIM = 64
DIL_WIDTH = DIL_HEADS * DIL_HEAD_DIM
ATT_BLOCK = 128
N_GROUPS = 4
EXPERTS_PER_GROUP = 4
N_EXPERTS = N_GROUPS * EXPERTS_PER_GROUP
CHUNK = 64

EV_Q, EV_K, EV_V, EV_Z = 0, 512, 1024, 1536
EV_R, EV_RK, EV_RV, EV_SMALL = 2048, 2560, 3072, 3584
SMALL_W, SMALL_A, SMALL_V = 0, DECAY_LORA, DECAY_LORA + AAA_LORA
SMALL_AB, SMALL_G = LANES, LANES + 2 * GDN_HEADS
EV_NPAD = EV_SMALL + 2 * LANES


def _cparams(sem):
    return pltpu.CompilerParams(dimension_semantics=sem, vmem_limit_bytes=VMEM_LIMIT)


def _sigmoid(x):
    return 0.5 * jnp.tanh(0.5 * x) + 0.5


def _softplus(x):
    return jnp.maximum(x, 0.0) + jnp.log(1.0 + jnp.exp(-jnp.abs(x)))


def _rmsnorm(x, gain):
    return x * lax.rsqrt(jnp.mean(x * x, axis=-1, keepdims=True) + RMS_EPS) * gain


def _hi_lo(v):
    hi = v.astype(BF16)
    return hi, (v - hi.astype(F32)).astype(BF16)


def _group_sum(x, block_ones):
    hi, lo = _hi_lo(x)
    return (jnp.dot(hi, block_ones, preferred_element_type=F32)
            + jnp.dot(lo, block_ones, preferred_element_type=F32))


def _resident(shape):
    nd = len(shape)
    return pl.BlockSpec(shape, lambda *_: (0,) * nd, pipeline_mode=pl.Buffered(1))


def _rope_table_kernel(pos_ref, freq_ref, sign_ref, cos_ref, sin_ref):
    ang = pos_ref[...].astype(F32) * freq_ref[...]
    cos_ref[...] = jnp.cos(ang)
    sin_ref[...] = jnp.sin(ang) * sign_ref[...]


def rope_tables(positions_flat, tm=1024):
    T = positions_flat.shape[0]
    half = DIL_HEAD_DIM // 2
    inv_freq = ROPE_THETA ** (-jnp.arange(half, dtype=F32) * 2.0 / DIL_HEAD_DIM)
    lane = np.arange(LANES)
    freq = inv_freq[lane % half][None, :]
    sign = jnp.asarray(np.where(lane % DIL_HEAD_DIM < half, -1.0, 1.0), F32)[None, :]
    return pl.pallas_call(
        _rope_table_kernel,
        grid=(T // tm,),
        in_specs=[pl.BlockSpec((tm, 1), lambda i: (i, 0)), _resident((1, LANES)), _resident((1, LANES))],
        out_specs=[pl.BlockSpec((tm, LANES), lambda i: (i, 0))] * 2,
        out_shape=[jax.ShapeDtypeStruct((T, LANES), F32)] * 2,
        compiler_params=_cparams(("parallel",)),
        name="rope_tables",
    )(positions_flat.reshape(T, 1), freq, sign)


def _norm_proj_kernel(x_ref, g_ref, w_ref, o_ref):
    h = _rmsnorm(x_ref[...], g_ref[...]).astype(BF16)
    o_ref[...] = jnp.dot(h, w_ref[...], preferred_element_type=F32)


def norm_proj(x, gain, w, tm=512):
    T, D = x.shape
    N = w.shape[1]
    return pl.pallas_call(
        _norm_proj_kernel,
        grid=(T // tm,),
        in_specs=[pl.BlockSpec((tm, D), lambda i: (i, 0)), _resident((1, D)), _resident((D, N))],
        out_specs=pl.BlockSpec((tm, N), lambda i: (i, 0)),
        out_shape=jax.ShapeDtypeStruct((T, N), F32),
        compiler_params=_cparams(("parallel",)),
        name="norm_proj",
    )(x, gain.reshape(1, D), w)


def _deinterleave_store(out_ref, val, d, a_ref, b_ref):
    tm = val.shape[0]
    if d == 1:
        out_ref[0, 0] = val.astype(out_ref.dtype)
        return
    quarter = tm // 4
    for j in range(val.shape[1] // LANES):
        cs = slice(j * LANES, (j + 1) * LANES)
        a_ref[j] = val[:, cs]
        if d == 4:
            for r in range(4):
                out_ref[0, r, :, cs] = a_ref[j, pl.ds(r, quarter, stride=4), :].astype(out_ref.dtype)
            continue
        assert d == 16
        for r1 in range(4):
            b_ref[j, r1 * quarter:(r1 + 1) * quarter, :] = a_ref[j, pl.ds(r1, quarter, stride=4), :]
        for r1 in range(4):
            for r2 in range(4):
                out_ref[0, r1 + 4 * r2, :, cs] = (
                    b_ref[j, pl.ds(r1 * quarter + r2, quarter // 4, stride=4), :].astype(out_ref.dtype))


def _interleave_load(in_ref, d, a_ref, b_ref):
    if d == 1:
        return in_ref[0, 0].astype(F32)
    n_tiles, tm, _ = a_ref.shape
    quarter = tm // 4
    for j in range(n_tiles):
        cs = slice(j * LANES, (j + 1) * LANES)
        if d == 4:
            for r in range(4):
                a_ref[j, pl.ds(r, quarter, stride=4), :] = in_ref[0, r, :, cs].astype(F32)
            continue
        assert d == 16
        for r1 in range(4):
            for r2 in range(4):
                b_ref[j, pl.ds(r1 * quarter + r2, quarter // 4, stride=4), :] = in_ref[0, r1 + 4 * r2, :, cs].astype(F32)
        for r1 in range(4):
            a_ref[j, pl.ds(r1, quarter, stride=4), :] = b_ref[j, r1 * quarter:(r1 + 1) * quarter, :]
    return jnp.concatenate([a_ref[j] for j in range(n_tiles)], axis=-1)


def _norm_proj_rope_kernel(x_ref, g_ref, w_ref, cos_ref, sin_ref, *rest, q_scale):
    out_refs, (a_ref, b_ref) = rest[:-2], rest[-2:]
    n_groups = len(DIL_PATTERNS)
    h = _rmsnorm(x_ref[0], g_ref[...]).astype(BF16)
    cos = cos_ref[0]
    sin = sin_ref[0]
    lane = lax.broadcasted_iota(jnp.int32, cos.shape, 1)
    first_half = (lane % DIL_HEAD_DIM) < (DIL_HEAD_DIM // 2)
    W = DIL_WIDTH
    for c, out_ref in enumerate(out_refs):
        which, g = divmod(c, n_groups)
        acc = jnp.dot(h, w_ref[:, c * W:(c + 1) * W], preferred_element_type=F32)
        if which < 2:
            parts = []
            for j in range(W // LANES):
                blk = acc[:, j * LANES:(j + 1) * LANES]
                rot = jnp.where(first_half, pltpu.roll(blk, LANES - 32, axis=1), pltpu.roll(blk, 32, axis=1))
                out = blk * cos + rot * sin
                parts.append(out * q_scale if which == 0 else out)
            acc = jnp.concatenate(parts, axis=-1)
        _deinterleave_store(out_ref, acc, DIL_PATTERNS[g][1], a_ref, b_ref)


def norm_proj_rope(x3, gain, w, cos3, sin3, q_scale, tm=1024):
    B, S, D = x3.shape
    W = DIL_WIDTH
    out_specs, out_shape = [], []
    for _ in range(3):
        for _, d in DIL_PATTERNS:
            out_specs.append(pl.BlockSpec((1, d, tm // d, W), lambda b, i: (b, 0, i, 0)))
            out_shape.append(jax.ShapeDtypeStruct((B, d, S // d, W), BF16))
    return pl.pallas_call(
        functools.partial(_norm_proj_rope_kernel, q_scale=q_scale),
        grid=(B, S // tm),
        in_specs=[pl.BlockSpec((1, tm, D), lambda b, i: (b, i, 0)), _resident((1, D)), _resident(w.shape),
                  pl.BlockSpec((1, tm, LANES), lambda b, i: (b, i, 0)),
                  pl.BlockSpec((1, tm, LANES), lambda b, i: (b, i, 0))],
        out_specs=out_specs,
        out_shape=out_shape,
        scratch_shapes=[pltpu.VMEM((W // LANES, tm, LANES), F32), pltpu.VMEM((W // LANES, tm, LANES), F32)],
        compiler_params=_cparams(("parallel", "parallel")),
        name="norm_proj_rope",
    )(x3, gain.reshape(1, D), w, cos3, sin3)


def _bmm(a, b):
    return jnp.einsum('nik,nkj->nij', a.astype(BF16), b.astype(BF16), preferred_element_type=F32)


def _bmm_nt(a, b):
    return jnp.einsum('nik,njk->nij', a.astype(BF16), b.astype(BF16), preferred_element_type=F32)


def _bmm_tn(a, b):
    return jnp.einsum('nci,ncj->nij', a.astype(BF16), b.astype(BF16), preferred_element_type=F32)


def _unit_lower_inverse(L):
    C = L.shape[-1]
    ri = lax.broadcasted_iota(jnp.int32, (C, C), 0)
    ci = lax.broadcasted_iota(jnp.int32, (C, C), 1)
    same16 = (ri // 16) == (ci // 16)
    same32 = (ri // 32) == (ci // 32)
    eye = (ri == ci).astype(F32)
    Ld = jnp.where(same16, L, 0.0)
    X = eye - Ld
    P = Ld
    for _ in range(3):
        P = _bmm(P, P)
        X = X + _bmm(X, P)
    for off in (jnp.where(same32 & ~same16, L, 0.0), jnp.where(~same32, L, 0.0)):
        X = X - _bmm(_bmm(X, off), X)
    return X


def _shifted_rows(buf_ref, cur, prev, first, shifts):
    TB = cur.shape[0]
    buf_ref[0:SUBLANES, :] = jnp.where(first, 0.0, prev)
    buf_ref[SUBLANES:, :] = cur
    return [buf_ref[SUBLANES - s:SUBLANES - s + TB, :] for s in shifts]


def _split_heads(x, n_heads, width):
    return jnp.stack([x[:, h * width:(h + 1) * width] for h in range(n_heads)], axis=0)


def _merge_heads(x):
    return jnp.concatenate([x[h] for h in range(x.shape[0])], axis=-1)


def _gdn_kernel(q_ref, k_ref, v_ref, z_ref, ab_ref, qp_ref, kp_ref, vp_ref, cw_ref, alog_ref, dtb_ref, nw_ref,
                o_ref, buf_ref, s_ref, qd_ref, pg_ref, qg_ref, ov_ref, egl_ref, oc_ref):
    H, DK, DV, C = GDN_HEADS, GDN_DK, GDN_DV, CHUNK
    TB = q_ref.shape[1]
    NC = TB // C
    first = pl.program_id(1) == 0

    @pl.when(first)
    def _():
        s_ref[...] = jnp.zeros_like(s_ref)

    def conv_silu(cur_ref, prev_ref, j):
        cur = cur_ref[0]
        x3, x2, x1 = _shifted_rows(buf_ref, cur, prev_ref[0], first, (3, 2, 1))
        w = cw_ref[j]
        y = x3 * w[0:1] + x2 * w[1:2] + x1 * w[2:3] + cur * w[3:4]
        return y * _sigmoid(y)

    q = _split_heads(conv_silu(q_ref, qp_ref, 0), H, DK)
    k = _split_heads(conv_silu(k_ref, kp_ref, 1), H, DK)
    v = _split_heads(conv_silu(v_ref, vp_ref, 2), H, DV)
    q = q * lax.rsqrt(jnp.sum(q * q, axis=-1, keepdims=True) + L2_EPS) * (DK ** -0.5)
    k = k * lax.rsqrt(jnp.sum(k * k, axis=-1, keepdims=True) + L2_EPS)
    ab = ab_ref[0]
    a = jnp.stack([ab[:, h:h + 1] for h in range(H)], axis=0)
    b = jnp.stack([ab[:, H + h:H + h + 1] for h in range(H)], axis=0)
    beta = _sigmoid(b)
    g = -jnp.exp(alog_ref[...]) * _softplus(a + dtb_ref[...])

    N = H * NC
    q = q.reshape(N, C, DK)
    k = k.reshape(N, C, DK)
    v = v.reshape(N, C, DV)
    beta = beta.reshape(N, C, 1)
    g = g.reshape(N, C, 1)

    ri = lax.broadcasted_iota(jnp.int32, (C, C), 0)
    ci = lax.broadcasted_iota(jnp.int32, (C, C), 1)
    causal = ri >= ci
    strict = ri > ci
    gb = jnp.broadcast_to(g, (N, C, C))
    g_row = jnp.sum(jnp.where(ri == ci, gb, 0.0), axis=1, keepdims=True)
    gc_col = jnp.sum(jnp.where(causal, jnp.broadcast_to(g_row, (N, C, C)), 0.0), axis=2, keepdims=True)
    gc_row = jnp.sum(jnp.where(ri <= ci, gb, 0.0), axis=1, keepdims=True)
    decay = jnp.where(causal, jnp.exp(jnp.where(causal, gc_col - gc_row, 0.0)), 0.0)

    kb = k * beta
    vb = v * beta
    kq = _bmm_nt(jnp.concatenate([kb, q], axis=1), k)
    L = jnp.where(strict, kq[:, :C] * decay, 0.0)
    aqk = kq[:, C:] * decay
    tinv = _unit_lower_inverse(L)
    egc = jnp.exp(gc_col)
    uw = _bmm(tinv, jnp.concatenate([vb, kb * egc], axis=2))
    auw = _bmm(aqk, uw)
    g_last = gc_col[:, C - 1:C, :]
    k_dec = k * jnp.exp(g_last - gc_col)
    kuw = _bmm_tn(k_dec, uw)
    qd_ref[...] = (q * egc - auw[:, :, DV:]).reshape(H, NC, C, DK)
    ov_ref[...] = auw[:, :, :DV].reshape(H, NC, C, DV)
    pg_ref[...] = kuw[:, :, DV:].reshape(H, NC, DK, DK)
    qg_ref[...] = kuw[:, :, :DV].reshape(H, NC, DK, DV)
    egl_ref[...] = jnp.broadcast_to(jnp.exp(g_last), (N, 1, LANES)).reshape(H, NC, 1, LANES)

    for c in range(NC):
        S = s_ref[...]
        oc_ref[:, c] = _bmm(qd_ref[:, c], S) + ov_ref[:, c]
        s_ref[...] = S * egl_ref[:, c] - _bmm(pg_ref[:, c], S) + qg_ref[:, c]

    o = oc_ref[...].reshape(H, TB, DV)
    o = o * lax.rsqrt(jnp.mean(o * o, axis=-1, keepdims=True) + RMS_EPS) * nw_ref[...]
    z = _split_heads(z_ref[0], H, DV)
    o_ref[0] = _merge_heads(o * (z * _sigmoid(z)))


def gdn_mixer(proj3, conv_w, a_log, dt_bias, norm_w, tb=512):
    B, S, _ = proj3.shape
    H, DK, DV, C = GDN_HEADS, GDN_DK, GDN_DV, CHUNK
    W = H * DK
    NC = tb // C
    nq, nk, nv, nz, nab = EV_Q // W, EV_K // W, EV_V // W, EV_Z // W, (EV_SMALL + SMALL_AB) // LANES
    rows8 = tb // SUBLANES

    def cur(cb):
        return pl.BlockSpec((1, tb, W), lambda b, i, cb=cb: (b, i, cb))

    def prev(cb):
        return pl.BlockSpec((1, SUBLANES, W), lambda b, i, cb=cb: (b, jnp.maximum(i * rows8 - 1, 0), cb))

    cw = conv_w.reshape(GDN_CONV, 3, W).transpose(1, 0, 2)
    return pl.pallas_call(
        _gdn_kernel,
        grid=(B, S // tb),
        in_specs=[cur(nq), cur(nk), cur(nv), cur(nz),
                  pl.BlockSpec((1, tb, LANES), lambda b, i: (b, i, nab)),
                  prev(nq), prev(nk), prev(nv),
                  _resident((3, GDN_CONV, W)), _resident((H, 1, 1)), _resident((H, 1, 1)),
                  _resident((1, 1, DV))],
        out_specs=pl.BlockSpec((1, tb, H * DV), lambda b, i: (b, i, 0)),
        out_shape=jax.ShapeDtypeStruct((B, S, H * DV), F32),
        scratch_shapes=[pltpu.VMEM((tb + SUBLANES, W), F32),
                        pltpu.VMEM((H, DK, DV), F32),
                        pltpu.VMEM((H, NC, C, DK), F32),
                        pltpu.VMEM((H, NC, DK, DK), F32),
                        pltpu.VMEM((H, NC, DK, DV), F32),
                        pltpu.VMEM((H, NC, C, DV), F32),
                        pltpu.VMEM((H, NC, 1, LANES), F32),
                        pltpu.VMEM((H, NC, C, DV), F32)],
        compiler_params=_cparams(("parallel", "arbitrary")),
        name="gdn_mixer",
    )(proj3, proj3, proj3, proj3, proj3, proj3, proj3, proj3, cw,
      a_log.reshape(H, 1, 1), dt_bias.reshape(H, 1, 1), norm_w.reshape(1, 1, DV))


def _rwkv_kernel(*refs, has_vres):
    if has_vres:
        (r_ref, k_ref, v_ref, lo_ref, rp_ref, kp_ref, vp_ref, lop_ref, vf_ref,
         mu_ref, mul_ref, vec_ref, w2_ref, a2_ref, g2_ref, v2_ref,
         y_ref, buf_ref, bufl_ref, s_ref, rr_ref, pp_ref, qq_ref, yv_ref, egc_ref, yc_ref) = refs
    else:
        (r_ref, k_ref, v_ref, lo_ref, rp_ref, kp_ref, vp_ref, lop_ref,
         mu_ref, mul_ref, vec_ref, w2_ref, a2_ref, g2_ref,
         y_ref, vf_out_ref, buf_ref, bufl_ref, s_ref, rr_ref, pp_ref, qq_ref, yv_ref, egc_ref, yc_ref) = refs
    H, D, C = RWKV_HEADS, RWKV_HEAD, CHUNK
    TB = r_ref.shape[1]
    NC = TB // C
    first = pl.program_id(1) == 0

    @pl.when(first)
    def _():
        s_ref[...] = jnp.zeros_like(s_ref)

    def mix(cur_ref, prev_ref, mu, buf):
        cur = cur_ref[0]
        (sh,) = _shifted_rows(buf, cur, prev_ref[0], first, (1,))
        return cur + mu * (sh - cur)

    r = mix(r_ref, rp_ref, mu_ref[0:1], buf_ref)
    k = mix(k_ref, kp_ref, mu_ref[1:2], buf_ref)
    v = mix(v_ref, vp_ref, mu_ref[2:3], buf_ref)
    lo = mix(lo_ref, lop_ref, mul_ref[...], bufl_ref)
    lo0, lo1 = lo[:, :LANES], lo[:, LANES:]
    w0, a0, k_k, k_a, r_k, ln_w, ln_b = (vec_ref[j:j + 1] for j in range(7))

    w_log = -_softplus(-(w0 + jnp.dot(jnp.tanh(lo0), w2_ref[...], preferred_element_type=F32))) - 0.5
    lw = -jnp.exp(w_log)
    a = _sigmoid(a0 + jnp.dot(lo0, a2_ref[...], preferred_element_type=F32))
    gate = jnp.dot(_sigmoid(lo1), g2_ref[...], preferred_element_type=F32)
    if has_vres:
        v0 = vec_ref[7:8]
        v = v + (vf_ref[0] - v) * _sigmoid(v0 + jnp.dot(lo0, v2_ref[...], preferred_element_type=F32))
    else:
        vf_out_ref[0] = v

    ri = lax.broadcasted_iota(jnp.int32, (C, C), 0)
    ci = lax.broadcasted_iota(jnp.int32, (C, C), 1)
    tril = (ri >= ci).astype(BF16)
    lw_hi = lw.astype(BF16)
    lw_rest = lw - lw_hi.astype(F32)
    lw_mid = lw_rest.astype(BF16)
    lw_lo = (lw_rest - lw_mid.astype(F32)).astype(BF16)
    gi = jnp.concatenate(
        [sum(jnp.dot(tril, part[c * C:(c + 1) * C], preferred_element_type=F32) for part in (lw_hi, lw_mid, lw_lo))
         for c in range(NC)], axis=0)

    kk = _split_heads(k * k_k, H, D)
    kk = kk * lax.rsqrt(jnp.sum(kk * kk, axis=-1, keepdims=True) + L2_EPS)
    k = k * (1.0 + (a - 1.0) * k_a)
    r_h = _split_heads(r, H, D)
    k_h = _split_heads(k, H, D)
    v_h = _split_heads(v, H, D)
    a_h = _split_heads(a, H, D)
    gi_h = _split_heads(gi, H, D)
    lw_h = _split_heads(lw, H, D)
    bonus = jnp.sum(r_h * k_h * _split_heads(r_k, H, D), axis=-1, keepdims=True) * v_h

    N = H * NC
    rc, kc, vc, kkc, ac, gic, lwc = (t.reshape(N, C, D) for t in (r_h, k_h, v_h, kk, a_h, gi_h, lw_h))
    g_end = gic[:, C - 1:C, :]
    e_neg = jnp.exp(-gic)
    e_tail = jnp.exp(g_end - gic)
    at = -kkc * jnp.exp(gic - lwc)
    bvec = kkc * ac
    rt = rc * jnp.exp(gic)
    bt = bvec * e_neg
    kt = kc * e_neg
    b_dec = bvec * e_tail
    k_dec = kc * e_tail
    strict = ri > ci
    causal = ri >= ci
    amat = _bmm_nt(jnp.concatenate([at, rt], axis=1), jnp.concatenate([bt, kt], axis=1))
    a_ab = jnp.where(strict, amat[:, :C, :C], 0.0)
    a_ak = jnp.where(strict, amat[:, :C, C:], 0.0)
    a_r = jnp.concatenate([jnp.where(causal, amat[:, C:, :C], 0.0), jnp.where(causal, amat[:, C:, C:], 0.0)], axis=2)
    tinv = _unit_lower_inverse(-a_ab)
    x2 = _bmm(tinv, jnp.concatenate([at, _bmm(a_ak, vc)], axis=2))
    xv = jnp.concatenate([x2, jnp.concatenate([jnp.zeros_like(vc), vc], axis=2)], axis=1)
    y2 = _bmm(a_r, xv)
    pq = _bmm_tn(xv, jnp.concatenate([b_dec, k_dec], axis=1))
    rr_ref[...] = (rt + y2[:, :, :D]).reshape(H, NC, C, D)
    yv_ref[...] = y2[:, :, D:].reshape(H, NC, C, D)
    pp_ref[...] = pq[:, :D].reshape(H, NC, D, D)
    qq_ref[...] = pq[:, D:].reshape(H, NC, D, D)
    egc_ref[...] = jnp.exp(g_end).reshape(H, NC, 1, D)

    for c in range(NC):
        S = s_ref[...]
        yc_ref[:, c] = _bmm_nt(rr_ref[:, c], S) + yv_ref[:, c]
        s_ref[...] = S * egc_ref[:, c] + _bmm(S, pp_ref[:, c]) + qq_ref[:, c]

    y = yc_ref[...].reshape(H, TB, D)
    mean = jnp.mean(y, axis=-1, keepdims=True)
    yc = y - mean
    var = jnp.mean(yc * yc, axis=-1, keepdims=True)
    y = _merge_heads(yc * lax.rsqrt(var + RWKV_LN_EPS)) * ln_w + ln_b
    y_ref[0] = (y + _merge_heads(bonus)) * gate


def rwkv_mixer(proj3, mu, mu_lora, vecs, w2, a2, g2, v_first=None, v2=None, tb=256):
    B, S, _ = proj3.shape
    H, D, C = RWKV_HEADS, RWKV_HEAD, CHUNK
    W = RWKV_DIM
    WL = 2 * LANES
    NC = tb // C
    rows8 = tb // SUBLANES
    has_vres = v_first is not None

    def cur(col, width):
        return pl.BlockSpec((1, tb, width), lambda b, i, cb=col // width: (b, i, cb))

    def prev(col, width):
        return pl.BlockSpec((1, SUBLANES, width),
                            lambda b, i, cb=col // width: (b, jnp.maximum(i * rows8 - 1, 0), cb))

    in_specs = [cur(EV_R, W), cur(EV_RK, W), cur(EV_RV, W), cur(EV_SMALL, WL),
                prev(EV_R, W), prev(EV_RK, W), prev(EV_RV, W), prev(EV_SMALL, WL)]
    args = [proj3] * 8
    if has_vres:
        in_specs.append(pl.BlockSpec((1, tb, W), lambda b, i: (b, i, 0)))
        args.append(v_first)
    in_specs += [_resident(mu.shape), _resident(mu_lora.shape), _resident(vecs.shape),
                 _resident(w2.shape), _resident(a2.shape), _resident(g2.shape)]
    args += [mu, mu_lora, vecs, w2, a2, g2]
    if has_vres:
        in_specs.append(_resident(v2.shape))
        args.append(v2)
    out_block = pl.BlockSpec((1, tb, W), lambda b, i: (b, i, 0))
    out_sds = jax.ShapeDtypeStruct((B, S, W), F32)
    res = pl.pallas_call(
        functools.partial(_rwkv_kernel, has_vres=has_vres),
        grid=(B, S // tb),
        in_specs=in_specs,
        out_specs=out_block if has_vres else [out_block, out_block],
        out_shape=out_sds if has_vres else [out_sds, out_sds],
        scratch_shapes=[pltpu.VMEM((tb + SUBLANES, W), F32),
                        pltpu.VMEM((tb + SUBLANES, WL), F32),
                        pltpu.VMEM((H, D, D), F32),
                        pltpu.VMEM((H, NC, C, D), F32),
                        pltpu.VMEM((H, NC, D, D), F32),
                        pltpu.VMEM((H, NC, D, D), F32),
                        pltpu.VMEM((H, NC, C, D), F32),
                        pltpu.VMEM((H, NC, 1, D), F32),
                        pltpu.VMEM((H, NC, C, D), F32)],
        compiler_params=_cparams(("parallel", "arbitrary")),
        name="rwkv_mixer",
    )(*args)
    if has_vres:
        return res, v_first
    return res[0], res[1]


def _dil_attn_kernel(q_ref, kc_ref, kp_ref, vc_ref, vp_ref, o_ref, lse_ref, kbuf, vbuf):
    Dh, Bk = DIL_HEAD_DIM, ATT_BLOCK
    QB = q_ref.shape[2]
    has_prev = pl.program_id(2) > 0
    kbuf[0:Bk] = kp_ref[0, 0]
    kbuf[Bk:] = kc_ref[0, 0]
    vbuf[0:Bk] = vp_ref[0, 0]
    vbuf[Bk:] = vc_ref[0, 0]
    qi = lax.broadcasted_iota(jnp.int32, (Bk, 2 * Bk), 0)
    cj = lax.broadcasted_iota(jnp.int32, (Bk, 2 * Bk), 1)
    band = (cj >= qi) & (cj <= qi + Bk)
    lane = lax.broadcasted_iota(jnp.int32, (Bk, LANES), 1)
    low = lane < Dh
    low_kv = lax.broadcasted_iota(jnp.int32, (2 * Bk, LANES), 1) < Dh
    one_kv = jnp.ones((2 * Bk, LANES), BF16)
    NP = DIL_WIDTH // LANES

    def pairs(x):
        return jnp.stack([x[:, hp * LANES:(hp + 1) * LANES] for hp in range(NP)], axis=0)

    for j in range(QB // Bk):
        ok = band if j > 0 else band & ((cj >= Bk) | has_prev)
        ok2 = jnp.concatenate([ok, ok], axis=0)
        q_pairs = pairs(q_ref[0, 0, j * Bk:(j + 1) * Bk, :])
        zero = jnp.zeros_like(q_pairs)
        q2 = jnp.concatenate([jnp.where(low, q_pairs, zero), jnp.where(low, zero, q_pairs)], axis=1)
        k_pairs = pairs(kbuf[j * Bk:(j + 2) * Bk, :])
        v_pairs = pairs(vbuf[j * Bk:(j + 2) * Bk, :])
        s = jnp.einsum('hqd,hkd->hqk', q2, k_pairs, preferred_element_type=F32)
        s = jnp.where(ok2, s, -jnp.inf)
        m = jnp.max(s, axis=-1, keepdims=True).reshape(2 * NP, Bk, 1)
        p = jnp.exp2(s - m.reshape(NP, 2 * Bk, 1)).astype(BF16).reshape(2 * NP, Bk, 2 * Bk)
        v_ext = jnp.stack([jnp.where(low_kv, v_pairs, one_kv), jnp.where(low_kv, one_kv, v_pairs)], axis=1)
        po = jnp.einsum('hqk,hkd->hqd', p, v_ext.reshape(2 * NP, 2 * Bk, LANES), preferred_element_type=F32)
        swapped = pltpu.roll(po.reshape(2 * NP * Bk, LANES), Dh, axis=1).reshape(2 * NP, Bk, LANES)
        r = po / swapped
        o_ref[0, 0, j * Bk:(j + 1) * Bk, :] = jnp.concatenate(
            [jnp.where(low, r[2 * hp], r[2 * hp + 1]) for hp in range(NP)], axis=-1).astype(o_ref.dtype)
        lse_blk = jnp.zeros((Bk, LANES), F32)
        for h in range(2 * NP):
            lse_blk = jnp.where(lane == h, m[h] + jnp.log2((swapped if h % 2 == 0 else po)[h]), lse_blk)
        lse_ref[0, 0, j * Bk:(j + 1) * Bk, :] = lse_blk


def dil_attention_group(q, k, v):
    B, d, L, W = q.shape
    qb = min(4 * ATT_BLOCK, L)
    nj = qb // ATT_BLOCK
    cur = pl.BlockSpec((1, 1, qb, W), lambda b, r, n: (b, r, n, 0))
    prev = pl.BlockSpec((1, 1, ATT_BLOCK, W), lambda b, r, n: (b, r, jnp.maximum(n * nj - 1, 0), 0))
    return pl.pallas_call(
        _dil_attn_kernel,
        grid=(B, d, L // qb),
        in_specs=[cur, cur, prev, cur, prev],
        out_specs=[cur, pl.BlockSpec((1, 1, qb, LANES), lambda b, r, n: (b, r, n, 0))],
        out_shape=[jax.ShapeDtypeStruct((B, d, L, W), BF16), jax.ShapeDtypeStruct((B, d, L, LANES), F32)],
        scratch_shapes=[pltpu.VMEM((ATT_BLOCK + qb, W), BF16), pltpu.VMEM((ATT_BLOCK + qb, W), BF16)],
        compiler_params=_cparams(("parallel", "parallel", "arbitrary")),
        name=f"dil_attn_d{d}",
    )(q, k, k, v, v)


def _merged_attention(o_refs, l_refs, e_ref, a_ref, b_ref, la_ref, lb_ref):
    dils = [d for _, d in DIL_PATTERNS]
    lses = [_interleave_load(r, d, la_ref, lb_ref) for r, d in zip(l_refs, dils)]
    m = jnp.maximum(jnp.maximum(lses[0], lses[1]), lses[2])
    es = [jnp.exp2(l - m) for l in lses]
    inv = 1.0 / (es[0] + es[1] + es[2])
    expand = e_ref[...]
    mixed = None
    for e, o_ref, d in zip(es, o_refs, dils):
        term = _group_sum(e * inv, expand) * _interleave_load(o_ref, d, a_ref, b_ref)
        mixed = term if mixed is None else mixed + term
    return mixed


def _head_expand_matrix():
    expand = np.zeros((LANES, DIL_WIDTH), np.float32)
    for h in range(DIL_HEADS):
        expand[h, h * DIL_HEAD_DIM:(h + 1) * DIL_HEAD_DIM] = 1.0
    return jnp.asarray(expand, BF16)


MOE_TM = 512
MOE_RB = 144
MOE_NBLK = (MOE_TM + N_GROUPS * (MOE_RB - 1)) // MOE_RB
LOGIT_ROWS = 24


def _moe_route(h, wr_ref, before_ref):
    NG, EPG, RB = N_GROUPS, EXPERTS_PER_GROUP, MOE_RB
    tm = h.shape[0]
    hb, h_lo = _hi_lo(h)
    nt = (((1,), (1,)), ((), ()))
    lg = lax.dot_general(wr_ref[...], hb, nt, preferred_element_type=F32)
    lg = lg[:LANES] + lg[LANES:] + lax.dot_general(wr_ref[:LANES], h_lo, nt, preferred_element_type=F32)
    L = lg[:LOGIT_ROWS]
    r = lax.broadcasted_iota(jnp.int32, L.shape, 0)
    big = jnp.int32(1 << 30)
    gl = jnp.where(r < NG, L, -jnp.inf)
    gmax = jnp.max(gl, axis=0, keepdims=True)
    gsel = jnp.min(jnp.where(gl == gmax, r, big), axis=0, keepdims=True)
    p_group = 1.0 / jnp.sum(jnp.where(r < NG, jnp.exp(L - gmax), 0.0), axis=0, keepdims=True)
    lo = NG + gsel * EPG
    el = jnp.where((r >= lo) & (r < lo + EPG), L, -jnp.inf)
    v1 = jnp.max(el, axis=0, keepdims=True)
    i1 = jnp.min(jnp.where(el == v1, r, big), axis=0, keepdims=True)
    el2 = jnp.where(r == i1, -jnp.inf, el)
    v2 = jnp.max(el2, axis=0, keepdims=True)
    i2 = jnp.min(jnp.where(el2 == v2, r, big), axis=0, keepdims=True)
    t = jnp.exp(v2 - v1)
    gates = jnp.where(r == i1, p_group / (1.0 + t), 0.0) + jnp.where(r == i2, p_group * t / (1.0 + t), 0.0)
    gate4 = [jnp.sum(jnp.where(r == lo + j, gates, 0.0), axis=0, keepdims=True) for j in range(EPG)]

    r8 = lax.broadcasted_iota(jnp.int32, (SUBLANES, tm), 0)
    onehot = (r8 == gsel).astype(F32)
    rank = jnp.dot(onehot.astype(BF16), before_ref[...], preferred_element_type=F32)
    count = jnp.sum(onehot, axis=1, keepdims=True).astype(jnp.int32)
    nblk = sum((count > j * RB).astype(jnp.int32) for j in range(pl.cdiv(tm, RB)))
    padded = (nblk * RB).astype(F32)
    start, starts = jnp.zeros((1, 1), F32), []
    for g in range(NG):
        starts.append(start)
        start = start + padded[g:g + 1]
    start8 = jnp.concatenate(starts + [jnp.zeros((SUBLANES - NG, 1), F32)], axis=0)
    dest = jnp.sum(onehot * (start8 + rank), axis=0, keepdims=True)
    return dest, gate4, nblk


def _moe_kernel(x_ref, *refs, final_norm, mixer_proj):
    NG, EPG, RB, NBLK = N_GROUPS, EXPERTS_PER_GROUP, MOE_RB, MOE_NBLK
    x = x_ref[...]
    if mixer_proj == "even":
        ya_ref, yb_ref, wa_ref, wb_ref = refs[:4]
        refs = refs[4:]
        x = (x + jnp.dot(ya_ref[...].astype(BF16), wa_ref[...], preferred_element_type=F32)
             + jnp.dot(yb_ref[...].astype(BF16), wb_ref[...], preferred_element_type=F32))
    elif mixer_proj == "odd":
        mixed = _merged_attention(refs[0:3], refs[3:6], refs[6], *refs[-4:])
        x = x + jnp.dot(mixed.astype(BF16), refs[7][...], preferred_element_type=F32)
        refs = refs[8:-4]
    g_ref, wr_ref, before_ref, w1_ref, w3_ref, w2_ref, fg_ref, o_ref, p_ref, hb_ref, g2_ref, hh_ref, ys_ref = refs
    FF = w1_ref.shape[3]
    tm = x_ref.shape[0]
    h = _rmsnorm(x, g_ref[...])
    dest, gate4, nblk = _moe_route(h, wr_ref, before_ref)
    rows = lax.broadcasted_iota(jnp.int32, (NBLK * RB, tm), 0)
    p_ref[...] = jnp.where(rows == dest.astype(jnp.int32), 1.0, 0.0).astype(BF16)
    hb_ref[...] = h.astype(BF16)
    g4 = jnp.concatenate(gate4 + [jnp.zeros((SUBLANES - EPG, tm), F32)], axis=0)
    g_hi = g4.astype(BF16).astype(F32)
    g2_ref[...] = jnp.concatenate([g_hi, g4 - g_hi, jnp.zeros((LANES - 2 * SUBLANES, tm), F32)], axis=0).astype(BF16)

    ends, acc = [], jnp.int32(0)
    for g in range(NG):
        acc = acc + jnp.sum(nblk[g:g + 1])
        ends.append(acc)
    for b in range(NBLK):
        rs = slice(b * RB, (b + 1) * RB)
        grp = sum((b >= e).astype(jnp.int32) for e in ends[:-1])

        @pl.when(b < ends[-1])
        def _():
            pb = p_ref[rs]
            xr = jnp.dot(pb, hb_ref[...], preferred_element_type=F32).astype(BF16)
            gate = lax.dot_general(pb, g2_ref[...], (((1,), (1,)), ((), ())), preferred_element_type=F32)
            for j in range(EPG):
                a = jnp.dot(xr, w1_ref[0, grp * EPG + j], preferred_element_type=F32)
                u = jnp.dot(xr, w3_ref[0, grp * EPG + j], preferred_element_type=F32)
                gj = gate[:, j:j + 1] + gate[:, SUBLANES + j:SUBLANES + j + 1]
                hh_ref[:, j * FF:(j + 1) * FF] = ((a * _sigmoid(a)) * u * gj).astype(BF16)
            ys_ref[rs] = jnp.dot(hh_ref[...], w2_ref[0, grp], preferred_element_type=F32).astype(BF16)

        @pl.when(b >= ends[-1])
        def _():
            ys_ref[rs] = jnp.zeros((RB, ys_ref.shape[1]), BF16)

    def scatter_back(n_blocks):
        rows_used = slice(0, n_blocks * RB)
        y = x + lax.dot_general(p_ref[rows_used], ys_ref[rows_used], (((0,), (0,)), ((), ())),
                                preferred_element_type=F32)
        o_ref[...] = _rmsnorm(y, fg_ref[...]) if final_norm else y

    usual = NG + 1
    pl.when(ends[-1] <= usual)(lambda: scatter_back(usual))
    pl.when(ends[-1] > usual)(lambda: scatter_back(NBLK))


def moe_block(x, gain, w_router, w1, w3, w2, layer, final_gain=None, mixer=None):
    T, D = x.shape
    _, NE, _, FF = w1.shape
    tm, RB, NBLK = MOE_TM, MOE_RB, MOE_NBLK
    final_norm = final_gain is not None
    fg = (final_gain if final_norm else jnp.ones((D,), F32)).reshape(1, D)

    def of_layer(w):
        return pl.BlockSpec((1,) + w.shape[1:], lambda i: (layer, 0, 0, 0), pipeline_mode=pl.Buffered(1))

    mixer_specs, mixer_args, mixer_scratch = [], [], []
    kind = None if mixer is None else mixer[0]
    if kind == "even":
        _, ya, yb, wa, wb = mixer
        mixer_specs = [pl.BlockSpec((tm, ya.shape[1]), lambda i: (i, 0)), pl.BlockSpec((tm, yb.shape[1]), lambda i: (i, 0)),
                       _resident(wa.shape), _resident(wb.shape)]
        mixer_args = [ya, yb, wa, wb]
    elif kind == "odd":
        _, outs, lses, w_out = mixer
        tiles_per_seq = outs[0].shape[1] * outs[0].shape[2] // tm
        for width, arrs in ((DIL_WIDTH, outs), (LANES, lses)):
            for (_, d), arr in zip(DIL_PATTERNS, arrs):
                mixer_specs.append(pl.BlockSpec((1, d, tm // d, width),
                                                lambda i: (i // tiles_per_seq, 0, i % tiles_per_seq, 0)))
                mixer_args.append(arr)
        mixer_specs += [_resident((LANES, DIL_WIDTH)), _resident(w_out.shape)]
        mixer_args += [_head_expand_matrix(), w_out]
        n_tiles = DIL_WIDTH // LANES
        mixer_scratch = [pltpu.VMEM((n_tiles, tm, LANES), F32), pltpu.VMEM((n_tiles, tm, LANES), F32),
                         pltpu.VMEM((1, tm, LANES), F32), pltpu.VMEM((1, tm, LANES), F32)]
    return pl.pallas_call(
        functools.partial(_moe_kernel, final_norm=final_norm, mixer_proj=kind),
        grid=(T // tm,),
        in_specs=[pl.BlockSpec((tm, D), lambda i: (i, 0)), *mixer_specs, _resident((1, D)), _resident(w_router.shape),
                  _resident((tm, tm)), of_layer(w1), of_layer(w3), of_layer(w2), _resident((1, D))],
        out_specs=pl.BlockSpec((tm, D), lambda i: (i, 0)),
        out_shape=jax.ShapeDtypeStruct((T, D), F32),
        scratch_shapes=[pltpu.VMEM((NBLK * RB, tm), BF16), pltpu.VMEM((tm, D), BF16),
                        pltpu.VMEM((LANES, tm), BF16), pltpu.VMEM((RB, EXPERTS_PER_GROUP * FF), BF16),
                        pltpu.VMEM((NBLK * RB, D), BF16), *mixer_scratch],
        compiler_params=_cparams(("parallel",)),
        name="moe_block",
    )(x, *mixer_args, gain.reshape(1, D), w_router, jnp.asarray(np.triu(np.ones((tm, tm), np.float32), 1), BF16),
      w1, w3, w2, fg)


def _router_weights(w_group, w_expert):
    w = _place([(0, w_group)] + [(N_GROUPS + g * EXPERTS_PER_GROUP, w_expert[g]) for g in range(N_GROUPS)], LANES).T
    return jnp.concatenate(_hi_lo(w), axis=0)


def _place(cols, total):
    lead, dtype = cols[0][1].shape[:-1], cols[0][1].dtype
    parts, pos = [], 0
    for off, arr in sorted(cols, key=lambda c: c[0]):
        if off > pos:
            parts.append(jnp.zeros(lead + (off - pos,), dtype))
        parts.append(arr)
        pos = off + arr.shape[-1]
    if total > pos:
        parts.append(jnp.zeros(lead + (total - pos,), dtype))
    return jnp.concatenate(parts, axis=-1)


def _even_layout(t, vres=None):
    gw = GDN_HEADS * GDN_DK
    sizes = [gw, gw, gw, gw, GDN_HEADS, GDN_HEADS, RWKV_DIM, RWKV_DIM, RWKV_DIM, DECAY_LORA, AAA_LORA, GATE_LORA]
    gq, gk, gv, gz, ga, gb, rr, rk, rv, lw, la, lg = jnp.split(t, np.cumsum(sizes)[:-1].tolist(), axis=-1)
    cols = [(EV_Q, gq), (EV_K, gk), (EV_V, gv), (EV_Z, gz), (EV_R, rr), (EV_RK, rk), (EV_RV, rv),
            (EV_SMALL + SMALL_W, lw), (EV_SMALL + SMALL_A, la), (EV_SMALL + SMALL_G, lg),
            (EV_SMALL + SMALL_AB, ga), (EV_SMALL + SMALL_AB + GDN_HEADS, gb)]
    if vres is not None:
        cols.append((EV_SMALL + SMALL_V, vres))
    return _place(cols, EV_NPAD)


def _lora_up(w, lane):
    return jnp.concatenate([jnp.zeros((lane, w.shape[1]), F32), w,
                            jnp.zeros((LANES - lane - w.shape[0], w.shape[1]), F32)], axis=0)


def kernel(x, positions, ev_norm, ev_w_in, rwkv_vres_down, ev_w_out, gdn_conv_w, gdn_A_log, gdn_dt_bias, gdn_norm,
           rwkv_mu, rwkv_w0, rwkv_w2, rwkv_a0, rwkv_a2, rwkv_g2, rwkv_k_k, rwkv_k_a, rwkv_r_k, rwkv_ln_w, rwkv_ln_b,
           rwkv_vres_mu, rwkv_v0, rwkv_v2, od_norm, od_w_in, od_w_out, ffn_norm, moe_w_group, moe_w_expert,
           moe_w1, moe_w3, moe_w2, final_norm):
    B, S, D = x.shape
    T = B * S
    depth = ffn_norm.shape[0]
    xf = x.reshape(T, D)
    cos, sin = rope_tables(positions.reshape(T))
    n_groups = len(DIL_PATTERNS)
    moe_w1_bf = moe_w1.astype(BF16)
    moe_w3_bf = moe_w3.astype(BF16)
    moe_w2_bf = moe_w2.astype(BF16).reshape(depth, N_GROUPS, -1, D)
    v_first = None
    for layer in range(depth):
        i = layer // 2
        if layer % 2 == 0:
            vres_w = None if i == 0 else rwkv_vres_down[i - 1].astype(BF16)
            w_in = _even_layout(ev_w_in[i].astype(BF16), vres_w)
            proj3 = norm_proj(xf, ev_norm[i], w_in).reshape(B, S, EV_NPAD)
            mu_r, mu_k, mu_v, mu_w, mu_a, mu_g = jnp.split(
                rwkv_mu[i], np.cumsum([RWKV_DIM, RWKV_DIM, RWKV_DIM, DECAY_LORA, AAA_LORA]).tolist())
            mu = jnp.stack([mu_r, mu_k, mu_v])
            lora_cols = [(SMALL_W, mu_w), (SMALL_A, mu_a), (SMALL_G, mu_g)]
            if i > 0:
                lora_cols.append((SMALL_V, rwkv_vres_mu[i - 1]))
            mu_lora = _place(lora_cols, 2 * LANES).reshape(1, 2 * LANES)
            vec_rows = [rwkv_w0[i], rwkv_a0[i], rwkv_k_k[i], rwkv_k_a[i], rwkv_r_k[i].reshape(-1), rwkv_ln_w[i], rwkv_ln_b[i]]
            if i > 0:
                vec_rows.append(rwkv_v0[i - 1])
            vecs = jnp.stack(vec_rows)
            w2 = _lora_up(rwkv_w2[i], SMALL_W)
            a2 = _lora_up(rwkv_a2[i], SMALL_A)
            g2 = _lora_up(rwkv_g2[i], SMALL_G - LANES)
            ya = gdn_mixer(proj3, gdn_conv_w[i], gdn_A_log[i], gdn_dt_bias[i], gdn_norm[i])
            if i == 0:
                yb, v_first = rwkv_mixer(proj3, mu, mu_lora, vecs, w2, a2, g2)
            else:
                yb, _ = rwkv_mixer(proj3, mu, mu_lora, vecs, w2, a2, g2, v_first, _lora_up(rwkv_v2[i - 1], SMALL_V))
            w_out = ev_w_out[i].astype(BF16)
            na = GDN_HEADS * GDN_DV
            mixer = ("even", ya.reshape(T, na), yb.reshape(T, RWKV_DIM), w_out[:na], w_out[na:])
        else:
            w_in = od_w_in[i].astype(BF16)
            qkv = norm_proj_rope(xf.reshape(B, S, D), od_norm[i], w_in, cos.reshape(B, S, LANES),
                                 sin.reshape(B, S, LANES), DIL_HEAD_DIM ** -0.5 * math.log2(math.e))
            outs, lses = [], []
            for gi, (window, dilation) in enumerate(DIL_PATTERNS):
                assert window // dilation == ATT_BLOCK and (S // dilation) % (2 * ATT_BLOCK) == 0
                o, lse = dil_attention_group(qkv[gi], qkv[n_groups + gi], qkv[2 * n_groups + gi])
                outs.append(o)
                lses.append(lse)
            mixer = ("odd", outs, lses, od_w_out[i].astype(BF16))
        w_router = _router_weights(moe_w_group[layer], moe_w_expert[layer])
        xf = moe_block(xf, ffn_norm[layer], w_router, moe_w1_bf, moe_w3_bf, moe_w2_bf, layer,
                       final_norm if layer == depth - 1 else None, mixer)
    return xf.reshape(B, S, D)
```

```python
import functools
import math

import jax
import jax.numpy as jnp
import numpy as np
from jax import lax
from jax.experimental import pallas as pl
from jax.experimental.pallas import tpu as pltpu

F32 = jnp.float32
BF16 = jnp.bfloat16

RMS_EPS = 1e-6
L2_EPS = 1e-6
RWKV_LN_EPS = 64e-5
ROPE_THETA = 10000.0

LANES = 128
SUBLANES = 8
VMEM_LIMIT = 56 * 1024 * 1024

GDN_HEADS = 4
GDN_DK = 128
GDN_DV = 128
GDN_CONV = 4
RWKV_HEADS = 8
RWKV_HEAD = 64
RWKV_DIM = RWKV_HEADS * RWKV_HEAD
DECAY_LORA, AAA_LORA, MV_LORA, GATE_LORA = 32, 32, 32, 96
DIL_PATTERNS = ((128, 1), (512, 4), (2048, 16))
DIL_HEADS = 8
DIL_HEAD_DIM = 64
DIL_WIDTH = DIL_HEADS * DIL_HEAD_DIM
ATT_BLOCK = 128
N_GROUPS = 4
EXPERTS_PER_GROUP = 4
CHUNK = 64
INV_BLOCK = 16

EV_Q, EV_K, EV_V, EV_Z = 0, 512, 1024, 1536
EV_R, EV_RK, EV_RV, EV_SMALL = 2048, 2560, 3072, 3584
SMALL_W, SMALL_A, SMALL_V = 0, DECAY_LORA, DECAY_LORA + AAA_LORA
SMALL_AB, SMALL_G = LANES, LANES + 2 * GDN_HEADS
EV_NPAD = EV_SMALL + 2 * LANES


def _cparams(sem):
    return pltpu.CompilerParams(dimension_semantics=sem, vmem_limit_bytes=VMEM_LIMIT)


def _sigmoid(x):
    return 0.5 * jnp.tanh(0.5 * x) + 0.5


def _softplus(x):
    return jnp.maximum(x, 0.0) + jnp.log(1.0 + jnp.exp(-jnp.abs(x)))


def _rmsnorm(x, gain):
    return x * lax.rsqrt(jnp.mean(x * x, axis=-1, keepdims=True) + RMS_EPS) * gain


def _hi_lo(v):
    hi = v.astype(BF16)
    return hi, (v - hi.astype(F32)).astype(BF16)


def _group_sum(x, block_ones):
    hi, lo = _hi_lo(x)
    return (jnp.dot(hi, block_ones, preferred_element_type=F32)
            + jnp.dot(lo, block_ones, preferred_element_type=F32))


def _resident(shape):
    nd = len(shape)
    return pl.BlockSpec(shape, lambda *_: (0,) * nd, pipeline_mode=pl.Buffered(1))


def _rope_table_kernel(pos_ref, freq_ref, sign_ref, cos_ref, sin_ref):
    ang = pos_ref[...].astype(F32) * freq_ref[...]
    cos_ref[...] = jnp.cos(ang)
    sin_ref[...] = jnp.sin(ang) * sign_ref[...]


def rope_tables(positions_flat, tm=1024):
    T = positions_flat.shape[0]
    half = DIL_HEAD_DIM // 2
    inv_freq = ROPE_THETA ** (-jnp.arange(half, dtype=F32) * 2.0 / DIL_HEAD_DIM)
    lane = np.arange(LANES)
    freq = inv_freq[lane % half][None, :]
    sign = jnp.asarray(np.where(lane % DIL_HEAD_DIM < half, -1.0, 1.0), F32)[None, :]
    return pl.pallas_call(
        _rope_table_kernel,
        grid=(T // tm,),
        in_specs=[pl.BlockSpec((tm, 1), lambda i: (i, 0)), _resident((1, LANES)), _resident((1, LANES))],
        out_specs=[pl.BlockSpec((tm, LANES), lambda i: (i, 0))] * 2,
        out_shape=[jax.ShapeDtypeStruct((T, LANES), F32)] * 2,
        compiler_params=_cparams(("parallel",)),
        name="rope_tables",
    )(positions_flat.reshape(T, 1), freq, sign)


def _norm_proj_kernel(x_ref, g_ref, w_ref, o_ref):
    h = _rmsnorm(x_ref[...], g_ref[...]).astype(BF16)
    o_ref[...] = jnp.dot(h, w_ref[...], preferred_element_type=F32)


def norm_proj(x, gain, w, tm=512):
    T, D = x.shape
    N = w.shape[1]
    return pl.pallas_call(
        _norm_proj_kernel,
        grid=(T // tm,),
        in_specs=[pl.BlockSpec((tm, D), lambda i: (i, 0)), _resident((1, D)), _resident((D, N))],
        out_specs=pl.BlockSpec((tm, N), lambda i: (i, 0)),
        out_shape=jax.ShapeDtypeStruct((T, N), F32),
        compiler_params=_cparams(("parallel",)),
        name="norm_proj",
    )(x, gain.reshape(1, D), w)


def _deinterleave_store(out_ref, val, d, a_ref, b_ref):
    tm = val.shape[0]
    if d == 1:
        out_ref[0, 0] = val.astype(out_ref.dtype)
        return
    quarter = tm // 4
    for j in range(val.shape[1] // LANES):
        cs = slice(j * LANES, (j + 1) * LANES)
        a_ref[j] = val[:, cs]
        if d == 4:
            for r in range(4):
                out_ref[0, r, :, cs] = a_ref[j, pl.ds(r, quarter, stride=4), :].astype(out_ref.dtype)
            continue
        assert d == 16
        for r1 in range(4):
            b_ref[j, r1 * quarter:(r1 + 1) * quarter, :] = a_ref[j, pl.ds(r1, quarter, stride=4), :]
        for r1 in range(4):
            for r2 in range(4):
                out_ref[0, r1 + 4 * r2, :, cs] = (
                    b_ref[j, pl.ds(r1 * quarter + r2, quarter // 4, stride=4), :].astype(out_ref.dtype))


def _interleave_load(in_ref, d, a_ref, b_ref):
    if d == 1:
        return in_ref[0, 0].astype(F32)
    n_tiles, tm, _ = a_ref.shape
    quarter = tm // 4
    for j in range(n_tiles):
        cs = slice(j * LANES, (j + 1) * LANES)
        if d == 4:
            for r in range(4):
                a_ref[j, pl.ds(r, quarter, stride=4), :] = in_ref[0, r, :, cs].astype(F32)
            continue
        assert d == 16
        for r1 in range(4):
            for r2 in range(4):
                b_ref[j, pl.ds(r1 * quarter + r2, quarter // 4, stride=4), :] = in_ref[0, r1 + 4 * r2, :, cs].astype(F32)
        for r1 in range(4):
            a_ref[j, pl.ds(r1, quarter, stride=4), :] = b_ref[j, r1 * quarter:(r1 + 1) * quarter, :]
    return jnp.concatenate([a_ref[j] for j in range(n_tiles)], axis=-1)


def _norm_proj_rope_kernel(x_ref, g_ref, w_ref, cos_ref, sin_ref, *rest, q_scale):
    out_refs, (a_ref, b_ref) = rest[:-2], rest[-2:]
    n_groups = len(DIL_PATTERNS)
    h = _rmsnorm(x_ref[0], g_ref[...]).astype(BF16)
    cos = cos_ref[0]
    sin = sin_ref[0]
    lane = lax.broadcasted_iota(jnp.int32, cos.shape, 1)
    first_half = (lane % DIL_HEAD_DIM) < (DIL_HEAD_DIM // 2)
    W = DIL_WIDTH
    for c, out_ref in enumerate(out_refs):
        which, g = divmod(c, n_groups)
        acc = jnp.dot(h, w_ref[:, c * W:(c + 1) * W], preferred_element_type=F32)
        if which < 2:
            parts = []
            for j in range(W // LANES):
                blk = acc[:, j * LANES:(j + 1) * LANES]
                half = DIL_HEAD_DIM // 2
                rot = jnp.where(first_half, pltpu.roll(blk, LANES - half, axis=1), pltpu.roll(blk, half, axis=1))
                out = blk * cos + rot * sin
                parts.append(out * q_scale if which == 0 else out)
            acc = jnp.concatenate(parts, axis=-1)
        _deinterleave_store(out_ref, acc, DIL_PATTERNS[g][1], a_ref, b_ref)


def norm_proj_rope(x3, gain, w, cos3, sin3, q_scale, tm=1024):
    B, S, D = x3.shape
    W = DIL_WIDTH
    out_specs, out_shape = [], []
    for _ in range(3):
        for _, d in DIL_PATTERNS:
            out_specs.append(pl.BlockSpec((1, d, tm // d, W), lambda b, i: (b, 0, i, 0)))
            out_shape.append(jax.ShapeDtypeStruct((B, d, S // d, W), BF16))
    return pl.pallas_call(
        functools.partial(_norm_proj_rope_kernel, q_scale=q_scale),
        grid=(B, S // tm),
        in_specs=[pl.BlockSpec((1, tm, D), lambda b, i: (b, i, 0)), _resident((1, D)), _resident(w.shape),
                  pl.BlockSpec((1, tm, LANES), lambda b, i: (b, i, 0)),
                  pl.BlockSpec((1, tm, LANES), lambda b, i: (b, i, 0))],
        out_specs=out_specs,
        out_shape=out_shape,
        scratch_shapes=[pltpu.VMEM((W // LANES, tm, LANES), F32), pltpu.VMEM((W // LANES, tm, LANES), F32)],
        compiler_params=_cparams(("parallel", "parallel")),
        name="norm_proj_rope",
    )(x3, gain.reshape(1, D), w, cos3, sin3)


def _bmm(a, b):
    return jnp.einsum('nik,nkj->nij', a.astype(BF16), b.astype(BF16), preferred_element_type=F32)


def _bmm_nt(a, b):
    return jnp.einsum('nik,njk->nij', a.astype(BF16), b.astype(BF16), preferred_element_type=F32)


def _bmm_tn(a, b):
    return jnp.einsum('nci,ncj->nij', a.astype(BF16), b.astype(BF16), preferred_element_type=F32)


def _unit_lower_inverse(L):
    C = L.shape[-1]
    ri = lax.broadcasted_iota(jnp.int32, (C, C), 0)
    ci = lax.broadcasted_iota(jnp.int32, (C, C), 1)

    def same_block(size):
        return (ri // size) == (ci // size)

    eye = (ri == ci).astype(F32)
    Ld = jnp.where(same_block(INV_BLOCK), L, 0.0)
    X = eye - Ld
    P = Ld
    for _ in range(int(math.log2(INV_BLOCK)) - 1):
        P = _bmm(P, P)
        X = X + _bmm(X, P)
    size = INV_BLOCK
    while size < C:
        off = jnp.where(same_block(2 * size) & ~same_block(size), L, 0.0)
        X = X - _bmm(_bmm(X, off), X)
        size *= 2
    return X


def _shifted_rows(buf_ref, cur, prev, first, shifts):
    TB = cur.shape[0]
    buf_ref[0:SUBLANES, :] = jnp.where(first, 0.0, prev)
    buf_ref[SUBLANES:, :] = cur
    return [buf_ref[SUBLANES - s:SUBLANES - s + TB, :] for s in shifts]


def _split_heads(x, n_heads, width):
    return jnp.stack([x[:, h * width:(h + 1) * width] for h in range(n_heads)], axis=0)


def _merge_heads(x):
    return jnp.concatenate([x[h] for h in range(x.shape[0])], axis=-1)


def _gdn_kernel(q_ref, k_ref, v_ref, z_ref, ab_ref, qp_ref, kp_ref, vp_ref, cw_ref, alog_ref, dtb_ref, nw_ref,
                o_ref, buf_ref, s_ref, qd_ref, pg_ref, qg_ref, ov_ref, egl_ref, oc_ref):
    H, DK, DV, C = GDN_HEADS, GDN_DK, GDN_DV, CHUNK
    TB = q_ref.shape[1]
    NC = TB // C
    first = pl.program_id(1) == 0

    @pl.when(first)
    def _():
        s_ref[...] = jnp.zeros_like(s_ref)

    def conv_silu(cur_ref, prev_ref, j):
        cur = cur_ref[0]
        x3, x2, x1 = _shifted_rows(buf_ref, cur, prev_ref[0], first, (3, 2, 1))
        w = cw_ref[j]
        y = x3 * w[0:1] + x2 * w[1:2] + x1 * w[2:3] + cur * w[3:4]
        return y * _sigmoid(y)

    q = _split_heads(conv_silu(q_ref, qp_ref, 0), H, DK)
    k = _split_heads(conv_silu(k_ref, kp_ref, 1), H, DK)
    v = _split_heads(conv_silu(v_ref, vp_ref, 2), H, DV)
    q = q * lax.rsqrt(jnp.sum(q * q, axis=-1, keepdims=True) + L2_EPS) * (DK ** -0.5)
    k = k * lax.rsqrt(jnp.sum(k * k, axis=-1, keepdims=True) + L2_EPS)
    ab = ab_ref[0]
    a = jnp.stack([ab[:, h:h + 1] for h in range(H)], axis=0)
    b = jnp.stack([ab[:, H + h:H + h + 1] for h in range(H)], axis=0)
    beta = _sigmoid(b)
    g = -jnp.exp(alog_ref[...]) * _softplus(a + dtb_ref[...])

    N = H * NC
    q = q.reshape(N, C, DK)
    k = k.reshape(N, C, DK)
    v = v.reshape(N, C, DV)
    beta = beta.reshape(N, C, 1)
    g = g.reshape(N, C, 1)

    ri = lax.broadcasted_iota(jnp.int32, (C, C), 0)
    ci = lax.broadcasted_iota(jnp.int32, (C, C), 1)
    causal = ri >= ci
    strict = ri > ci
    gb = jnp.broadcast_to(g, (N, C, C))
    g_row = jnp.sum(jnp.where(ri == ci, gb, 0.0), axis=1, keepdims=True)
    gc_col = jnp.sum(jnp.where(causal, jnp.broadcast_to(g_row, (N, C, C)), 0.0), axis=2, keepdims=True)
    gc_row = jnp.sum(jnp.where(ri <= ci, gb, 0.0), axis=1, keepdims=True)
    decay = jnp.where(causal, jnp.exp(jnp.where(causal, gc_col - gc_row, 0.0)), 0.0)

    kb = k * beta
    vb = v * beta
    kq = _bmm_nt(jnp.concatenate([kb, q], axis=1), k)
    L = jnp.where(strict, kq[:, :C] * decay, 0.0)
    aqk = kq[:, C:] * decay
    tinv = _unit_lower_inverse(L)
    egc = jnp.exp(gc_col)
    uw = _bmm(tinv, jnp.concatenate([vb, kb * egc], axis=2))
    auw = _bmm(aqk, uw)
    g_last = gc_col[:, C - 1:C, :]
    k_dec = k * jnp.exp(g_last - gc_col)
    kuw = _bmm_tn(k_dec, uw)
    qd_ref[...] = (q * egc - auw[:, :, DV:]).reshape(H, NC, C, DK)
    ov_ref[...] = auw[:, :, :DV].reshape(H, NC, C, DV)
    pg_ref[...] = kuw[:, :, DV:].reshape(H, NC, DK, DK)
    qg_ref[...] = kuw[:, :, :DV].reshape(H, NC, DK, DV)
    egl_ref[...] = jnp.broadcast_to(jnp.exp(g_last), (N, 1, LANES)).reshape(H, NC, 1, LANES)

    for c in range(NC):
        S = s_ref[...]
        oc_ref[:, c] = _bmm(qd_ref[:, c], S) + ov_ref[:, c]
        s_ref[...] = S * egl_ref[:, c] - _bmm(pg_ref[:, c], S) + qg_ref[:, c]

    o = oc_ref[...].reshape(H, TB, DV)
    o = o * lax.rsqrt(jnp.mean(o * o, axis=-1, keepdims=True) + RMS_EPS) * nw_ref[...]
    z = _split_heads(z_ref[0], H, DV)
    o_ref[0] = _merge_heads(o * (z * _sigmoid(z)))


def gdn_mixer(proj3, conv_w, a_log, dt_bias, norm_w, tb=512):
    B, S, _ = proj3.shape
    H, DK, DV, C = GDN_HEADS, GDN_DK, GDN_DV, CHUNK
    W = H * DK
    NC = tb // C
    nq, nk, nv, nz, nab = EV_Q // W, EV_K // W, EV_V // W, EV_Z // W, (EV_SMALL + SMALL_AB) // LANES
    rows8 = tb // SUBLANES

    def cur(cb):
        return pl.BlockSpec((1, tb, W), lambda b, i, cb=cb: (b, i, cb))

    def prev(cb):
        return pl.BlockSpec((1, SUBLANES, W), lambda b, i, cb=cb: (b, jnp.maximum(i * rows8 - 1, 0), cb))

    cw = conv_w.reshape(GDN_CONV, 3, W).transpose(1, 0, 2)
    return pl.pallas_call(
        _gdn_kernel,
        grid=(B, S // tb),
        in_specs=[cur(nq), cur(nk), cur(nv), cur(nz),
                  pl.BlockSpec((1, tb, LANES), lambda b, i: (b, i, nab)),
                  prev(nq), prev(nk), prev(nv),
                  _resident((3, GDN_CONV, W)), _resident((H, 1, 1)), _resident((H, 1, 1)),
                  _resident((1, 1, DV))],
        out_specs=pl.BlockSpec((1, tb, H * DV), lambda b, i: (b, i, 0)),
        out_shape=jax.ShapeDtypeStruct((B, S, H * DV), F32),
        scratch_shapes=[pltpu.VMEM((tb + SUBLANES, W), F32),
                        pltpu.VMEM((H, DK, DV), F32),
                        pltpu.VMEM((H, NC, C, DK), F32),
                        pltpu.VMEM((H, NC, DK, DK), F32),
                        pltpu.VMEM((H, NC, DK, DV), F32),
                        pltpu.VMEM((H, NC, C, DV), F32),
                        pltpu.VMEM((H, NC, 1, LANES), F32),
                        pltpu.VMEM((H, NC, C, DV), F32)],
        compiler_params=_cparams(("parallel", "arbitrary")),
        name="gdn_mixer",
    )(proj3, proj3, proj3, proj3, proj3, proj3, proj3, proj3, cw,
      a_log.reshape(H, 1, 1), dt_bias.reshape(H, 1, 1), norm_w.reshape(1, 1, DV))


def _rwkv_kernel(*refs, has_vres):
    if has_vres:
        (r_ref, k_ref, v_ref, lo_ref, rp_ref, kp_ref, vp_ref, lop_ref, vf_ref,
         mu_ref, mul_ref, vec_ref, w2_ref, a2_ref, g2_ref, v2_ref,
         y_ref, buf_ref, bufl_ref, s_ref, rr_ref, pp_ref, qq_ref, yv_ref, egc_ref, yc_ref) = refs
    else:
        (r_ref, k_ref, v_ref, lo_ref, rp_ref, kp_ref, vp_ref, lop_ref,
         mu_ref, mul_ref, vec_ref, w2_ref, a2_ref, g2_ref,
         y_ref, vf_out_ref, buf_ref, bufl_ref, s_ref, rr_ref, pp_ref, qq_ref, yv_ref, egc_ref, yc_ref) = refs
    H, D, C = RWKV_HEADS, RWKV_HEAD, CHUNK
    TB = r_ref.shape[1]
    NC = TB // C
    first = pl.program_id(1) == 0

    @pl.when(first)
    def _():
        s_ref[...] = jnp.zeros_like(s_ref)

    def mix(cur_ref, prev_ref, mu, buf):
        cur = cur_ref[0]
        (sh,) = _shifted_rows(buf, cur, prev_ref[0], first, (1,))
        return cur + mu * (sh - cur)

    r = mix(r_ref, rp_ref, mu_ref[0:1], buf_ref)
    k = mix(k_ref, kp_ref, mu_ref[1:2], buf_ref)
    v = mix(v_ref, vp_ref, mu_ref[2:3], buf_ref)
    lo = mix(lo_ref, lop_ref, mul_ref[...], bufl_ref)
    lo0, lo1 = lo[:, :LANES], lo[:, LANES:]
    w0, a0, k_k, k_a, r_k, ln_w, ln_b = (vec_ref[j:j + 1] for j in range(7))

    w_log = -_softplus(-(w0 + jnp.dot(jnp.tanh(lo0), w2_ref[...], preferred_element_type=F32))) - 0.5
    lw = -jnp.exp(w_log)
    a = _sigmoid(a0 + jnp.dot(lo0, a2_ref[...], preferred_element_type=F32))
    gate = jnp.dot(_sigmoid(lo1), g2_ref[...], preferred_element_type=F32)
    if has_vres:
        v0 = vec_ref[7:8]
        v = v + (vf_ref[0] - v) * _sigmoid(v0 + jnp.dot(lo0, v2_ref[...], preferred_element_type=F32))
    else:
        vf_out_ref[0] = v

    ri = lax.broadcasted_iota(jnp.int32, (C, C), 0)
    ci = lax.broadcasted_iota(jnp.int32, (C, C), 1)
    tril = (ri >= ci).astype(BF16)
    lw_hi = lw.astype(BF16)
    lw_rest = lw - lw_hi.astype(F32)
    lw_mid = lw_rest.astype(BF16)
    lw_lo = (lw_rest - lw_mid.astype(F32)).astype(BF16)
    gi = jnp.concatenate(
        [sum(jnp.dot(tril, part[c * C:(c + 1) * C], preferred_element_type=F32) for part in (lw_hi, lw_mid, lw_lo))
         for c in range(NC)], axis=0)

    kk = _split_heads(k * k_k, H, D)
    kk = kk * lax.rsqrt(jnp.sum(kk * kk, axis=-1, keepdims=True) + L2_EPS)
    k = k * (1.0 + (a - 1.0) * k_a)
    r_h = _split_heads(r, H, D)
    k_h = _split_heads(k, H, D)
    v_h = _split_heads(v, H, D)
    a_h = _split_heads(a, H, D)
    gi_h = _split_heads(gi, H, D)
    lw_h = _split_heads(lw, H, D)
    bonus = jnp.sum(r_h * k_h * _split_heads(r_k, H, D), axis=-1, keepdims=True) * v_h

    N = H * NC
    rc, kc, vc, kkc, ac, gic, lwc = (t.reshape(N, C, D) for t in (r_h, k_h, v_h, kk, a_h, gi_h, lw_h))
    g_end = gic[:, C - 1:C, :]
    e_neg = jnp.exp(-gic)
    e_tail = jnp.exp(g_end - gic)
    at = -kkc * jnp.exp(gic - lwc)
    bvec = kkc * ac
    rt = rc * jnp.exp(gic)
    bt = bvec * e_neg
    kt = kc * e_neg
    b_dec = bvec * e_tail
    k_dec = kc * e_tail
    strict = ri > ci
    causal = ri >= ci
    amat = _bmm_nt(jnp.concatenate([at, rt], axis=1), jnp.concatenate([bt, kt], axis=1))
    a_ab = jnp.where(strict, amat[:, :C, :C], 0.0)
    a_ak = jnp.where(strict, amat[:, :C, C:], 0.0)
    a_r = jnp.concatenate([jnp.where(causal, amat[:, C:, :C], 0.0), jnp.where(causal, amat[:, C:, C:], 0.0)], axis=2)
    tinv = _unit_lower_inverse(-a_ab)
    x2 = _bmm(tinv, jnp.concatenate([at, _bmm(a_ak, vc)], axis=2))
    xv = jnp.concatenate([x2, jnp.concatenate([jnp.zeros_like(vc), vc], axis=2)], axis=1)
    y2 = _bmm(a_r, xv)
    pq = _bmm_tn(xv, jnp.concatenate([b_dec, k_dec], axis=1))
    rr_ref[...] = (rt + y2[:, :, :D]).reshape(H, NC, C, D)
    yv_ref[...] = y2[:, :, D:].reshape(H, NC, C, D)
    pp_ref[...] = pq[:, :D].reshape(H, NC, D, D)
    qq_ref[...] = pq[:, D:].reshape(H, NC, D, D)
    egc_ref[...] = jnp.exp(g_end).reshape(H, NC, 1, D)

    for c in range(NC):
        S = s_ref[...]
        yc_ref[:, c] = _bmm_nt(rr_ref[:, c], S) + yv_ref[:, c]
        s_ref[...] = S * egc_ref[:, c] + _bmm(S, pp_ref[:, c]) + qq_ref[:, c]

    y = yc_ref[...].reshape(H, TB, D)
    mean = jnp.mean(y, axis=-1, keepdims=True)
    yc = y - mean
    var = jnp.mean(yc * yc, axis=-1, keepdims=True)
    y = _merge_heads(yc * lax.rsqrt(var + RWKV_LN_EPS)) * ln_w + ln_b
    y_ref[0] = (y + _merge_heads(bonus)) * gate


def rwkv_mixer(proj3, mu, mu_lora, vecs, w2, a2, g2, v_first=None, v2=None, tb=256):
    B, S, _ = proj3.shape
    H, D, C = RWKV_HEADS, RWKV_HEAD, CHUNK
    W = RWKV_DIM
    WL = 2 * LANES
    NC = tb // C
    rows8 = tb // SUBLANES
    has_vres = v_first is not None

    def cur(col, width):
        return pl.BlockSpec((1, tb, width), lambda b, i, cb=col // width: (b, i, cb))

    def prev(col, width):
        return pl.BlockSpec((1, SUBLANES, width),
                            lambda b, i, cb=col // width: (b, jnp.maximum(i * rows8 - 1, 0), cb))

    in_specs = [cur(EV_R, W), cur(EV_RK, W), cur(EV_RV, W), cur(EV_SMALL, WL),
                prev(EV_R, W), prev(EV_RK, W), prev(EV_RV, W), prev(EV_SMALL, WL)]
    args = [proj3] * 8
    if has_vres:
        in_specs.append(pl.BlockSpec((1, tb, W), lambda b, i: (b, i, 0)))
        args.append(v_first)
    in_specs += [_resident(mu.shape), _resident(mu_lora.shape), _resident(vecs.shape),
                 _resident(w2.shape), _resident(a2.shape), _resident(g2.shape)]
    args += [mu, mu_lora, vecs, w2, a2, g2]
    if has_vres:
        in_specs.append(_resident(v2.shape))
        args.append(v2)
    out_block = pl.BlockSpec((1, tb, W), lambda b, i: (b, i, 0))
    out_sds = jax.ShapeDtypeStruct((B, S, W), F32)
    res = pl.pallas_call(
        functools.partial(_rwkv_kernel, has_vres=has_vres),
        grid=(B, S // tb),
        in_specs=in_specs,
        out_specs=out_block if has_vres else [out_block, out_block],
        out_shape=out_sds if has_vres else [out_sds, out_sds],
        scratch_shapes=[pltpu.VMEM((tb + SUBLANES, W), F32),
                        pltpu.VMEM((tb + SUBLANES, WL), F32),
                        pltpu.VMEM((H, D, D), F32),
                        pltpu.VMEM((H, NC, C, D), F32),
                        pltpu.VMEM((H, NC, D, D), F32),
                        pltpu.VMEM((H, NC, D, D), F32),
                        pltpu.VMEM((H, NC, C, D), F32),
                        pltpu.VMEM((H, NC, 1, D), F32),
                        pltpu.VMEM((H, NC, C, D), F32)],
        compiler_params=_cparams(("parallel", "arbitrary")),
        name="rwkv_mixer",
    )(*args)
    if has_vres:
        return res, v_first
    return res[0], res[1]


def _dil_attn_kernel(q_ref, kc_ref, kp_ref, vc_ref, vp_ref, o_ref, lse_ref, kbuf, vbuf):
    Dh, Bk = DIL_HEAD_DIM, ATT_BLOCK
    QB = q_ref.shape[2]
    has_prev = pl.program_id(2) > 0
    kbuf[0:Bk] = kp_ref[0, 0]
    kbuf[Bk:] = kc_ref[0, 0]
    vbuf[0:Bk] = vp_ref[0, 0]
    vbuf[Bk:] = vc_ref[0, 0]
    qi = lax.broadcasted_iota(jnp.int32, (Bk, 2 * Bk), 0)
    cj = lax.broadcasted_iota(jnp.int32, (Bk, 2 * Bk), 1)
    band = (cj >= qi) & (cj <= qi + Bk)
    lane = lax.broadcasted_iota(jnp.int32, (Bk, LANES), 1)
    low = lane < Dh
    low_kv = lax.broadcasted_iota(jnp.int32, (2 * Bk, LANES), 1) < Dh
    one_kv = jnp.ones((2 * Bk, LANES), BF16)
    NP = DIL_WIDTH // LANES

    def pairs(x):
        return jnp.stack([x[:, hp * LANES:(hp + 1) * LANES] for hp in range(NP)], axis=0)

    for j in range(QB // Bk):
        ok = band if j > 0 else band & ((cj >= Bk) | has_prev)
        ok2 = jnp.concatenate([ok, ok], axis=0)
        q_pairs = pairs(q_ref[0, 0, j * Bk:(j + 1) * Bk, :])
        zero = jnp.zeros_like(q_pairs)
        q2 = jnp.concatenate([jnp.where(low, q_pairs, zero), jnp.where(low, zero, q_pairs)], axis=1)
        k_pairs = pairs(kbuf[j * Bk:(j + 2) * Bk, :])
        v_pairs = pairs(vbuf[j * Bk:(j + 2) * Bk, :])
        s = jnp.einsum('hqd,hkd->hqk', q2, k_pairs, preferred_element_type=F32)
        s = jnp.where(ok2, s, -jnp.inf)
        m = jnp.max(s, axis=-1, keepdims=True).reshape(2 * NP, Bk, 1)
        p = jnp.exp2(s - m.reshape(NP, 2 * Bk, 1)).astype(BF16).reshape(2 * NP, Bk, 2 * Bk)
        v_ext = jnp.stack([jnp.where(low_kv, v_pairs, one_kv), jnp.where(low_kv, one_kv, v_pairs)], axis=1)
        po = jnp.einsum('hqk,hkd->hqd', p, v_ext.reshape(2 * NP, 2 * Bk, LANES), preferred_element_type=F32)
        swapped = pltpu.roll(po.reshape(2 * NP * Bk, LANES), Dh, axis=1).reshape(2 * NP, Bk, LANES)
        r = po / swapped
        o_ref[0, 0, j * Bk:(j + 1) * Bk, :] = jnp.concatenate(
            [jnp.where(low, r[2 * hp], r[2 * hp + 1]) for hp in range(NP)], axis=-1).astype(o_ref.dtype)
        lse_blk = jnp.zeros((Bk, LANES), F32)
        for h in range(2 * NP):
            lse_blk = jnp.where(lane == h, m[h] + jnp.log2((swapped if h % 2 == 0 else po)[h]), lse_blk)
        lse_ref[0, 0, j * Bk:(j + 1) * Bk, :] = lse_blk


def dil_attention_group(q, k, v):
    B, d, L, W = q.shape
    qb = min(4 * ATT_BLOCK, L)
    nj = qb // ATT_BLOCK
    cur = pl.BlockSpec((1, 1, qb, W), lambda b, r, n: (b, r, n, 0))
    prev = pl.BlockSpec((1, 1, ATT_BLOCK, W), lambda b, r, n: (b, r, jnp.maximum(n * nj - 1, 0), 0))
    return pl.pallas_call(
        _dil_attn_kernel,
        grid=(B, d, L // qb),
        in_specs=[cur, cur, prev, cur, prev],
        out_specs=[cur, pl.BlockSpec((1, 1, qb, LANES), lambda b, r, n: (b, r, n, 0))],
        out_shape=[jax.ShapeDtypeStruct((B, d, L, W), BF16), jax.ShapeDtypeStruct((B, d, L, LANES), F32)],
        scratch_shapes=[pltpu.VMEM((ATT_BLOCK + qb, W), BF16), pltpu.VMEM((ATT_BLOCK + qb, W), BF16)],
        compiler_params=_cparams(("parallel", "parallel", "arbitrary")),
        name=f"dil_attn_d{d}",
    )(q, k, k, v, v)


def _merged_attention(o_refs, l_refs, e_ref, a_ref, b_ref, la_ref, lb_ref):
    dils = [d for _, d in DIL_PATTERNS]
    lses = [_interleave_load(r, d, la_ref, lb_ref) for r, d in zip(l_refs, dils)]
    m = jnp.maximum(jnp.maximum(lses[0], lses[1]), lses[2])
    es = [jnp.exp2(l - m) for l in lses]
    inv = 1.0 / (es[0] + es[1] + es[2])
    expand = e_ref[...]
    mixed = None
    for e, o_ref, d in zip(es, o_refs, dils):
        term = _group_sum(e * inv, expand) * _interleave_load(o_ref, d, a_ref, b_ref)
        mixed = term if mixed is None else mixed + term
    return mixed


def _head_expand_matrix():
    expand = np.zeros((LANES, DIL_WIDTH), np.float32)
    for h in range(DIL_HEADS):
        expand[h, h * DIL_HEAD_DIM:(h + 1) * DIL_HEAD_DIM] = 1.0
    return jnp.asarray(expand, BF16)


MOE_TM = 512
MOE_RB = 144
MOE_NBLK = (MOE_TM + N_GROUPS * (MOE_RB - 1)) // MOE_RB
LOGIT_ROWS = pl.cdiv(N_GROUPS * (1 + EXPERTS_PER_GROUP), SUBLANES) * SUBLANES


def _moe_route(h, wr_ref, before_ref):
    NG, EPG, RB = N_GROUPS, EXPERTS_PER_GROUP, MOE_RB
    tm = h.shape[0]
    hb, h_lo = _hi_lo(h)
    nt = (((1,), (1,)), ((), ()))
    lg = lax.dot_general(wr_ref[...], hb, nt, preferred_element_type=F32)
    lg = lg[:LANES] + lg[LANES:] + lax.dot_general(wr_ref[:LANES], h_lo, nt, preferred_element_type=F32)
    L = lg[:LOGIT_ROWS]
    r = lax.broadcasted_iota(jnp.int32, L.shape, 0)
    big = jnp.int32(1 << 30)
    gl = jnp.where(r < NG, L, -jnp.inf)
    gmax = jnp.max(gl, axis=0, keepdims=True)
    gsel = jnp.min(jnp.where(gl == gmax, r, big), axis=0, keepdims=True)
    p_group = 1.0 / jnp.sum(jnp.where(r < NG, jnp.exp(L - gmax), 0.0), axis=0, keepdims=True)
    lo = NG + gsel * EPG
    el = jnp.where((r >= lo) & (r < lo + EPG), L, -jnp.inf)
    v1 = jnp.max(el, axis=0, keepdims=True)
    i1 = jnp.min(jnp.where(el == v1, r, big), axis=0, keepdims=True)
    el2 = jnp.where(r == i1, -jnp.inf, el)
    v2 = jnp.max(el2, axis=0, keepdims=True)
    i2 = jnp.min(jnp.where(el2 == v2, r, big), axis=0, keepdims=True)
    t = jnp.exp(v2 - v1)
    gates = jnp.where(r == i1, p_group / (1.0 + t), 0.0) + jnp.where(r == i2, p_group * t / (1.0 + t), 0.0)
    gate4 = [jnp.sum(jnp.where(r == lo + j, gates, 0.0), axis=0, keepdims=True) for j in range(EPG)]

    r8 = lax.broadcasted_iota(jnp.int32, (SUBLANES, tm), 0)
    onehot = (r8 == gsel).astype(F32)
    rank = jnp.dot(onehot.astype(BF16), before_ref[...], preferred_element_type=F32)
    count = jnp.sum(onehot, axis=1, keepdims=True).astype(jnp.int32)
    nblk = sum((count > j * RB).astype(jnp.int32) for j in range(pl.cdiv(tm, RB)))
    padded = (nblk * RB).astype(F32)
    start, starts = jnp.zeros((1, 1), F32), []
    for g in range(NG):
        starts.append(start)
        start = start + padded[g:g + 1]
    start8 = jnp.concatenate(starts + [jnp.zeros((SUBLANES - NG, 1), F32)], axis=0)
    dest = jnp.sum(onehot * (start8 + rank), axis=0, keepdims=True)
    return dest, gate4, nblk


def _moe_kernel(x_ref, *refs, final_norm, mixer_proj):
    NG, EPG, RB, NBLK = N_GROUPS, EXPERTS_PER_GROUP, MOE_RB, MOE_NBLK
    x = x_ref[...]
    if mixer_proj == "even":
        ya_ref, yb_ref, wa_ref, wb_ref = refs[:4]
        refs = refs[4:]
        x = (x + jnp.dot(ya_ref[...].astype(BF16), wa_ref[...], preferred_element_type=F32)
             + jnp.dot(yb_ref[...].astype(BF16), wb_ref[...], preferred_element_type=F32))
    elif mixer_proj == "odd":
        mixed = _merged_attention(refs[0:3], refs[3:6], refs[6], *refs[-4:])
        x = x + jnp.dot(mixed.astype(BF16), refs[7][...], preferred_element_type=F32)
        refs = refs[8:-4]
    g_ref, wr_ref, before_ref, w1_ref, w3_ref, w2_ref, fg_ref, o_ref, p_ref, hb_ref, g2_ref, hh_ref, ys_ref = refs
    FF = w1_ref.shape[3]
    tm = x_ref.shape[0]
    h = _rmsnorm(x, g_ref[...])
    dest, gate4, nblk = _moe_route(h, wr_ref, before_ref)
    rows = lax.broadcasted_iota(jnp.int32, (NBLK * RB, tm), 0)
    p_ref[...] = jnp.where(rows == dest.astype(jnp.int32), 1.0, 0.0).astype(BF16)
    hb_ref[...] = h.astype(BF16)
    g4 = jnp.concatenate(gate4 + [jnp.zeros((SUBLANES - EPG, tm), F32)], axis=0)
    g_hi = g4.astype(BF16).astype(F32)
    g2_ref[...] = jnp.concatenate([g_hi, g4 - g_hi, jnp.zeros((LANES - 2 * SUBLANES, tm), F32)], axis=0).astype(BF16)

    ends, acc = [], jnp.int32(0)
    for g in range(NG):
        acc = acc + jnp.sum(nblk[g:g + 1])
        ends.append(acc)
    for b in range(NBLK):
        rs = slice(b * RB, (b + 1) * RB)
        grp = sum((b >= e).astype(jnp.int32) for e in ends[:-1])

        @pl.when(b < ends[-1])
        def _():
            pb = p_ref[rs]
            xr = jnp.dot(pb, hb_ref[...], preferred_element_type=F32).astype(BF16)
            gate = lax.dot_general(pb, g2_ref[...], (((1,), (1,)), ((), ())), preferred_element_type=F32)
            for j in range(EPG):
                a = jnp.dot(xr, w1_ref[0, grp * EPG + j], preferred_element_type=F32)
                u = jnp.dot(xr, w3_ref[0, grp * EPG + j], preferred_element_type=F32)
                gj = gate[:, j:j + 1] + gate[:, SUBLANES + j:SUBLANES + j + 1]
                hh_ref[:, j * FF:(j + 1) * FF] = ((a * _sigmoid(a)) * u * gj).astype(BF16)
            ys_ref[rs] = jnp.dot(hh_ref[...], w2_ref[0, grp], preferred_element_type=F32).astype(BF16)

        @pl.when(b >= ends[-1])
        def _():
            ys_ref[rs] = jnp.zeros((RB, ys_ref.shape[1]), BF16)

    def scatter_back(n_blocks):
        rows_used = slice(0, n_blocks * RB)
        y = x + lax.dot_general(p_ref[rows_used], ys_ref[rows_used], (((0,), (0,)), ((), ())),
                                preferred_element_type=F32)
        o_ref[...] = _rmsnorm(y, fg_ref[...]) if final_norm else y

    usual = NG + 1
    pl.when(ends[-1] <= usual)(lambda: scatter_back(usual))
    pl.when(ends[-1] > usual)(lambda: scatter_back(NBLK))


def moe_block(x, gain, w_router, w1, w3, w2, layer, final_gain=None, mixer=None):
    T, D = x.shape
    _, NE, _, FF = w1.shape
    tm, RB, NBLK = MOE_TM, MOE_RB, MOE_NBLK
    final_norm = final_gain is not None
    fg = (final_gain if final_norm else jnp.ones((D,), F32)).reshape(1, D)

    def of_layer(w):
        return pl.BlockSpec((1,) + w.shape[1:], lambda i: (layer, 0, 0, 0), pipeline_mode=pl.Buffered(1))

    mixer_specs, mixer_args, mixer_scratch = [], [], []
    kind = None if mixer is None else mixer[0]
    if kind == "even":
        _, ya, yb, wa, wb = mixer
        mixer_specs = [pl.BlockSpec((tm, ya.shape[1]), lambda i: (i, 0)), pl.BlockSpec((tm, yb.shape[1]), lambda i: (i, 0)),
                       _resident(wa.shape), _resident(wb.shape)]
        mixer_args = [ya, yb, wa, wb]
    elif kind == "odd":
        _, outs, lses, w_out = mixer
        tiles_per_seq = outs[0].shape[1] * outs[0].shape[2] // tm
        for width, arrs in ((DIL_WIDTH, outs), (LANES, lses)):
            for (_, d), arr in zip(DIL_PATTERNS, arrs):
                mixer_specs.append(pl.BlockSpec((1, d, tm // d, width),
                                                lambda i: (i // tiles_per_seq, 0, i % tiles_per_seq, 0)))
                mixer_args.append(arr)
        mixer_specs += [_resident((LANES, DIL_WIDTH)), _resident(w_out.shape)]
        mixer_args += [_head_expand_matrix(), w_out]
        n_tiles = DIL_WIDTH // LANES
        mixer_scratch = [pltpu.VMEM((n_tiles, tm, LANES), F32), pltpu.VMEM((n_tiles, tm, LANES), F32),
                         pltpu.VMEM((1, tm, LANES), F32), pltpu.VMEM((1, tm, LANES), F32)]
    return pl.pallas_call(
        functools.partial(_moe_kernel, final_norm=final_norm, mixer_proj=kind),
        grid=(T // tm,),
        in_specs=[pl.BlockSpec((tm, D), lambda i: (i, 0)), *mixer_specs, _resident((1, D)), _resident(w_router.shape),
                  _resident((tm, tm)), of_layer(w1), of_layer(w3), of_layer(w2), _resident((1, D))],
        out_specs=pl.BlockSpec((tm, D), lambda i: (i, 0)),
        out_shape=jax.ShapeDtypeStruct((T, D), F32),
        scratch_shapes=[pltpu.VMEM((NBLK * RB, tm), BF16), pltpu.VMEM((tm, D), BF16),
                        pltpu.VMEM((LANES, tm), BF16), pltpu.VMEM((RB, EXPERTS_PER_GROUP * FF), BF16),
                        pltpu.VMEM((NBLK * RB, D), BF16), *mixer_scratch],
        compiler_params=_cparams(("parallel",)),
        name="moe_block",
    )(x, *mixer_args, gain.reshape(1, D), w_router, jnp.asarray(np.triu(np.ones((tm, tm), np.float32), 1), BF16),
      w1, w3, w2, fg)


def _router_weights(w_group, w_expert):
    w = _place([(0, w_group)] + [(N_GROUPS + g * EXPERTS_PER_GROUP, w_expert[g]) for g in range(N_GROUPS)], LANES).T
    return jnp.concatenate(_hi_lo(w), axis=0)


def _place(cols, total):
    lead, dtype = cols[0][1].shape[:-1], cols[0][1].dtype
    parts, pos = [], 0
    for off, arr in sorted(cols, key=lambda c: c[0]):
        if off > pos:
            parts.append(jnp.zeros(lead + (off - pos,), dtype))
        parts.append(arr)
        pos = off + arr.shape[-1]
    if total > pos:
        parts.append(jnp.zeros(lead + (total - pos,), dtype))
    return jnp.concatenate(parts, axis=-1)


def _even_layout(t, vres=None):
    gw = GDN_HEADS * GDN_DK
    sizes = [gw, gw, gw, gw, GDN_HEADS, GDN_HEADS, RWKV_DIM, RWKV_DIM, RWKV_DIM, DECAY_LORA, AAA_LORA, GATE_LORA]
    gq, gk, gv, gz, ga, gb, rr, rk, rv, lw, la, lg = jnp.split(t, np.cumsum(sizes)[:-1].tolist(), axis=-1)
    cols = [(EV_Q, gq), (EV_K, gk), (EV_V, gv), (EV_Z, gz), (EV_R, rr), (EV_RK, rk), (EV_RV, rv),
            (EV_SMALL + SMALL_W, lw), (EV_SMALL + SMALL_A, la), (EV_SMALL + SMALL_G, lg),
            (EV_SMALL + SMALL_AB, ga), (EV_SMALL + SMALL_AB + GDN_HEADS, gb)]
    if vres is not None:
        cols.append((EV_SMALL + SMALL_V, vres))
    return _place(cols, EV_NPAD)


def _lora_up(w, lane):
    return jnp.concatenate([jnp.zeros((lane, w.shape[1]), F32), w,
                            jnp.zeros((LANES - lane - w.shape[0], w.shape[1]), F32)], axis=0)


def kernel(x, positions, ev_norm, ev_w_in, rwkv_vres_down, ev_w_out, gdn_conv_w, gdn_A_log, gdn_dt_bias, gdn_norm,
           rwkv_mu, rwkv_w0, rwkv_w2, rwkv_a0, rwkv_a2, rwkv_g2, rwkv_k_k, rwkv_k_a, rwkv_r_k, rwkv_ln_w, rwkv_ln_b,
           rwkv_vres_mu, rwkv_v0, rwkv_v2, od_norm, od_w_in, od_w_out, ffn_norm, moe_w_group, moe_w_expert,
           moe_w1, moe_w3, moe_w2, final_norm):
    B, S, D = x.shape
    T = B * S
    depth = ffn_norm.shape[0]
    xf = x.reshape(T, D)
    cos, sin = rope_tables(positions.reshape(T))
    n_groups = len(DIL_PATTERNS)
    moe_w1_bf = moe_w1.astype(BF16)
    moe_w3_bf = moe_w3.astype(BF16)
    moe_w2_bf = moe_w2.astype(BF16).reshape(depth, N_GROUPS, -1, D)
    v_first = None
    for layer in range(depth):
        i = layer // 2
        if layer % 2 == 0:
            vres_w = None if i == 0 else rwkv_vres_down[i - 1].astype(BF16)
            w_in = _even_layout(ev_w_in[i].astype(BF16), vres_w)
            proj3 = norm_proj(xf, ev_norm[i], w_in).reshape(B, S, EV_NPAD)
            mu_r, mu_k, mu_v, mu_w, mu_a, mu_g = jnp.split(
                rwkv_mu[i], np.cumsum([RWKV_DIM, RWKV_DIM, RWKV_DIM, DECAY_LORA, AAA_LORA]).tolist())
            mu = jnp.stack([mu_r, mu_k, mu_v])
            lora_cols = [(SMALL_W, mu_w), (SMALL_A, mu_a), (SMALL_G, mu_g)]
            if i > 0:
                lora_cols.append((SMALL_V, rwkv_vres_mu[i - 1]))
            mu_lora = _place(lora_cols, 2 * LANES).reshape(1, 2 * LANES)
            vec_rows = [rwkv_w0[i], rwkv_a0[i], rwkv_k_k[i], rwkv_k_a[i], rwkv_r_k[i].reshape(-1), rwkv_ln_w[i], rwkv_ln_b[i]]
            if i > 0:
                vec_rows.append(rwkv_v0[i - 1])
            vecs = jnp.stack(vec_rows)
            w2 = _lora_up(rwkv_w2[i], SMALL_W)
            a2 = _lora_up(rwkv_a2[i], SMALL_A)
            g2 = _lora_up(rwkv_g2[i], SMALL_G - LANES)
            ya = gdn_mixer(proj3, gdn_conv_w[i], gdn_A_log[i], gdn_dt_bias[i], gdn_norm[i])
            if i == 0:
                yb, v_first = rwkv_mixer(proj3, mu, mu_lora, vecs, w2, a2, g2)
            else:
                yb, _ = rwkv_mixer(proj3, mu, mu_lora, vecs, w2, a2, g2, v_first, _lora_up(rwkv_v2[i - 1], SMALL_V))
            w_out = ev_w_out[i].astype(BF16)
            na = GDN_HEADS * GDN_DV
            mixer = ("even", ya.reshape(T, na), yb.reshape(T, RWKV_DIM), w_out[:na], w_out[na:])
        else:
            w_in = od_w_in[i].astype(BF16)
            qkv = norm_proj_rope(xf.reshape(B, S, D), od_norm[i], w_in, cos.reshape(B, S, LANES),
                                 sin.reshape(B, S, LANES), DIL_HEAD_DIM ** -0.5 * math.log2(math.e))
            outs, lses = [], []
            for gi, (window, dilation) in enumerate(DIL_PATTERNS):
                assert window // dilation == ATT_BLOCK and (S // dilation) % (2 * ATT_BLOCK) == 0
                o, lse = dil_attention_group(qkv[gi], qkv[n_groups + gi], qkv[2 * n_groups + gi])
                outs.append(o)
                lses.append(lse)
            mixer = ("odd", outs, lses, od_w_out[i].astype(BF16))
        w_router = _router_weights(moe_w_group[layer], moe_w_expert[layer])
        xf = moe_block(xf, ffn_norm[layer], w_router, moe_w1_bf, moe_w3_bf, moe_w2_bf, layer,
                       final_norm if layer == depth - 1 else None, mixer)
    return xf.reshape(B, S, D)
```

```python
import functools
import math

import jax
import jax.numpy as jnp
import numpy as np
from jax import lax
from jax.experimental import pallas as pl
from jax.experimental.pallas import tpu as pltpu

F32 = jnp.float32
BF16 = jnp.bfloat16

RMS_EPS = 1e-6
L2_EPS = 1e-6
RWKV_LN_EPS = 64e-5
ROPE_THETA = 10000.0

LANES = 128
SUBLANES = 8
VMEM_LIMIT = 56 * 1024 * 1024

GDN_HEADS = 4
GDN_DK = 128
GDN_DV = 128
GDN_CONV = 4
RWKV_HEADS = 8
RWKV_HEAD = 64
RWKV_DIM = RWKV_HEADS * RWKV_HEAD
DECAY_LORA, AAA_LORA, MV_LORA, GATE_LORA = 32, 32, 32, 96
DIL_PATTERNS = ((128, 1), (512, 4), (2048, 16))
DIL_HEADS = 8
DIL_HEAD_DIM = 64
DIL_WIDTH = DIL_HEADS * DIL_HEAD_DIM
ATT_BLOCK = 128
N_GROUPS = 4
EXPERTS_PER_GROUP = 4
CHUNK = 64
INV_BLOCK = 16

EV_Q, EV_K, EV_V, EV_Z = 0, 512, 1024, 1536
EV_R, EV_RK, EV_RV, EV_SMALL = 2048, 2560, 3072, 3584
SMALL_W, SMALL_A, SMALL_V = 0, DECAY_LORA, DECAY_LORA + AAA_LORA
SMALL_AB, SMALL_G = LANES, LANES + 2 * GDN_HEADS
EV_NPAD = EV_SMALL + 2 * LANES


def _cparams(sem):
    return pltpu.CompilerParams(dimension_semantics=sem, vmem_limit_bytes=VMEM_LIMIT)


def _sigmoid(x):
    return 0.5 * jnp.tanh(0.5 * x) + 0.5


def _softplus(x):
    return jnp.maximum(x, 0.0) + jnp.log(1.0 + jnp.exp(-jnp.abs(x)))


def _rmsnorm(x, gain):
    return x * lax.rsqrt(jnp.mean(x * x, axis=-1, keepdims=True) + RMS_EPS) * gain


def _hi_lo(v):
    hi = v.astype(BF16)
    return hi, (v - hi.astype(F32)).astype(BF16)


def _group_sum(x, block_ones):
    hi, lo = _hi_lo(x)
    return (jnp.dot(hi, block_ones, preferred_element_type=F32)
            + jnp.dot(lo, block_ones, preferred_element_type=F32))


def _resident(shape):
    nd = len(shape)
    return pl.BlockSpec(shape, lambda *_: (0,) * nd, pipeline_mode=pl.Buffered(1))


def _rope_table_kernel(pos_ref, freq_ref, sign_ref, cos_ref, sin_ref):
    ang = pos_ref[...].astype(F32) * freq_ref[...]
    cos_ref[...] = jnp.cos(ang)
    sin_ref[...] = jnp.sin(ang) * sign_ref[...]


def rope_tables(positions_flat, tm=1024):
    T = positions_flat.shape[0]
    half = DIL_HEAD_DIM // 2
    inv_freq = ROPE_THETA ** (-jnp.arange(half, dtype=F32) * 2.0 / DIL_HEAD_DIM)
    lane = np.arange(LANES)
    freq = inv_freq[lane % half][None, :]
    sign = jnp.asarray(np.where(lane % DIL_HEAD_DIM < half, -1.0, 1.0), F32)[None, :]
    return pl.pallas_call(
        _rope_table_kernel,
        grid=(T // tm,),
        in_specs=[pl.BlockSpec((tm, 1), lambda i: (i, 0)), _resident((1, LANES)), _resident((1, LANES))],
        out_specs=[pl.BlockSpec((tm, LANES), lambda i: (i, 0))] * 2,
        out_shape=[jax.ShapeDtypeStruct((T, LANES), F32)] * 2,
        compiler_params=_cparams(("parallel",)),
        name="rope_tables",
    )(positions_flat.reshape(T, 1), freq, sign)


def _norm_proj_kernel(x_ref, g_ref, w_ref, o_ref):
    h = _rmsnorm(x_ref[...], g_ref[...]).astype(BF16)
    o_ref[...] = jnp.dot(h, w_ref[...], preferred_element_type=F32)


def norm_proj(x, gain, w, tm=512):
    T, D = x.shape
    N = w.shape[1]
    return pl.pallas_call(
        _norm_proj_kernel,
        grid=(T // tm,),
        in_specs=[pl.BlockSpec((tm, D), lambda i: (i, 0)), _resident((1, D)), _resident((D, N))],
        out_specs=pl.BlockSpec((tm, N), lambda i: (i, 0)),
        out_shape=jax.ShapeDtypeStruct((T, N), F32),
        compiler_params=_cparams(("parallel",)),
        name="norm_proj",
    )(x, gain.reshape(1, D), w)


def _deinterleave_store(out_ref, val, d, a_ref, b_ref):
    tm = val.shape[0]
    if d == 1:
        out_ref[0, 0] = val.astype(out_ref.dtype)
        return
    quarter = tm // 4
    for j in range(val.shape[1] // LANES):
        cs = slice(j * LANES, (j + 1) * LANES)
        a_ref[j] = val[:, cs]
        if d == 4:
            for r in range(4):
                out_ref[0, r, :, cs] = a_ref[j, pl.ds(r, quarter, stride=4), :].astype(out_ref.dtype)
            continue
        assert d == 16
        for r1 in range(4):
            b_ref[j, r1 * quarter:(r1 + 1) * quarter, :] = a_ref[j, pl.ds(r1, quarter, stride=4), :]
        for r1 in range(4):
            for r2 in range(4):
                out_ref[0, r1 + 4 * r2, :, cs] = (
                    b_ref[j, pl.ds(r1 * quarter + r2, quarter // 4, stride=4), :].astype(out_ref.dtype))


def _interleave_load(in_ref, d, a_ref, b_ref):
    if d == 1:
        return in_ref[0, 0].astype(F32)
    n_tiles, tm, _ = a_ref.shape
    quarter = tm // 4
    for j in range(n_tiles):
        cs = slice(j * LANES, (j + 1) * LANES)
        if d == 4:
            for r in range(4):
                a_ref[j, pl.ds(r, quarter, stride=4), :] = in_ref[0, r, :, cs].astype(F32)
            continue
        assert d == 16
        for r1 in range(4):
            for r2 in range(4):
                b_ref[j, pl.ds(r1 * quarter + r2, quarter // 4, stride=4), :] = in_ref[0, r1 + 4 * r2, :, cs].astype(F32)
        for r1 in range(4):
            a_ref[j, pl.ds(r1, quarter, stride=4), :] = b_ref[j, r1 * quarter:(r1 + 1) * quarter, :]
    return jnp.concatenate([a_ref[j] for j in range(n_tiles)], axis=-1)


def _norm_proj_rope_kernel(x_ref, g_ref, w_ref, cos_ref, sin_ref, *rest, q_scale):
    out_refs, (a_ref, b_ref) = rest[:-2], rest[-2:]
    n_groups = len(DIL_PATTERNS)
    h = _rmsnorm(x_ref[0], g_ref[...]).astype(BF16)
    cos = cos_ref[0]
    sin = sin_ref[0]
    lane = lax.broadcasted_iota(jnp.int32, cos.shape, 1)
    first_half = (lane % DIL_HEAD_DIM) < (DIL_HEAD_DIM // 2)
    W = DIL_WIDTH
    for c, out_ref in enumerate(out_refs):
        which, g = divmod(c, n_groups)
        acc = jnp.dot(h, w_ref[:, c * W:(c + 1) * W], preferred_element_type=F32)
        if which < 2:
            parts = []
            for j in range(W // LANES):
                blk = acc[:, j * LANES:(j + 1) * LANES]
                half = DIL_HEAD_DIM // 2
                rot = jnp.where(first_half, pltpu.roll(blk, LANES - half, axis=1), pltpu.roll(blk, half, axis=1))
                out = blk * cos + rot * sin
                parts.append(out * q_scale if which == 0 else out)
            acc = jnp.concatenate(parts, axis=-1)
        _deinterleave_store(out_ref, acc, DIL_PATTERNS[g][1], a_ref, b_ref)


def norm_proj_rope(x3, gain, w, cos3, sin3, q_scale, tm=1024):
    B, S, D = x3.shape
    W = DIL_WIDTH
    out_specs, out_shape = [], []
    for _ in range(3):
        for _, d in DIL_PATTERNS:
            out_specs.append(pl.BlockSpec((1, d, tm // d, W), lambda b, i: (b, 0, i, 0)))
            out_shape.append(jax.ShapeDtypeStruct((B, d, S // d, W), BF16))
    return pl.pallas_call(
        functools.partial(_norm_proj_rope_kernel, q_scale=q_scale),
        grid=(B, S // tm),
        in_specs=[pl.BlockSpec((1, tm, D), lambda b, i: (b, i, 0)), _resident((1, D)), _resident(w.shape),
                  pl.BlockSpec((1, tm, LANES), lambda b, i: (b, i, 0)),
                  pl.BlockSpec((1, tm, LANES), lambda b, i: (b, i, 0))],
        out_specs=out_specs,
        out_shape=out_shape,
        scratch_shapes=[pltpu.VMEM((W // LANES, tm, LANES), F32), pltpu.VMEM((W // LANES, tm, LANES), F32)],
        compiler_params=_cparams(("parallel", "parallel")),
        name="norm_proj_rope",
    )(x3, gain.reshape(1, D), w, cos3, sin3)


def _bmm(a, b):
    return jnp.einsum('nik,nkj->nij', a.astype(BF16), b.astype(BF16), preferred_element_type=F32)


def _bmm_nt(a, b):
    return jnp.einsum('nik,njk->nij', a.astype(BF16), b.astype(BF16), preferred_element_type=F32)


def _bmm_tn(a, b):
    return jnp.einsum('nci,ncj->nij', a.astype(BF16), b.astype(BF16), preferred_element_type=F32)


def _unit_lower_inverse(L):
    C = L.shape[-1]
    ri = lax.broadcasted_iota(jnp.int32, (C, C), 0)
    ci = lax.broadcasted_iota(jnp.int32, (C, C), 1)

    def same_block(size):
        return (ri // size) == (ci // size)

    eye = (ri == ci).astype(F32)
    Ld = jnp.where(same_block(INV_BLOCK), L, 0.0)
    X = eye - Ld
    P = Ld
    for _ in range(int(math.log2(INV_BLOCK)) - 1):
        P = _bmm(P, P)
        X = X + _bmm(X, P)
    size = INV_BLOCK
    while size < C:
        off = jnp.where(same_block(2 * size) & ~same_block(size), L, 0.0)
        X = X - _bmm(_bmm(X, off), X)
        size *= 2
    return X


def _shifted_rows(buf_ref, cur, prev, first, shifts):
    TB = cur.shape[0]
    buf_ref[0:SUBLANES, :] = jnp.where(first, 0.0, prev)
    buf_ref[SUBLANES:, :] = cur
    return [buf_ref[SUBLANES - s:SUBLANES - s + TB, :] for s in shifts]


def _split_heads(x, n_heads, width):
    return jnp.stack([x[:, h * width:(h + 1) * width] for h in range(n_heads)], axis=0)


def _merge_heads(x):
    return jnp.concatenate([x[h] for h in range(x.shape[0])], axis=-1)


def _gdn_kernel(q_ref, k_ref, v_ref, z_ref, ab_ref, qp_ref, kp_ref, vp_ref, cw_ref, alog_ref, dtb_ref, nw_ref,
                o_ref, buf_ref, s_ref, qd_ref, pg_ref, qg_ref, ov_ref, egl_ref, oc_ref):
    H, DK, DV, C = GDN_HEADS, GDN_DK, GDN_DV, CHUNK
    TB = q_ref.shape[1]
    NC = TB // C
    first = pl.program_id(1) == 0

    @pl.when(first)
    def _():
        s_ref[...] = jnp.zeros_like(s_ref)

    def conv_silu(cur_ref, prev_ref, j):
        cur = cur_ref[0]
        x3, x2, x1 = _shifted_rows(buf_ref, cur, prev_ref[0], first, (3, 2, 1))
        w = cw_ref[j]
        y = x3 * w[0:1] + x2 * w[1:2] + x1 * w[2:3] + cur * w[3:4]
        return y * _sigmoid(y)

    q = _split_heads(conv_silu(q_ref, qp_ref, 0), H, DK)
    k = _split_heads(conv_silu(k_ref, kp_ref, 1), H, DK)
    v = _split_heads(conv_silu(v_ref, vp_ref, 2), H, DV)
    q = q * lax.rsqrt(jnp.sum(q * q, axis=-1, keepdims=True) + L2_EPS) * (DK ** -0.5)
    k = k * lax.rsqrt(jnp.sum(k * k, axis=-1, keepdims=True) + L2_EPS)
    ab = ab_ref[0]
    a = jnp.stack([ab[:, h:h + 1] for h in range(H)], axis=0)
    b = jnp.stack([ab[:, H + h:H + h + 1] for h in range(H)], axis=0)
    beta = _sigmoid(b)
    g = -jnp.exp(alog_ref[...]) * _softplus(a + dtb_ref[...])

    N = H * NC
    q = q.reshape(N, C, DK)
    k = k.reshape(N, C, DK)
    v = v.reshape(N, C, DV)
    beta = beta.reshape(N, C, 1)
    g = g.reshape(N, C, 1)

    ri = lax.broadcasted_iota(jnp.int32, (C, C), 0)
    ci = lax.broadcasted_iota(jnp.int32, (C, C), 1)
    causal = ri >= ci
    strict = ri > ci
    gb = jnp.broadcast_to(g, (N, C, C))
    g_row = jnp.sum(jnp.where(ri == ci, gb, 0.0), axis=1, keepdims=True)
    gc_col = jnp.sum(jnp.where(causal, jnp.broadcast_to(g_row, (N, C, C)), 0.0), axis=2, keepdims=True)
    gc_row = jnp.sum(jnp.where(ri <= ci, gb, 0.0), axis=1, keepdims=True)
    decay = jnp.where(causal, jnp.exp(jnp.where(causal, gc_col - gc_row, 0.0)), 0.0)

    kb = k * beta
    vb = v * beta
    kq = _bmm_nt(jnp.concatenate([kb, q], axis=1), k)
    L = jnp.where(strict, kq[:, :C] * decay, 0.0)
    aqk = kq[:, C:] * decay
    tinv = _unit_lower_inverse(L)
    egc = jnp.exp(gc_col)
    uw = _bmm(tinv, jnp.concatenate([vb, kb * egc], axis=2))
    auw = _bmm(aqk, uw)
    g_last = gc_col[:, C - 1:C, :]
    k_dec = k * jnp.exp(g_last - gc_col)
    kuw = _bmm_tn(k_dec, uw)
    qd_ref[...] = (q * egc - auw[:, :, DV:]).reshape(H, NC, C, DK)
    ov_ref[...] = auw[:, :, :DV].reshape(H, NC, C, DV)
    pg_ref[...] = kuw[:, :, DV:].reshape(H, NC, DK, DK)
    qg_ref[...] = kuw[:, :, :DV].reshape(H, NC, DK, DV)
    egl_ref[...] = jnp.broadcast_to(jnp.exp(g_last), (N, 1, LANES)).reshape(H, NC, 1, LANES)

    for c in range(NC):
        S = s_ref[...]
        oc_ref[:, c] = _bmm(qd_ref[:, c], S) + ov_ref[:, c]
        s_ref[...] = S * egl_ref[:, c] - _bmm(pg_ref[:, c], S) + qg_ref[:, c]

    o = oc_ref[...].reshape(H, TB, DV)
    o = o * lax.rsqrt(jnp.mean(o * o, axis=-1, keepdims=True) + RMS_EPS) * nw_ref[...]
    z = _split_heads(z_ref[0], H, DV)
    o_ref[0] = _merge_heads(o * (z * _sigmoid(z)))


def gdn_mixer(proj3, conv_w, a_log, dt_bias, norm_w, tb=512):
    B, S, _ = proj3.shape
    H, DK, DV, C = GDN_HEADS, GDN_DK, GDN_DV, CHUNK
    W = H * DK
    NC = tb // C
    nq, nk, nv, nz, nab = EV_Q // W, EV_K // W, EV_V // W, EV_Z // W, (EV_SMALL + SMALL_AB) // LANES
    rows8 = tb // SUBLANES

    def cur(cb):
        return pl.BlockSpec((1, tb, W), lambda b, i, cb=cb: (b, i, cb))

    def prev(cb):
        return pl.BlockSpec((1, SUBLANES, W), lambda b, i, cb=cb: (b, jnp.maximum(i * rows8 - 1, 0), cb))

    cw = conv_w.reshape(GDN_CONV, 3, W).transpose(1, 0, 2)
    return pl.pallas_call(
        _gdn_kernel,
        grid=(B, S // tb),
        in_specs=[cur(nq), cur(nk), cur(nv), cur(nz),
                  pl.BlockSpec((1, tb, LANES), lambda b, i: (b, i, nab)),
                  prev(nq), prev(nk), prev(nv),
                  _resident((3, GDN_CONV, W)), _resident((H, 1, 1)), _resident((H, 1, 1)),
                  _resident((1, 1, DV))],
        out_specs=pl.BlockSpec((1, tb, H * DV), lambda b, i: (b, i, 0)),
        out_shape=jax.ShapeDtypeStruct((B, S, H * DV), F32),
        scratch_shapes=[pltpu.VMEM((tb + SUBLANES, W), F32),
                        pltpu.VMEM((H, DK, DV), F32),
                        pltpu.VMEM((H, NC, C, DK), F32),
                        pltpu.VMEM((H, NC, DK, DK), F32),
                        pltpu.VMEM((H, NC, DK, DV), F32),
                        pltpu.VMEM((H, NC, C, DV), F32),
                        pltpu.VMEM((H, NC, 1, LANES), F32),
                        pltpu.VMEM((H, NC, C, DV), F32)],
        compiler_params=_cparams(("parallel", "arbitrary")),
        name="gdn_mixer",
    )(proj3, proj3, proj3, proj3, proj3, proj3, proj3, proj3, cw,
      a_log.reshape(H, 1, 1), dt_bias.reshape(H, 1, 1), norm_w.reshape(1, 1, DV))


def _rwkv_kernel(*refs, has_vres):
    if has_vres:
        (r_ref, k_ref, v_ref, lo_ref, rp_ref, kp_ref, vp_ref, lop_ref, vf_ref,
         mu_ref, mul_ref, vec_ref, w2_ref, a2_ref, g2_ref, v2_ref,
         y_ref, buf_ref, bufl_ref, s_ref, rr_ref, pp_ref, qq_ref, yv_ref, egc_ref, yc_ref) = refs
    else:
        (r_ref, k_ref, v_ref, lo_ref, rp_ref, kp_ref, vp_ref, lop_ref,
         mu_ref, mul_ref, vec_ref, w2_ref, a2_ref, g2_ref,
         y_ref, vf_out_ref, buf_ref, bufl_ref, s_ref, rr_ref, pp_ref, qq_ref, yv_ref, egc_ref, yc_ref) = refs
    H, D, C = RWKV_HEADS, RWKV_HEAD, CHUNK
    TB = r_ref.shape[1]
    NC = TB // C
    first = pl.program_id(1) == 0

    @pl.when(first)
    def _():
        s_ref[...] = jnp.zeros_like(s_ref)

    def mix(cur_ref, prev_ref, mu, buf):
        cur = cur_ref[0]
        (sh,) = _shifted_rows(buf, cur, prev_ref[0], first, (1,))
        return cur + mu * (sh - cur)

    r = mix(r_ref, rp_ref, mu_ref[0:1], buf_ref)
    k = mix(k_ref, kp_ref, mu_ref[1:2], buf_ref)
    v = mix(v_ref, vp_ref, mu_ref[2:3], buf_ref)
    lo = mix(lo_ref, lop_ref, mul_ref[...], bufl_ref)
    lo0, lo1 = lo[:, :LANES], lo[:, LANES:]
    w0, a0, k_k, k_a, r_k, ln_w, ln_b = (vec_ref[j:j + 1] for j in range(7))

    w_log = -_softplus(-(w0 + jnp.dot(jnp.tanh(lo0), w2_ref[...], preferred_element_type=F32))) - 0.5
    lw = -jnp.exp(w_log)
    a = _sigmoid(a0 + jnp.dot(lo0, a2_ref[...], preferred_element_type=F32))
    gate = jnp.dot(_sigmoid(lo1), g2_ref[...], preferred_element_type=F32)
    if has_vres:
        v0 = vec_ref[7:8]
        v = v + (vf_ref[0] - v) * _sigmoid(v0 + jnp.dot(lo0, v2_ref[...], preferred_element_type=F32))
    else:
        vf_out_ref[0] = v

    ri = lax.broadcasted_iota(jnp.int32, (C, C), 0)
    ci = lax.broadcasted_iota(jnp.int32, (C, C), 1)
    tril = (ri >= ci).astype(BF16)
    lw_hi = lw.astype(BF16)
    lw_rest = lw - lw_hi.astype(F32)
    lw_mid = lw_rest.astype(BF16)
    lw_lo = (lw_rest - lw_mid.astype(F32)).astype(BF16)
    gi = jnp.concatenate(
        [sum(jnp.dot(tril, part[c * C:(c + 1) * C], preferred_element_type=F32) for part in (lw_hi, lw_mid, lw_lo))
         for c in range(NC)], axis=0)

    kk = _split_heads(k * k_k, H, D)
    kk = kk * lax.rsqrt(jnp.sum(kk * kk, axis=-1, keepdims=True) + L2_EPS)
    k = k * (1.0 + (a - 1.0) * k_a)
    r_h = _split_heads(r, H, D)
    k_h = _split_heads(k, H, D)
    v_h = _split_heads(v, H, D)
    a_h = _split_heads(a, H, D)
    gi_h = _split_heads(gi, H, D)
    lw_h = _split_heads(lw, H, D)
    bonus = jnp.sum(r_h * k_h * _split_heads(r_k, H, D), axis=-1, keepdims=True) * v_h

    N = H * NC
    rc, kc, vc, kkc, ac, gic, lwc = (t.reshape(N, C, D) for t in (r_h, k_h, v_h, kk, a_h, gi_h, lw_h))
    g_end = gic[:, C - 1:C, :]
    e_neg = jnp.exp(-gic)
    e_tail = jnp.exp(g_end - gic)
    at = -kkc * jnp.exp(gic - lwc)
    bvec = kkc * ac
    rt = rc * jnp.exp(gic)
    bt = bvec * e_neg
    kt = kc * e_neg
    b_dec = bvec * e_tail
    k_dec = kc * e_tail
    strict = ri > ci
    causal = ri >= ci
    amat = _bmm_nt(jnp.concatenate([at, rt], axis=1), jnp.concatenate([bt, kt], axis=1))
    a_ab = jnp.where(strict, amat[:, :C, :C], 0.0)
    a_ak = jnp.where(strict, amat[:, :C, C:], 0.0)
    a_r = jnp.concatenate([jnp.where(causal, amat[:, C:, :C], 0.0), jnp.where(causal, amat[:, C:, C:], 0.0)], axis=2)
    tinv = _unit_lower_inverse(-a_ab)
    x2 = _bmm(tinv, jnp.concatenate([at, _bmm(a_ak, vc)], axis=2))
    xv = jnp.concatenate([x2, jnp.concatenate([jnp.zeros_like(vc), vc], axis=2)], axis=1)
    y2 = _bmm(a_r, xv)
    pq = _bmm_tn(xv, jnp.concatenate([b_dec, k_dec], axis=1))
    rr_ref[...] = (rt + y2[:, :, :D]).reshape(H, NC, C, D)
    yv_ref[...] = y2[:, :, D:].reshape(H, NC, C, D)
    pp_ref[...] = pq[:, :D].reshape(H, NC, D, D)
    qq_ref[...] = pq[:, D:].reshape(H, NC, D, D)
    egc_ref[...] = jnp.exp(g_end).reshape(H, NC, 1, D)

    for c in range(NC):
        S = s_ref[...]
        yc_ref[:, c] = _bmm_nt(rr_ref[:, c], S) + yv_ref[:, c]
        s_ref[...] = S * egc_ref[:, c] + _bmm(S, pp_ref[:, c]) + qq_ref[:, c]

    y = yc_ref[...].reshape(H, TB, D)
    mean = jnp.mean(y, axis=-1, keepdims=True)
    yc = y - mean
    var = jnp.mean(yc * yc, axis=-1, keepdims=True)
    y = _merge_heads(yc * lax.rsqrt(var + RWKV_LN_EPS)) * ln_w + ln_b
    y_ref[0] = (y + _merge_heads(bonus)) * gate


def rwkv_mixer(proj3, mu, mu_lora, vecs, w2, a2, g2, v_first=None, v2=None, tb=256):
    B, S, _ = proj3.shape
    H, D, C = RWKV_HEADS, RWKV_HEAD, CHUNK
    W = RWKV_DIM
    WL = 2 * LANES
    NC = tb // C
    rows8 = tb // SUBLANES
    has_vres = v_first is not None

    def cur(col, width):
        return pl.BlockSpec((1, tb, width), lambda b, i, cb=col // width: (b, i, cb))

    def prev(col, width):
        return pl.BlockSpec((1, SUBLANES, width),
                            lambda b, i, cb=col // width: (b, jnp.maximum(i * rows8 - 1, 0), cb))

    in_specs = [cur(EV_R, W), cur(EV_RK, W), cur(EV_RV, W), cur(EV_SMALL, WL),
                prev(EV_R, W), prev(EV_RK, W), prev(EV_RV, W), prev(EV_SMALL, WL)]
    args = [proj3] * 8
    if has_vres:
        in_specs.append(pl.BlockSpec((1, tb, W), lambda b, i: (b, i, 0)))
        args.append(v_first)
    in_specs += [_resident(mu.shape), _resident(mu_lora.shape), _resident(vecs.shape),
                 _resident(w2.shape), _resident(a2.shape), _resident(g2.shape)]
    args += [mu, mu_lora, vecs, w2, a2, g2]
    if has_vres:
        in_specs.append(_resident(v2.shape))
        args.append(v2)
    out_block = pl.BlockSpec((1, tb, W), lambda b, i: (b, i, 0))
    out_sds = jax.ShapeDtypeStruct((B, S, W), F32)
    res = pl.pallas_call(
        functools.partial(_rwkv_kernel, has_vres=has_vres),
        grid=(B, S // tb),
        in_specs=in_specs,
        out_specs=out_block if has_vres else [out_block, out_block],
        out_shape=out_sds if has_vres else [out_sds, out_sds],
        scratch_shapes=[pltpu.VMEM((tb + SUBLANES, W), F32),
                        pltpu.VMEM((tb + SUBLANES, WL), F32),
                        pltpu.VMEM((H, D, D), F32),
                        pltpu.VMEM((H, NC, C, D), F32),
                        pltpu.VMEM((H, NC, D, D), F32),
                        pltpu.VMEM((H, NC, D, D), F32),
                        pltpu.VMEM((H, NC, C, D), F32),
                        pltpu.VMEM((H, NC, 1, D), F32),
                        pltpu.VMEM((H, NC, C, D), F32)],
        compiler_params=_cparams(("parallel", "arbitrary")),
        name="rwkv_mixer",
    )(*args)
    if has_vres:
        return res, v_first
    return res[0], res[1]


def _dil_attn_kernel(q_ref, kc_ref, kp_ref, vc_ref, vp_ref, o_ref, lse_ref, kbuf, vbuf):
    Dh, Bk = DIL_HEAD_DIM, ATT_BLOCK
    RR, QB = q_ref.shape[1], q_ref.shape[2]
    has_prev = pl.program_id(2) > 0
    qi = lax.broadcasted_iota(jnp.int32, (Bk, 2 * Bk), 0)
    cj = lax.broadcasted_iota(jnp.int32, (Bk, 2 * Bk), 1)
    band = (cj >= qi) & (cj <= qi + Bk)
    lane = lax.broadcasted_iota(jnp.int32, (Bk, LANES), 1)
    low = lane < Dh
    low_kv = lax.broadcasted_iota(jnp.int32, (2 * Bk, LANES), 1) < Dh
    one_kv = jnp.ones((2 * Bk, LANES), BF16)
    NP = DIL_WIDTH // LANES

    def pairs(x):
        return jnp.stack([x[:, hp * LANES:(hp + 1) * LANES] for hp in range(NP)], axis=0)

    for rr, j in [(rr, j) for rr in range(RR) for j in range(QB // Bk)]:
        if j == 0:
            kbuf[0:Bk] = kp_ref[0, rr]
            kbuf[Bk:] = kc_ref[0, rr]
            vbuf[0:Bk] = vp_ref[0, rr]
            vbuf[Bk:] = vc_ref[0, rr]
        ok = band if j > 0 else band & ((cj >= Bk) | has_prev)
        ok2 = jnp.concatenate([ok, ok], axis=0)
        q_pairs = pairs(q_ref[0, rr, j * Bk:(j + 1) * Bk, :])
        zero = jnp.zeros_like(q_pairs)
        q2 = jnp.concatenate([jnp.where(low, q_pairs, zero), jnp.where(low, zero, q_pairs)], axis=1)
        k_pairs = pairs(kbuf[j * Bk:(j + 2) * Bk, :])
        v_pairs = pairs(vbuf[j * Bk:(j + 2) * Bk, :])
        s = jnp.einsum('hqd,hkd->hqk', q2, k_pairs, preferred_element_type=F32)
        s = jnp.where(ok2, s, -jnp.inf)
        m = jnp.max(s, axis=-1, keepdims=True).reshape(2 * NP, Bk, 1)
        p = jnp.exp2(s - m.reshape(NP, 2 * Bk, 1)).astype(BF16).reshape(2 * NP, Bk, 2 * Bk)
        v_ext = jnp.stack([jnp.where(low_kv, v_pairs, one_kv), jnp.where(low_kv, one_kv, v_pairs)], axis=1)
        po = jnp.einsum('hqk,hkd->hqd', p, v_ext.reshape(2 * NP, 2 * Bk, LANES), preferred_element_type=F32)
        swapped = pltpu.roll(po.reshape(2 * NP * Bk, LANES), Dh, axis=1).reshape(2 * NP, Bk, LANES)
        r = po / swapped
        o_ref[0, rr, j * Bk:(j + 1) * Bk, :] = jnp.concatenate(
            [jnp.where(low, r[2 * hp], r[2 * hp + 1]) for hp in range(NP)], axis=-1).astype(o_ref.dtype)
        lse_blk = jnp.zeros((Bk, LANES), F32)
        for h in range(2 * NP):
            lse_blk = jnp.where(lane == h, m[h] + jnp.log2((swapped if h % 2 == 0 else po)[h]), lse_blk)
        lse_ref[0, rr, j * Bk:(j + 1) * Bk, :] = lse_blk


def dil_attention_group(q, k, v, step_tokens=8 * ATT_BLOCK):
    B, d, L, W = q.shape
    qb = min(step_tokens, L)
    rr = min(step_tokens // qb, d)
    nj = qb // ATT_BLOCK
    cur = pl.BlockSpec((1, rr, qb, W), lambda b, r, n: (b, r, n, 0))
    prev = pl.BlockSpec((1, rr, ATT_BLOCK, W), lambda b, r, n: (b, r, jnp.maximum(n * nj - 1, 0), 0))
    return pl.pallas_call(
        _dil_attn_kernel,
        grid=(B, d // rr, L // qb),
        in_specs=[cur, cur, prev, cur, prev],
        out_specs=[cur, pl.BlockSpec((1, rr, qb, LANES), lambda b, r, n: (b, r, n, 0))],
        out_shape=[jax.ShapeDtypeStruct((B, d, L, W), BF16), jax.ShapeDtypeStruct((B, d, L, LANES), F32)],
        scratch_shapes=[pltpu.VMEM((ATT_BLOCK + qb, W), BF16), pltpu.VMEM((ATT_BLOCK + qb, W), BF16)],
        compiler_params=_cparams(("parallel", "parallel", "arbitrary")),
        name=f"dil_attn_d{d}",
    )(q, k, k, v, v)


def _merged_attention(o_refs, l_refs, e_ref, a_ref, b_ref, la_ref, lb_ref):
    dils = [d for _, d in DIL_PATTERNS]
    lses = [_interleave_load(r, d, la_ref, lb_ref) for r, d in zip(l_refs, dils)]
    m = jnp.maximum(jnp.maximum(lses[0], lses[1]), lses[2])
    es = [jnp.exp2(l - m) for l in lses]
    inv = 1.0 / (es[0] + es[1] + es[2])
    expand = e_ref[...]
    mixed = None
    for e, o_ref, d in zip(es, o_refs, dils):
        term = _group_sum(e * inv, expand) * _interleave_load(o_ref, d, a_ref, b_ref)
        mixed = term if mixed is None else mixed + term
    return mixed


def _head_expand_matrix():
    expand = np.zeros((LANES, DIL_WIDTH), np.float32)
    for h in range(DIL_HEADS):
        expand[h, h * DIL_HEAD_DIM:(h + 1) * DIL_HEAD_DIM] = 1.0
    return jnp.asarray(expand, BF16)


MOE_TM = 512
MOE_RB = 144
MOE_NBLK = (MOE_TM + N_GROUPS * (MOE_RB - 1)) // MOE_RB
LOGIT_ROWS = pl.cdiv(N_GROUPS * (1 + EXPERTS_PER_GROUP), SUBLANES) * SUBLANES


def _moe_route(h, wr_ref, before_ref):
    NG, EPG, RB = N_GROUPS, EXPERTS_PER_GROUP, MOE_RB
    tm = h.shape[0]
    hb, h_lo = _hi_lo(h)
    nt = (((1,), (1,)), ((), ()))
    lg = lax.dot_general(wr_ref[...], hb, nt, preferred_element_type=F32)
    lg = lg[:LANES] + lg[LANES:] + lax.dot_general(wr_ref[:LANES], h_lo, nt, preferred_element_type=F32)
    L = lg[:LOGIT_ROWS]
    r = lax.broadcasted_iota(jnp.int32, L.shape, 0)
    big = jnp.int32(1 << 30)
    gl = jnp.where(r < NG, L, -jnp.inf)
    gmax = jnp.max(gl, axis=0, keepdims=True)
    gsel = jnp.min(jnp.where(gl == gmax, r, big), axis=0, keepdims=True)
    p_group = 1.0 / jnp.sum(jnp.where(r < NG, jnp.exp(L - gmax), 0.0), axis=0, keepdims=True)
    lo = NG + gsel * EPG
    el = jnp.where((r >= lo) & (r < lo + EPG), L, -jnp.inf)
    v1 = jnp.max(el, axis=0, keepdims=True)
    i1 = jnp.min(jnp.where(el == v1, r, big), axis=0, keepdims=True)
    el2 = jnp.where(r == i1, -jnp.inf, el)
    v2 = jnp.max(el2, axis=0, keepdims=True)
    i2 = jnp.min(jnp.where(el2 == v2, r, big), axis=0, keepdims=True)
    t = jnp.exp(v2 - v1)
    gates = jnp.where(r == i1, p_group / (1.0 + t), 0.0) + jnp.where(r == i2, p_group * t / (1.0 + t), 0.0)
    gate4 = [jnp.sum(jnp.where(r == lo + j, gates, 0.0), axis=0, keepdims=True) for j in range(EPG)]

    r8 = lax.broadcasted_iota(jnp.int32, (SUBLANES, tm), 0)
    onehot = (r8 == gsel).astype(F32)
    rank = jnp.dot(onehot.astype(BF16), before_ref[...], preferred_element_type=F32)
    count = jnp.sum(onehot, axis=1, keepdims=True).astype(jnp.int32)
    nblk = sum((count > j * RB).astype(jnp.int32) for j in range(pl.cdiv(tm, RB)))
    padded = (nblk * RB).astype(F32)
    start, starts = jnp.zeros((1, 1), F32), []
    for g in range(NG):
        starts.append(start)
        start = start + padded[g:g + 1]
    start8 = jnp.concatenate(starts + [jnp.zeros((SUBLANES - NG, 1), F32)], axis=0)
    dest = jnp.sum(onehot * (start8 + rank), axis=0, keepdims=True)
    return dest, gate4, nblk


def _moe_kernel(x_ref, *refs, final_norm, mixer_proj):
    NG, EPG, RB, NBLK = N_GROUPS, EXPERTS_PER_GROUP, MOE_RB, MOE_NBLK
    x = x_ref[...]
    if mixer_proj == "even":
        ya_ref, yb_ref, wa_ref, wb_ref = refs[:4]
        refs = refs[4:]
        x = (x + jnp.dot(ya_ref[...].astype(BF16), wa_ref[...], preferred_element_type=F32)
             + jnp.dot(yb_ref[...].astype(BF16), wb_ref[...], preferred_element_type=F32))
    elif mixer_proj == "odd":
        mixed = _merged_attention(refs[0:3], refs[3:6], refs[6], *refs[-4:])
        x = x + jnp.dot(mixed.astype(BF16), refs[7][...], preferred_element_type=F32)
        refs = refs[8:-4]
    g_ref, wr_ref, before_ref, w1_ref, w3_ref, w2_ref, fg_ref, o_ref, p_ref, hb_ref, g2_ref, hh_ref, ys_ref = refs
    FF = w1_ref.shape[3]
    tm = x_ref.shape[0]
    h = _rmsnorm(x, g_ref[...])
    dest, gate4, nblk = _moe_route(h, wr_ref, before_ref)
    rows = lax.broadcasted_iota(jnp.int32, (NBLK * RB, tm), 0)
    p_ref[...] = jnp.where(rows == dest.astype(jnp.int32), 1.0, 0.0).astype(BF16)
    hb_ref[...] = h.astype(BF16)
    g4 = jnp.concatenate(gate4 + [jnp.zeros((SUBLANES - EPG, tm), F32)], axis=0)
    g_hi = g4.astype(BF16).astype(F32)
    g2_ref[...] = jnp.concatenate([g_hi, g4 - g_hi, jnp.zeros((LANES - 2 * SUBLANES, tm), F32)], axis=0).astype(BF16)

    ends, acc = [], jnp.int32(0)
    for g in range(NG):
        acc = acc + jnp.sum(nblk[g:g + 1])
        ends.append(acc)
    for b in range(NBLK):
        rs = slice(b * RB, (b + 1) * RB)
        grp = sum((b >= e).astype(jnp.int32) for e in ends[:-1])

        @pl.when(b < ends[-1])
        def _():
            pb = p_ref[rs]
            xr = jnp.dot(pb, hb_ref[...], preferred_element_type=F32).astype(BF16)
            gate = lax.dot_general(pb, g2_ref[...], (((1,), (1,)), ((), ())), preferred_element_type=F32)
            for j in range(EPG):
                a = jnp.dot(xr, w1_ref[0, grp * EPG + j], preferred_element_type=F32)
                u = jnp.dot(xr, w3_ref[0, grp * EPG + j], preferred_element_type=F32)
                gj = gate[:, j:j + 1] + gate[:, SUBLANES + j:SUBLANES + j + 1]
                hh_ref[:, j * FF:(j + 1) * FF] = ((a * _sigmoid(a)) * u * gj).astype(BF16)
            ys_ref[rs] = jnp.dot(hh_ref[...], w2_ref[0, grp], preferred_element_type=F32).astype(BF16)

        @pl.when(b >= ends[-1])
        def _():
            ys_ref[rs] = jnp.zeros((RB, ys_ref.shape[1]), BF16)

    def scatter_back(n_blocks):
        rows_used = slice(0, n_blocks * RB)
        y = x + lax.dot_general(p_ref[rows_used], ys_ref[rows_used], (((0,), (0,)), ((), ())),
                                preferred_element_type=F32)
        o_ref[...] = _rmsnorm(y, fg_ref[...]) if final_norm else y

    usual = NG + 1
    pl.when(ends[-1] <= usual)(lambda: scatter_back(usual))
    pl.when(ends[-1] > usual)(lambda: scatter_back(NBLK))


def moe_block(x, gain, w_router, w1, w3, w2, layer, final_gain=None, mixer=None):
    T, D = x.shape
    _, NE, _, FF = w1.shape
    tm, RB, NBLK = MOE_TM, MOE_RB, MOE_NBLK
    final_norm = final_gain is not None
    fg = (final_gain if final_norm else jnp.ones((D,), F32)).reshape(1, D)

    def of_layer(w):
        return pl.BlockSpec((1,) + w.shape[1:], lambda i: (layer, 0, 0, 0), pipeline_mode=pl.Buffered(1))

    mixer_specs, mixer_args, mixer_scratch = [], [], []
    kind = None if mixer is None else mixer[0]
    if kind == "even":
        _, ya, yb, wa, wb = mixer
        mixer_specs = [pl.BlockSpec((tm, ya.shape[1]), lambda i: (i, 0)), pl.BlockSpec((tm, yb.shape[1]), lambda i: (i, 0)),
                       _resident(wa.shape), _resident(wb.shape)]
        mixer_args = [ya, yb, wa, wb]
    elif kind == "odd":
        _, outs, lses, w_out = mixer
        tiles_per_seq = outs[0].shape[1] * outs[0].shape[2] // tm
        for width, arrs in ((DIL_WIDTH, outs), (LANES, lses)):
            for (_, d), arr in zip(DIL_PATTERNS, arrs):
                mixer_specs.append(pl.BlockSpec((1, d, tm // d, width),
                                                lambda i: (i // tiles_per_seq, 0, i % tiles_per_seq, 0)))
                mixer_args.append(arr)
        mixer_specs += [_resident((LANES, DIL_WIDTH)), _resident(w_out.shape)]
        mixer_args += [_head_expand_matrix(), w_out]
        n_tiles = DIL_WIDTH // LANES
        mixer_scratch = [pltpu.VMEM((n_tiles, tm, LANES), F32), pltpu.VMEM((n_tiles, tm, LANES), F32),
                         pltpu.VMEM((1, tm, LANES), F32), pltpu.VMEM((1, tm, LANES), F32)]
    return pl.pallas_call(
        functools.partial(_moe_kernel, final_norm=final_norm, mixer_proj=kind),
        grid=(T // tm,),
        in_specs=[pl.BlockSpec((tm, D), lambda i: (i, 0)), *mixer_specs, _resident((1, D)), _resident(w_router.shape),
                  _resident((tm, tm)), of_layer(w1), of_layer(w3), of_layer(w2), _resident((1, D))],
        out_specs=pl.BlockSpec((tm, D), lambda i: (i, 0)),
        out_shape=jax.ShapeDtypeStruct((T, D), F32),
        scratch_shapes=[pltpu.VMEM((NBLK * RB, tm), BF16), pltpu.VMEM((tm, D), BF16),
                        pltpu.VMEM((LANES, tm), BF16), pltpu.VMEM((RB, EXPERTS_PER_GROUP * FF), BF16),
                        pltpu.VMEM((NBLK * RB, D), BF16), *mixer_scratch],
        compiler_params=_cparams(("parallel",)),
        name="moe_block",
    )(x, *mixer_args, gain.reshape(1, D), w_router, jnp.asarray(np.triu(np.ones((tm, tm), np.float32), 1), BF16),
      w1, w3, w2, fg)


def _router_weights(w_group, w_expert):
    w = _place([(0, w_group)] + [(N_GROUPS + g * EXPERTS_PER_GROUP, w_expert[g]) for g in range(N_GROUPS)], LANES).T
    return jnp.concatenate(_hi_lo(w), axis=0)


def _place(cols, total):
    lead, dtype = cols[0][1].shape[:-1], cols[0][1].dtype
    parts, pos = [], 0
    for off, arr in sorted(cols, key=lambda c: c[0]):
        if off > pos:
            parts.append(jnp.zeros(lead + (off - pos,), dtype))
        parts.append(arr)
        pos = off + arr.shape[-1]
    if total > pos:
        parts.append(jnp.zeros(lead + (total - pos,), dtype))
    return jnp.concatenate(parts, axis=-1)


def _even_layout(t, vres=None):
    gw = GDN_HEADS * GDN_DK
    sizes = [gw, gw, gw, gw, GDN_HEADS, GDN_HEADS, RWKV_DIM, RWKV_DIM, RWKV_DIM, DECAY_LORA, AAA_LORA, GATE_LORA]
    gq, gk, gv, gz, ga, gb, rr, rk, rv, lw, la, lg = jnp.split(t, np.cumsum(sizes)[:-1].tolist(), axis=-1)
    cols = [(EV_Q, gq), (EV_K, gk), (EV_V, gv), (EV_Z, gz), (EV_R, rr), (EV_RK, rk), (EV_RV, rv),
            (EV_SMALL + SMALL_W, lw), (EV_SMALL + SMALL_A, la), (EV_SMALL + SMALL_G, lg),
            (EV_SMALL + SMALL_AB, ga), (EV_SMALL + SMALL_AB + GDN_HEADS, gb)]
    if vres is not None:
        cols.append((EV_SMALL + SMALL_V, vres))
    return _place(cols, EV_NPAD)


def _lora_up(w, lane):
    return jnp.concatenate([jnp.zeros((lane, w.shape[1]), F32), w,
                            jnp.zeros((LANES - lane - w.shape[0], w.shape[1]), F32)], axis=0)


def kernel(x, positions, ev_norm, ev_w_in, rwkv_vres_down, ev_w_out, gdn_conv_w, gdn_A_log, gdn_dt_bias, gdn_norm,
           rwkv_mu, rwkv_w0, rwkv_w2, rwkv_a0, rwkv_a2, rwkv_g2, rwkv_k_k, rwkv_k_a, rwkv_r_k, rwkv_ln_w, rwkv_ln_b,
           rwkv_vres_mu, rwkv_v0, rwkv_v2, od_norm, od_w_in, od_w_out, ffn_norm, moe_w_group, moe_w_expert,
           moe_w1, moe_w3, moe_w2, final_norm):
    B, S, D = x.shape
    T = B * S
    depth = ffn_norm.shape[0]
    xf = x.reshape(T, D)
    cos, sin = rope_tables(positions.reshape(T))
    n_groups = len(DIL_PATTERNS)
    moe_w1_bf = moe_w1.astype(BF16)
    moe_w3_bf = moe_w3.astype(BF16)
    moe_w2_bf = moe_w2.astype(BF16).reshape(depth, N_GROUPS, -1, D)
    v_first = None
    for layer in range(depth):
        i = layer // 2
        if layer % 2 == 0:
            vres_w = None if i == 0 else rwkv_vres_down[i - 1].astype(BF16)
            w_in = _even_layout(ev_w_in[i].astype(BF16), vres_w)
            proj3 = norm_proj(xf, ev_norm[i], w_in).reshape(B, S, EV_NPAD)
            mu_r, mu_k, mu_v, mu_w, mu_a, mu_g = jnp.split(
                rwkv_mu[i], np.cumsum([RWKV_DIM, RWKV_DIM, RWKV_DIM, DECAY_LORA, AAA_LORA]).tolist())
            mu = jnp.stack([mu_r, mu_k, mu_v])
            lora_cols = [(SMALL_W, mu_w), (SMALL_A, mu_a), (SMALL_G, mu_g)]
            if i > 0:
                lora_cols.append((SMALL_V, rwkv_vres_mu[i - 1]))
            mu_lora = _place(lora_cols, 2 * LANES).reshape(1, 2 * LANES)
            vec_rows = [rwkv_w0[i], rwkv_a0[i], rwkv_k_k[i], rwkv_k_a[i], rwkv_r_k[i].reshape(-1), rwkv_ln_w[i], rwkv_ln_b[i]]
            if i > 0:
                vec_rows.append(rwkv_v0[i - 1])
            vecs = jnp.stack(vec_rows)
            w2 = _lora_up(rwkv_w2[i], SMALL_W)
            a2 = _lora_up(rwkv_a2[i], SMALL_A)
            g2 = _lora_up(rwkv_g2[i], SMALL_G - LANES)
            ya = gdn_mixer(proj3, gdn_conv_w[i], gdn_A_log[i], gdn_dt_bias[i], gdn_norm[i])
            if i == 0:
                yb, v_first = rwkv_mixer(proj3, mu, mu_lora, vecs, w2, a2, g2)
            else:
                yb, _ = rwkv_mixer(proj3, mu, mu_lora, vecs, w2, a2, g2, v_first, _lora_up(rwkv_v2[i - 1], SMALL_V))
            w_out = ev_w_out[i].astype(BF16)
            na = GDN_HEADS * GDN_DV
            mixer = ("even", ya.reshape(T, na), yb.reshape(T, RWKV_DIM), w_out[:na], w_out[na:])
        else:
            w_in = od_w_in[i].astype(BF16)
            qkv = norm_proj_rope(xf.reshape(B, S, D), od_norm[i], w_in, cos.reshape(B, S, LANES),
                                 sin.reshape(B, S, LANES), DIL_HEAD_DIM ** -0.5 * math.log2(math.e))
            outs, lses = [], []
            for gi, (window, dilation) in enumerate(DIL_PATTERNS):
                assert window // dilation == ATT_BLOCK and (S // dilation) % (2 * ATT_BLOCK) == 0
                o, lse = dil_attention_group(qkv[gi], qkv[n_groups + gi], qkv[2 * n_groups + gi])
                outs.append(o)
                lses.append(lse)
            mixer = ("odd", outs, lses, od_w_out[i].astype(BF16))
        w_router = _router_weights(moe_w_group[layer], moe_w_expert[layer])
        xf = moe_block(xf, ffn_norm[layer], w_router, moe_w1_bf, moe_w3_bf, moe_w2_bf, layer,
                       final_norm if layer == depth - 1 else None, mixer)
    return xf.reshape(B, S, D)
```

```python
import functools
import math

import jax
import jax.numpy as jnp
import numpy as np
from jax import lax
from jax.experimental import pallas as pl
from jax.experimental.pallas import tpu as pltpu

F32 = jnp.float32
BF16 = jnp.bfloat16

RMS_EPS = 1e-6
L2_EPS = 1e-6
RWKV_LN_EPS = 64e-5
ROPE_THETA = 10000.0

LANES = 128
SUBLANES = 8
VMEM_LIMIT = 56 * 1024 * 1024

GDN_HEADS = 4
GDN_DK = 128
GDN_DV = 128
GDN_CONV = 4
RWKV_HEADS = 8
RWKV_HEAD = 64
RWKV_DIM = RWKV_HEADS * RWKV_HEAD
DECAY_LORA, AAA_LORA, MV_LORA, GATE_LORA = 32, 32, 32, 96
DIL_PATTERNS = ((128, 1), (512, 4), (2048, 16))
DIL_HEADS = 8
DIL_HEAD_DIM = 64
DIL_WIDTH = DIL_HEADS * DIL_HEAD_DIM
ATT_BLOCK = 128
N_GROUPS = 4
EXPERTS_PER_GROUP = 4
CHUNK = 64
INV_BLOCK = 16

EV_Q, EV_K, EV_V, EV_Z = 0, 512, 1024, 1536
EV_R, EV_RK, EV_RV, EV_SMALL = 2048, 2560, 3072, 3584
SMALL_W, SMALL_A, SMALL_V = 0, DECAY_LORA, DECAY_LORA + AAA_LORA
SMALL_AB, SMALL_G = LANES, LANES + 2 * GDN_HEADS
EV_NPAD = EV_SMALL + 2 * LANES


def _cparams(sem):
    return pltpu.CompilerParams(dimension_semantics=sem, vmem_limit_bytes=VMEM_LIMIT)


def _sigmoid(x):
    return 0.5 * jnp.tanh(0.5 * x) + 0.5


def _softplus(x):
    return jnp.maximum(x, 0.0) + jnp.log(1.0 + jnp.exp(-jnp.abs(x)))


def _rmsnorm(x, gain):
    return x * lax.rsqrt(jnp.mean(x * x, axis=-1, keepdims=True) + RMS_EPS) * gain


def _hi_lo(v):
    hi = v.astype(BF16)
    return hi, (v - hi.astype(F32)).astype(BF16)


def _group_sum(x, block_ones):
    hi, lo = _hi_lo(x)
    return (jnp.dot(hi, block_ones, preferred_element_type=F32)
            + jnp.dot(lo, block_ones, preferred_element_type=F32))


def _resident(shape):
    nd = len(shape)
    return pl.BlockSpec(shape, lambda *_: (0,) * nd, pipeline_mode=pl.Buffered(1))


def _rope_table_kernel(pos_ref, freq_ref, sign_ref, cos_ref, sin_ref):
    ang = pos_ref[...].astype(F32) * freq_ref[...]
    cos_ref[...] = jnp.cos(ang)
    sin_ref[...] = jnp.sin(ang) * sign_ref[...]


def rope_tables(positions_flat, tm=1024):
    T = positions_flat.shape[0]
    half = DIL_HEAD_DIM // 2
    inv_freq = ROPE_THETA ** (-jnp.arange(half, dtype=F32) * 2.0 / DIL_HEAD_DIM)
    lane = np.arange(LANES)
    freq = inv_freq[lane % half][None, :]
    sign = jnp.asarray(np.where(lane % DIL_HEAD_DIM < half, -1.0, 1.0), F32)[None, :]
    return pl.pallas_call(
        _rope_table_kernel,
        grid=(T // tm,),
        in_specs=[pl.BlockSpec((tm, 1), lambda i: (i, 0)), _resident((1, LANES)), _resident((1, LANES))],
        out_specs=[pl.BlockSpec((tm, LANES), lambda i: (i, 0))] * 2,
        out_shape=[jax.ShapeDtypeStruct((T, LANES), F32)] * 2,
        compiler_params=_cparams(("parallel",)),
        name="rope_tables",
    )(positions_flat.reshape(T, 1), freq, sign)


def _norm_proj_kernel(x_ref, g_ref, w_ref, o_ref):
    h = _rmsnorm(x_ref[...], g_ref[...]).astype(BF16)
    o_ref[...] = jnp.dot(h, w_ref[...], preferred_element_type=F32)


def norm_proj(x, gain, w, tm=512):
    T, D = x.shape
    N = w.shape[1]
    return pl.pallas_call(
        _norm_proj_kernel,
        grid=(T // tm,),
        in_specs=[pl.BlockSpec((tm, D), lambda i: (i, 0)), _resident((1, D)), _resident((D, N))],
        out_specs=pl.BlockSpec((tm, N), lambda i: (i, 0)),
        out_shape=jax.ShapeDtypeStruct((T, N), F32),
        compiler_params=_cparams(("parallel",)),
        name="norm_proj",
    )(x, gain.reshape(1, D), w)


def _deinterleave_store(out_ref, val, d, a_ref, b_ref):
    tm = val.shape[0]
    if d == 1:
        out_ref[0, 0] = val.astype(out_ref.dtype)
        return
    quarter = tm // 4
    for j in range(val.shape[1] // LANES):
        cs = slice(j * LANES, (j + 1) * LANES)
        a_ref[j] = val[:, cs]
        if d == 4:
            for r in range(4):
                out_ref[0, r, :, cs] = a_ref[j, pl.ds(r, quarter, stride=4), :].astype(out_ref.dtype)
            continue
        assert d == 16
        for r1 in range(4):
            b_ref[j, r1 * quarter:(r1 + 1) * quarter, :] = a_ref[j, pl.ds(r1, quarter, stride=4), :]
        for r1 in range(4):
            for r2 in range(4):
                out_ref[0, r1 + 4 * r2, :, cs] = (
                    b_ref[j, pl.ds(r1 * quarter + r2, quarter // 4, stride=4), :].astype(out_ref.dtype))


def _interleave_load(in_ref, d, a_ref, b_ref):
    if d == 1:
        return in_ref[0, 0].astype(F32)
    n_tiles, tm, _ = a_ref.shape
    quarter = tm // 4
    for j in range(n_tiles):
        cs = slice(j * LANES, (j + 1) * LANES)
        if d == 4:
            for r in range(4):
                a_ref[j, pl.ds(r, quarter, stride=4), :] = in_ref[0, r, :, cs].astype(F32)
            continue
        assert d == 16
        for r1 in range(4):
            for r2 in range(4):
                b_ref[j, pl.ds(r1 * quarter + r2, quarter // 4, stride=4), :] = in_ref[0, r1 + 4 * r2, :, cs].astype(F32)
        for r1 in range(4):
            a_ref[j, pl.ds(r1, quarter, stride=4), :] = b_ref[j, r1 * quarter:(r1 + 1) * quarter, :]
    return jnp.concatenate([a_ref[j] for j in range(n_tiles)], axis=-1)


def _norm_proj_rope_kernel(x_ref, g_ref, w_ref, cos_ref, sin_ref, *rest, q_scale):
    out_refs, (a_ref, b_ref) = rest[:-2], rest[-2:]
    n_groups = len(DIL_PATTERNS)
    h = _rmsnorm(x_ref[0], g_ref[...]).astype(BF16)
    cos = cos_ref[0]
    sin = sin_ref[0]
    lane = lax.broadcasted_iota(jnp.int32, cos.shape, 1)
    first_half = (lane % DIL_HEAD_DIM) < (DIL_HEAD_DIM // 2)
    W = DIL_WIDTH
    for c, out_ref in enumerate(out_refs):
        which, g = divmod(c, n_groups)
        acc = jnp.dot(h, w_ref[:, c * W:(c + 1) * W], preferred_element_type=F32)
        if which < 2:
            parts = []
            for j in range(W // LANES):
                blk = acc[:, j * LANES:(j + 1) * LANES]
                half = DIL_HEAD_DIM // 2
                rot = jnp.where(first_half, pltpu.roll(blk, LANES - half, axis=1), pltpu.roll(blk, half, axis=1))
                out = blk * cos + rot * sin
                parts.append(out * q_scale if which == 0 else out)
            acc = jnp.concatenate(parts, axis=-1)
        _deinterleave_store(out_ref, acc, DIL_PATTERNS[g][1], a_ref, b_ref)


def norm_proj_rope(x3, gain, w, cos3, sin3, q_scale, tm=1024):
    B, S, D = x3.shape
    W = DIL_WIDTH
    out_specs, out_shape = [], []
    for _ in range(3):
        for _, d in DIL_PATTERNS:
            out_specs.append(pl.BlockSpec((1, d, tm // d, W), lambda b, i: (b, 0, i, 0)))
            out_shape.append(jax.ShapeDtypeStruct((B, d, S // d, W), BF16))
    return pl.pallas_call(
        functools.partial(_norm_proj_rope_kernel, q_scale=q_scale),
        grid=(B, S // tm),
        in_specs=[pl.BlockSpec((1, tm, D), lambda b, i: (b, i, 0)), _resident((1, D)), _resident(w.shape),
                  pl.BlockSpec((1, tm, LANES), lambda b, i: (b, i, 0)),
                  pl.BlockSpec((1, tm, LANES), lambda b, i: (b, i, 0))],
        out_specs=out_specs,
        out_shape=out_shape,
        scratch_shapes=[pltpu.VMEM((W // LANES, tm, LANES), F32), pltpu.VMEM((W // LANES, tm, LANES), F32)],
        compiler_params=_cparams(("parallel", "parallel")),
        name="norm_proj_rope",
    )(x3, gain.reshape(1, D), w, cos3, sin3)


def _bmm(a, b):
    return jnp.einsum('nik,nkj->nij', a.astype(BF16), b.astype(BF16), preferred_element_type=F32)


def _bmm_nt(a, b):
    return jnp.einsum('nik,njk->nij', a.astype(BF16), b.astype(BF16), preferred_element_type=F32)


def _bmm_tn(a, b):
    return jnp.einsum('nci,ncj->nij', a.astype(BF16), b.astype(BF16), preferred_element_type=F32)


def _unit_lower_inverse(L):
    C = L.shape[-1]
    ri = lax.broadcasted_iota(jnp.int32, (C, C), 0)
    ci = lax.broadcasted_iota(jnp.int32, (C, C), 1)

    def same_block(size):
        return (ri // size) == (ci // size)

    eye = (ri == ci).astype(F32)
    Ld = jnp.where(same_block(INV_BLOCK), L, 0.0)
    X = eye - Ld
    P = Ld
    for _ in range(int(math.log2(INV_BLOCK)) - 1):
        P = _bmm(P, P)
        X = X + _bmm(X, P)
    size = INV_BLOCK
    while size < C:
        off = jnp.where(same_block(2 * size) & ~same_block(size), L, 0.0)
        X = X - _bmm(_bmm(X, off), X)
        size *= 2
    return X


def _shifted_rows(buf_ref, cur, prev, first, shifts):
    TB = cur.shape[0]
    buf_ref[0:SUBLANES, :] = jnp.where(first, 0.0, prev)
    buf_ref[SUBLANES:, :] = cur
    return [buf_ref[SUBLANES - s:SUBLANES - s + TB, :] for s in shifts]


def _split_heads(x, n_heads, width):
    return jnp.stack([x[:, h * width:(h + 1) * width] for h in range(n_heads)], axis=0)


def _merge_heads(x):
    return jnp.concatenate([x[h] for h in range(x.shape[0])], axis=-1)


def _gdn_kernel(q_ref, k_ref, v_ref, z_ref, ab_ref, qp_ref, kp_ref, vp_ref, cw_ref, alog_ref, dtb_ref, nw_ref,
                o_ref, buf_ref, s_ref, qd_ref, pg_ref, qg_ref, ov_ref, egl_ref, oc_ref):
    HH, DK, DV, C = GDN_HEADS, GDN_DK, GDN_DV, CHUNK
    BB, TB = q_ref.shape[0], q_ref.shape[1]
    H = BB * HH
    NC = TB // C
    first = pl.program_id(1) == 0

    @pl.when(first)
    def _():
        s_ref[...] = jnp.zeros_like(s_ref)

    def conv_silu(cur_ref, prev_ref, j, width):
        outs = []
        for bb in range(BB):
            cur = cur_ref[bb]
            x3, x2, x1 = _shifted_rows(buf_ref, cur, prev_ref[bb], first, (3, 2, 1))
            w = cw_ref[j]
            y = x3 * w[0:1] + x2 * w[1:2] + x1 * w[2:3] + cur * w[3:4]
            outs.append(_split_heads(y * _sigmoid(y), HH, width))
        return jnp.concatenate(outs, axis=0)

    q = conv_silu(q_ref, qp_ref, 0, DK)
    k = conv_silu(k_ref, kp_ref, 1, DK)
    v = conv_silu(v_ref, vp_ref, 2, DV)
    q = q * lax.rsqrt(jnp.sum(q * q, axis=-1, keepdims=True) + L2_EPS) * (DK ** -0.5)
    k = k * lax.rsqrt(jnp.sum(k * k, axis=-1, keepdims=True) + L2_EPS)
    a = jnp.stack([ab_ref[bb][:, h:h + 1] for bb in range(BB) for h in range(HH)], axis=0)
    b = jnp.stack([ab_ref[bb][:, HH + h:HH + h + 1] for bb in range(BB) for h in range(HH)], axis=0)
    beta = _sigmoid(b)
    per_head = lambda ref: jnp.concatenate([ref[...]] * BB, axis=0)
    g = -jnp.exp(per_head(alog_ref)) * _softplus(a + per_head(dtb_ref))

    N = H * NC
    q = q.reshape(N, C, DK)
    k = k.reshape(N, C, DK)
    v = v.reshape(N, C, DV)
    beta = beta.reshape(N, C, 1)
    g = g.reshape(N, C, 1)

    ri = lax.broadcasted_iota(jnp.int32, (C, C), 0)
    ci = lax.broadcasted_iota(jnp.int32, (C, C), 1)
    causal = ri >= ci
    strict = ri > ci
    gb = jnp.broadcast_to(g, (N, C, C))
    g_row = jnp.sum(jnp.where(ri == ci, gb, 0.0), axis=1, keepdims=True)
    gc_col = jnp.sum(jnp.where(causal, jnp.broadcast_to(g_row, (N, C, C)), 0.0), axis=2, keepdims=True)
    gc_row = jnp.sum(jnp.where(ri <= ci, gb, 0.0), axis=1, keepdims=True)
    decay = jnp.where(causal, jnp.exp(jnp.where(causal, gc_col - gc_row, 0.0)), 0.0)

    kb = k * beta
    vb = v * beta
    kq = _bmm_nt(jnp.concatenate([kb, q], axis=1), k)
    L = jnp.where(strict, kq[:, :C] * decay, 0.0)
    aqk = kq[:, C:] * decay
    tinv = _unit_lower_inverse(L)
    egc = jnp.exp(gc_col)
    uw = _bmm(tinv, jnp.concatenate([vb, kb * egc], axis=2))
    auw = _bmm(aqk, uw)
    g_last = gc_col[:, C - 1:C, :]
    k_dec = k * jnp.exp(g_last - gc_col)
    kuw = _bmm_tn(k_dec, uw)
    qd_ref[...] = (q * egc - auw[:, :, DV:]).reshape(H, NC, C, DK)
    ov_ref[...] = auw[:, :, :DV].reshape(H, NC, C, DV)
    pg_ref[...] = kuw[:, :, DV:].reshape(H, NC, DK, DK)
    qg_ref[...] = kuw[:, :, :DV].reshape(H, NC, DK, DV)
    egl_ref[...] = jnp.broadcast_to(jnp.exp(g_last), (N, 1, LANES)).reshape(H, NC, 1, LANES)

    for c in range(NC):
        S = s_ref[...]
        oc_ref[:, c] = _bmm(qd_ref[:, c], S) + ov_ref[:, c]
        s_ref[...] = S * egl_ref[:, c] - _bmm(pg_ref[:, c], S) + qg_ref[:, c]

    o = oc_ref[...].reshape(H, TB, DV)
    o = o * lax.rsqrt(jnp.mean(o * o, axis=-1, keepdims=True) + RMS_EPS) * nw_ref[...]
    for bb in range(BB):
        z = _split_heads(z_ref[bb], HH, DV)
        o_ref[bb] = _merge_heads(o[bb * HH:(bb + 1) * HH] * (z * _sigmoid(z)))


def gdn_mixer(proj3, conv_w, a_log, dt_bias, norm_w, tb=256, seqs=2):
    B, S, _ = proj3.shape
    H, DK, DV, C = GDN_HEADS, GDN_DK, GDN_DV, CHUNK
    W = H * DK
    NC = tb // C
    HS = seqs * H
    nq, nk, nv, nz, nab = EV_Q // W, EV_K // W, EV_V // W, EV_Z // W, (EV_SMALL + SMALL_AB) // LANES
    rows8 = tb // SUBLANES

    def cur(cb):
        return pl.BlockSpec((seqs, tb, W), lambda b, i, cb=cb: (b, i, cb))

    def prev(cb):
        return pl.BlockSpec((seqs, SUBLANES, W), lambda b, i, cb=cb: (b, jnp.maximum(i * rows8 - 1, 0), cb))

    cw = conv_w.reshape(GDN_CONV, 3, W).transpose(1, 0, 2)
    return pl.pallas_call(
        _gdn_kernel,
        grid=(B // seqs, S // tb),
        in_specs=[cur(nq), cur(nk), cur(nv), cur(nz),
                  pl.BlockSpec((seqs, tb, LANES), lambda b, i: (b, i, nab)),
                  prev(nq), prev(nk), prev(nv),
                  _resident((3, GDN_CONV, W)), _resident((H, 1, 1)), _resident((H, 1, 1)),
                  _resident((1, 1, DV))],
        out_specs=pl.BlockSpec((seqs, tb, H * DV), lambda b, i: (b, i, 0)),
        out_shape=jax.ShapeDtypeStruct((B, S, H * DV), F32),
        scratch_shapes=[pltpu.VMEM((tb + SUBLANES, W), F32),
                        pltpu.VMEM((HS, DK, DV), F32),
                        pltpu.VMEM((HS, NC, C, DK), F32),
                        pltpu.VMEM((HS, NC, DK, DK), F32),
                        pltpu.VMEM((HS, NC, DK, DV), F32),
                        pltpu.VMEM((HS, NC, C, DV), F32),
                        pltpu.VMEM((HS, NC, 1, LANES), F32),
                        pltpu.VMEM((HS, NC, C, DV), F32)],
        compiler_params=_cparams(("parallel", "arbitrary")),
        name="gdn_mixer",
    )(proj3, proj3, proj3, proj3, proj3, proj3, proj3, proj3, cw,
      a_log.reshape(H, 1, 1), dt_bias.reshape(H, 1, 1), norm_w.reshape(1, 1, DV))


def _rwkv_kernel(*refs, has_vres):
    if has_vres:
        (r_ref, k_ref, v_ref, lo_ref, rp_ref, kp_ref, vp_ref, lop_ref, vf_ref,
         mu_ref, mul_ref, vec_ref, w2_ref, a2_ref, g2_ref, v2_ref,
         y_ref, buf_ref, bufl_ref, s_ref, rr_ref, pp_ref, qq_ref, yv_ref, egc_ref, yc_ref) = refs
    else:
        (r_ref, k_ref, v_ref, lo_ref, rp_ref, kp_ref, vp_ref, lop_ref,
         mu_ref, mul_ref, vec_ref, w2_ref, a2_ref, g2_ref,
         y_ref, vf_out_ref, buf_ref, bufl_ref, s_ref, rr_ref, pp_ref, qq_ref, yv_ref, egc_ref, yc_ref) = refs
    H, D, C = RWKV_HEADS, RWKV_HEAD, CHUNK
    TB = r_ref.shape[1]
    NC = TB // C
    first = pl.program_id(1) == 0

    @pl.when(first)
    def _():
        s_ref[...] = jnp.zeros_like(s_ref)

    def mix(cur_ref, prev_ref, mu, buf):
        cur = cur_ref[0]
        (sh,) = _shifted_rows(buf, cur, prev_ref[0], first, (1,))
        return cur + mu * (sh - cur)

    r = mix(r_ref, rp_ref, mu_ref[0:1], buf_ref)
    k = mix(k_ref, kp_ref, mu_ref[1:2], buf_ref)
    v = mix(v_ref, vp_ref, mu_ref[2:3], buf_ref)
    lo = mix(lo_ref, lop_ref, mul_ref[...], bufl_ref)
    lo0, lo1 = lo[:, :LANES], lo[:, LANES:]
    w0, a0, k_k, k_a, r_k, ln_w, ln_b = (vec_ref[j:j + 1] for j in range(7))

    w_log = -_softplus(-(w0 + jnp.dot(jnp.tanh(lo0), w2_ref[...], preferred_element_type=F32))) - 0.5
    lw = -jnp.exp(w_log)
    a = _sigmoid(a0 + jnp.dot(lo0, a2_ref[...], preferred_element_type=F32))
    gate = jnp.dot(_sigmoid(lo1), g2_ref[...], preferred_element_type=F32)
    if has_vres:
        v0 = vec_ref[7:8]
        v = v + (vf_ref[0] - v) * _sigmoid(v0 + jnp.dot(lo0, v2_ref[...], preferred_element_type=F32))
    else:
        vf_out_ref[0] = v

    ri = lax.broadcasted_iota(jnp.int32, (C, C), 0)
    ci = lax.broadcasted_iota(jnp.int32, (C, C), 1)
    tril = (ri >= ci).astype(BF16)
    lw_hi = lw.astype(BF16)
    lw_rest = lw - lw_hi.astype(F32)
    lw_mid = lw_rest.astype(BF16)
    lw_lo = (lw_rest - lw_mid.astype(F32)).astype(BF16)
    gi = jnp.concatenate(
        [sum(jnp.dot(tril, part[c * C:(c + 1) * C], preferred_element_type=F32) for part in (lw_hi, lw_mid, lw_lo))
         for c in range(NC)], axis=0)

    kk = _split_heads(k * k_k, H, D)
    kk = kk * lax.rsqrt(jnp.sum(kk * kk, axis=-1, keepdims=True) + L2_EPS)
    k = k * (1.0 + (a - 1.0) * k_a)
    r_h = _split_heads(r, H, D)
    k_h = _split_heads(k, H, D)
    v_h = _split_heads(v, H, D)
    a_h = _split_heads(a, H, D)
    gi_h = _split_heads(gi, H, D)
    lw_h = _split_heads(lw, H, D)
    bonus = jnp.sum(r_h * k_h * _split_heads(r_k, H, D), axis=-1, keepdims=True) * v_h

    N = H * NC
    rc, kc, vc, kkc, ac, gic, lwc = (t.reshape(N, C, D) for t in (r_h, k_h, v_h, kk, a_h, gi_h, lw_h))
    g_end = gic[:, C - 1:C, :]
    e_neg = jnp.exp(-gic)
    e_tail = jnp.exp(g_end - gic)
    at = -kkc * jnp.exp(gic - lwc)
    bvec = kkc * ac
    rt = rc * jnp.exp(gic)
    bt = bvec * e_neg
    kt = kc * e_neg
    b_dec = bvec * e_tail
    k_dec = kc * e_tail
    strict = ri > ci
    causal = ri >= ci
    amat = _bmm_nt(jnp.concatenate([at, rt], axis=1), jnp.concatenate([bt, kt], axis=1))
    a_ab = jnp.where(strict, amat[:, :C, :C], 0.0)
    a_ak = jnp.where(strict, amat[:, :C, C:], 0.0)
    a_r = jnp.concatenate([jnp.where(causal, amat[:, C:, :C], 0.0), jnp.where(causal, amat[:, C:, C:], 0.0)], axis=2)
    tinv = _unit_lower_inverse(-a_ab)
    x2 = _bmm(tinv, jnp.concatenate([at, _bmm(a_ak, vc)], axis=2))
    xv = jnp.concatenate([x2, jnp.concatenate([jnp.zeros_like(vc), vc], axis=2)], axis=1)
    y2 = _bmm(a_r, xv)
    pq = _bmm_tn(xv, jnp.concatenate([b_dec, k_dec], axis=1))
    rr_ref[...] = (rt + y2[:, :, :D]).reshape(H, NC, C, D)
    yv_ref[...] = y2[:, :, D:].reshape(H, NC, C, D)
    pp_ref[...] = pq[:, :D].reshape(H, NC, D, D)
    qq_ref[...] = pq[:, D:].reshape(H, NC, D, D)
    egc_ref[...] = jnp.exp(g_end).reshape(H, NC, 1, D)

    for c in range(NC):
        S = s_ref[...]
        yc_ref[:, c] = _bmm_nt(rr_ref[:, c], S) + yv_ref[:, c]
        s_ref[...] = S * egc_ref[:, c] + _bmm(S, pp_ref[:, c]) + qq_ref[:, c]

    y = yc_ref[...].reshape(H, TB, D)
    mean = jnp.mean(y, axis=-1, keepdims=True)
    yc = y - mean
    var = jnp.mean(yc * yc, axis=-1, keepdims=True)
    y = _merge_heads(yc * lax.rsqrt(var + RWKV_LN_EPS)) * ln_w + ln_b
    y_ref[0] = (y + _merge_heads(bonus)) * gate


def rwkv_mixer(proj3, mu, mu_lora, vecs, w2, a2, g2, v_first=None, v2=None, tb=256):
    B, S, _ = proj3.shape
    H, D, C = RWKV_HEADS, RWKV_HEAD, CHUNK
    W = RWKV_DIM
    WL = 2 * LANES
    NC = tb // C
    rows8 = tb // SUBLANES
    has_vres = v_first is not None

    def cur(col, width):
        return pl.BlockSpec((1, tb, width), lambda b, i, cb=col // width: (b, i, cb))

    def prev(col, width):
        return pl.BlockSpec((1, SUBLANES, width),
                            lambda b, i, cb=col // width: (b, jnp.maximum(i * rows8 - 1, 0), cb))

    in_specs = [cur(EV_R, W), cur(EV_RK, W), cur(EV_RV, W), cur(EV_SMALL, WL),
                prev(EV_R, W), prev(EV_RK, W), prev(EV_RV, W), prev(EV_SMALL, WL)]
    args = [proj3] * 8
    if has_vres:
        in_specs.append(pl.BlockSpec((1, tb, W), lambda b, i: (b, i, 0)))
        args.append(v_first)
    in_specs += [_resident(mu.shape), _resident(mu_lora.shape), _resident(vecs.shape),
                 _resident(w2.shape), _resident(a2.shape), _resident(g2.shape)]
    args += [mu, mu_lora, vecs, w2, a2, g2]
    if has_vres:
        in_specs.append(_resident(v2.shape))
        args.append(v2)
    out_block = pl.BlockSpec((1, tb, W), lambda b, i: (b, i, 0))
    out_sds = jax.ShapeDtypeStruct((B, S, W), F32)
    res = pl.pallas_call(
        functools.partial(_rwkv_kernel, has_vres=has_vres),
        grid=(B, S // tb),
        in_specs=in_specs,
        out_specs=out_block if has_vres else [out_block, out_block],
        out_shape=out_sds if has_vres else [out_sds, out_sds],
        scratch_shapes=[pltpu.VMEM((tb + SUBLANES, W), F32),
                        pltpu.VMEM((tb + SUBLANES, WL), F32),
                        pltpu.VMEM((H, D, D), F32),
                        pltpu.VMEM((H, NC, C, D), F32),
                        pltpu.VMEM((H, NC, D, D), F32),
                        pltpu.VMEM((H, NC, D, D), F32),
                        pltpu.VMEM((H, NC, C, D), F32),
                        pltpu.VMEM((H, NC, 1, D), F32),
                        pltpu.VMEM((H, NC, C, D), F32)],
        compiler_params=_cparams(("parallel", "arbitrary")),
        name="rwkv_mixer",
    )(*args)
    if has_vres:
        return res, v_first
    return res[0], res[1]


def _dil_attn_kernel(q_ref, kc_ref, kp_ref, vc_ref, vp_ref, o_ref, lse_ref, kbuf, vbuf):
    Dh, Bk = DIL_HEAD_DIM, ATT_BLOCK
    RR, QB = q_ref.shape[1], q_ref.shape[2]
    has_prev = pl.program_id(2) > 0
    qi = lax.broadcasted_iota(jnp.int32, (Bk, 2 * Bk), 0)
    cj = lax.broadcasted_iota(jnp.int32, (Bk, 2 * Bk), 1)
    band = (cj >= qi) & (cj <= qi + Bk)
    lane = lax.broadcasted_iota(jnp.int32, (Bk, LANES), 1)
    low = lane < Dh
    low_kv = lax.broadcasted_iota(jnp.int32, (2 * Bk, LANES), 1) < Dh
    one_kv = jnp.ones((2 * Bk, LANES), BF16)
    NP = DIL_WIDTH // LANES

    def pairs(x):
        return jnp.stack([x[:, hp * LANES:(hp + 1) * LANES] for hp in range(NP)], axis=0)

    for rr, j in [(rr, j) for rr in range(RR) for j in range(QB // Bk)]:
        if j == 0:
            kbuf[0:Bk] = kp_ref[0, rr]
            kbuf[Bk:] = kc_ref[0, rr]
            vbuf[0:Bk] = vp_ref[0, rr]
            vbuf[Bk:] = vc_ref[0, rr]
        ok = band if j > 0 else band & ((cj >= Bk) | has_prev)
        ok2 = jnp.concatenate([ok, ok], axis=0)
        q_pairs = pairs(q_ref[0, rr, j * Bk:(j + 1) * Bk, :])
        zero = jnp.zeros_like(q_pairs)
        q2 = jnp.concatenate([jnp.where(low, q_pairs, zero), jnp.where(low, zero, q_pairs)], axis=1)
        k_pairs = pairs(kbuf[j * Bk:(j + 2) * Bk, :])
        v_pairs = pairs(vbuf[j * Bk:(j + 2) * Bk, :])
        s = jnp.einsum('hqd,hkd->hqk', q2, k_pairs, preferred_element_type=F32)
        s = jnp.where(ok2, s, -jnp.inf)
        m = jnp.max(s, axis=-1, keepdims=True).reshape(2 * NP, Bk, 1)
        p = jnp.exp2(s - m.reshape(NP, 2 * Bk, 1)).astype(BF16).reshape(2 * NP, Bk, 2 * Bk)
        v_ext = jnp.stack([jnp.where(low_kv, v_pairs, one_kv), jnp.where(low_kv, one_kv, v_pairs)], axis=1)
        po = jnp.einsum('hqk,hkd->hqd', p, v_ext.reshape(2 * NP, 2 * Bk, LANES), preferred_element_type=F32)
        swapped = pltpu.roll(po.reshape(2 * NP * Bk, LANES), Dh, axis=1).reshape(2 * NP, Bk, LANES)
        r = po / swapped
        o_ref[0, rr, j * Bk:(j + 1) * Bk, :] = jnp.concatenate(
            [jnp.where(low, r[2 * hp], r[2 * hp + 1]) for hp in range(NP)], axis=-1).astype(o_ref.dtype)
        lse_blk = jnp.zeros((Bk, LANES), F32)
        for h in range(2 * NP):
            lse_blk = jnp.where(lane == h, m[h] + jnp.log2((swapped if h % 2 == 0 else po)[h]), lse_blk)
        lse_ref[0, rr, j * Bk:(j + 1) * Bk, :] = lse_blk


def dil_attention_group(q, k, v, step_tokens=8 * ATT_BLOCK):
    B, d, L, W = q.shape
    qb = min(step_tokens, L)
    rr = min(step_tokens // qb, d)
    nj = qb // ATT_BLOCK
    cur = pl.BlockSpec((1, rr, qb, W), lambda b, r, n: (b, r, n, 0))
    prev = pl.BlockSpec((1, rr, ATT_BLOCK, W), lambda b, r, n: (b, r, jnp.maximum(n * nj - 1, 0), 0))
    return pl.pallas_call(
        _dil_attn_kernel,
        grid=(B, d // rr, L // qb),
        in_specs=[cur, cur, prev, cur, prev],
        out_specs=[cur, pl.BlockSpec((1, rr, qb, LANES), lambda b, r, n: (b, r, n, 0))],
        out_shape=[jax.ShapeDtypeStruct((B, d, L, W), BF16), jax.ShapeDtypeStruct((B, d, L, LANES), F32)],
        scratch_shapes=[pltpu.VMEM((ATT_BLOCK + qb, W), BF16), pltpu.VMEM((ATT_BLOCK + qb, W), BF16)],
        compiler_params=_cparams(("parallel", "parallel", "arbitrary")),
        name=f"dil_attn_d{d}",
    )(q, k, k, v, v)


def _merged_attention(o_refs, l_refs, e_ref, a_ref, b_ref, la_ref, lb_ref):
    dils = [d for _, d in DIL_PATTERNS]
    lses = [_interleave_load(r, d, la_ref, lb_ref) for r, d in zip(l_refs, dils)]
    m = jnp.maximum(jnp.maximum(lses[0], lses[1]), lses[2])
    es = [jnp.exp2(l - m) for l in lses]
    inv = 1.0 / (es[0] + es[1] + es[2])
    expand = e_ref[...]
    mixed = None
    for e, o_ref, d in zip(es, o_refs, dils):
        term = _group_sum(e * inv, expand) * _interleave_load(o_ref, d, a_ref, b_ref)
        mixed = term if mixed is None else mixed + term
    return mixed


def _head_expand_matrix():
    expand = np.zeros((LANES, DIL_WIDTH), np.float32)
    for h in range(DIL_HEADS):
        expand[h, h * DIL_HEAD_DIM:(h + 1) * DIL_HEAD_DIM] = 1.0
    return jnp.asarray(expand, BF16)


MOE_TM = 512
MOE_RB = 144
MOE_NBLK = (MOE_TM + N_GROUPS * (MOE_RB - 1)) // MOE_RB
LOGIT_ROWS = pl.cdiv(N_GROUPS * (1 + EXPERTS_PER_GROUP), SUBLANES) * SUBLANES


def _moe_route(h, wr_ref, before_ref):
    NG, EPG, RB = N_GROUPS, EXPERTS_PER_GROUP, MOE_RB
    tm = h.shape[0]
    hb, h_lo = _hi_lo(h)
    nt = (((1,), (1,)), ((), ()))
    lg = lax.dot_general(wr_ref[...], hb, nt, preferred_element_type=F32)
    lg = lg[:LANES] + lg[LANES:] + lax.dot_general(wr_ref[:LANES], h_lo, nt, preferred_element_type=F32)
    L = lg[:LOGIT_ROWS]
    r = lax.broadcasted_iota(jnp.int32, L.shape, 0)
    big = jnp.int32(1 << 30)
    gl = jnp.where(r < NG, L, -jnp.inf)
    gmax = jnp.max(gl, axis=0, keepdims=True)
    gsel = jnp.min(jnp.where(gl == gmax, r, big), axis=0, keepdims=True)
    p_group = 1.0 / jnp.sum(jnp.where(r < NG, jnp.exp(L - gmax), 0.0), axis=0, keepdims=True)
    lo = NG + gsel * EPG
    el = jnp.where((r >= lo) & (r < lo + EPG), L, -jnp.inf)
    v1 = jnp.max(el, axis=0, keepdims=True)
    i1 = jnp.min(jnp.where(el == v1, r, big), axis=0, keepdims=True)
    el2 = jnp.where(r == i1, -jnp.inf, el)
    v2 = jnp.max(el2, axis=0, keepdims=True)
    i2 = jnp.min(jnp.where(el2 == v2, r, big), axis=0, keepdims=True)
    t = jnp.exp(v2 - v1)
    gates = jnp.where(r == i1, p_group / (1.0 + t), 0.0) + jnp.where(r == i2, p_group * t / (1.0 + t), 0.0)
    gate4 = [jnp.sum(jnp.where(r == lo + j, gates, 0.0), axis=0, keepdims=True) for j in range(EPG)]

    r8 = lax.broadcasted_iota(jnp.int32, (SUBLANES, tm), 0)
    onehot = (r8 == gsel).astype(F32)
    rank = jnp.dot(onehot.astype(BF16), before_ref[...], preferred_element_type=F32)
    count = jnp.sum(onehot, axis=1, keepdims=True).astype(jnp.int32)
    nblk = sum((count > j * RB).astype(jnp.int32) for j in range(pl.cdiv(tm, RB)))
    padded = (nblk * RB).astype(F32)
    start, starts = jnp.zeros((1, 1), F32), []
    for g in range(NG):
        starts.append(start)
        start = start + padded[g:g + 1]
    start8 = jnp.concatenate(starts + [jnp.zeros((SUBLANES - NG, 1), F32)], axis=0)
    dest = jnp.sum(onehot * (start8 + rank), axis=0, keepdims=True)
    return dest, gate4, nblk


def _moe_kernel(x_ref, *refs, final_norm, mixer_proj):
    NG, EPG, RB, NBLK = N_GROUPS, EXPERTS_PER_GROUP, MOE_RB, MOE_NBLK
    x = x_ref[...]
    if mixer_proj == "even":
        ya_ref, yb_ref, wa_ref, wb_ref = refs[:4]
        refs = refs[4:]
        x = (x + jnp.dot(ya_ref[...].astype(BF16), wa_ref[...], preferred_element_type=F32)
             + jnp.dot(yb_ref[...].astype(BF16), wb_ref[...], preferred_element_type=F32))
    elif mixer_proj == "odd":
        mixed = _merged_attention(refs[0:3], refs[3:6], refs[6], *refs[-4:])
        x = x + jnp.dot(mixed.astype(BF16), refs[7][...], preferred_element_type=F32)
        refs = refs[8:-4]
    g_ref, wr_ref, before_ref, w1_ref, w3_ref, w2_ref, fg_ref, o_ref, p_ref, hb_ref, g2_ref, hh_ref, ys_ref = refs
    FF = w1_ref.shape[3]
    tm = x_ref.shape[0]
    h = _rmsnorm(x, g_ref[...])
    dest, gate4, nblk = _moe_route(h, wr_ref, before_ref)
    rows = lax.broadcasted_iota(jnp.int32, (NBLK * RB, tm), 0)
    p_ref[...] = jnp.where(rows == dest.astype(jnp.int32), 1.0, 0.0).astype(BF16)
    hb_ref[...] = h.astype(BF16)
    g4 = jnp.concatenate(gate4 + [jnp.zeros((SUBLANES - EPG, tm), F32)], axis=0)
    g_hi = g4.astype(BF16).astype(F32)
    g2_ref[...] = jnp.concatenate([g_hi, g4 - g_hi, jnp.zeros((LANES - 2 * SUBLANES, tm), F32)], axis=0).astype(BF16)

    ends, acc = [], jnp.int32(0)
    for g in range(NG):
        acc = acc + jnp.sum(nblk[g:g + 1])
        ends.append(acc)
    for b in range(NBLK):
        rs = slice(b * RB, (b + 1) * RB)
        grp = sum((b >= e).astype(jnp.int32) for e in ends[:-1])

        @pl.when(b < ends[-1])
        def _():
            pb = p_ref[rs]
            xr = jnp.dot(pb, hb_ref[...], preferred_element_type=F32).astype(BF16)
            gate = lax.dot_general(pb, g2_ref[...], (((1,), (1,)), ((), ())), preferred_element_type=F32)
            for j in range(EPG):
                a = jnp.dot(xr, w1_ref[0, grp * EPG + j], preferred_element_type=F32)
                u = jnp.dot(xr, w3_ref[0, grp * EPG + j], preferred_element_type=F32)
                gj = gate[:, j:j + 1] + gate[:, SUBLANES + j:SUBLANES + j + 1]
                hh_ref[:, j * FF:(j + 1) * FF] = ((a * _sigmoid(a)) * u * gj).astype(BF16)
            ys_ref[rs] = jnp.dot(hh_ref[...], w2_ref[0, grp], preferred_element_type=F32).astype(BF16)

        @pl.when(b >= ends[-1])
        def _():
            ys_ref[rs] = jnp.zeros((RB, ys_ref.shape[1]), BF16)

    def scatter_back(n_blocks):
        rows_used = slice(0, n_blocks * RB)
        y = x + lax.dot_general(p_ref[rows_used], ys_ref[rows_used], (((0,), (0,)), ((), ())),
                                preferred_element_type=F32)
        o_ref[...] = _rmsnorm(y, fg_ref[...]) if final_norm else y

    usual = NG + 1
    pl.when(ends[-1] <= usual)(lambda: scatter_back(usual))
    pl.when(ends[-1] > usual)(lambda: scatter_back(NBLK))


def moe_block(x, gain, w_router, w1, w3, w2, layer, final_gain=None, mixer=None):
    T, D = x.shape
    _, NE, _, FF = w1.shape
    tm, RB, NBLK = MOE_TM, MOE_RB, MOE_NBLK
    final_norm = final_gain is not None
    fg = (final_gain if final_norm else jnp.ones((D,), F32)).reshape(1, D)

    def of_layer(w):
        return pl.BlockSpec((1,) + w.shape[1:], lambda i: (layer, 0, 0, 0), pipeline_mode=pl.Buffered(1))

    mixer_specs, mixer_args, mixer_scratch = [], [], []
    kind = None if mixer is None else mixer[0]
    if kind == "even":
        _, ya, yb, wa, wb = mixer
        mixer_specs = [pl.BlockSpec((tm, ya.shape[1]), lambda i: (i, 0)), pl.BlockSpec((tm, yb.shape[1]), lambda i: (i, 0)),
                       _resident(wa.shape), _resident(wb.shape)]
        mixer_args = [ya, yb, wa, wb]
    elif kind == "odd":
        _, outs, lses, w_out = mixer
        tiles_per_seq = outs[0].shape[1] * outs[0].shape[2] // tm
        for width, arrs in ((DIL_WIDTH, outs), (LANES, lses)):
            for (_, d), arr in zip(DIL_PATTERNS, arrs):
                mixer_specs.append(pl.BlockSpec((1, d, tm // d, width),
                                                lambda i: (i // tiles_per_seq, 0, i % tiles_per_seq, 0)))
                mixer_args.append(arr)
        mixer_specs += [_resident((LANES, DIL_WIDTH)), _resident(w_out.shape)]
        mixer_args += [_head_expand_matrix(), w_out]
        n_tiles = DIL_WIDTH // LANES
        mixer_scratch = [pltpu.VMEM((n_tiles, tm, LANES), F32), pltpu.VMEM((n_tiles, tm, LANES), F32),
                         pltpu.VMEM((1, tm, LANES), F32), pltpu.VMEM((1, tm, LANES), F32)]
    return pl.pallas_call(
        functools.partial(_moe_kernel, final_norm=final_norm, mixer_proj=kind),
        grid=(T // tm,),
        in_specs=[pl.BlockSpec((tm, D), lambda i: (i, 0)), *mixer_specs, _resident((1, D)), _resident(w_router.shape),
                  _resident((tm, tm)), of_layer(w1), of_layer(w3), of_layer(w2), _resident((1, D))],
        out_specs=pl.BlockSpec((tm, D), lambda i: (i, 0)),
        out_shape=jax.ShapeDtypeStruct((T, D), F32),
        scratch_shapes=[pltpu.VMEM((NBLK * RB, tm), BF16), pltpu.VMEM((tm, D), BF16),
                        pltpu.VMEM((LANES, tm), BF16), pltpu.VMEM((RB, EXPERTS_PER_GROUP * FF), BF16),
                        pltpu.VMEM((NBLK * RB, D), BF16), *mixer_scratch],
        compiler_params=_cparams(("parallel",)),
        name="moe_block",
    )(x, *mixer_args, gain.reshape(1, D), w_router, jnp.asarray(np.triu(np.ones((tm, tm), np.float32), 1), BF16),
      w1, w3, w2, fg)


def _router_weights(w_group, w_expert):
    w = _place([(0, w_group)] + [(N_GROUPS + g * EXPERTS_PER_GROUP, w_expert[g]) for g in range(N_GROUPS)], LANES).T
    return jnp.concatenate(_hi_lo(w), axis=0)


def _place(cols, total):
    lead, dtype = cols[0][1].shape[:-1], cols[0][1].dtype
    parts, pos = [], 0
    for off, arr in sorted(cols, key=lambda c: c[0]):
        if off > pos:
            parts.append(jnp.zeros(lead + (off - pos,), dtype))
        parts.append(arr)
        pos = off + arr.shape[-1]
    if total > pos:
        parts.append(jnp.zeros(lead + (total - pos,), dtype))
    return jnp.concatenate(parts, axis=-1)


def _even_layout(t, vres=None):
    gw = GDN_HEADS * GDN_DK
    sizes = [gw, gw, gw, gw, GDN_HEADS, GDN_HEADS, RWKV_DIM, RWKV_DIM, RWKV_DIM, DECAY_LORA, AAA_LORA, GATE_LORA]
    gq, gk, gv, gz, ga, gb, rr, rk, rv, lw, la, lg = jnp.split(t, np.cumsum(sizes)[:-1].tolist(), axis=-1)
    cols = [(EV_Q, gq), (EV_K, gk), (EV_V, gv), (EV_Z, gz), (EV_R, rr), (EV_RK, rk), (EV_RV, rv),
            (EV_SMALL + SMALL_W, lw), (EV_SMALL + SMALL_A, la), (EV_SMALL + SMALL_G, lg),
            (EV_SMALL + SMALL_AB, ga), (EV_SMALL + SMALL_AB + GDN_HEADS, gb)]
    if vres is not None:
        cols.append((EV_SMALL + SMALL_V, vres))
    return _place(cols, EV_NPAD)


def _lora_up(w, lane):
    return jnp.concatenate([jnp.zeros((lane, w.shape[1]), F32), w,
                            jnp.zeros((LANES - lane - w.shape[0], w.shape[1]), F32)], axis=0)


def kernel(x, positions, ev_norm, ev_w_in, rwkv_vres_down, ev_w_out, gdn_conv_w, gdn_A_log, gdn_dt_bias, gdn_norm,
           rwkv_mu, rwkv_w0, rwkv_w2, rwkv_a0, rwkv_a2, rwkv_g2, rwkv_k_k, rwkv_k_a, rwkv_r_k, rwkv_ln_w, rwkv_ln_b,
           rwkv_vres_mu, rwkv_v0, rwkv_v2, od_norm, od_w_in, od_w_out, ffn_norm, moe_w_group, moe_w_expert,
           moe_w1, moe_w3, moe_w2, final_norm):
    B, S, D = x.shape
    T = B * S
    depth = ffn_norm.shape[0]
    xf = x.reshape(T, D)
    cos, sin = rope_tables(positions.reshape(T))
    n_groups = len(DIL_PATTERNS)
    moe_w1_bf = moe_w1.astype(BF16)
    moe_w3_bf = moe_w3.astype(BF16)
    moe_w2_bf = moe_w2.astype(BF16).reshape(depth, N_GROUPS, -1, D)
    v_first = None
    for layer in range(depth):
        i = layer // 2
        if layer % 2 == 0:
            vres_w = None if i == 0 else rwkv_vres_down[i - 1].astype(BF16)
            w_in = _even_layout(ev_w_in[i].astype(BF16), vres_w)
            proj3 = norm_proj(xf, ev_norm[i], w_in).reshape(B, S, EV_NPAD)
            mu_r, mu_k, mu_v, mu_w, mu_a, mu_g = jnp.split(
                rwkv_mu[i], np.cumsum([RWKV_DIM, RWKV_DIM, RWKV_DIM, DECAY_LORA, AAA_LORA]).tolist())
            mu = jnp.stack([mu_r, mu_k, mu_v])
            lora_cols = [(SMALL_W, mu_w), (SMALL_A, mu_a), (SMALL_G, mu_g)]
            if i > 0:
                lora_cols.append((SMALL_V, rwkv_vres_mu[i - 1]))
            mu_lora = _place(lora_cols, 2 * LANES).reshape(1, 2 * LANES)
            vec_rows = [rwkv_w0[i], rwkv_a0[i], rwkv_k_k[i], rwkv_k_a[i], rwkv_r_k[i].reshape(-1), rwkv_ln_w[i], rwkv_ln_b[i]]
            if i > 0:
                vec_rows.append(rwkv_v0[i - 1])
            vecs = jnp.stack(vec_rows)
            w2 = _lora_up(rwkv_w2[i], SMALL_W)
            a2 = _lora_up(rwkv_a2[i], SMALL_A)
            g2 = _lora_up(rwkv_g2[i], SMALL_G - LANES)
            ya = gdn_mixer(proj3, gdn_conv_w[i], gdn_A_log[i], gdn_dt_bias[i], gdn_norm[i])
            if i == 0:
                yb, v_first = rwkv_mixer(proj3, mu, mu_lora, vecs, w2, a2, g2)
            else:
                yb, _ = rwkv_mixer(proj3, mu, mu_lora, vecs, w2, a2, g2, v_first, _lora_up(rwkv_v2[i - 1], SMALL_V))
            w_out = ev_w_out[i].astype(BF16)
            na = GDN_HEADS * GDN_DV
            mixer = ("even", ya.reshape(T, na), yb.reshape(T, RWKV_DIM), w_out[:na], w_out[na:])
        else:
            w_in = od_w_in[i].astype(BF16)
            qkv = norm_proj_rope(xf.reshape(B, S, D), od_norm[i], w_in, cos.reshape(B, S, LANES),
                                 sin.reshape(B, S, LANES), DIL_HEAD_DIM ** -0.5 * math.log2(math.e))
            outs, lses = [], []
            for gi, (window, dilation) in enumerate(DIL_PATTERNS):
                assert window // dilation == ATT_BLOCK and (S // dilation) % (2 * ATT_BLOCK) == 0
                o, lse = dil_attention_group(qkv[gi], qkv[n_groups + gi], qkv[2 * n_groups + gi])
                outs.append(o)
                lses.append(lse)
            mixer = ("odd", outs, lses, od_w_out[i].astype(BF16))
        w_router = _router_weights(moe_w_group[layer], moe_w_expert[layer])
        xf = moe_block(xf, ffn_norm[layer], w_router, moe_w1_bf, moe_w3_bf, moe_w2_bf, layer,
                       final_norm if layer == depth - 1 else None, mixer)
    return xf.reshape(B, S, D)
```

```python
import functools
import math

import jax
import jax.numpy as jnp
import numpy as np
from jax import lax
from jax.experimental import pallas as pl
from jax.experimental.pallas import tpu as pltpu

F32 = jnp.float32
BF16 = jnp.bfloat16

RMS_EPS = 1e-6
L2_EPS = 1e-6
RWKV_LN_EPS = 64e-5
ROPE_THETA = 10000.0

LANES = 128
SUBLANES = 8
VMEM_LIMIT = 56 * 1024 * 1024

GDN_HEADS = 4
GDN_DK = 128
GDN_DV = 128
GDN_CONV = 4
RWKV_HEADS = 8
RWKV_HEAD = 64
RWKV_DIM = RWKV_HEADS * RWKV_HEAD
DECAY_LORA, AAA_LORA, MV_LORA, GATE_LORA = 32, 32, 32, 96
DIL_PATTERNS = ((128, 1), (512, 4), (2048, 16))
DIL_HEADS = 8
DIL_HEAD_DIM = 64
DIL_WIDTH = DIL_HEADS * DIL_HEAD_DIM
ATT_BLOCK = 128
N_GROUPS = 4
EXPERTS_PER_GROUP = 4
CHUNK = 64
INV_BLOCK = 16

EV_Q, EV_K, EV_V, EV_Z = 0, 512, 1024, 1536
EV_R, EV_RK, EV_RV, EV_SMALL = 2048, 2560, 3072, 3584
SMALL_W, SMALL_A, SMALL_V = 0, DECAY_LORA, DECAY_LORA + AAA_LORA
SMALL_AB, SMALL_G = LANES, LANES + 2 * GDN_HEADS
EV_NPAD = EV_SMALL + 2 * LANES


def _cparams(sem):
    return pltpu.CompilerParams(dimension_semantics=sem, vmem_limit_bytes=VMEM_LIMIT)


def _sigmoid(x):
    return 0.5 * jnp.tanh(0.5 * x) + 0.5


def _softplus(x):
    return jnp.maximum(x, 0.0) + jnp.log(1.0 + jnp.exp(-jnp.abs(x)))


def _rmsnorm(x, gain):
    return x * lax.rsqrt(jnp.mean(x * x, axis=-1, keepdims=True) + RMS_EPS) * gain


def _hi_lo(v):
    hi = v.astype(BF16)
    return hi, (v - hi.astype(F32)).astype(BF16)


def _group_sum(x, block_ones):
    hi, lo = _hi_lo(x)
    return (jnp.dot(hi, block_ones, preferred_element_type=F32)
            + jnp.dot(lo, block_ones, preferred_element_type=F32))


def _resident(shape):
    nd = len(shape)
    return pl.BlockSpec(shape, lambda *_: (0,) * nd, pipeline_mode=pl.Buffered(1))


def _rope_table_kernel(pos_ref, freq_ref, sign_ref, cos_ref, sin_ref):
    ang = pos_ref[...].astype(F32) * freq_ref[...]
    cos_ref[...] = jnp.cos(ang)
    sin_ref[...] = jnp.sin(ang) * sign_ref[...]


def rope_tables(positions_flat, tm=1024):
    T = positions_flat.shape[0]
    half = DIL_HEAD_DIM // 2
    inv_freq = ROPE_THETA ** (-jnp.arange(half, dtype=F32) * 2.0 / DIL_HEAD_DIM)
    lane = np.arange(LANES)
    freq = inv_freq[lane % half][None, :]
    sign = jnp.asarray(np.where(lane % DIL_HEAD_DIM < half, -1.0, 1.0), F32)[None, :]
    return pl.pallas_call(
        _rope_table_kernel,
        grid=(T // tm,),
        in_specs=[pl.BlockSpec((tm, 1), lambda i: (i, 0)), _resident((1, LANES)), _resident((1, LANES))],
        out_specs=[pl.BlockSpec((tm, LANES), lambda i: (i, 0))] * 2,
        out_shape=[jax.ShapeDtypeStruct((T, LANES), F32)] * 2,
        compiler_params=_cparams(("parallel",)),
        name="rope_tables",
    )(positions_flat.reshape(T, 1), freq, sign)


def _norm_proj_kernel(x_ref, g_ref, w_ref, o_ref):
    h = _rmsnorm(x_ref[...], g_ref[...]).astype(BF16)
    o_ref[...] = jnp.dot(h, w_ref[...], preferred_element_type=F32)


def norm_proj(x, gain, w, tm=512):
    T, D = x.shape
    N = w.shape[1]
    return pl.pallas_call(
        _norm_proj_kernel,
        grid=(T // tm,),
        in_specs=[pl.BlockSpec((tm, D), lambda i: (i, 0)), _resident((1, D)), _resident((D, N))],
        out_specs=pl.BlockSpec((tm, N), lambda i: (i, 0)),
        out_shape=jax.ShapeDtypeStruct((T, N), F32),
        compiler_params=_cparams(("parallel",)),
        name="norm_proj",
    )(x, gain.reshape(1, D), w)


def _deinterleave_store(out_ref, val, d, a_ref, b_ref):
    tm = val.shape[0]
    if d == 1:
        out_ref[0, 0] = val.astype(out_ref.dtype)
        return
    quarter = tm // 4
    for j in range(val.shape[1] // LANES):
        cs = slice(j * LANES, (j + 1) * LANES)
        a_ref[j] = val[:, cs]
        if d == 4:
            for r in range(4):
                out_ref[0, r, :, cs] = a_ref[j, pl.ds(r, quarter, stride=4), :].astype(out_ref.dtype)
            continue
        assert d == 16
        for r1 in range(4):
            b_ref[j, r1 * quarter:(r1 + 1) * quarter, :] = a_ref[j, pl.ds(r1, quarter, stride=4), :]
        for r1 in range(4):
            for r2 in range(4):
                out_ref[0, r1 + 4 * r2, :, cs] = (
                    b_ref[j, pl.ds(r1 * quarter + r2, quarter // 4, stride=4), :].astype(out_ref.dtype))


def _interleave_load(in_ref, d, a_ref, b_ref):
    if d == 1:
        return in_ref[0, 0].astype(F32)
    n_tiles, tm, _ = a_ref.shape
    quarter = tm // 4
    for j in range(n_tiles):
        cs = slice(j * LANES, (j + 1) * LANES)
        if d == 4:
            for r in range(4):
                a_ref[j, pl.ds(r, quarter, stride=4), :] = in_ref[0, r, :, cs].astype(F32)
            continue
        assert d == 16
        for r1 in range(4):
            for r2 in range(4):
                b_ref[j, pl.ds(r1 * quarter + r2, quarter // 4, stride=4), :] = in_ref[0, r1 + 4 * r2, :, cs].astype(F32)
        for r1 in range(4):
            a_ref[j, pl.ds(r1, quarter, stride=4), :] = b_ref[j, r1 * quarter:(r1 + 1) * quarter, :]
    return jnp.concatenate([a_ref[j] for j in range(n_tiles)], axis=-1)


def _norm_proj_rope_kernel(x_ref, g_ref, w_ref, cos_ref, sin_ref, *rest, q_scale):
    out_refs, (a_ref, b_ref) = rest[:-2], rest[-2:]
    n_groups = len(DIL_PATTERNS)
    h = _rmsnorm(x_ref[0], g_ref[...]).astype(BF16)
    cos = cos_ref[0]
    sin = sin_ref[0]
    lane = lax.broadcasted_iota(jnp.int32, cos.shape, 1)
    first_half = (lane % DIL_HEAD_DIM) < (DIL_HEAD_DIM // 2)
    W = DIL_WIDTH
    for c, out_ref in enumerate(out_refs):
        which, g = divmod(c, n_groups)
        acc = jnp.dot(h, w_ref[:, c * W:(c + 1) * W], preferred_element_type=F32)
        if which < 2:
            parts = []
            for j in range(W // LANES):
                blk = acc[:, j * LANES:(j + 1) * LANES]
                half = DIL_HEAD_DIM // 2
                rot = jnp.where(first_half, pltpu.roll(blk, LANES - half, axis=1), pltpu.roll(blk, half, axis=1))
                out = blk * cos + rot * sin
                parts.append(out * q_scale if which == 0 else out)
            acc = jnp.concatenate(parts, axis=-1)
        _deinterleave_store(out_ref, acc, DIL_PATTERNS[g][1], a_ref, b_ref)


def norm_proj_rope(x3, gain, w, cos3, sin3, q_scale, tm=1024):
    B, S, D = x3.shape
    W = DIL_WIDTH
    out_specs, out_shape = [], []
    for _ in range(3):
        for _, d in DIL_PATTERNS:
            out_specs.append(pl.BlockSpec((1, d, tm // d, W), lambda b, i: (b, 0, i, 0)))
            out_shape.append(jax.ShapeDtypeStruct((B, d, S // d, W), BF16))
    return pl.pallas_call(
        functools.partial(_norm_proj_rope_kernel, q_scale=q_scale),
        grid=(B, S // tm),
        in_specs=[pl.BlockSpec((1, tm, D), lambda b, i: (b, i, 0)), _resident((1, D)), _resident(w.shape),
                  pl.BlockSpec((1, tm, LANES), lambda b, i: (b, i, 0)),
                  pl.BlockSpec((1, tm, LANES), lambda b, i: (b, i, 0))],
        out_specs=out_specs,
        out_shape=out_shape,
        scratch_shapes=[pltpu.VMEM((W // LANES, tm, LANES), F32), pltpu.VMEM((W // LANES, tm, LANES), F32)],
        compiler_params=_cparams(("parallel", "parallel")),
        name="norm_proj_rope",
    )(x3, gain.reshape(1, D), w, cos3, sin3)


def _bmm(a, b):
    return jnp.einsum('nik,nkj->nij', a.astype(BF16), b.astype(BF16), preferred_element_type=F32)


def _bmm_nt(a, b):
    return jnp.einsum('nik,njk->nij', a.astype(BF16), b.astype(BF16), preferred_element_type=F32)


def _bmm_tn(a, b):
    return jnp.einsum('nci,ncj->nij', a.astype(BF16), b.astype(BF16), preferred_element_type=F32)


def _unit_lower_inverse(L, pair=False):
    N, C, _ = L.shape
    width = 2 * C if pair else C
    if pair:
        L = jnp.concatenate([L[:N // 2], L[N // 2:]], axis=2)
    ri = lax.broadcasted_iota(jnp.int32, (C, width), 0)
    lane = lax.broadcasted_iota(jnp.int32, (C, width), 1)
    left = lane < C
    ci = jnp.where(left, lane, lane - C)

    def same_block(size):
        return (ri // size) == (ci // size)

    def mm(a, b):
        if not pair:
            return _bmm(a, b)
        b = b.astype(BF16)
        zero = jnp.zeros_like(b)
        return _bmm(a, jnp.concatenate([jnp.where(left, b, zero), jnp.where(left, zero, b)], axis=1))

    eye = (ri == ci).astype(F32)
    Ld = jnp.where(same_block(INV_BLOCK), L, 0.0)
    X = eye - Ld
    P = Ld
    for _ in range(int(math.log2(INV_BLOCK)) - 1):
        P = mm(P, P)
        X = X + mm(X, P)
    size = INV_BLOCK
    while size < C:
        off = jnp.where(same_block(2 * size) & ~same_block(size), L, 0.0)
        X = X - mm(mm(X, off), X)
        size *= 2
    return jnp.concatenate([X[:, :, :C], X[:, :, C:]], axis=0) if pair else X


def _shifted_rows(buf_ref, cur, prev, first, shifts):
    TB = cur.shape[0]
    buf_ref[0:SUBLANES, :] = jnp.where(first, 0.0, prev)
    buf_ref[SUBLANES:, :] = cur
    return [buf_ref[SUBLANES - s:SUBLANES - s + TB, :] for s in shifts]


def _split_heads(x, n_heads, width):
    return jnp.stack([x[:, h * width:(h + 1) * width] for h in range(n_heads)], axis=0)


def _merge_heads(x):
    return jnp.concatenate([x[h] for h in range(x.shape[0])], axis=-1)


def _gdn_kernel(q_ref, k_ref, v_ref, z_ref, ab_ref, qp_ref, kp_ref, vp_ref, cw_ref, alog_ref, dtb_ref, nw_ref,
                o_ref, buf_ref, s_ref, qd_ref, pg_ref, qg_ref, ov_ref, egl_ref, oc_ref):
    HH, DK, DV, C = GDN_HEADS, GDN_DK, GDN_DV, CHUNK
    BB, TB = q_ref.shape[0], q_ref.shape[1]
    H = BB * HH
    NC = TB // C
    first = pl.program_id(1) == 0

    @pl.when(first)
    def _():
        s_ref[...] = jnp.zeros_like(s_ref)

    def conv_silu(cur_ref, prev_ref, j, width):
        outs = []
        for bb in range(BB):
            cur = cur_ref[bb]
            x3, x2, x1 = _shifted_rows(buf_ref, cur, prev_ref[bb], first, (3, 2, 1))
            w = cw_ref[j]
            y = x3 * w[0:1] + x2 * w[1:2] + x1 * w[2:3] + cur * w[3:4]
            outs.append(_split_heads(y * _sigmoid(y), HH, width))
        return jnp.concatenate(outs, axis=0)

    q = conv_silu(q_ref, qp_ref, 0, DK)
    k = conv_silu(k_ref, kp_ref, 1, DK)
    v = conv_silu(v_ref, vp_ref, 2, DV)
    q = q * lax.rsqrt(jnp.sum(q * q, axis=-1, keepdims=True) + L2_EPS) * (DK ** -0.5)
    k = k * lax.rsqrt(jnp.sum(k * k, axis=-1, keepdims=True) + L2_EPS)
    a = jnp.stack([ab_ref[bb][:, h:h + 1] for bb in range(BB) for h in range(HH)], axis=0)
    b = jnp.stack([ab_ref[bb][:, HH + h:HH + h + 1] for bb in range(BB) for h in range(HH)], axis=0)
    beta = _sigmoid(b)
    per_head = lambda ref: jnp.concatenate([ref[...]] * BB, axis=0)
    g = -jnp.exp(per_head(alog_ref)) * _softplus(a + per_head(dtb_ref))

    N = H * NC
    q = q.reshape(N, C, DK)
    k = k.reshape(N, C, DK)
    v = v.reshape(N, C, DV)
    beta = beta.reshape(N, C, 1)
    g = g.reshape(N, C, 1)

    ri = lax.broadcasted_iota(jnp.int32, (C, C), 0)
    ci = lax.broadcasted_iota(jnp.int32, (C, C), 1)
    causal = ri >= ci
    strict = ri > ci
    gb = jnp.broadcast_to(g, (N, C, C))
    g_row = jnp.sum(jnp.where(ri == ci, gb, 0.0), axis=1, keepdims=True)
    gc_col = jnp.sum(jnp.where(causal, jnp.broadcast_to(g_row, (N, C, C)), 0.0), axis=2, keepdims=True)
    gc_row = jnp.sum(jnp.where(ri <= ci, gb, 0.0), axis=1, keepdims=True)
    decay = jnp.where(causal, jnp.exp(jnp.where(causal, gc_col - gc_row, 0.0)), 0.0)

    kb = k * beta
    vb = v * beta
    kq = _bmm_nt(jnp.concatenate([kb, q], axis=1), k)
    L = jnp.where(strict, kq[:, :C] * decay, 0.0)
    aqk = kq[:, C:] * decay
    tinv = _unit_lower_inverse(L)
    egc = jnp.exp(gc_col)
    uw = _bmm(tinv, jnp.concatenate([vb, kb * egc], axis=2))
    auw = _bmm(aqk, uw)
    g_last = gc_col[:, C - 1:C, :]
    k_dec = k * jnp.exp(g_last - gc_col)
    kuw = _bmm_tn(k_dec, uw)
    qd_ref[...] = (q * egc - auw[:, :, DV:]).reshape(H, NC, C, DK)
    ov_ref[...] = auw[:, :, :DV].reshape(H, NC, C, DV)
    pg_ref[...] = kuw[:, :, DV:].reshape(H, NC, DK, DK)
    qg_ref[...] = kuw[:, :, :DV].reshape(H, NC, DK, DV)
    egl_ref[...] = jnp.broadcast_to(jnp.exp(g_last), (N, 1, LANES)).reshape(H, NC, 1, LANES)

    for c in range(NC):
        S = s_ref[...]
        oc_ref[:, c] = _bmm(qd_ref[:, c], S) + ov_ref[:, c]
        s_ref[...] = S * egl_ref[:, c] - _bmm(pg_ref[:, c], S) + qg_ref[:, c]

    o = oc_ref[...].reshape(H, TB, DV)
    o = o * lax.rsqrt(jnp.mean(o * o, axis=-1, keepdims=True) + RMS_EPS) * nw_ref[...]
    for bb in range(BB):
        z = _split_heads(z_ref[bb], HH, DV)
        o_ref[bb] = _merge_heads(o[bb * HH:(bb + 1) * HH] * (z * _sigmoid(z)))


def gdn_mixer(proj3, conv_w, a_log, dt_bias, norm_w, tb=256, seqs=2):
    B, S, _ = proj3.shape
    H, DK, DV, C = GDN_HEADS, GDN_DK, GDN_DV, CHUNK
    W = H * DK
    NC = tb // C
    HS = seqs * H
    nq, nk, nv, nz, nab = EV_Q // W, EV_K // W, EV_V // W, EV_Z // W, (EV_SMALL + SMALL_AB) // LANES
    rows8 = tb // SUBLANES

    def cur(cb):
        return pl.BlockSpec((seqs, tb, W), lambda b, i, cb=cb: (b, i, cb))

    def prev(cb):
        return pl.BlockSpec((seqs, SUBLANES, W), lambda b, i, cb=cb: (b, jnp.maximum(i * rows8 - 1, 0), cb))

    cw = conv_w.reshape(GDN_CONV, 3, W).transpose(1, 0, 2)
    return pl.pallas_call(
        _gdn_kernel,
        grid=(B // seqs, S // tb),
        in_specs=[cur(nq), cur(nk), cur(nv), cur(nz),
                  pl.BlockSpec((seqs, tb, LANES), lambda b, i: (b, i, nab)),
                  prev(nq), prev(nk), prev(nv),
                  _resident((3, GDN_CONV, W)), _resident((H, 1, 1)), _resident((H, 1, 1)),
                  _resident((1, 1, DV))],
        out_specs=pl.BlockSpec((seqs, tb, H * DV), lambda b, i: (b, i, 0)),
        out_shape=jax.ShapeDtypeStruct((B, S, H * DV), F32),
        scratch_shapes=[pltpu.VMEM((tb + SUBLANES, W), F32),
                        pltpu.VMEM((HS, DK, DV), F32),
                        pltpu.VMEM((HS, NC, C, DK), F32),
                        pltpu.VMEM((HS, NC, DK, DK), F32),
                        pltpu.VMEM((HS, NC, DK, DV), F32),
                        pltpu.VMEM((HS, NC, C, DV), F32),
                        pltpu.VMEM((HS, NC, 1, LANES), F32),
                        pltpu.VMEM((HS, NC, C, DV), F32)],
        compiler_params=_cparams(("parallel", "arbitrary")),
        name="gdn_mixer",
    )(proj3, proj3, proj3, proj3, proj3, proj3, proj3, proj3, cw,
      a_log.reshape(H, 1, 1), dt_bias.reshape(H, 1, 1), norm_w.reshape(1, 1, DV))


def _rwkv_kernel(*refs, has_vres):
    if has_vres:
        (r_ref, k_ref, v_ref, lo_ref, rp_ref, kp_ref, vp_ref, lop_ref, vf_ref,
         mu_ref, mul_ref, vec_ref, w2_ref, a2_ref, g2_ref, v2_ref,
         y_ref, buf_ref, bufl_ref, s_ref, rr_ref, pp_ref, qq_ref, yv_ref, egc_ref, yc_ref) = refs
    else:
        (r_ref, k_ref, v_ref, lo_ref, rp_ref, kp_ref, vp_ref, lop_ref,
         mu_ref, mul_ref, vec_ref, w2_ref, a2_ref, g2_ref,
         y_ref, vf_out_ref, buf_ref, bufl_ref, s_ref, rr_ref, pp_ref, qq_ref, yv_ref, egc_ref, yc_ref) = refs
    H, D, C = RWKV_HEADS, RWKV_HEAD, CHUNK
    TB = r_ref.shape[1]
    NC = TB // C
    first = pl.program_id(1) == 0

    @pl.when(first)
    def _():
        s_ref[...] = jnp.zeros_like(s_ref)

    def mix(cur_ref, prev_ref, mu, buf):
        cur = cur_ref[0]
        (sh,) = _shifted_rows(buf, cur, prev_ref[0], first, (1,))
        return cur + mu * (sh - cur)

    r = mix(r_ref, rp_ref, mu_ref[0:1], buf_ref)
    k = mix(k_ref, kp_ref, mu_ref[1:2], buf_ref)
    v = mix(v_ref, vp_ref, mu_ref[2:3], buf_ref)
    lo = mix(lo_ref, lop_ref, mul_ref[...], bufl_ref)
    lo0, lo1 = lo[:, :LANES], lo[:, LANES:]
    w0, a0, k_k, k_a, r_k, ln_w, ln_b = (vec_ref[j:j + 1] for j in range(7))

    w_log = -_softplus(-(w0 + jnp.dot(jnp.tanh(lo0), w2_ref[...], preferred_element_type=F32))) - 0.5
    lw = -jnp.exp(w_log)
    a = _sigmoid(a0 + jnp.dot(lo0, a2_ref[...], preferred_element_type=F32))
    gate = jnp.dot(_sigmoid(lo1), g2_ref[...], preferred_element_type=F32)
    if has_vres:
        v0 = vec_ref[7:8]
        v = v + (vf_ref[0] - v) * _sigmoid(v0 + jnp.dot(lo0, v2_ref[...], preferred_element_type=F32))
    else:
        vf_out_ref[0] = v

    ri = lax.broadcasted_iota(jnp.int32, (C, C), 0)
    ci = lax.broadcasted_iota(jnp.int32, (C, C), 1)
    tril = (ri >= ci).astype(BF16)
    lw_hi = lw.astype(BF16)
    lw_rest = lw - lw_hi.astype(F32)
    lw_mid = lw_rest.astype(BF16)
    lw_lo = (lw_rest - lw_mid.astype(F32)).astype(BF16)
    gi = jnp.concatenate(
        [sum(jnp.dot(tril, part[c * C:(c + 1) * C], preferred_element_type=F32) for part in (lw_hi, lw_mid, lw_lo))
         for c in range(NC)], axis=0)

    kk = _split_heads(k * k_k, H, D)
    kk = kk * lax.rsqrt(jnp.sum(kk * kk, axis=-1, keepdims=True) + L2_EPS)
    k = k * (1.0 + (a - 1.0) * k_a)
    r_h = _split_heads(r, H, D)
    k_h = _split_heads(k, H, D)
    v_h = _split_heads(v, H, D)
    a_h = _split_heads(a, H, D)
    gi_h = _split_heads(gi, H, D)
    lw_h = _split_heads(lw, H, D)
    bonus = jnp.sum(r_h * k_h * _split_heads(r_k, H, D), axis=-1, keepdims=True) * v_h

    N = H * NC
    rc, kc, vc, kkc, ac, gic, lwc = (t.reshape(N, C, D) for t in (r_h, k_h, v_h, kk, a_h, gi_h, lw_h))
    g_end = gic[:, C - 1:C, :]
    e_neg = jnp.exp(-gic)
    e_tail = jnp.exp(g_end - gic)
    at = -kkc * jnp.exp(gic - lwc)
    bvec = kkc * ac
    rt = rc * jnp.exp(gic)
    bt = bvec * e_neg
    kt = kc * e_neg
    b_dec = bvec * e_tail
    k_dec = kc * e_tail
    strict = ri > ci
    causal = ri >= ci
    amat = _bmm_nt(jnp.concatenate([at, rt], axis=1), jnp.concatenate([bt, kt], axis=1))
    a_ab = jnp.where(strict, amat[:, :C, :C], 0.0)
    a_ak = jnp.where(strict, amat[:, :C, C:], 0.0)
    a_r = jnp.concatenate([jnp.where(causal, amat[:, C:, :C], 0.0), jnp.where(causal, amat[:, C:, C:], 0.0)], axis=2)
    tinv = _unit_lower_inverse(-a_ab, pair=True)
    x2 = _bmm(tinv, jnp.concatenate([at, _bmm(a_ak, vc)], axis=2))
    xv = jnp.concatenate([x2, jnp.concatenate([jnp.zeros_like(vc), vc], axis=2)], axis=1)
    y2 = _bmm(a_r, xv)
    pq = _bmm_tn(xv, jnp.concatenate([b_dec, k_dec], axis=1))
    rr_ref[...] = (rt + y2[:, :, :D]).reshape(H, NC, C, D)
    yv_ref[...] = y2[:, :, D:].reshape(H, NC, C, D)
    pp_ref[...] = pq[:, :D].reshape(H, NC, D, D)
    qq_ref[...] = pq[:, D:].reshape(H, NC, D, D)
    egc_ref[...] = jnp.exp(g_end).reshape(H, NC, 1, D)

    for c in range(NC):
        S = s_ref[...]
        yc_ref[:, c] = _bmm_nt(rr_ref[:, c], S) + yv_ref[:, c]
        s_ref[...] = S * egc_ref[:, c] + _bmm(S, pp_ref[:, c]) + qq_ref[:, c]

    y = yc_ref[...].reshape(H, TB, D)
    mean = jnp.mean(y, axis=-1, keepdims=True)
    yc = y - mean
    var = jnp.mean(yc * yc, axis=-1, keepdims=True)
    y = _merge_heads(yc * lax.rsqrt(var + RWKV_LN_EPS)) * ln_w + ln_b
    y_ref[0] = (y + _merge_heads(bonus)) * gate


def rwkv_mixer(proj3, mu, mu_lora, vecs, w2, a2, g2, v_first=None, v2=None, tb=256):
    B, S, _ = proj3.shape
    H, D, C = RWKV_HEADS, RWKV_HEAD, CHUNK
    W = RWKV_DIM
    WL = 2 * LANES
    NC = tb // C
    rows8 = tb // SUBLANES
    has_vres = v_first is not None

    def cur(col, width):
        return pl.BlockSpec((1, tb, width), lambda b, i, cb=col // width: (b, i, cb))

    def prev(col, width):
        return pl.BlockSpec((1, SUBLANES, width),
                            lambda b, i, cb=col // width: (b, jnp.maximum(i * rows8 - 1, 0), cb))

    in_specs = [cur(EV_R, W), cur(EV_RK, W), cur(EV_RV, W), cur(EV_SMALL, WL),
                prev(EV_R, W), prev(EV_RK, W), prev(EV_RV, W), prev(EV_SMALL, WL)]
    args = [proj3] * 8
    if has_vres:
        in_specs.append(pl.BlockSpec((1, tb, W), lambda b, i: (b, i, 0)))
        args.append(v_first)
    in_specs += [_resident(mu.shape), _resident(mu_lora.shape), _resident(vecs.shape),
                 _resident(w2.shape), _resident(a2.shape), _resident(g2.shape)]
    args += [mu, mu_lora, vecs, w2, a2, g2]
    if has_vres:
        in_specs.append(_resident(v2.shape))
        args.append(v2)
    out_block = pl.BlockSpec((1, tb, W), lambda b, i: (b, i, 0))
    out_sds = jax.ShapeDtypeStruct((B, S, W), F32)
    res = pl.pallas_call(
        functools.partial(_rwkv_kernel, has_vres=has_vres),
        grid=(B, S // tb),
        in_specs=in_specs,
        out_specs=out_block if has_vres else [out_block, out_block],
        out_shape=out_sds if has_vres else [out_sds, out_sds],
        scratch_shapes=[pltpu.VMEM((tb + SUBLANES, W), F32),
                        pltpu.VMEM((tb + SUBLANES, WL), F32),
                        pltpu.VMEM((H, D, D), F32),
                        pltpu.VMEM((H, NC, C, D), F32),
                        pltpu.VMEM((H, NC, D, D), F32),
                        pltpu.VMEM((H, NC, D, D), F32),
                        pltpu.VMEM((H, NC, C, D), F32),
                        pltpu.VMEM((H, NC, 1, D), F32),
                        pltpu.VMEM((H, NC, C, D), F32)],
        compiler_params=_cparams(("parallel", "arbitrary")),
        name="rwkv_mixer",
    )(*args)
    if has_vres:
        return res, v_first
    return res[0], res[1]


def _dil_attn_kernel(q_ref, kc_ref, kp_ref, vc_ref, vp_ref, o_ref, lse_ref, kbuf, vbuf):
    Dh, Bk = DIL_HEAD_DIM, ATT_BLOCK
    RR, QB = q_ref.shape[1], q_ref.shape[2]
    has_prev = pl.program_id(2) > 0
    qi = lax.broadcasted_iota(jnp.int32, (Bk, 2 * Bk), 0)
    cj = lax.broadcasted_iota(jnp.int32, (Bk, 2 * Bk), 1)
    band = (cj >= qi) & (cj <= qi + Bk)
    lane = lax.broadcasted_iota(jnp.int32, (Bk, LANES), 1)
    low = lane < Dh
    low_kv = lax.broadcasted_iota(jnp.int32, (2 * Bk, LANES), 1) < Dh
    one_kv = jnp.ones((2 * Bk, LANES), BF16)
    NP = DIL_WIDTH // LANES

    def pairs(x):
        return jnp.stack([x[:, hp * LANES:(hp + 1) * LANES] for hp in range(NP)], axis=0)

    for rr, j in [(rr, j) for rr in range(RR) for j in range(QB // Bk)]:
        if j == 0:
            kbuf[0:Bk] = kp_ref[0, rr]
            kbuf[Bk:] = kc_ref[0, rr]
            vbuf[0:Bk] = vp_ref[0, rr]
            vbuf[Bk:] = vc_ref[0, rr]
        ok = band if j > 0 else band & ((cj >= Bk) | has_prev)
        ok2 = jnp.concatenate([ok, ok], axis=0)
        q_pairs = pairs(q_ref[0, rr, j * Bk:(j + 1) * Bk, :])
        zero = jnp.zeros_like(q_pairs)
        q2 = jnp.concatenate([jnp.where(low, q_pairs, zero), jnp.where(low, zero, q_pairs)], axis=1)
        k_pairs = pairs(kbuf[j * Bk:(j + 2) * Bk, :])
        v_pairs = pairs(vbuf[j * Bk:(j + 2) * Bk, :])
        s = jnp.einsum('hqd,hkd->hqk', q2, k_pairs, preferred_element_type=F32)
        s = jnp.where(ok2, s, -jnp.inf)
        m = jnp.max(s, axis=-1, keepdims=True).reshape(2 * NP, Bk, 1)
        p = jnp.exp2(s - m.reshape(NP, 2 * Bk, 1)).astype(BF16).reshape(2 * NP, Bk, 2 * Bk)
        v_ext = jnp.stack([jnp.where(low_kv, v_pairs, one_kv), jnp.where(low_kv, one_kv, v_pairs)], axis=1)
        po = jnp.einsum('hqk,hkd->hqd', p, v_ext.reshape(2 * NP, 2 * Bk, LANES), preferred_element_type=F32)
        swapped = pltpu.roll(po.reshape(2 * NP * Bk, LANES), Dh, axis=1).reshape(2 * NP, Bk, LANES)
        r = po / swapped
        o_ref[0, rr, j * Bk:(j + 1) * Bk, :] = jnp.concatenate(
            [jnp.where(low, r[2 * hp], r[2 * hp + 1]) for hp in range(NP)], axis=-1).astype(o_ref.dtype)
        lse_blk = jnp.zeros((Bk, LANES), F32)
        for h in range(2 * NP):
            lse_blk = jnp.where(lane == h, m[h] + jnp.log2((swapped if h % 2 == 0 else po)[h]), lse_blk)
        lse_ref[0, rr, j * Bk:(j + 1) * Bk, :] = lse_blk


def dil_attention_group(q, k, v, step_tokens=8 * ATT_BLOCK):
    B, d, L, W = q.shape
    qb = min(step_tokens, L)
    rr = min(step_tokens // qb, d)
    nj = qb // ATT_BLOCK
    cur = pl.BlockSpec((1, rr, qb, W), lambda b, r, n: (b, r, n, 0))
    prev = pl.BlockSpec((1, rr, ATT_BLOCK, W), lambda b, r, n: (b, r, jnp.maximum(n * nj - 1, 0), 0))
    return pl.pallas_call(
        _dil_attn_kernel,
        grid=(B, d // rr, L // qb),
        in_specs=[cur, cur, prev, cur, prev],
        out_specs=[cur, pl.BlockSpec((1, rr, qb, LANES), lambda b, r, n: (b, r, n, 0))],
        out_shape=[jax.ShapeDtypeStruct((B, d, L, W), BF16), jax.ShapeDtypeStruct((B, d, L, LANES), F32)],
        scratch_shapes=[pltpu.VMEM((ATT_BLOCK + qb, W), BF16), pltpu.VMEM((ATT_BLOCK + qb, W), BF16)],
        compiler_params=_cparams(("parallel", "parallel", "arbitrary")),
        name=f"dil_attn_d{d}",
    )(q, k, k, v, v)


def _merged_attention(o_refs, l_refs, e_ref, a_ref, b_ref, la_ref, lb_ref):
    dils = [d for _, d in DIL_PATTERNS]
    lses = [_interleave_load(r, d, la_ref, lb_ref) for r, d in zip(l_refs, dils)]
    m = jnp.maximum(jnp.maximum(lses[0], lses[1]), lses[2])
    es = [jnp.exp2(l - m) for l in lses]
    inv = 1.0 / (es[0] + es[1] + es[2])
    expand = e_ref[...]
    mixed = None
    for e, o_ref, d in zip(es, o_refs, dils):
        term = _group_sum(e * inv, expand) * _interleave_load(o_ref, d, a_ref, b_ref)
        mixed = term if mixed is None else mixed + term
    return mixed


def _head_expand_matrix():
    expand = np.zeros((LANES, DIL_WIDTH), np.float32)
    for h in range(DIL_HEADS):
        expand[h, h * DIL_HEAD_DIM:(h + 1) * DIL_HEAD_DIM] = 1.0
    return jnp.asarray(expand, BF16)


MOE_TM = 512
MOE_RB = 144
MOE_NBLK = (MOE_TM + N_GROUPS * (MOE_RB - 1)) // MOE_RB
LOGIT_ROWS = pl.cdiv(N_GROUPS * (1 + EXPERTS_PER_GROUP), SUBLANES) * SUBLANES


def _moe_route(h, wr_ref, before_ref):
    NG, EPG, RB = N_GROUPS, EXPERTS_PER_GROUP, MOE_RB
    tm = h.shape[0]
    hb, h_lo = _hi_lo(h)
    nt = (((1,), (1,)), ((), ()))
    lg = lax.dot_general(wr_ref[...], hb, nt, preferred_element_type=F32)
    lg = lg[:LANES] + lg[LANES:] + lax.dot_general(wr_ref[:LANES], h_lo, nt, preferred_element_type=F32)
    L = lg[:LOGIT_ROWS]
    r = lax.broadcasted_iota(jnp.int32, L.shape, 0)
    big = jnp.int32(1 << 30)
    gl = jnp.where(r < NG, L, -jnp.inf)
    gmax = jnp.max(gl, axis=0, keepdims=True)
    gsel = jnp.min(jnp.where(gl == gmax, r, big), axis=0, keepdims=True)
    p_group = 1.0 / jnp.sum(jnp.where(r < NG, jnp.exp(L - gmax), 0.0), axis=0, keepdims=True)
    lo = NG + gsel * EPG
    el = jnp.where((r >= lo) & (r < lo + EPG), L, -jnp.inf)
    v1 = jnp.max(el, axis=0, keepdims=True)
    i1 = jnp.min(jnp.where(el == v1, r, big), axis=0, keepdims=True)
    el2 = jnp.where(r == i1, -jnp.inf, el)
    v2 = jnp.max(el2, axis=0, keepdims=True)
    i2 = jnp.min(jnp.where(el2 == v2, r, big), axis=0, keepdims=True)
    t = jnp.exp(v2 - v1)
    gates = jnp.where(r == i1, p_group / (1.0 + t), 0.0) + jnp.where(r == i2, p_group * t / (1.0 + t), 0.0)
    gate4 = [jnp.sum(jnp.where(r == lo + j, gates, 0.0), axis=0, keepdims=True) for j in range(EPG)]

    r8 = lax.broadcasted_iota(jnp.int32, (SUBLANES, tm), 0)
    onehot = (r8 == gsel).astype(F32)
    rank = jnp.dot(onehot.astype(BF16), before_ref[...], preferred_element_type=F32)
    count = jnp.sum(onehot, axis=1, keepdims=True).astype(jnp.int32)
    nblk = sum((count > j * RB).astype(jnp.int32) for j in range(pl.cdiv(tm, RB)))
    padded = (nblk * RB).astype(F32)
    start, starts = jnp.zeros((1, 1), F32), []
    for g in range(NG):
        starts.append(start)
        start = start + padded[g:g + 1]
    start8 = jnp.concatenate(starts + [jnp.zeros((SUBLANES - NG, 1), F32)], axis=0)
    dest = jnp.sum(onehot * (start8 + rank), axis=0, keepdims=True)
    return dest, gate4, nblk


def _moe_kernel(x_ref, *refs, final_norm, mixer_proj):
    NG, EPG, RB, NBLK = N_GROUPS, EXPERTS_PER_GROUP, MOE_RB, MOE_NBLK
    x = x_ref[...]
    if mixer_proj == "even":
        ya_ref, yb_ref, wa_ref, wb_ref = refs[:4]
        refs = refs[4:]
        x = (x + jnp.dot(ya_ref[...].astype(BF16), wa_ref[...], preferred_element_type=F32)
             + jnp.dot(yb_ref[...].astype(BF16), wb_ref[...], preferred_element_type=F32))
    elif mixer_proj == "odd":
        mixed = _merged_attention(refs[0:3], refs[3:6], refs[6], *refs[-4:])
        x = x + jnp.dot(mixed.astype(BF16), refs[7][...], preferred_element_type=F32)
        refs = refs[8:-4]
    g_ref, wr_ref, before_ref, w1_ref, w3_ref, w2_ref, fg_ref, o_ref, p_ref, hb_ref, g2_ref, hh_ref, ys_ref = refs
    FF = w1_ref.shape[3]
    tm = x_ref.shape[0]
    h = _rmsnorm(x, g_ref[...])
    dest, gate4, nblk = _moe_route(h, wr_ref, before_ref)
    rows = lax.broadcasted_iota(jnp.int32, (NBLK * RB, tm), 0)
    p_ref[...] = jnp.where(rows == dest.astype(jnp.int32), 1.0, 0.0).astype(BF16)
    hb_ref[...] = h.astype(BF16)
    g4 = jnp.concatenate(gate4 + [jnp.zeros((SUBLANES - EPG, tm), F32)], axis=0)
    g_hi = g4.astype(BF16).astype(F32)
    g2_ref[...] = jnp.concatenate([g_hi, g4 - g_hi, jnp.zeros((LANES - 2 * SUBLANES, tm), F32)], axis=0).astype(BF16)

    ends, acc = [], jnp.int32(0)
    for g in range(NG):
        acc = acc + jnp.sum(nblk[g:g + 1])
        ends.append(acc)
    for b in range(NBLK):
        rs = slice(b * RB, (b + 1) * RB)
        grp = sum((b >= e).astype(jnp.int32) for e in ends[:-1])

        @pl.when(b < ends[-1])
        def _():
            pb = p_ref[rs]
            xr = jnp.dot(pb, hb_ref[...], preferred_element_type=F32).astype(BF16)
            gate = lax.dot_general(pb, g2_ref[...], (((1,), (1,)), ((), ())), preferred_element_type=F32)
            for j in range(EPG):
                a = jnp.dot(xr, w1_ref[0, grp * EPG + j], preferred_element_type=F32)
                u = jnp.dot(xr, w3_ref[0, grp * EPG + j], preferred_element_type=F32)
                gj = gate[:, j:j + 1] + gate[:, SUBLANES + j:SUBLANES + j + 1]
                hh_ref[:, j * FF:(j + 1) * FF] = ((a * _sigmoid(a)) * u * gj).astype(BF16)
            ys_ref[rs] = jnp.dot(hh_ref[...], w2_ref[0, grp], preferred_element_type=F32).astype(BF16)

        @pl.when(b >= ends[-1])
        def _():
            ys_ref[rs] = jnp.zeros((RB, ys_ref.shape[1]), BF16)

    def scatter_back(n_blocks):
        rows_used = slice(0, n_blocks * RB)
        y = x + lax.dot_general(p_ref[rows_used], ys_ref[rows_used], (((0,), (0,)), ((), ())),
                                preferred_element_type=F32)
        o_ref[...] = _rmsnorm(y, fg_ref[...]) if final_norm else y

    usual = NG + 1
    pl.when(ends[-1] <= usual)(lambda: scatter_back(usual))
    pl.when(ends[-1] > usual)(lambda: scatter_back(NBLK))


def moe_block(x, gain, w_router, w1, w3, w2, layer, final_gain=None, mixer=None):
    T, D = x.shape
    _, NE, _, FF = w1.shape
    tm, RB, NBLK = MOE_TM, MOE_RB, MOE_NBLK
    final_norm = final_gain is not None
    fg = (final_gain if final_norm else jnp.ones((D,), F32)).reshape(1, D)

    def of_layer(w):
        return pl.BlockSpec((1,) + w.shape[1:], lambda i: (layer, 0, 0, 0), pipeline_mode=pl.Buffered(1))

    mixer_specs, mixer_args, mixer_scratch = [], [], []
    kind = None if mixer is None else mixer[0]
    if kind == "even":
        _, ya, yb, wa, wb = mixer
        mixer_specs = [pl.BlockSpec((tm, ya.shape[1]), lambda i: (i, 0)), pl.BlockSpec((tm, yb.shape[1]), lambda i: (i, 0)),
                       _resident(wa.shape), _resident(wb.shape)]
        mixer_args = [ya, yb, wa, wb]
    elif kind == "odd":
        _, outs, lses, w_out = mixer
        tiles_per_seq = outs[0].shape[1] * outs[0].shape[2] // tm
        for width, arrs in ((DIL_WIDTH, outs), (LANES, lses)):
            for (_, d), arr in zip(DIL_PATTERNS, arrs):
                mixer_specs.append(pl.BlockSpec((1, d, tm // d, width),
                                                lambda i: (i // tiles_per_seq, 0, i % tiles_per_seq, 0)))
                mixer_args.append(arr)
        mixer_specs += [_resident((LANES, DIL_WIDTH)), _resident(w_out.shape)]
        mixer_args += [_head_expand_matrix(), w_out]
        n_tiles = DIL_WIDTH // LANES
        mixer_scratch = [pltpu.VMEM((n_tiles, tm, LANES), F32), pltpu.VMEM((n_tiles, tm, LANES), F32),
                         pltpu.VMEM((1, tm, LANES), F32), pltpu.VMEM((1, tm, LANES), F32)]
    return pl.pallas_call(
        functools.partial(_moe_kernel, final_norm=final_norm, mixer_proj=kind),
        grid=(T // tm,),
        in_specs=[pl.BlockSpec((tm, D), lambda i: (i, 0)), *mixer_specs, _resident((1, D)), _resident(w_router.shape),
                  _resident((tm, tm)), of_layer(w1), of_layer(w3), of_layer(w2), _resident((1, D))],
        out_specs=pl.BlockSpec((tm, D), lambda i: (i, 0)),
        out_shape=jax.ShapeDtypeStruct((T, D), F32),
        scratch_shapes=[pltpu.VMEM((NBLK * RB, tm), BF16), pltpu.VMEM((tm, D), BF16),
                        pltpu.VMEM((LANES, tm), BF16), pltpu.VMEM((RB, EXPERTS_PER_GROUP * FF), BF16),
                        pltpu.VMEM((NBLK * RB, D), BF16), *mixer_scratch],
        compiler_params=_cparams(("parallel",)),
        name="moe_block",
    )(x, *mixer_args, gain.reshape(1, D), w_router, jnp.asarray(np.triu(np.ones((tm, tm), np.float32), 1), BF16),
      w1, w3, w2, fg)


def _router_weights(w_group, w_expert):
    w = _place([(0, w_group)] + [(N_GROUPS + g * EXPERTS_PER_GROUP, w_expert[g]) for g in range(N_GROUPS)], LANES).T
    return jnp.concatenate(_hi_lo(w), axis=0)


def _place(cols, total):
    lead, dtype = cols[0][1].shape[:-1], cols[0][1].dtype
    parts, pos = [], 0
    for off, arr in sorted(cols, key=lambda c: c[0]):
        if off > pos:
            parts.append(jnp.zeros(lead + (off - pos,), dtype))
        parts.append(arr)
        pos = off + arr.shape[-1]
    if total > pos:
        parts.append(jnp.zeros(lead + (total - pos,), dtype))
    return jnp.concatenate(parts, axis=-1)


def _even_layout(t, vres=None):
    gw = GDN_HEADS * GDN_DK
    sizes = [gw, gw, gw, gw, GDN_HEADS, GDN_HEADS, RWKV_DIM, RWKV_DIM, RWKV_DIM, DECAY_LORA, AAA_LORA, GATE_LORA]
    gq, gk, gv, gz, ga, gb, rr, rk, rv, lw, la, lg = jnp.split(t, np.cumsum(sizes)[:-1].tolist(), axis=-1)
    cols = [(EV_Q, gq), (EV_K, gk), (EV_V, gv), (EV_Z, gz), (EV_R, rr), (EV_RK, rk), (EV_RV, rv),
            (EV_SMALL + SMALL_W, lw), (EV_SMALL + SMALL_A, la), (EV_SMALL + SMALL_G, lg),
            (EV_SMALL + SMALL_AB, ga), (EV_SMALL + SMALL_AB + GDN_HEADS, gb)]
    if vres is not None:
        cols.append((EV_SMALL + SMALL_V, vres))
    return _place(cols, EV_NPAD)


def _lora_up(w, lane):
    return jnp.concatenate([jnp.zeros((lane, w.shape[1]), F32), w,
                            jnp.zeros((LANES - lane - w.shape[0], w.shape[1]), F32)], axis=0)


def kernel(x, positions, ev_norm, ev_w_in, rwkv_vres_down, ev_w_out, gdn_conv_w, gdn_A_log, gdn_dt_bias, gdn_norm,
           rwkv_mu, rwkv_w0, rwkv_w2, rwkv_a0, rwkv_a2, rwkv_g2, rwkv_k_k, rwkv_k_a, rwkv_r_k, rwkv_ln_w, rwkv_ln_b,
           rwkv_vres_mu, rwkv_v0, rwkv_v2, od_norm, od_w_in, od_w_out, ffn_norm, moe_w_group, moe_w_expert,
           moe_w1, moe_w3, moe_w2, final_norm):
    B, S, D = x.shape
    T = B * S
    depth = ffn_norm.shape[0]
    xf = x.reshape(T, D)
    cos, sin = rope_tables(positions.reshape(T))
    n_groups = len(DIL_PATTERNS)
    moe_w1_bf = moe_w1.astype(BF16)
    moe_w3_bf = moe_w3.astype(BF16)
    moe_w2_bf = moe_w2.astype(BF16).reshape(depth, N_GROUPS, -1, D)
    v_first = None
    for layer in range(depth):
        i = layer // 2
        if layer % 2 == 0:
            vres_w = None if i == 0 else rwkv_vres_down[i - 1].astype(BF16)
            w_in = _even_layout(ev_w_in[i].astype(BF16), vres_w)
            proj3 = norm_proj(xf, ev_norm[i], w_in).reshape(B, S, EV_NPAD)
            mu_r, mu_k, mu_v, mu_w, mu_a, mu_g = jnp.split(
                rwkv_mu[i], np.cumsum([RWKV_DIM, RWKV_DIM, RWKV_DIM, DECAY_LORA, AAA_LORA]).tolist())
            mu = jnp.stack([mu_r, mu_k, mu_v])
            lora_cols = [(SMALL_W, mu_w), (SMALL_A, mu_a), (SMALL_G, mu_g)]
            if i > 0:
                lora_cols.append((SMALL_V, rwkv_vres_mu[i - 1]))
            mu_lora = _place(lora_cols, 2 * LANES).reshape(1, 2 * LANES)
            vec_rows = [rwkv_w0[i], rwkv_a0[i], rwkv_k_k[i], rwkv_k_a[i], rwkv_r_k[i].reshape(-1), rwkv_ln_w[i], rwkv_ln_b[i]]
            if i > 0:
                vec_rows.append(rwkv_v0[i - 1])
            vecs = jnp.stack(vec_rows)
            w2 = _lora_up(rwkv_w2[i], SMALL_W)
            a2 = _lora_up(rwkv_a2[i], SMALL_A)
            g2 = _lora_up(rwkv_g2[i], SMALL_G - LANES)
            ya = gdn_mixer(proj3, gdn_conv_w[i], gdn_A_log[i], gdn_dt_bias[i], gdn_norm[i])
            if i == 0:
                yb, v_first = rwkv_mixer(proj3, mu, mu_lora, vecs, w2, a2, g2)
            else:
                yb, _ = rwkv_mixer(proj3, mu, mu_lora, vecs, w2, a2, g2, v_first, _lora_up(rwkv_v2[i - 1], SMALL_V))
            w_out = ev_w_out[i].astype(BF16)
            na = GDN_HEADS * GDN_DV
            mixer = ("even", ya.reshape(T, na), yb.reshape(T, RWKV_DIM), w_out[:na], w_out[na:])
        else:
            w_in = od_w_in[i].astype(BF16)
            qkv = norm_proj_rope(xf.reshape(B, S, D), od_norm[i], w_in, cos.reshape(B, S, LANES),
                                 sin.reshape(B, S, LANES), DIL_HEAD_DIM ** -0.5 * math.log2(math.e))
            outs, lses = [], []
            for gi, (window, dilation) in enumerate(DIL_PATTERNS):
                assert window // dilation == ATT_BLOCK and (S // dilation) % (2 * ATT_BLOCK) == 0
                o, lse = dil_attention_group(qkv[gi], qkv[n_groups + gi], qkv[2 * n_groups + gi])
                outs.append(o)
                lses.append(lse)
            mixer = ("odd", outs, lses, od_w_out[i].astype(BF16))
        w_router = _router_weights(moe_w_group[layer], moe_w_expert[layer])
        xf = moe_block(xf, ffn_norm[layer], w_router, moe_w1_bf, moe_w3_bf, moe_w2_bf, layer,
                       final_norm if layer == depth - 1 else None, mixer)
    return xf.reshape(B, S, D)
```
